```python
import jax
import jax.numpy as jnp
from jax import lax
import numpy as np

D_MODEL = 1024
BATCH = 32
SEQ = 256
DEPTH = 2
DEC_BATCH = 4
DEC_SEQ = 2048
PAST_LEN = 256

GRID_W = 64
EPS = 1e-6
ROPE_BASE = 10000.0
Q_BLOCK = 128
MLSTM_CHUNK = 64

M_HEADS = 4
M_DK = 128
M_DV = 128
M_WIDTH = M_HEADS * M_DV
A_HEADS = 8
A_NOPE = 64
A_ROPE = 32
A_QK = A_NOPE + A_ROPE
A_V = 64
A_Q_LORA = 256
A_KV_LORA = 128
A_WIDTH = A_HEADS * A_V
G_HEADS = 8
G_KV_HEADS = 2
G_GROUP = G_HEADS // G_KV_HEADS
G_HD = 64
G_WIDTH = G_HEADS * G_HD
N_BRANCH = 3
BRANCH_WIDTH = 512
D_FF = ((8 * D_MODEL // 3 + 255) // 256) * 256
N_MOD = 6

IN_SIZES = (
    N_BRANCH * D_MODEL,
    M_HEADS * M_DK, M_HEADS * M_DK,
    M_HEADS * M_DV, M_HEADS * M_DV,
    4 * M_HEADS,
    A_Q_LORA, A_KV_LORA, A_ROPE,
    G_HEADS * G_HD, G_KV_HEADS * G_HD, G_KV_HEADS * G_HD,
)
D_IN = sum(IN_SIZES)
IN_SPLITS = tuple(sum(IN_SIZES[:i + 1]) for i in range(len(IN_SIZES) - 1))

kernel_name = 'hybrid_mlstm_mla_gqa_dit_step'


def rms_norm(x, g):
    xf = x.astype(jnp.float32)
    y = xf * lax.rsqrt(jnp.mean(xf * xf, axis=-1, keepdims=True) + EPS)
    return (y * g.astype(jnp.float32)).astype(x.dtype)


def axial_rope(rows, rot_dim):
    n_freq = rot_dim // 4
    freqs = ROPE_BASE ** (-jnp.arange(n_freq, dtype=jnp.float32) / n_freq)
    row = jnp.repeat(jnp.arange(rows, dtype=jnp.float32), GRID_W)
    col = jnp.tile(jnp.arange(GRID_W, dtype=jnp.float32), rows)
    ang = jnp.concatenate([row[:, None] * freqs, col[:, None] * freqs], axis=-1)
    return jnp.cos(ang), jnp.sin(ang)


def apply_rope(x, cos, sin):
    half = x.shape[-1] // 2
    xf = x.astype(jnp.float32)
    x1, x2 = xf[..., :half], xf[..., half:]
    shape = (x.shape[1],) + (1,) * (x.ndim - 3) + (half,)
    c, s = cos.reshape(shape), sin.reshape(shape)
    return jnp.concatenate([x1 * c - x2 * s, x1 * s + x2 * c], axis=-1).astype(x.dtype)


def block_attention(q, k, v):
    b, t, hk, g, d = q.shape
    nblk = t // Q_BLOCK
    scale = d ** -0.5
    qb = jnp.moveaxis(q.reshape(b, nblk, Q_BLOCK, hk, g, d), 1, 0)

    def one_block(qi):
        s = jnp.einsum('bqhgd,bshd->bhgqs', qi, k, preferred_element_type=jnp.float32) * scale
        pr = jax.nn.softmax(s, axis=-1).astype(v.dtype)
        return jnp.einsum('bhgqs,bshe->bqhge', pr, v)

    out = lax.map(one_block, qb)
    return jnp.moveaxis(out, 0, 1).reshape(b, t, hk, g, v.shape[-1])


def mlstm_scan(q, k, v, i_pre, f_pre, state):
    b, nh, t, _ = q.shape
    L = MLSTM_CHUNK
    nc = t // L

    def chunks(a):
        return jnp.moveaxis(a.reshape((b, nh, nc, L) + a.shape[3:]), 2, 0)

    logf = jax.nn.log_sigmoid(f_pre)
    xs = (chunks(q), chunks(k), chunks(v), chunks(i_pre), chunks(logf))
    lower = jnp.tril(jnp.ones((L, L), dtype=bool))

    def step(carry, inp):
        C, n, m = carry
        qc, kc, vc, ic, fc = inp
        bcum = jnp.cumsum(fc, axis=-1)
        d_log = bcum[..., :, None] - bcum[..., None, :] + ic[..., None, :]
        d_log = jnp.where(lower, d_log, -jnp.inf)
        inter = bcum + m[..., None]
        m_t = jnp.maximum(inter, jnp.max(d_log, axis=-1))
        w_inter = jnp.exp(inter - m_t)
        s = jnp.einsum('bhld,bhsd->bhls', qc, kc) * jnp.exp(d_log - m_t[..., None])
        num = jnp.einsum('bhld,bhde->bhle', qc, C) * w_inter[..., None] + jnp.einsum('bhls,bhse->bhle', s, vc)
        den = jnp.einsum('bhld,bhd->bhl', qc, n) * w_inter + jnp.sum(s, axis=-1)
        hc = num / jnp.maximum(jnp.abs(den), jnp.exp(-m_t))[..., None]
        g = bcum[..., -1:] - bcum + ic
        total = bcum[..., -1] + m
        m_new = jnp.maximum(total, jnp.max(g, axis=-1))
        wg = jnp.exp(g - m_new[..., None])
        decay = jnp.exp(total - m_new)
        C_new = C * decay[..., None, None] + jnp.einsum('bhs,bhsd,bhse->bhde', wg, kc, vc)
        n_new = n * decay[..., None] + jnp.einsum('bhs,bhsd->bhd', wg, kc)
        return (C_new, n_new, m_new), hc

    state, hs = lax.scan(step, state, xs)
    return jnp.moveaxis(hs, 0, 2).reshape(b, nh, t, -1), state


def trunk_layer(x, cond, p, ctx, rope_a, rope_g):
    b, t, _ = x.shape
    f32 = jnp.float32
    mod = (jax.nn.silu(cond) @ p['w_mod'] + p['b_mod'])[..., None, :]
    sh1, sc1, gt1, sh2, sc2, gt2 = jnp.split(mod, N_MOD, axis=-1)
    h = rms_norm(x, p['norm1_g']) * (1 + sc1) + sh1
    (gate_pre, mq, mk, mv, mo, mg, aq, akv, akr, gq, gk, gv) = jnp.split(h @ p['w_in'], IN_SPLITS, axis=-1)

    def heads_first(a, nh):
        return a.reshape(b, t, nh, -1).transpose(0, 2, 1, 3).astype(f32)

    qm = heads_first(mq, M_HEADS)
    km = heads_first(mk, M_HEADS) * (M_DK ** -0.5)
    vm = heads_first(mv, M_HEADS)
    gp = (mg + p['b_mgate']).astype(f32).reshape(b, t, 4, M_HEADS).transpose(2, 0, 3, 1)
    if ctx is None:
        zero = (jnp.zeros((b, M_HEADS, M_DK, M_DV), f32), jnp.zeros((b, M_HEADS, M_DK), f32),
                jnp.zeros((b, M_HEADS), f32))
        init_f, init_b = zero, zero
    else:
        init_f = (ctx['mC'][:, 0].astype(f32), ctx['mn'][:, 0].astype(f32), ctx['mm'][:, 0].astype(f32))
        init_b = (ctx['mC'][:, 1].astype(f32), ctx['mn'][:, 1].astype(f32), ctx['mm'][:, 1].astype(f32))
    h_f, st_f = mlstm_scan(qm, km, vm, gp[0], gp[1], init_f)

    def rev(a):
        return jnp.flip(a, axis=2)

    h_b, st_b = mlstm_scan(rev(qm), rev(km), rev(vm), rev(gp[2]), rev(gp[3]), init_b)
    hm = (h_f + rev(h_b)).transpose(0, 2, 1, 3).astype(x.dtype)
    ym = (rms_norm(hm, p['m_norm_g']) * jax.nn.sigmoid(mo.reshape(b, t, M_HEADS, M_DV))).reshape(b, t, M_WIDTH)

    def mla_kv(ckv_, kr_):
        s = ckv_.shape[1]
        kv = (ckv_ @ p['w_ukv']).reshape(b, s, A_HEADS, A_NOPE + A_V)
        k_rope = jnp.broadcast_to(kr_[:, :, None, :], (b, s, A_HEADS, A_ROPE))
        k_ = rms_norm(jnp.concatenate([kv[..., :A_NOPE], k_rope], axis=-1), p['a_knorm_g'])
        return k_, kv[..., A_NOPE:]

    ckv = rms_norm(akv, p['a_kvlora_g'])
    qa = (rms_norm(aq, p['a_qlora_g']) @ p['w_uq']).reshape(b, t, A_HEADS, A_QK)
    qa = rms_norm(qa, p['a_qnorm_g'])
    ka, va = mla_kv(ckv, akr)
    if ctx is not None:
        cos_a, sin_a = rope_a
        qa = jnp.concatenate([qa[..., :A_NOPE], apply_rope(qa[..., A_NOPE:], cos_a, sin_a)], axis=-1)
        ka = jnp.concatenate([ka[..., :A_NOPE], apply_rope(ka[..., A_NOPE:], cos_a, sin_a)], axis=-1)
        kc, vc = mla_kv(ctx['ckv'], ctx['kr'])
        ka = jnp.concatenate([kc, ka], axis=1)
        va = jnp.concatenate([vc, va], axis=1)
    ya = block_attention(qa[:, :, :, None, :], ka, va).reshape(b, t, A_WIDTH)

    qg = rms_norm(gq.reshape(b, t, G_HEADS, G_HD), p['g_qnorm_g'])
    kg = rms_norm(gk.reshape(b, t, G_KV_HEADS, G_HD), p['g_knorm_g'])
    vg = gv.reshape(b, t, G_KV_HEADS, G_HD)
    if ctx is None:
        k_all, v_all = kg, vg
    else:
        cos_g, sin_g = rope_g
        qg = apply_rope(qg, cos_g, sin_g)
        k_all = jnp.concatenate([ctx['gk'], apply_rope(kg, cos_g, sin_g)], axis=1)
        v_all = jnp.concatenate([ctx['gv'], vg], axis=1)
    yg = block_attention(qg.reshape(b, t, G_KV_HEADS, G_GROUP, G_HD), k_all, v_all).reshape(b, t, G_WIDTH)

    ys = jnp.stack([ym, ya, yg], axis=2)
    branches = jnp.einsum('btnw,nwd->btnd', ys, p['w_branch'])
    gates = jax.nn.sigmoid(gate_pre.reshape(b, t, N_BRANCH, D_MODEL))
    mixed = jnp.einsum('btnd,btnd->btd', gates, branches) @ p['w_out']
    x = x + gt1 * mixed

    h2 = rms_norm(x, p['norm2_g']) * (1 + sc2) + sh2
    u_gate, u_val = jnp.split(h2 @ p['w_ffn_in'], 2, axis=-1)
    x = x + gt2 * ((jax.nn.silu(u_gate) * u_val) @ p['w_ffn_out'])

    if ctx is None:
        new_ctx = dict(mC=jnp.stack([st_f[0], st_b[0]], axis=1), mn=jnp.stack([st_f[1], st_b[1]], axis=1),
                       mm=jnp.stack([st_f[2], st_b[2]], axis=1), ckv=ckv, kr=akr, gk=kg, gv=vg)
    else:
        new_ctx = None
    return x, new_ctx


def setup_inputs(seed: int = 0) -> dict:
    key = jax.random.key(seed)
    keys = iter(jax.random.split(key, 40))

    def nrm(shape, scale):
        return jax.random.normal(next(keys), shape, jnp.float32) * scale

    def gain(shape):
        return 1.0 + nrm(shape, 0.02)

    gate_base = jnp.repeat(jnp.array([0.0, 3.0, 0.0, 3.0], jnp.float32), M_HEADS)
    return {
        'x_prompt': nrm((BATCH, SEQ, D_MODEL), 1.0),
        'x_sample': nrm((DEC_BATCH, DEC_SEQ, D_MODEL), 1.0),
        'state_mlstm_C': nrm((DEC_BATCH, DEPTH, 2, M_HEADS, M_DK, M_DV), 0.1),
        'state_mlstm_n': nrm((DEC_BATCH, DEPTH, 2, M_HEADS, M_DK), 0.1),
        'state_mlstm_m': nrm((DEC_BATCH, DEPTH, 2, M_HEADS), 1.0),
        'cache_mla_ckv': nrm((DEC_BATCH, DEPTH, PAST_LEN, A_KV_LORA), 1.0),
        'cache_mla_krope': nrm((DEC_BATCH, DEPTH, PAST_LEN, A_ROPE), 1.0),
        'cache_gqa_k': nrm((DEC_BATCH, DEPTH, PAST_LEN, G_KV_HEADS, G_HD), 1.0),
        'cache_gqa_v': nrm((DEC_BATCH, DEPTH, PAST_LEN, G_KV_HEADS, G_HD), 1.0),
        'c': nrm((DEC_BATCH, D_MODEL), 1.0),
        'c_ctx': nrm((D_MODEL,), 1.0),
        'w_mod': nrm((DEPTH, D_MODEL, N_MOD * D_MODEL), 0.5 * D_MODEL ** -0.5),
        'b_mod': nrm((DEPTH, N_MOD * D_MODEL), 0.02),
        'norm1_g': gain((DEPTH, D_MODEL)),
        'w_in': nrm((DEPTH, D_MODEL, D_IN), D_MODEL ** -0.5),
        'b_mgate': gate_base + nrm((DEPTH, 4 * M_HEADS), 0.1),
        'm_norm_g': gain((DEPTH, M_DV)),
        'a_qlora_g': gain((DEPTH, A_Q_LORA)),
        'a_kvlora_g': gain((DEPTH, A_KV_LORA)),
        'w_uq': nrm((DEPTH, A_Q_LORA, A_HEADS * A_QK), A_Q_LORA ** -0.5),
        'w_ukv': nrm((DEPTH, A_KV_LORA, A_HEADS * (A_NOPE + A_V)), A_KV_LORA ** -0.5),
        'a_qnorm_g': gain((DEPTH, A_QK)),
        'a_knorm_g': gain((DEPTH, A_QK)),
        'g_qnorm_g': gain((DEPTH, G_HD)),
        'g_knorm_g': gain((DEPTH, G_HD)),
        'w_branch': nrm((DEPTH, N_BRANCH, BRANCH_WIDTH, D_MODEL), BRANCH_WIDTH ** -0.5),
        'w_out': nrm((DEPTH, D_MODEL, D_MODEL), D_MODEL ** -0.5),
        'norm2_g': gain((DEPTH, D_MODEL)),
        'w_ffn_in': nrm((DEPTH, D_MODEL, 2 * D_FF), D_MODEL ** -0.5),
        'w_ffn_out': nrm((DEPTH, D_FF, D_MODEL), D_FF ** -0.5),
    }


def reference(x_prompt, x_sample, state_mlstm_C, state_mlstm_n, state_mlstm_m, cache_mla_ckv, cache_mla_krope,
              cache_gqa_k, cache_gqa_v, c, c_ctx, w_mod, b_mod, norm1_g, w_in, b_mgate, m_norm_g, a_qlora_g,
              a_kvlora_g, w_uq, w_ukv, a_qnorm_g, a_knorm_g, g_qnorm_g, g_knorm_g, w_branch, w_out, norm2_g,
              w_ffn_in, w_ffn_out):
    def params(l):
        return dict(w_mod=w_mod[l], b_mod=b_mod[l], norm1_g=norm1_g[l], w_in=w_in[l], b_mgate=b_mgate[l],
                    m_norm_g=m_norm_g[l], a_qlora_g=a_qlora_g[l], a_kvlora_g=a_kvlora_g[l], w_uq=w_uq[l],
                    w_ukv=w_ukv[l], a_qnorm_g=a_qnorm_g[l], a_knorm_g=a_knorm_g[l], g_qnorm_g=g_qnorm_g[l],
                    g_knorm_g=g_knorm_g[l], w_branch=w_branch[l], w_out=w_out[l], norm2_g=norm2_g[l],
                    w_ffn_in=w_ffn_in[l], w_ffn_out=w_ffn_out[l])

    xp = x_prompt
    ctx_layers = []
    for l in range(DEPTH):
        xp, st = trunk_layer(xp, c_ctx, params(l), None, None, None)
        ctx_layers.append(st)

    def stack(name):
        return jnp.stack([s[name] for s in ctx_layers], axis=1).astype(x_prompt.dtype)

    new_mlstm_C = stack('mC')
    new_mlstm_n = stack('mn')
    new_mlstm_m = stack('mm')
    new_mla_ckv = stack('ckv')
    new_mla_krope = stack('kr')
    new_gqa_k = stack('gk')
    new_gqa_v = stack('gv')

    rows = x_sample.shape[1] // GRID_W
    rope_a = axial_rope(rows, A_ROPE)
    rope_g = axial_rope(rows, G_HD)
    xs = x_sample
    for l in range(DEPTH):
        ctx = dict(mC=state_mlstm_C[:, l], mn=state_mlstm_n[:, l], mm=state_mlstm_m[:, l],
                   ckv=cache_mla_ckv[:, l], kr=cache_mla_krope[:, l], gk=cache_gqa_k[:, l], gv=cache_gqa_v[:, l])
        xs, _ = trunk_layer(xs, c, params(l), ctx, rope_a, rope_g)

    return (xp, xs, new_mlstm_C, new_mlstm_n, new_mlstm_m, new_mla_ckv, new_mla_krope, new_gqa_k, new_gqa_v)
```

```python
import functools

import numpy as np
import jax
import jax.numpy as jnp
from jax import lax
from jax.experimental import pallas as pl
from jax.experimental.pallas import tpu as pltpu

F32 = jnp.float32
BF16 = jnp.bfloat16

LANES = 128
SUBLANES = 8
VMEM_LIMIT_BYTES = 56 * 1024 * 1024

EPS = 1e-6
ROPE_BASE = 10000.0
GRID_W = 64

M_HEADS = 4
M_DK = 128
M_DV = 128
A_HEADS = 8
A_NOPE = 64
A_ROPE = 32
A_QK = A_NOPE + A_ROPE
A_V = 64
A_Q_LORA = 256
A_KV_LORA = 128
G_HEADS = 8
G_KV_HEADS = 2
G_GROUP = G_HEADS // G_KV_HEADS
G_HD = 64
N_BRANCH = 3
BRANCH_WIDTH = 512
N_MOD = 6

TOKEN_BLOCK = 256
Q_BLOCK = 256
MLSTM_CHUNK = 128


def _params(n_axes):
    return pltpu.CompilerParams(dimension_semantics=("arbitrary",) * n_axes,
                                vmem_limit_bytes=VMEM_LIMIT_BYTES)


def _resident(shape):
    nd = len(shape)
    return pl.BlockSpec(shape, lambda *_: (0,) * nd, pipeline_mode=pl.Buffered(1))


def _lane(shape, axis=None):
    return lax.broadcasted_iota(jnp.int32, shape, len(shape) - 1 if axis is None else axis)


def _dot(a, b):
    return jnp.dot(a, b, preferred_element_type=F32)


def _dot_nt(a, b):
    return lax.dot_general(a, b, (((1,), (1,)), ((), ())), preferred_element_type=F32)


def _split3(a):
    hi = a.astype(BF16)
    r1 = a - hi.astype(F32)
    mid = r1.astype(BF16)
    lo = (r1 - mid.astype(F32)).astype(BF16)
    return hi, mid, lo


def _dot01(a, m01):
    hi, mid, lo = _split3(a)
    return _dot(hi, m01) + _dot(mid, m01) + _dot(lo, m01)


def _dot01_left(m01, a):
    hi, mid, lo = _split3(a)
    return _dot(m01, hi) + _dot(m01, mid) + _dot(m01, lo)


def _sigmoid(x):
    return 1.0 / (1.0 + jnp.exp(-x))


def _log_sigmoid(x):
    return jnp.minimum(x, 0.0) - jnp.log(1.0 + jnp.exp(-jnp.abs(x)))


def _rms(x, width):
    ms = jnp.sum(x * x, axis=-1, keepdims=True) * (1.0 / width)
    return x * lax.rsqrt(ms + EPS)


def _rope(x, cos, sin_signed, half, period):
    n = x.shape[-1]
    first = (_lane(x.shape) % period) < half
    swapped = jnp.where(first, pltpu.roll(x, n - half, x.ndim - 1), pltpu.roll(x, half, x.ndim - 1))
    return x * cos + swapped * sin_signed


def _mod_kernel(c_ref, w_ref, b_ref, o_ref):
    c = c_ref[...]
    a = c * _sigmoid(c)
    hi, mid, lo = _split3(a)
    w = w_ref[...]
    whi = w.astype(BF16)
    wlo = (w - whi.astype(F32)).astype(BF16)
    acc = _dot(hi, whi) + _dot(mid, whi) + _dot(hi, wlo) + _dot(lo, whi) + _dot(mid, wlo)
    o_ref[...] = acc + b_ref[...]


def _mod_call(cond, w_mod, b_mod):
    depth, d, n = w_mod.shape
    rows = cond.shape[0]
    tn = 1536
    return pl.pallas_call(
        _mod_kernel,
        out_shape=jax.ShapeDtypeStruct((depth, rows, n), F32),
        grid=(depth, n // tn),
        in_specs=[
            pl.BlockSpec((rows, d), lambda l, j: (0, 0)),
            pl.BlockSpec((None, d, tn), lambda l, j: (l, 0, j)),
            pl.BlockSpec((None, 1, tn), lambda l, j: (l, 0, j)),
        ],
        out_specs=pl.BlockSpec((None, rows, tn), lambda l, j: (l, 0, j)),
        compiler_params=_params(2),
        name="adaln_mod",
    )(cond, w_mod, b_mod.reshape(depth, 1, n))


_GATE0, _GATE1 = 0, 3072
_MQ0 = 3072
_MK0 = 3584
_MV0 = 4096
_MO0 = 4608
_MG0 = 5120
_AQ0 = 5248
_AKV0 = 5504
_AKR0 = 5632
_GQ0 = 5760
_GK0 = 6272
_GV0 = 6400
_WIN_COLS = 6528


def _inproj_kernel(x_ref, mod_ref, g1_ref, w_ref, bmg_ref, gql_ref, wuq_ref, gkvl_ref, gkn_ref,
                   gates_ref, qkv_ref, og_ref, mg_ref, qa_ref, ckv_ref, akr_ref, gq_ref, gk_ref, gv_ref):
    x = x_ref[...]
    d = x.shape[-1]
    sh1 = mod_ref[0:1, :]
    sc1 = mod_ref[1:2, :]
    h = (_rms(x, d) * g1_ref[...]) * (1.0 + sc1) + sh1
    hb = h.astype(BF16)

    def proj(c0, width):
        return _dot(hb, w_ref[:, c0:c0 + width])

    gates_ref[...] = _sigmoid(proj(_GATE0, _GATE1 - _GATE0))
    hw = M_HEADS * M_DK
    qkv_ref[:, 0:hw] = proj(_MQ0, hw).astype(qkv_ref.dtype)
    qkv_ref[:, hw:2 * hw] = (proj(_MK0, hw) * (M_DK ** -0.5)).astype(qkv_ref.dtype)
    qkv_ref[:, 2 * hw:3 * hw] = proj(_MV0, hw).astype(qkv_ref.dtype)
    og_ref[...] = _sigmoid(proj(_MO0, hw))
    mg_ref[...] = proj(_MG0, LANES) + bmg_ref[...]

    aq = _rms(proj(_AQ0, A_Q_LORA), A_Q_LORA) * gql_ref[...]
    qa_ref[...] = _dot(aq.astype(BF16), wuq_ref[...])
    ckv_ref[...] = _rms(proj(_AKV0, A_KV_LORA), A_KV_LORA) * gkvl_ref[...]
    akr_ref[...] = proj(_AKR0, LANES)

    gq_ref[...] = proj(_GQ0, G_HEADS * G_HD)
    gk = proj(_GK0, G_KV_HEADS * G_HD)
    low = _lane(gk.shape) < G_HD
    sq = gk * gk
    ms_lo = jnp.sum(jnp.where(low, sq, 0.0), axis=-1, keepdims=True) * (1.0 / G_HD)
    ms_hi = jnp.sum(jnp.where(low, 0.0, sq), axis=-1, keepdims=True) * (1.0 / G_HD)
    gk_ref[...] = gk * lax.rsqrt(jnp.where(low, ms_lo, ms_hi) + EPS) * gkn_ref[...]
    gv_ref[...] = proj(_GV0, G_KV_HEADS * G_HD)


def _inproj_call(x, mod, lw, mod_row0, rows_per_mod):
    n, d = x.shape
    tm = TOKEN_BLOCK
    hw = M_HEADS * M_DK

    def mod_idx(i):
        return (mod_row0 + (i * tm) // rows_per_mod, 0, 0)

    def rows(width):
        return pl.BlockSpec((tm, width), lambda i: (i, 0))

    out_widths = [3 * d, 3 * hw, hw, LANES, A_HEADS * LANES, A_KV_LORA, LANES, G_HEADS * G_HD,
                  G_KV_HEADS * G_HD, G_KV_HEADS * G_HD]
    out_dtypes = [F32, BF16, F32, F32, F32, F32, F32, F32, F32, F32]
    return pl.pallas_call(
        _inproj_kernel,
        out_shape=[jax.ShapeDtypeStruct((n, w), t) for w, t in zip(out_widths, out_dtypes)],
        grid=(n // tm,),
        in_specs=[
            rows(d),
            pl.BlockSpec((None, N_MOD, d), mod_idx),
            _resident((1, d)),
            _resident((d, _WIN_COLS)),
            _resident((1, LANES)),
            _resident((1, A_Q_LORA)),
            _resident((A_Q_LORA, A_HEADS * LANES)),
            _resident((1, A_KV_LORA)),
            _resident((1, G_KV_HEADS * G_HD)),
        ],
        out_specs=[rows(w) for w in out_widths],
        compiler_params=_params(1),
        name="in_proj",
    )(x, mod, lw["norm1_g"], lw["w_in"], lw["b_mgate"], lw["a_qlora_g"], lw["w_uq"], lw["a_kvlora_g"],
      lw["g_knorm_g"])


def _mlstm_kernel(*refs, n_chunks, has_init, emit_state):
    q_ref, k_ref, v_ref, og_ref, gcol_ref, grow_ref, gn_ref = refs[:7]
    pos = 7
    if has_init:
        c0_ref, n0_ref, m0_ref = refs[pos:pos + 3]
        pos += 3
    y_ref = refs[pos]
    pos += 1
    if emit_state:
        cf_ref, nf_ref, mf_ref = refs[pos:pos + 3]
        pos += 3
    hf_ref, hb_ref, c_scr, n_scr, m_scr = refs[pos:]

    ln = MLSTM_CHUNK
    row = lax.broadcasted_iota(jnp.int32, (ln, ln), 0)
    col = lax.broadcasted_iota(jnp.int32, (ln, ln), 1)
    lower = col <= row
    upper = col >= row
    tril = jnp.where(lower, 1.0, 0.0).astype(BF16)
    triu = jnp.where(upper, 1.0, 0.0).astype(BF16)

    if has_init:
        c_scr[...] = c0_ref[...]
        n_scr[...] = n0_ref[...]
        m_scr[...] = m0_ref[...]
    else:
        c_scr[...] = jnp.zeros_like(c_scr)
        n_scr[...] = jnp.zeros_like(n_scr)
        m_scr[...] = jnp.zeros_like(m_scr)

    def chunk(c, direction):
        fwd = direction == 0
        ii, ff = (0, 1) if fwd else (2, 3)
        r0 = pl.multiple_of(c * ln, ln)
        q = q_ref[pl.ds(r0, ln), :]
        k = k_ref[pl.ds(r0, ln), :]
        v = v_ref[pl.ds(r0, ln), :]
        gc = gcol_ref[c]
        gr = grow_ref[c]
        i_colb = jnp.broadcast_to(gc[:, ii:ii + 1], (ln, ln))
        lf_colb = jnp.broadcast_to(_log_sigmoid(gc[:, ff:ff + 1]), (ln, ln))
        lf_rows = _log_sigmoid(gr)
        i_row = gr[ii:ii + 1, :]
        lf_row = lf_rows[ff:ff + 1, :]

        pref_colb = _dot01_left(tril, lf_colb)
        tot = pref_colb[ln - 1:ln, :]
        pref_row = _dot01(lf_rows, triu)[ff:ff + 1, :]
        if fwd:
            b_colb, b_row, mask = pref_colb, pref_row, lower
        else:
            b_colb = tot - pref_colb + lf_colb
            b_row = tot - pref_row + lf_row
            mask = upper

        cst = c_scr[direction]
        nst = n_scr[direction]
        mst = m_scr[direction]

        d_log = jnp.where(mask, b_colb - b_row + i_row, -jnp.inf)
        inter = b_colb + mst
        m_t = jnp.maximum(inter, jnp.max(d_log, axis=-1, keepdims=True))
        w_inter = jnp.exp(inter - m_t)
        s = _dot_nt(q, k) * jnp.exp(d_log - m_t)
        num = _dot(q, cst.astype(BF16)) * w_inter + _dot(s.astype(BF16), v)
        qn = jnp.sum(q.astype(F32) * nst, axis=-1, keepdims=True)
        den = qn * w_inter + jnp.sum(s, axis=-1, keepdims=True)
        hc = num / jnp.maximum(jnp.abs(den), jnp.exp(-m_t))

        g = tot - b_colb + i_colb
        total = tot + mst
        m_new = jnp.maximum(total, jnp.max(g, axis=0, keepdims=True))
        wg = jnp.exp(g - m_new)
        decay = jnp.exp(total - m_new)
        kw = k.astype(F32) * wg
        upd = _dot(kw.T.astype(BF16), v)
        c_scr[direction] = cst * decay + upd
        n_scr[direction] = nst * decay + jnp.sum(kw, axis=0, keepdims=True)
        m_scr[direction] = m_new
        return r0, hc

    def step(j, carry):
        r0, hc = chunk(j, 0)
        hf_ref[pl.ds(r0, ln), :] = hc
        r0, hc = chunk(n_chunks - 1 - j, 1)
        hb_ref[pl.ds(r0, ln), :] = hc
        return carry

    lax.fori_loop(0, n_chunks, step, 0)

    hm = hf_ref[...] + hb_ref[...]
    y_ref[...] = _rms(hm, M_DV) * gn_ref[...] * og_ref[...]
    if emit_state:
        cf_ref[...] = c_scr[...]
        nf_ref[...] = n_scr[...]
        mf_ref[...] = m_scr[...]


def _mlstm_call(qkv, og, mg, lw, batch, seq, init, emit_state):
    n = batch * seq
    ln = MLSTM_CHUNK
    nc = seq // ln
    h = M_HEADS
    g4 = mg[:, :4 * h].reshape(batch, nc, ln, 4, h)
    gcol = g4.transpose(0, 4, 1, 2, 3)
    grow = jnp.pad(g4.transpose(0, 4, 1, 3, 2), ((0, 0),) * 3 + ((0, SUBLANES - 4), (0, 0)))

    def tok(colblock):
        return pl.BlockSpec((seq, LANES), lambda b, hh: (b, colblock(hh)))

    in_specs = [
        tok(lambda hh: hh), tok(lambda hh: h + hh), tok(lambda hh: 2 * h + hh),
        tok(lambda hh: hh),
        pl.BlockSpec((None, None, nc, ln, 4), lambda b, hh: (b, hh, 0, 0, 0)),
        pl.BlockSpec((None, None, nc, SUBLANES, ln), lambda b, hh: (b, hh, 0, 0, 0)),
        pl.BlockSpec((1, M_DV), lambda b, hh: (0, 0)),
    ]
    args = [qkv, qkv, qkv, og, gcol, grow, lw["m_norm_g"]]
    if init is not None:
        c0, n0, m0 = init
        in_specs += [
            pl.BlockSpec((None, 2, None, M_DK, M_DV), lambda b, hh: (b, 0, hh, 0, 0)),
            pl.BlockSpec((None, 2, None, 1, M_DK), lambda b, hh: (b, 0, hh, 0, 0)),
            pl.BlockSpec((None, 2, None, 1, ln), lambda b, hh: (b, 0, hh, 0, 0)),
        ]
        args += [c0, n0.reshape(batch, 2, h, 1, M_DK),
                 jnp.broadcast_to(m0[..., None, None], (batch, 2, h, 1, ln))]
    out_shape = [jax.ShapeDtypeStruct((n, h * M_DV), F32)]
    out_specs = [tok(lambda hh: hh)]
    if emit_state:
        out_shape += [
            jax.ShapeDtypeStruct((batch, 2, h, M_DK, M_DV), F32),
            jax.ShapeDtypeStruct((batch, 2, h, 1, M_DK), F32),
            jax.ShapeDtypeStruct((batch, 2, h, 1, ln), F32),
        ]
        out_specs += [
            pl.BlockSpec((None, 2, None, M_DK, M_DV), lambda b, hh: (b, 0, hh, 0, 0)),
            pl.BlockSpec((None, 2, None, 1, M_DK), lambda b, hh: (b, 0, hh, 0, 0)),
            pl.BlockSpec((None, 2, None, 1, ln), lambda b, hh: (b, 0, hh, 0, 0)),
        ]
    outs = pl.pallas_call(
        functools.partial(_mlstm_kernel, n_chunks=nc, has_init=init is not None, emit_state=emit_state),
        out_shape=out_shape,
        grid=(batch, h),
        in_specs=in_specs,
        out_specs=out_specs,
        scratch_shapes=[
            pltpu.VMEM((seq, M_DV), F32), pltpu.VMEM((seq, M_DV), F32),
            pltpu.VMEM((2, M_DK, M_DV), F32), pltpu.VMEM((2, 1, M_DK), F32), pltpu.VMEM((2, 1, ln), F32),
        ],
        compiler_params=_params(2),
        name="mlstm",
    )(*args)
    if emit_state:
        ym, cf, nf, mf = outs
        return ym, (cf, nf[:, :, :, 0, :], mf[:, :, :, 0, 0])
    return outs[0], None


def _mla_kernel(*refs, past, rotary):
    q_ref, ckv_ref, akr_ref = refs[:3]
    pos = 3
    if past:
        ckvc_ref, krc_ref = refs[pos:pos + 2]
        pos += 2
    w_ref, qg_ref, kg_ref = refs[pos:pos + 3]
    pos += 3
    if rotary:
        cq_ref, sq_ref, ck_ref, sk_ref = refs[pos:pos + 4]
        pos += 4
    o_ref, kbuf, vbuf = refs[pos:]
    seq = ckv_ref.shape[0]

    @pl.when(pl.program_id(2) == 0)
    def _():
        for e in range(2):
            w_h = w_ref[:, e * LANES:(e + 1) * LANES]

            def keys(ckv, kr):
                kv = _dot(ckv.astype(BF16), w_h)
                kcat = jnp.where(_lane(kv.shape) >= A_V, kv, kr)
                return kv, _rms(kcat, A_QK) * kg_ref[...]

            kv, kn = keys(ckv_ref[...], akr_ref[...])
            if rotary:
                kn = _rope(kn, ck_ref[...], sk_ref[...], A_ROPE // 2, LANES)
            kbuf[e, past:past + seq, :] = kn.astype(BF16)
            vbuf[e, past:past + seq, :] = kv.astype(BF16)
            if past:
                kvc, knc = keys(ckvc_ref[...], krc_ref[...])
                kbuf[e, 0:past, :] = knc.astype(BF16)
                vbuf[e, 0:past, :] = kvc.astype(BF16)

    outs = []
    for e in range(2):
        qn = _rms(q_ref[:, e * LANES:(e + 1) * LANES], A_QK) * qg_ref[...]
        if rotary:
            qn = _rope(qn, cq_ref[...], sq_ref[...], A_ROPE // 2, LANES)
        qb = (qn * (A_QK ** -0.5)).astype(BF16)
        s = _dot_nt(qb, kbuf[e])
        p = jnp.exp(s - jnp.max(s, axis=-1, keepdims=True))
        l = jnp.sum(p, axis=-1, keepdims=True)
        outs.append(_dot(p.astype(BF16), vbuf[e]) / l)
    o_ref[...] = jnp.where(_lane(outs[0].shape) < A_V, outs[0], pltpu.roll(outs[1], A_V, 1))


def _mla_call(qa, ckv, akr, lw, batch, seq, cache, rope_tabs):
    n = batch * seq
    tq = min(Q_BLOCK, seq)
    nq = seq // tq
    past = 0 if cache is None else cache[0].shape[1]
    in_specs = [
        pl.BlockSpec((tq, 2 * LANES), lambda b, hp, qi: (b * nq + qi, hp)),
        pl.BlockSpec((seq, LANES), lambda b, hp, qi: (b, 0)),
        pl.BlockSpec((seq, LANES), lambda b, hp, qi: (b, 0)),
    ]
    args = [qa, ckv, akr]
    if past:
        in_specs += [pl.BlockSpec((None, past, LANES), lambda b, hp, qi: (b, 0, 0))] * 2
        args += [cache[0], cache[1]]
    in_specs += [
        pl.BlockSpec((A_KV_LORA, 2 * LANES), lambda b, hp, qi: (0, hp)),
        pl.BlockSpec((1, LANES), lambda b, hp, qi: (0, 0)),
        pl.BlockSpec((1, LANES), lambda b, hp, qi: (0, 0)),
    ]
    args += [lw["w_ukv"], lw["a_qnorm_g"], lw["a_knorm_g"]]
    if rope_tabs is not None:
        cos, sin = rope_tabs
        in_specs += [pl.BlockSpec((tq, LANES), lambda b, hp, qi: (qi, 0))] * 2
        in_specs += [pl.BlockSpec((seq, LANES), lambda b, hp, qi: (0, 0))] * 2
        args += [cos, sin, cos, sin]
    return pl.pallas_call(
        functools.partial(_mla_kernel, past=past, rotary=rope_tabs is not None),
        out_shape=jax.ShapeDtypeStruct((n, A_HEADS * A_V), F32),
        grid=(batch, A_HEADS // 2, nq),
        in_specs=in_specs,
        out_specs=pl.BlockSpec((tq, LANES), lambda b, hp, qi: (b * nq + qi, hp)),
        scratch_shapes=[pltpu.VMEM((2, past + seq, LANES), BF16), pltpu.VMEM((2, past + seq, LANES), BF16)],
        compiler_params=_params(3),
        name="mla_attn",
    )(*args)


def _gqa_kernel(*refs, past, rotary):
    q_ref, k_ref, v_ref = refs[:3]
    pos = 3
    if past:
        kc_ref, vc_ref = refs[pos:pos + 2]
        pos += 2
    qg_ref = refs[pos]
    pos += 1
    if rotary:
        cq_ref, sq_ref, ck_ref, sk_ref = refs[pos:pos + 4]
        pos += 4
    o_ref, kbuf, vbuf = refs[pos:]
    seq = k_ref.shape[0]
    grp = pl.program_id(1)
    qw = G_GROUP * G_HD

    @pl.when(pl.program_id(2) == 0)
    def _():
        def both_halves(x):
            swapped = pltpu.roll(x, G_HD, 1)
            low = _lane(x.shape) < G_HD
            return jnp.where(grp == 0, jnp.where(low, x, swapped), jnp.where(low, swapped, x))

        kn = k_ref[...]
        if rotary:
            kn = _rope(kn, ck_ref[...], sk_ref[...], G_HD // 2, G_HD)
        kk = both_halves(kn).astype(BF16)
        vv = both_halves(v_ref[...]).astype(BF16)
        kbuf[past:past + seq, 0:LANES] = kk
        kbuf[past:past + seq, LANES:2 * LANES] = kk
        vbuf[past:past + seq, :] = vv
        if past:
            kk = both_halves(kc_ref[...]).astype(BF16)
            kbuf[0:past, 0:LANES] = kk
            kbuf[0:past, LANES:2 * LANES] = kk
            vbuf[0:past, :] = both_halves(vc_ref[...]).astype(BF16)

    q = q_ref[...]
    r = lax.broadcasted_iota(jnp.int32, (qw, qw), 0) // G_HD
    c = lax.broadcasted_iota(jnp.int32, (qw, qw), 1) // G_HD
    seg = jnp.where(r == c, 1.0, 0.0).astype(BF16)
    ms = _dot01(q * q, seg) * (1.0 / G_HD)
    qn = q * lax.rsqrt(ms + EPS) * qg_ref[...]
    if rotary:
        qn = _rope(qn, cq_ref[...], sq_ref[...], G_HD // 2, G_HD)
    qn = qn * (G_HD ** -0.5)
    head = _lane(qn.shape) // G_HD
    outs = []
    for j in range(G_GROUP):
        qj = jnp.where(head == j, qn, 0.0).astype(BF16)
        s = _dot_nt(qj, kbuf[...])
        p = jnp.exp(s - jnp.max(s, axis=-1, keepdims=True))
        l = jnp.sum(p, axis=-1, keepdims=True)
        outs.append(_dot(p.astype(BF16), vbuf[...]) / l)
    low = _lane(outs[0].shape) < G_HD
    o_ref[:, 0:LANES] = jnp.where(low, outs[0], outs[1])
    o_ref[:, LANES:2 * LANES] = jnp.where(low, outs[2], outs[3])


def _gqa_call(gq, gk, gv, lw, batch, seq, cache, rope_tabs):
    n = batch * seq
    tq = min(Q_BLOCK, seq)
    nq = seq // tq
    past = 0 if cache is None else cache[0].shape[1]
    qw = G_GROUP * G_HD
    kvw = G_KV_HEADS * G_HD
    in_specs = [
        pl.BlockSpec((tq, qw), lambda b, g, qi: (b * nq + qi, g)),
        pl.BlockSpec((seq, kvw), lambda b, g, qi: (b, 0)),
        pl.BlockSpec((seq, kvw), lambda b, g, qi: (b, 0)),
    ]
    args = [gq, gk, gv]
    if past:
        in_specs += [pl.BlockSpec((None, past, kvw), lambda b, g, qi: (b, 0, 0))] * 2
        args += [cache[0], cache[1]]
    in_specs += [pl.BlockSpec((1, qw), lambda b, g, qi: (0, 0))]
    args += [lw["g_qnorm_g"]]
    if rope_tabs is not None:
        cos_q, sin_q, cos_k, sin_k = rope_tabs
        in_specs += [pl.BlockSpec((tq, qw), lambda b, g, qi: (qi, 0))] * 2
        in_specs += [pl.BlockSpec((seq, kvw), lambda b, g, qi: (0, 0))] * 2
        args += [cos_q, sin_q, cos_k, sin_k]
    return pl.pallas_call(
        functools.partial(_gqa_kernel, past=past, rotary=rope_tabs is not None),
        out_shape=jax.ShapeDtypeStruct((n, G_HEADS * G_HD), F32),
        grid=(batch, G_KV_HEADS, nq),
        in_specs=in_specs,
        out_specs=pl.BlockSpec((tq, qw), lambda b, g, qi: (b * nq + qi, g)),
        scratch_shapes=[pltpu.VMEM((past + seq, qw), BF16), pltpu.VMEM((past + seq, kvw), BF16)],
        compiler_params=_params(3),
        name="gqa_attn",
    )(*args)


def _merge_ffn_kernel(x_ref, ym_ref, ya_ref, yg_ref, gates_ref, mod_ref, g2_ref, wb_ref, wo_ref, wfi_ref,
                      wfo_ref, o_ref, *, ff_chunk):
    x = x_ref[...]
    d = x.shape[-1]
    mixed = None
    for i, y_ref in enumerate((ym_ref, ya_ref, yg_ref)):
        br = _dot(y_ref[...].astype(BF16), wb_ref[i]) * gates_ref[:, i * d:(i + 1) * d]
        mixed = br if mixed is None else mixed + br
    gt1 = mod_ref[2:3, :]
    x1 = x + gt1 * _dot(mixed.astype(BF16), wo_ref[...])

    sh2 = mod_ref[3:4, :]
    sc2 = mod_ref[4:5, :]
    gt2 = mod_ref[5:6, :]
    h2 = ((_rms(x1, d) * g2_ref[...]) * (1.0 + sc2) + sh2).astype(BF16)
    d_ff = wfo_ref.shape[0]
    acc = None
    for c0 in range(0, d_ff, ff_chunk):
        ug = _dot(h2, wfi_ref[:, c0:c0 + ff_chunk])
        uv = _dot(h2, wfi_ref[:, d_ff + c0:d_ff + c0 + ff_chunk])
        act = (ug * _sigmoid(ug) * uv).astype(BF16)
        part = _dot(act, wfo_ref[c0:c0 + ff_chunk, :])
        acc = part if acc is None else acc + part
    o_ref[...] = x1 + gt2 * acc


def _merge_ffn_call(x, ym, ya, yg, gates, mod, lw, mod_row0, rows_per_mod):
    n, d = x.shape
    tm = TOKEN_BLOCK
    d_ff = lw["w_ffn_out"].shape[0]
    ff_chunk = d_ff // 2 if (d_ff // 2) % LANES == 0 else d_ff

    def mod_idx(i):
        return (mod_row0 + (i * tm) // rows_per_mod, 0, 0)

    def rows(width):
        return pl.BlockSpec((tm, width), lambda i: (i, 0))

    return pl.pallas_call(
        functools.partial(_merge_ffn_kernel, ff_chunk=ff_chunk),
        out_shape=jax.ShapeDtypeStruct((n, d), F32),
        grid=(n // tm,),
        in_specs=[
            rows(d), rows(BRANCH_WIDTH), rows(BRANCH_WIDTH), rows(BRANCH_WIDTH), rows(N_BRANCH * d),
            pl.BlockSpec((None, N_MOD, d), mod_idx),
            _resident((1, d)),
            _resident((N_BRANCH, BRANCH_WIDTH, d)),
            _resident((d, d)),
            _resident((d, 2 * d_ff)),
            _resident((d_ff, d)),
        ],
        out_specs=rows(d),
        compiler_params=_params(1),
        name="merge_ffn",
    )(x, ym, ya, yg, gates, mod, lw["norm2_g"], lw["w_branch"], lw["w_out"], lw["w_ffn_in"], lw["w_ffn_out"])


def _pad_cols(w, width):
    return jnp.pad(w, ((0, 0), (0, width - w.shape[1])))


def _mla_q_layout(a):
    lead = a.shape[:-1]
    a = a.reshape(lead + (A_HEADS, A_QK))
    z = jnp.zeros(lead + (A_HEADS, LANES - A_QK), a.dtype)
    return jnp.concatenate([a[..., A_NOPE:], z, a[..., :A_NOPE]], axis=-1).reshape(lead + (A_HEADS * LANES,))


def _layer_weights(l, w_in, b_mgate, norm1_g, m_norm_g, a_qlora_g, a_kvlora_g, w_uq, w_ukv, a_qnorm_g,
                   a_knorm_g, g_qnorm_g, g_knorm_g, w_branch, w_out, norm2_g, w_ffn_in, w_ffn_out):
    d = w_in.shape[1]
    hw = M_HEADS * M_DK
    sizes = (N_BRANCH * d, hw, hw, hw, hw, 4 * M_HEADS, A_Q_LORA, A_KV_LORA, A_ROPE, G_HEADS * G_HD,
             G_KV_HEADS * G_HD, G_KV_HEADS * G_HD)
    splits = np.cumsum(sizes)[:-1].tolist()
    (wg, wmq, wmk, wmv, wmo, wmg, waq, wakv, wakr, wgq, wgk, wgv) = jnp.split(w_in[l], splits, axis=1)
    w_in_p = jnp.concatenate(
        [wg, wmq, wmk, wmv, wmo, _pad_cols(wmg, LANES), waq, wakv, _pad_cols(wakr, LANES), wgq, wgk, wgv],
        axis=1).astype(BF16)
    ukv = w_ukv[l].reshape(A_KV_LORA, A_HEADS, A_NOPE + A_V)
    ukv = jnp.concatenate([ukv[..., A_NOPE:], ukv[..., :A_NOPE]], axis=-1).reshape(A_KV_LORA, A_HEADS * LANES)
    qg = _mla_q_layout(jnp.tile(a_qnorm_g[l], A_HEADS)[None, :])[:, :LANES]
    kg = _mla_q_layout(jnp.tile(a_knorm_g[l], A_HEADS)[None, :])[:, :LANES]
    return dict(
        w_in=w_in_p,
        b_mgate=_pad_cols(b_mgate[l][None, :], LANES),
        norm1_g=norm1_g[l][None, :],
        m_norm_g=m_norm_g[l][None, :],
        a_qlora_g=a_qlora_g[l][None, :],
        a_kvlora_g=a_kvlora_g[l][None, :],
        w_uq=_mla_q_layout(w_uq[l]).astype(BF16),
        w_ukv=ukv.astype(BF16),
        a_qnorm_g=qg,
        a_knorm_g=kg,
        g_qnorm_g=jnp.tile(g_qnorm_g[l], G_GROUP)[None, :],
        g_knorm_g=jnp.tile(g_knorm_g[l], G_KV_HEADS)[None, :],
        w_branch=w_branch[l].astype(BF16),
        w_out=w_out[l].astype(BF16),
        norm2_g=norm2_g[l][None, :],
        w_ffn_in=w_ffn_in[l].astype(BF16),
        w_ffn_out=w_ffn_out[l].astype(BF16),
    )


def _axial_angles(seq, rot_dim):
    n_freq = rot_dim // 4
    freqs = ROPE_BASE ** (-jnp.arange(n_freq, dtype=F32) / n_freq)
    t = jnp.arange(seq)
    row = (t // GRID_W).astype(F32)
    col = (t % GRID_W).astype(F32)
    return jnp.concatenate([row[:, None] * freqs, col[:, None] * freqs], axis=-1)


def _rope_tables(seq):
    ang = _axial_angles(seq, A_ROPE)
    one = jnp.ones((seq, LANES - A_ROPE), F32)
    mla_cos = jnp.concatenate([jnp.cos(ang), jnp.cos(ang), one], axis=-1)
    mla_sin = jnp.concatenate([-jnp.sin(ang), jnp.sin(ang), 0.0 * one], axis=-1)
    ang = _axial_angles(seq, G_HD)
    cos = jnp.concatenate([jnp.cos(ang), jnp.cos(ang)], axis=-1)
    sin = jnp.concatenate([-jnp.sin(ang), jnp.sin(ang)], axis=-1)
    gqa = (jnp.tile(cos, (1, G_GROUP)), jnp.tile(sin, (1, G_GROUP)),
           jnp.tile(cos, (1, G_KV_HEADS)), jnp.tile(sin, (1, G_KV_HEADS)))
    return (mla_cos, mla_sin), gqa


def _layer(x, mod, lw, batch, seq, mod_row0, rows_per_mod, ctx, rope):
    (gates, qkv, og, mg, qa, ckv, akr, gq, gk, gv) = _inproj_call(x, mod, lw, mod_row0, rows_per_mod)
    if ctx is None:
        ym, state = _mlstm_call(qkv, og, mg, lw, batch, seq, None, True)
        ya = _mla_call(qa, ckv, akr, lw, batch, seq, None, None)
        yg = _gqa_call(gq, gk, gv, lw, batch, seq, None, None)
        new_ctx = dict(state=state, ckv=ckv, kr=akr[:, :A_ROPE], gk=gk, gv=gv)
    else:
        ym, _ = _mlstm_call(qkv, og, mg, lw, batch, seq, ctx["mlstm"], False)
        ya = _mla_call(qa, ckv, akr, lw, batch, seq, ctx["mla"], rope[0])
        yg = _gqa_call(gq, gk, gv, lw, batch, seq, ctx["gqa"], rope[1])
        new_ctx = None
    x = _merge_ffn_call(x, ym, ya, yg, gates, mod, lw, mod_row0, rows_per_mod)
    return x, new_ctx


def kernel(x_prompt, x_sample, state_mlstm_C, state_mlstm_n, state_mlstm_m, cache_mla_ckv, cache_mla_krope,
           cache_gqa_k, cache_gqa_v, c, c_ctx, w_mod, b_mod, norm1_g, w_in, b_mgate, m_norm_g, a_qlora_g,
           a_kvlora_g, w_uq, w_ukv, a_qnorm_g, a_knorm_g, g_qnorm_g, g_knorm_g, w_branch, w_out, norm2_g,
           w_ffn_in, w_ffn_out):
    batch, seq, d = x_prompt.shape
    dbatch, dseq, _ = x_sample.shape
    depth = w_in.shape[0]
    past = cache_mla_ckv.shape[2]

    n_rows = -(-(1 + dbatch) // SUBLANES) * SUBLANES
    cond = jnp.concatenate([c_ctx[None, :], c, jnp.zeros((n_rows - 1 - dbatch, d), F32)], axis=0)
    mod_all = _mod_call(cond, w_mod, b_mod).reshape(depth, n_rows, N_MOD, d)

    rope = _rope_tables(dseq)
    xp = x_prompt.reshape(batch * seq, d)
    xs = x_sample.reshape(dbatch * dseq, d)
    ctx_layers = []
    for l in range(depth):
        lw = _layer_weights(l, w_in, b_mgate, norm1_g, m_norm_g, a_qlora_g, a_kvlora_g, w_uq, w_ukv,
                            a_qnorm_g, a_knorm_g, g_qnorm_g, g_knorm_g, w_branch, w_out, norm2_g, w_ffn_in,
                            w_ffn_out)
        xp, st = _layer(xp, mod_all[l], lw, batch, seq, 0, batch * seq, None, None)
        ctx_layers.append(st)
        ctx = dict(
            mlstm=(state_mlstm_C[:, l], state_mlstm_n[:, l], state_mlstm_m[:, l]),
            mla=(cache_mla_ckv[:, l], _pad_cols(cache_mla_krope[:, l].reshape(dbatch * past, A_ROPE), LANES)
                 .reshape(dbatch, past, LANES)),
            gqa=(cache_gqa_k[:, l].reshape(dbatch, past, G_KV_HEADS * G_HD),
                 cache_gqa_v[:, l].reshape(dbatch, past, G_KV_HEADS * G_HD)),
        )
        xs, _ = _layer(xs, mod_all[l], lw, dbatch, dseq, 1, dseq, ctx, rope)

    def stack(fn):
        return jnp.stack([fn(s) for s in ctx_layers], axis=1)

    new_c = stack(lambda s: s["state"][0])
    new_n = stack(lambda s: s["state"][1])
    new_m = stack(lambda s: s["state"][2])
    new_ckv = stack(lambda s: s["ckv"].reshape(batch, seq, A_KV_LORA))
    new_kr = stack(lambda s: s["kr"].reshape(batch, seq, A_ROPE))
    new_gk = stack(lambda s: s["gk"].reshape(batch, seq, G_KV_HEADS, G_HD))
    new_gv = stack(lambda s: s["gv"].reshape(batch, seq, G_KV_HEADS, G_HD))
    return (xp.reshape(batch, seq, d), xs.reshape(dbatch, dseq, d), new_c, new_n, new_m, new_ckv, new_kr,
            new_gk, new_gv)
```

```python
import functools

import numpy as np
import jax
import jax.numpy as jnp
from jax import lax
from jax.experimental import pallas as pl
from jax.experimental.pallas import tpu as pltpu

F32 = jnp.float32
BF16 = jnp.bfloat16

LANES = 128
SUBLANES = 8
VMEM_LIMIT_BYTES = 56 * 1024 * 1024

EPS = 1e-6
ROPE_BASE = 10000.0
GRID_W = 64

M_HEADS = 4
M_DK = 128
M_DV = 128
A_HEADS = 8
A_NOPE = 64
A_ROPE = 32
A_QK = A_NOPE + A_ROPE
A_V = 64
A_Q_LORA = 256
A_KV_LORA = 128
G_HEADS = 8
G_KV_HEADS = 2
G_GROUP = G_HEADS // G_KV_HEADS
G_HD = 64
N_BRANCH = 3
BRANCH_WIDTH = 512
N_MOD = 6

TOKEN_BLOCK = 256
Q_BLOCK = 256
KEY_BLOCK = 256
MLA_HEADS_PER_STEP = 4
GQA_GROUPS_PER_STEP = 1
MLSTM_CHUNK = 128
LOG2E = 1.4426950408889634


def _params(n_axes):
    return pltpu.CompilerParams(dimension_semantics=("arbitrary",) * n_axes,
                                vmem_limit_bytes=VMEM_LIMIT_BYTES)


def _resident(shape):
    nd = len(shape)
    return pl.BlockSpec(shape, lambda *_: (0,) * nd, pipeline_mode=pl.Buffered(1))


def _lane(shape, axis=None):
    return lax.broadcasted_iota(jnp.int32, shape, len(shape) - 1 if axis is None else axis)


def _dot(a, b):
    return jnp.dot(a, b, preferred_element_type=F32)


def _dot_nt(a, b):
    return lax.dot_general(a, b, (((1,), (1,)), ((), ())), preferred_element_type=F32)


def _split3(a):
    hi = a.astype(BF16)
    r1 = a - hi.astype(F32)
    mid = r1.astype(BF16)
    lo = (r1 - mid.astype(F32)).astype(BF16)
    return hi, mid, lo


def _dot01(a, m01):
    hi, mid, lo = _split3(a)
    return _dot(hi, m01) + _dot(mid, m01) + _dot(lo, m01)


def _dot01_left(m01, a):
    hi, mid, lo = _split3(a)
    return _dot(m01, hi) + _dot(m01, mid) + _dot(m01, lo)


def _sigmoid(x):
    return 1.0 / (1.0 + jnp.exp(-x))


def _log_sigmoid(x):
    return jnp.minimum(x, 0.0) - jnp.log(1.0 + jnp.exp(-jnp.abs(x)))


def _rms(x, width):
    ms = jnp.sum(x * x, axis=-1, keepdims=True) * (1.0 / width)
    return x * lax.rsqrt(ms + EPS)


def _rope(x, cos, sin_signed, half, period):
    n = x.shape[-1]
    first = (_lane(x.shape) % period) < half
    swapped = jnp.where(first, pltpu.roll(x, n - half, x.ndim - 1), pltpu.roll(x, half, x.ndim - 1))
    return x * cos + swapped * sin_signed


def _mod_kernel(c_ref, w_ref, b_ref, o_ref):
    c = c_ref[...]
    a = c * _sigmoid(c)
    hi, mid, lo = _split3(a)
    w = w_ref[...]
    whi = w.astype(BF16)
    wlo = (w - whi.astype(F32)).astype(BF16)
    acc = _dot(hi, whi) + _dot(mid, whi) + _dot(hi, wlo) + _dot(lo, whi) + _dot(mid, wlo)
    o_ref[...] = acc + b_ref[...]


def _mod_call(cond, w_mod, b_mod):
    depth, d, n = w_mod.shape
    rows = cond.shape[0]
    tn = 1536
    return pl.pallas_call(
        _mod_kernel,
        out_shape=jax.ShapeDtypeStruct((depth, rows, n), F32),
        grid=(depth, n // tn),
        in_specs=[
            pl.BlockSpec((rows, d), lambda l, j: (0, 0)),
            pl.BlockSpec((None, d, tn), lambda l, j: (l, 0, j)),
            pl.BlockSpec((None, 1, tn), lambda l, j: (l, 0, j)),
        ],
        out_specs=pl.BlockSpec((None, rows, tn), lambda l, j: (l, 0, j)),
        compiler_params=_params(2),
        name="adaln_mod",
    )(cond, w_mod, b_mod.reshape(depth, 1, n))


_GATE0, _GATE1 = 0, 3072
_MQ0 = 3072
_MK0 = 3584
_MV0 = 4096
_MO0 = 4608
_MG0 = 5120
_AQ0 = 5248
_AKV0 = 5504
_AKR0 = 5632
_GQ0 = 5760
_GK0 = 6272
_GV0 = 6400
_WIN_COLS = 6528


def _head_pair_ms(x):
    low = (_lane(x.shape) % LANES) < G_HD
    sq = x * x
    cols = []
    for c0 in range(0, x.shape[-1], LANES):
        s = sq[:, c0:c0 + LANES]
        lo = jnp.sum(jnp.where(low[:, c0:c0 + LANES], s, 0.0), axis=-1, keepdims=True)
        hi = jnp.sum(jnp.where(low[:, c0:c0 + LANES], 0.0, s), axis=-1, keepdims=True)
        cols.append(jnp.where(low[:, c0:c0 + LANES], lo, hi))
    ms = cols[0] if len(cols) == 1 else jnp.concatenate(cols, axis=-1)
    return ms * (1.0 / G_HD)


def _inproj_kernel(*refs, rotary):
    (x_ref, mod_ref, g1_ref, w_ref, bmg_ref, gql_ref, wuq_ref, gkvl_ref, gkn_ref, aqn_ref, gqn_ref) = refs[:11]
    pos = 11
    if rotary:
        ca_ref, sa_ref, cg_ref, sg_ref = refs[pos:pos + 4]
        pos += 4
    (gates_ref, qkv_ref, og_ref, mg_ref, qa_ref, ckv_ref, akr_ref, gq_ref, gk_ref, gv_ref) = refs[pos:]
    x = x_ref[...]
    d = x.shape[-1]
    sh1 = mod_ref[0:1, :]
    sc1 = mod_ref[1:2, :]
    h = (_rms(x, d) * g1_ref[...]) * (1.0 + sc1) + sh1
    hb = h.astype(BF16)

    def proj(c0, width):
        return _dot(hb, w_ref[:, c0:c0 + width])

    aq = _rms(proj(_AQ0, A_Q_LORA), A_Q_LORA) * gql_ref[...]
    qa = _dot(aq.astype(BF16), wuq_ref[...])
    for hd in range(A_HEADS):
        qh = _rms(qa[:, hd * LANES:(hd + 1) * LANES], A_QK) * aqn_ref[...]
        if rotary:
            qh = _rope(qh, ca_ref[...], sa_ref[...], A_ROPE // 2, LANES)
        qa_ref[:, hd * LANES:(hd + 1) * LANES] = (qh * (A_QK ** -0.5 * LOG2E)).astype(qa_ref.dtype)
    ckv_ref[...] = _rms(proj(_AKV0, A_KV_LORA), A_KV_LORA) * gkvl_ref[...]
    akr_ref[...] = proj(_AKR0, LANES)

    for c0 in range(0, G_HEADS * G_HD, LANES):
        gq = proj(_GQ0 + c0, LANES)
        gq = gq * lax.rsqrt(_head_pair_ms(gq) + EPS) * gqn_ref[...]
        if rotary:
            gq = _rope(gq, cg_ref[...], sg_ref[...], G_HD // 2, G_HD)
        gq_ref[:, c0:c0 + LANES] = (gq * (G_HD ** -0.5 * LOG2E)).astype(gq_ref.dtype)
    gk = proj(_GK0, G_KV_HEADS * G_HD)
    gk = gk * lax.rsqrt(_head_pair_ms(gk) + EPS) * gkn_ref[...]
    if rotary:
        gk = _rope(gk, cg_ref[...], sg_ref[...], G_HD // 2, G_HD)
    gk_ref[...] = gk
    gv_ref[...] = proj(_GV0, G_KV_HEADS * G_HD)

    hw = M_HEADS * M_DK
    qkv_ref[:, 0:hw] = proj(_MQ0, hw).astype(qkv_ref.dtype)
    qkv_ref[:, hw:2 * hw] = (proj(_MK0, hw) * (M_DK ** -0.5)).astype(qkv_ref.dtype)
    qkv_ref[:, 2 * hw:3 * hw] = proj(_MV0, hw).astype(qkv_ref.dtype)
    og_ref[...] = _sigmoid(proj(_MO0, hw))
    mg_ref[...] = proj(_MG0, LANES) + bmg_ref[...]
    gates_ref[...] = _sigmoid(proj(_GATE0, _GATE1 - _GATE0))


def _inproj_call(x, mod, lw, mod_row0, rows_per_mod, rope, seq):
    n, d = x.shape
    tm = TOKEN_BLOCK
    hw = M_HEADS * M_DK

    def mod_idx(i):
        return (mod_row0 + (i * tm) // rows_per_mod, 0, 0)

    def rows(width):
        return pl.BlockSpec((tm, width), lambda i: (i, 0))

    in_specs = [
        rows(d),
        pl.BlockSpec((None, N_MOD, d), mod_idx),
        _resident((1, d)),
        _resident((d, _WIN_COLS)),
        _resident((1, LANES)),
        _resident((1, A_Q_LORA)),
        _resident((A_Q_LORA, A_HEADS * LANES)),
        _resident((1, A_KV_LORA)),
        _resident((1, LANES)),
        _resident((1, LANES)),
        _resident((1, LANES)),
    ]
    args = [x, mod, lw["norm1_g"], lw["w_in"], lw["b_mgate"], lw["a_qlora_g"], lw["w_uq"], lw["a_kvlora_g"],
            lw["g_knorm_g"], lw["a_qnorm_g"], lw["g_qnorm_g"]]
    if rope is not None:
        blocks_per_seq = seq // tm
        in_specs += [pl.BlockSpec((tm, LANES), lambda i: (i % blocks_per_seq, 0))] * 4
        args += [rope[0][0], rope[0][1], rope[1][0], rope[1][1]]
    out_widths = [3 * d, 3 * hw, hw, LANES, A_HEADS * LANES, A_KV_LORA, LANES, G_HEADS * G_HD,
                  G_KV_HEADS * G_HD, G_KV_HEADS * G_HD]
    out_dtypes = [F32, BF16, F32, F32, BF16, F32, F32, BF16, F32, F32]
    return pl.pallas_call(
        functools.partial(_inproj_kernel, rotary=rope is not None),
        out_shape=[jax.ShapeDtypeStruct((n, w), t) for w, t in zip(out_widths, out_dtypes)],
        grid=(n // tm,),
        in_specs=in_specs,
        out_specs=[rows(w) for w in out_widths],
        compiler_params=_params(1),
        name="in_proj",
    )(*args)


def _mlstm_kernel(*refs, n_chunks, has_init, emit_state):
    q_ref, k_ref, v_ref, og_ref, gcol_ref, grow_ref, gn_ref = refs[:7]
    pos = 7
    if has_init:
        c0_ref, n0_ref, m0_ref = refs[pos:pos + 3]
        pos += 3
    y_ref = refs[pos]
    pos += 1
    if emit_state:
        cf_ref, nf_ref, mf_ref = refs[pos:pos + 3]
        pos += 3
    hf_ref, hb_ref, c_scr, n_scr, m_scr = refs[pos:]

    ln = MLSTM_CHUNK
    row = lax.broadcasted_iota(jnp.int32, (ln, ln), 0)
    col = lax.broadcasted_iota(jnp.int32, (ln, ln), 1)
    lower = col <= row
    upper = col >= row
    tril = jnp.where(lower, 1.0, 0.0).astype(BF16)
    triu = jnp.where(upper, 1.0, 0.0).astype(BF16)

    if has_init:
        c_scr[...] = c0_ref[...]
        n_scr[...] = n0_ref[...]
        m_scr[...] = m0_ref[...]
    else:
        c_scr[...] = jnp.zeros_like(c_scr)
        n_scr[...] = jnp.zeros_like(n_scr)
        m_scr[...] = jnp.zeros_like(m_scr)

    def chunk(c, direction):
        fwd = direction == 0
        ii, ff = (0, 1) if fwd else (2, 3)
        r0 = pl.multiple_of(c * ln, ln)
        q = q_ref[pl.ds(r0, ln), :]
        k = k_ref[pl.ds(r0, ln), :]
        v = v_ref[pl.ds(r0, ln), :]
        gc = gcol_ref[c]
        gr = grow_ref[c]
        i_colb = jnp.broadcast_to(gc[:, ii:ii + 1], (ln, ln))
        lf_colb = jnp.broadcast_to(_log_sigmoid(gc[:, ff:ff + 1]), (ln, ln))
        lf_rows = _log_sigmoid(gr)
        i_row = gr[ii:ii + 1, :]
        lf_row = lf_rows[ff:ff + 1, :]

        pref_colb = _dot01_left(tril, lf_colb)
        tot = pref_colb[ln - 1:ln, :]
        pref_row = _dot01(lf_rows, triu)[ff:ff + 1, :]
        if fwd:
            b_colb, b_row, mask = pref_colb, pref_row, lower
        else:
            b_colb = tot - pref_colb + lf_colb
            b_row = tot - pref_row + lf_row
            mask = upper

        cst = c_scr[direction]
        nst = n_scr[direction]
        mst = m_scr[direction]

        d_log = jnp.where(mask, b_colb - b_row + i_row, -jnp.inf)
        inter = b_colb + mst
        m_t = jnp.maximum(inter, jnp.max(d_log, axis=-1, keepdims=True))
        w_inter = jnp.exp(inter - m_t)
        s = _dot_nt(q, k) * jnp.exp(d_log - m_t)
        num = _dot(q, cst.astype(BF16)) * w_inter + _dot(s.astype(BF16), v)
        qn = jnp.sum(q.astype(F32) * nst, axis=-1, keepdims=True)
        den = qn * w_inter + jnp.sum(s, axis=-1, keepdims=True)
        hc = num / jnp.maximum(jnp.abs(den), jnp.exp(-m_t))

        g = tot - b_colb + i_colb
        total = tot + mst
        m_new = jnp.maximum(total, jnp.max(g, axis=0, keepdims=True))
        wg = jnp.exp(g - m_new)
        decay = jnp.exp(total - m_new)
        kw = k.astype(F32) * wg
        upd = _dot(kw.T.astype(BF16), v)
        c_scr[direction] = cst * decay + upd
        n_scr[direction] = nst * decay + jnp.sum(kw, axis=0, keepdims=True)
        m_scr[direction] = m_new
        return r0, hc

    def step(j, carry):
        r0, hc = chunk(j, 0)
        hf_ref[pl.ds(r0, ln), :] = hc
        r0, hc = chunk(n_chunks - 1 - j, 1)
        hb_ref[pl.ds(r0, ln), :] = hc
        return carry

    lax.fori_loop(0, n_chunks, step, 0)

    hm = hf_ref[...] + hb_ref[...]
    y_ref[...] = _rms(hm, M_DV) * gn_ref[...] * og_ref[...]
    if emit_state:
        cf_ref[...] = c_scr[...]
        nf_ref[...] = n_scr[...]
        mf_ref[...] = m_scr[...]


def _mlstm_call(qkv, og, mg, lw, batch, seq, init, emit_state):
    n = batch * seq
    ln = MLSTM_CHUNK
    nc = seq // ln
    h = M_HEADS
    g4 = mg[:, :4 * h].reshape(batch, nc, ln, 4, h)
    gcol = g4.transpose(0, 4, 1, 2, 3)
    grow = jnp.pad(g4.transpose(0, 4, 1, 3, 2), ((0, 0),) * 3 + ((0, SUBLANES - 4), (0, 0)))

    def tok(colblock):
        return pl.BlockSpec((seq, LANES), lambda b, hh: (b, colblock(hh)))

    in_specs = [
        tok(lambda hh: hh), tok(lambda hh: h + hh), tok(lambda hh: 2 * h + hh),
        tok(lambda hh: hh),
        pl.BlockSpec((None, None, nc, ln, 4), lambda b, hh: (b, hh, 0, 0, 0)),
        pl.BlockSpec((None, None, nc, SUBLANES, ln), lambda b, hh: (b, hh, 0, 0, 0)),
        pl.BlockSpec((1, M_DV), lambda b, hh: (0, 0)),
    ]
    args = [qkv, qkv, qkv, og, gcol, grow, lw["m_norm_g"]]
    if init is not None:
        c0, n0, m0 = init
        in_specs += [
            pl.BlockSpec((None, 2, None, M_DK, M_DV), lambda b, hh: (b, 0, hh, 0, 0)),
            pl.BlockSpec((None, 2, None, 1, M_DK), lambda b, hh: (b, 0, hh, 0, 0)),
            pl.BlockSpec((None, 2, None, 1, ln), lambda b, hh: (b, 0, hh, 0, 0)),
        ]
        args += [c0, n0.reshape(batch, 2, h, 1, M_DK),
                 jnp.broadcast_to(m0[..., None, None], (batch, 2, h, 1, ln))]
    out_shape = [jax.ShapeDtypeStruct((n, h * M_DV), F32)]
    out_specs = [tok(lambda hh: hh)]
    if emit_state:
        out_shape += [
            jax.ShapeDtypeStruct((batch, 2, h, M_DK, M_DV), F32),
            jax.ShapeDtypeStruct((batch, 2, h, 1, M_DK), F32),
            jax.ShapeDtypeStruct((batch, 2, h, 1, ln), F32),
        ]
        out_specs += [
            pl.BlockSpec((None, 2, None, M_DK, M_DV), lambda b, hh: (b, 0, hh, 0, 0)),
            pl.BlockSpec((None, 2, None, 1, M_DK), lambda b, hh: (b, 0, hh, 0, 0)),
            pl.BlockSpec((None, 2, None, 1, ln), lambda b, hh: (b, 0, hh, 0, 0)),
        ]
    outs = pl.pallas_call(
        functools.partial(_mlstm_kernel, n_chunks=nc, has_init=init is not None, emit_state=emit_state),
        out_shape=out_shape,
        grid=(batch, h),
        in_specs=in_specs,
        out_specs=out_specs,
        scratch_shapes=[
            pltpu.VMEM((seq, M_DV), F32), pltpu.VMEM((seq, M_DV), F32),
            pltpu.VMEM((2, M_DK, M_DV), F32), pltpu.VMEM((2, 1, M_DK), F32), pltpu.VMEM((2, 1, ln), F32),
        ],
        compiler_params=_params(2),
        name="mlstm",
    )(*args)
    if emit_state:
        ym, cf, nf, mf = outs
        return ym, (cf, nf[:, :, :, 0, :], mf[:, :, :, 0, 0])
    return outs[0], None


HEAD_V = 64


def _flash_heads(qs, kv_chunk, n_keys, o_ref):
    m = [None] * len(qs)
    acc = [None] * len(qs)
    for c0 in range(0, n_keys, KEY_BLOCK):
        c1 = min(c0 + KEY_BLOCK, n_keys)
        for e, qb in enumerate(qs):
            kc, vc = kv_chunk(e, c0, c1)
            s = _dot_nt(qb, kc)
            mc = jnp.max(s, axis=-1, keepdims=True)
            if c0 == 0:
                m[e] = mc
                acc[e] = _dot(jnp.exp2(s - mc).astype(BF16), vc)
            else:
                m_new = jnp.maximum(m[e], mc)
                alpha = jnp.exp2(m[e] - m_new)
                acc[e] = acc[e] * alpha + _dot(jnp.exp2(s - m_new).astype(BF16), vc)
                m[e] = m_new
    outs = [a / pltpu.roll(a, HEAD_V, 1) for a in acc]
    low = _lane(outs[0].shape) < HEAD_V
    for e in range(0, len(outs), 2):
        o_ref[:, (e // 2) * LANES:(e // 2 + 1) * LANES] = jnp.where(low, outs[e], pltpu.roll(outs[e + 1], HEAD_V, 1))


def _mla_kernel(*refs, past, rotary, heads):
    q_ref, ckv_ref, akr_ref = refs[:3]
    pos = 3
    if past:
        ckvc_ref, krc_ref = refs[pos:pos + 2]
        pos += 2
    w_ref, kg_ref = refs[pos:pos + 2]
    pos += 2
    if rotary:
        ck_ref, sk_ref = refs[pos:pos + 2]
        pos += 2
    o_ref, kbuf, vbuf = refs[pos:]
    seq = ckv_ref.shape[0]

    @pl.when(pl.program_id(2) == 0)
    def _():
        for e in range(heads):
            w_h = w_ref[:, e * LANES:(e + 1) * LANES]

            def keys(ckv, kr):
                kv = _dot(ckv.astype(BF16), w_h)
                is_v = _lane(kv.shape) < A_V
                kcat = jnp.where(is_v, kr, kv)
                return jnp.where(is_v, kv, 1.0), _rms(kcat, A_QK) * kg_ref[...]

            v1, kn = keys(ckv_ref[...], akr_ref[...])
            if rotary:
                kn = _rope(kn, ck_ref[...], sk_ref[...], A_ROPE // 2, LANES)
            kbuf[e, past:past + seq, :] = kn.astype(BF16)
            vbuf[e, past:past + seq, :] = v1.astype(BF16)
            if past:
                v1c, knc = keys(ckvc_ref[...], krc_ref[...])
                kbuf[e, 0:past, :] = knc.astype(BF16)
                vbuf[e, 0:past, :] = v1c.astype(BF16)

    qs = [q_ref[:, e * LANES:(e + 1) * LANES] for e in range(heads)]
    _flash_heads(qs, lambda e, c0, c1: (kbuf[e, c0:c1, :], vbuf[e, c0:c1, :]), past + seq, o_ref)


def _mla_call(qa, ckv, akr, lw, batch, seq, cache, rope_tabs):
    n = batch * seq
    tq = min(Q_BLOCK, seq)
    nq = seq // tq
    past = 0 if cache is None else cache[0].shape[1]
    heads = MLA_HEADS_PER_STEP if nq > 1 else A_HEADS
    in_specs = [
        pl.BlockSpec((tq, heads * LANES), lambda b, hg, qi: (b * nq + qi, hg)),
        pl.BlockSpec((seq, LANES), lambda b, hg, qi: (b, 0)),
        pl.BlockSpec((seq, LANES), lambda b, hg, qi: (b, 0)),
    ]
    args = [qa, ckv, akr]
    if past:
        in_specs += [pl.BlockSpec((None, past, LANES), lambda b, hg, qi: (b, 0, 0))] * 2
        args += [cache[0], cache[1]]
    in_specs += [
        pl.BlockSpec((A_KV_LORA, heads * LANES), lambda b, hg, qi: (0, hg)),
        pl.BlockSpec((1, LANES), lambda b, hg, qi: (0, 0)),
    ]
    args += [lw["w_ukv"], lw["a_knorm_g"]]
    if rope_tabs is not None:
        cos, sin = rope_tabs
        in_specs += [pl.BlockSpec((seq, LANES), lambda b, hg, qi: (0, 0))] * 2
        args += [cos, sin]
    return pl.pallas_call(
        functools.partial(_mla_kernel, past=past, rotary=rope_tabs is not None, heads=heads),
        out_shape=jax.ShapeDtypeStruct((n, A_HEADS * A_V), F32),
        grid=(batch, A_HEADS // heads, nq),
        in_specs=in_specs,
        out_specs=pl.BlockSpec((tq, heads * A_V), lambda b, hg, qi: (b * nq + qi, hg)),
        scratch_shapes=[pltpu.VMEM((heads, past + seq, LANES), BF16),
                        pltpu.VMEM((heads, past + seq, LANES), BF16)],
        compiler_params=_params(3),
        name="mla_attn",
    )(*args)


def _gqa_kernel(*refs, past, groups):
    q_ref, k_ref, v_ref = refs[:3]
    pos = 3
    if past:
        kc_ref, vc_ref = refs[pos:pos + 2]
        pos += 2
    o_ref, kbuf, vbuf = refs[pos:]
    seq = k_ref.shape[0]
    qw = G_GROUP * G_HD

    @pl.when(pl.program_id(2) == 0)
    def _():
        for gi in range(groups):
            grp = pl.program_id(1) * groups + gi

            def low_half(x):
                return jnp.where(grp == 0, x, pltpu.roll(x, G_HD, 1))

            def fill(r0, k, v):
                low = _lane(k.shape) < G_HD
                k_lo = low_half(k)
                rows = k.shape[0]
                kbuf[gi, r0:r0 + rows, :] = jnp.where(low, k_lo, pltpu.roll(k_lo, G_HD, 1)).astype(BF16)
                vbuf[gi, r0:r0 + rows, :] = jnp.where(low, low_half(v), 1.0).astype(BF16)

            fill(past, k_ref[...], v_ref[...])
            if past:
                fill(0, kc_ref[...], vc_ref[...])

    qs = []
    for gi in range(groups):
        for j in range(G_GROUP):
            col = q_ref[:, gi * qw + (j // 2) * LANES:gi * qw + (j // 2 + 1) * LANES]
            keep = (_lane(col.shape) < G_HD) == (j % 2 == 0)
            qs.append(jnp.where(keep, col, jnp.zeros_like(col)))
    _flash_heads(qs, lambda e, c0, c1: (kbuf[e // G_GROUP, c0:c1, :], vbuf[e // G_GROUP, c0:c1, :]),
                 past + seq, o_ref)


def _gqa_call(gq, gk, gv, batch, seq, cache):
    n = batch * seq
    tq = min(Q_BLOCK, seq)
    nq = seq // tq
    past = 0 if cache is None else cache[0].shape[1]
    groups = GQA_GROUPS_PER_STEP if nq > 1 else G_KV_HEADS
    qw = G_GROUP * G_HD
    kvw = G_KV_HEADS * G_HD
    in_specs = [
        pl.BlockSpec((tq, groups * qw), lambda b, g, qi: (b * nq + qi, g)),
        pl.BlockSpec((seq, kvw), lambda b, g, qi: (b, 0)),
        pl.BlockSpec((seq, kvw), lambda b, g, qi: (b, 0)),
    ]
    args = [gq, gk, gv]
    if past:
        in_specs += [pl.BlockSpec((None, past, kvw), lambda b, g, qi: (b, 0, 0))] * 2
        args += [cache[0], cache[1]]
    return pl.pallas_call(
        functools.partial(_gqa_kernel, past=past, groups=groups),
        out_shape=jax.ShapeDtypeStruct((n, G_HEADS * G_HD), F32),
        grid=(batch, G_KV_HEADS // groups, nq),
        in_specs=in_specs,
        out_specs=pl.BlockSpec((tq, groups * qw), lambda b, g, qi: (b * nq + qi, g)),
        scratch_shapes=[pltpu.VMEM((groups, past + seq, LANES), BF16),
                        pltpu.VMEM((groups, past + seq, LANES), BF16)],
        compiler_params=_params(3),
        name="gqa_attn",
    )(*args)


def _merge_ffn_kernel(x_ref, ym_ref, ya_ref, yg_ref, gates_ref, mod_ref, g2_ref, wb_ref, wo_ref, wfi_ref,
                      wfo_ref, o_ref, *, ff_chunk):
    x = x_ref[...]
    d = x.shape[-1]
    mixed = None
    for i, y_ref in enumerate((ym_ref, ya_ref, yg_ref)):
        br = _dot(y_ref[...].astype(BF16), wb_ref[i]) * gates_ref[:, i * d:(i + 1) * d]
        mixed = br if mixed is None else mixed + br
    gt1 = mod_ref[2:3, :]
    x1 = x + gt1 * _dot(mixed.astype(BF16), wo_ref[...])

    sh2 = mod_ref[3:4, :]
    sc2 = mod_ref[4:5, :]
    gt2 = mod_ref[5:6, :]
    h2 = ((_rms(x1, d) * g2_ref[...]) * (1.0 + sc2) + sh2).astype(BF16)
    d_ff = wfo_ref.shape[0]
    acc = None
    for c0 in range(0, d_ff, ff_chunk):
        ug = _dot(h2, wfi_ref[:, c0:c0 + ff_chunk])
        uv = _dot(h2, wfi_ref[:, d_ff + c0:d_ff + c0 + ff_chunk])
        act = (ug * _sigmoid(ug) * uv).astype(BF16)
        part = _dot(act, wfo_ref[c0:c0 + ff_chunk, :])
        acc = part if acc is None else acc + part
    o_ref[...] = x1 + gt2 * acc


def _merge_ffn_call(x, ym, ya, yg, gates, mod, lw, mod_row0, rows_per_mod):
    n, d = x.shape
    tm = TOKEN_BLOCK
    d_ff = lw["w_ffn_out"].shape[0]
    ff_chunk = d_ff // 2 if (d_ff // 2) % LANES == 0 else d_ff

    def mod_idx(i):
        return (mod_row0 + (i * tm) // rows_per_mod, 0, 0)

    def rows(width):
        return pl.BlockSpec((tm, width), lambda i: (i, 0))

    return pl.pallas_call(
        functools.partial(_merge_ffn_kernel, ff_chunk=ff_chunk),
        out_shape=jax.ShapeDtypeStruct((n, d), F32),
        grid=(n // tm,),
        in_specs=[
            rows(d), rows(BRANCH_WIDTH), rows(BRANCH_WIDTH), rows(BRANCH_WIDTH), rows(N_BRANCH * d),
            pl.BlockSpec((None, N_MOD, d), mod_idx),
            _resident((1, d)),
            _resident((N_BRANCH, BRANCH_WIDTH, d)),
            _resident((d, d)),
            _resident((d, 2 * d_ff)),
            _resident((d_ff, d)),
        ],
        out_specs=rows(d),
        compiler_params=_params(1),
        name="merge_ffn",
    )(x, ym, ya, yg, gates, mod, lw["norm2_g"], lw["w_branch"], lw["w_out"], lw["w_ffn_in"], lw["w_ffn_out"])


def _pad_cols(w, width):
    return jnp.pad(w, ((0, 0), (0, width - w.shape[1])))


def _mla_q_layout(a):
    lead = a.shape[:-1]
    a = a.reshape(lead + (A_HEADS, A_QK))
    z = jnp.zeros(lead + (A_HEADS, LANES - A_QK), a.dtype)
    return jnp.concatenate([a[..., A_NOPE:], z, a[..., :A_NOPE]], axis=-1).reshape(lead + (A_HEADS * LANES,))


def _layer_weights(l, w_in, b_mgate, norm1_g, m_norm_g, a_qlora_g, a_kvlora_g, w_uq, w_ukv, a_qnorm_g,
                   a_knorm_g, g_qnorm_g, g_knorm_g, w_branch, w_out, norm2_g, w_ffn_in, w_ffn_out):
    d = w_in.shape[1]
    hw = M_HEADS * M_DK
    sizes = (N_BRANCH * d, hw, hw, hw, hw, 4 * M_HEADS, A_Q_LORA, A_KV_LORA, A_ROPE, G_HEADS * G_HD,
             G_KV_HEADS * G_HD, G_KV_HEADS * G_HD)
    splits = np.cumsum(sizes)[:-1].tolist()
    (wg, wmq, wmk, wmv, wmo, wmg, waq, wakv, wakr, wgq, wgk, wgv) = jnp.split(w_in[l], splits, axis=1)
    w_in_p = jnp.concatenate(
        [wg, wmq, wmk, wmv, wmo, _pad_cols(wmg, LANES), waq, wakv, _pad_cols(wakr, LANES), wgq, wgk, wgv],
        axis=1).astype(BF16)
    ukv = w_ukv[l].reshape(A_KV_LORA, A_HEADS, A_NOPE + A_V)
    ukv = jnp.concatenate([ukv[..., A_NOPE:], ukv[..., :A_NOPE]], axis=-1).reshape(A_KV_LORA, A_HEADS * LANES)
    qg = _mla_q_layout(jnp.tile(a_qnorm_g[l], A_HEADS)[None, :])[:, :LANES]
    kg = _mla_q_layout(jnp.tile(a_knorm_g[l], A_HEADS)[None, :])[:, :LANES]
    return dict(
        w_in=w_in_p,
        b_mgate=_pad_cols(b_mgate[l][None, :], LANES),
        norm1_g=norm1_g[l][None, :],
        m_norm_g=m_norm_g[l][None, :],
        a_qlora_g=a_qlora_g[l][None, :],
        a_kvlora_g=a_kvlora_g[l][None, :],
        w_uq=_mla_q_layout(w_uq[l]).astype(BF16),
        w_ukv=ukv.astype(BF16),
        a_qnorm_g=qg,
        a_knorm_g=kg,
        g_qnorm_g=jnp.tile(g_qnorm_g[l], LANES // G_HD)[None, :],
        g_knorm_g=jnp.tile(g_knorm_g[l], G_KV_HEADS)[None, :],
        w_branch=w_branch[l].astype(BF16),
        w_out=w_out[l].astype(BF16),
        norm2_g=norm2_g[l][None, :],
        w_ffn_in=w_ffn_in[l].astype(BF16),
        w_ffn_out=w_ffn_out[l].astype(BF16),
    )


def _axial_angles(seq, rot_dim):
    n_freq = rot_dim // 4
    freqs = ROPE_BASE ** (-jnp.arange(n_freq, dtype=F32) / n_freq)
    t = jnp.arange(seq)
    row = (t // GRID_W).astype(F32)
    col = (t % GRID_W).astype(F32)
    return jnp.concatenate([row[:, None] * freqs, col[:, None] * freqs], axis=-1)


def _rope_tables(seq):
    ang = _axial_angles(seq, A_ROPE)
    one = jnp.ones((seq, LANES - A_ROPE), F32)
    mla_cos = jnp.concatenate([jnp.cos(ang), jnp.cos(ang), one], axis=-1)
    mla_sin = jnp.concatenate([-jnp.sin(ang), jnp.sin(ang), 0.0 * one], axis=-1)
    ang = _axial_angles(seq, G_HD)
    cos = jnp.concatenate([jnp.cos(ang), jnp.cos(ang)], axis=-1)
    sin = jnp.concatenate([-jnp.sin(ang), jnp.sin(ang)], axis=-1)
    gqa = (jnp.tile(cos, (1, LANES // G_HD)), jnp.tile(sin, (1, LANES // G_HD)))
    return (mla_cos, mla_sin), gqa


def _layer(x, mod, lw, batch, seq, mod_row0, rows_per_mod, ctx, rope):
    (gates, qkv, og, mg, qa, ckv, akr, gq, gk, gv) = _inproj_call(x, mod, lw, mod_row0, rows_per_mod, rope, seq)
    if ctx is None:
        ym, state = _mlstm_call(qkv, og, mg, lw, batch, seq, None, True)
        ya = _mla_call(qa, ckv, akr, lw, batch, seq, None, None)
        yg = _gqa_call(gq, gk, gv, batch, seq, None)
        new_ctx = dict(state=state, ckv=ckv, kr=akr[:, :A_ROPE], gk=gk, gv=gv)
    else:
        ym, _ = _mlstm_call(qkv, og, mg, lw, batch, seq, ctx["mlstm"], False)
        ya = _mla_call(qa, ckv, akr, lw, batch, seq, ctx["mla"], rope[0])
        yg = _gqa_call(gq, gk, gv, batch, seq, ctx["gqa"])
        new_ctx = None
    x = _merge_ffn_call(x, ym, ya, yg, gates, mod, lw, mod_row0, rows_per_mod)
    return x, new_ctx


def kernel(x_prompt, x_sample, state_mlstm_C, state_mlstm_n, state_mlstm_m, cache_mla_ckv, cache_mla_krope,
           cache_gqa_k, cache_gqa_v, c, c_ctx, w_mod, b_mod, norm1_g, w_in, b_mgate, m_norm_g, a_qlora_g,
           a_kvlora_g, w_uq, w_ukv, a_qnorm_g, a_knorm_g, g_qnorm_g, g_knorm_g, w_branch, w_out, norm2_g,
           w_ffn_in, w_ffn_out):
    batch, seq, d = x_prompt.shape
    dbatch, dseq, _ = x_sample.shape
    depth = w_in.shape[0]
    past = cache_mla_ckv.shape[2]

    n_rows = -(-(1 + dbatch) // SUBLANES) * SUBLANES
    cond = jnp.concatenate([c_ctx[None, :], c, jnp.zeros((n_rows - 1 - dbatch, d), F32)], axis=0)
    mod_all = _mod_call(cond, w_mod, b_mod).reshape(depth, n_rows, N_MOD, d)

    rope = _rope_tables(dseq)
    xp = x_prompt.reshape(batch * seq, d)
    xs = x_sample.reshape(dbatch * dseq, d)
    ctx_layers = []
    for l in range(depth):
        lw = _layer_weights(l, w_in, b_mgate, norm1_g, m_norm_g, a_qlora_g, a_kvlora_g, w_uq, w_ukv,
                            a_qnorm_g, a_knorm_g, g_qnorm_g, g_knorm_g, w_branch, w_out, norm2_g, w_ffn_in,
                            w_ffn_out)
        xp, st = _layer(xp, mod_all[l], lw, batch, seq, 0, batch * seq, None, None)
        ctx_layers.append(st)
        ctx = dict(
            mlstm=(state_mlstm_C[:, l], state_mlstm_n[:, l], state_mlstm_m[:, l]),
            mla=(cache_mla_ckv[:, l], _pad_cols(cache_mla_krope[:, l].reshape(dbatch * past, A_ROPE), LANES)
                 .reshape(dbatch, past, LANES)),
            gqa=(cache_gqa_k[:, l].reshape(dbatch, past, G_KV_HEADS * G_HD),
                 cache_gqa_v[:, l].reshape(dbatch, past, G_KV_HEADS * G_HD)),
        )
        xs, _ = _layer(xs, mod_all[l], lw, dbatch, dseq, 1, dseq, ctx, rope)

    def stack(fn):
        return jnp.stack([fn(s) for s in ctx_layers], axis=1)

    new_c = stack(lambda s: s["state"][0])
    new_n = stack(lambda s: s["state"][1])
    new_m = stack(lambda s: s["state"][2])
    new_ckv = stack(lambda s: s["ckv"].reshape(batch, seq, A_KV_LORA))
    new_kr = stack(lambda s: s["kr"].reshape(batch, seq, A_ROPE))
    new_gk = stack(lambda s: s["gk"].reshape(batch, seq, G_KV_HEADS, G_HD))
    new_gv = stack(lambda s: s["gv"].reshape(batch, seq, G_KV_HEADS, G_HD))
    return (xp.reshape(batch, seq, d), xs.reshape(dbatch, dseq, d), new_c, new_n, new_m, new_ckv, new_kr,
            new_gk, new_gv)
```

```python
import functools

import numpy as np
import jax
import jax.numpy as jnp
from jax import lax
from jax.experimental import pallas as pl
from jax.experimental.pallas import tpu as pltpu

F32 = jnp.float32
BF16 = jnp.bfloat16

LANES = 128
SUBLANES = 8
VMEM_LIMIT_BYTES = 56 * 1024 * 1024

EPS = 1e-6
ROPE_BASE = 10000.0
GRID_W = 64

M_HEADS = 4
M_DK = 128
M_DV = 128
A_HEADS = 8
A_NOPE = 64
A_ROPE = 32
A_QK = A_NOPE + A_ROPE
A_V = 64
A_Q_LORA = 256
A_KV_LORA = 128
G_HEADS = 8
G_KV_HEADS = 2
G_GROUP = G_HEADS // G_KV_HEADS
G_HD = 64
N_BRANCH = 3
BRANCH_WIDTH = 512
N_MOD = 6

TOKEN_BLOCK = 256
Q_BLOCK = 256
KEY_BLOCK = 256
MLA_HEADS_PER_STEP = 4
GQA_GROUPS_PER_STEP = 1
MLSTM_CHUNK = 128
MLSTM_HEADS_PER_STEP = 4
LOG2E = 1.4426950408889634


def _params(n_axes):
    return pltpu.CompilerParams(dimension_semantics=("arbitrary",) * n_axes,
                                vmem_limit_bytes=VMEM_LIMIT_BYTES)


def _resident(shape):
    nd = len(shape)
    return pl.BlockSpec(shape, lambda *_: (0,) * nd, pipeline_mode=pl.Buffered(1))


def _lane(shape, axis=None):
    return lax.broadcasted_iota(jnp.int32, shape, len(shape) - 1 if axis is None else axis)


def _dot(a, b):
    return jnp.dot(a, b, preferred_element_type=F32)


def _dot_nt(a, b):
    return lax.dot_general(a, b, (((1,), (1,)), ((), ())), preferred_element_type=F32)


def _split3(a):
    hi = a.astype(BF16)
    r1 = a - hi.astype(F32)
    mid = r1.astype(BF16)
    lo = (r1 - mid.astype(F32)).astype(BF16)
    return hi, mid, lo


def _dot01(a, m01):
    hi, mid, lo = _split3(a)
    return _dot(hi, m01) + _dot(mid, m01) + _dot(lo, m01)


def _dot01_left(m01, a):
    hi, mid, lo = _split3(a)
    return _dot(m01, hi) + _dot(m01, mid) + _dot(m01, lo)


def _sigmoid(x):
    return 1.0 / (1.0 + jnp.exp(-x))


def _log_sigmoid(x):
    return jnp.minimum(x, 0.0) - jnp.log(1.0 + jnp.exp(-jnp.abs(x)))


def _rms(x, width):
    ms = jnp.sum(x * x, axis=-1, keepdims=True) * (1.0 / width)
    return x * lax.rsqrt(ms + EPS)


def _rope(x, cos, sin_signed, half, period):
    n = x.shape[-1]
    first = (_lane(x.shape) % period) < half
    swapped = jnp.where(first, pltpu.roll(x, n - half, x.ndim - 1), pltpu.roll(x, half, x.ndim - 1))
    return x * cos + swapped * sin_signed


def _mod_kernel(c_ref, w_ref, b_ref, o_ref):
    c = c_ref[...]
    a = c * _sigmoid(c)
    hi, mid, lo = _split3(a)
    w = w_ref[...]
    whi = w.astype(BF16)
    wlo = (w - whi.astype(F32)).astype(BF16)
    acc = _dot(hi, whi) + _dot(mid, whi) + _dot(hi, wlo) + _dot(lo, whi) + _dot(mid, wlo)
    o_ref[...] = acc + b_ref[...]


def _mod_call(cond, w_mod, b_mod):
    depth, d, n = w_mod.shape
    rows = cond.shape[0]
    tn = 1536
    return pl.pallas_call(
        _mod_kernel,
        out_shape=jax.ShapeDtypeStruct((depth, rows, n), F32),
        grid=(depth, n // tn),
        in_specs=[
            pl.BlockSpec((rows, d), lambda l, j: (0, 0)),
            pl.BlockSpec((None, d, tn), lambda l, j: (l, 0, j)),
            pl.BlockSpec((None, 1, tn), lambda l, j: (l, 0, j)),
        ],
        out_specs=pl.BlockSpec((None, rows, tn), lambda l, j: (l, 0, j)),
        compiler_params=_params(2),
        name="adaln_mod",
    )(cond, w_mod, b_mod.reshape(depth, 1, n))


_GATE0, _GATE1 = 0, 3072
_MQ0 = 3072
_MK0 = 3584
_MV0 = 4096
_MO0 = 4608
_MG0 = 5120
_AQ0 = 5248
_AKV0 = 5504
_AKR0 = 5632
_GQ0 = 5760
_GK0 = 6272
_GV0 = 6400
_WIN_COLS = 6528


def _head_pair_ms(x):
    low = (_lane(x.shape) % LANES) < G_HD
    sq = x * x
    cols = []
    for c0 in range(0, x.shape[-1], LANES):
        s = sq[:, c0:c0 + LANES]
        lo = jnp.sum(jnp.where(low[:, c0:c0 + LANES], s, 0.0), axis=-1, keepdims=True)
        hi = jnp.sum(jnp.where(low[:, c0:c0 + LANES], 0.0, s), axis=-1, keepdims=True)
        cols.append(jnp.where(low[:, c0:c0 + LANES], lo, hi))
    ms = cols[0] if len(cols) == 1 else jnp.concatenate(cols, axis=-1)
    return ms * (1.0 / G_HD)


def _inproj_kernel(*refs, rotary):
    (x_ref, mod_ref, g1_ref, w_ref, bmg_ref, gql_ref, wuq_ref, gkvl_ref, gkn_ref, aqn_ref, gqn_ref) = refs[:11]
    pos = 11
    if rotary:
        ca_ref, sa_ref, cg_ref, sg_ref = refs[pos:pos + 4]
        pos += 4
    (gates_ref, qkv_ref, og_ref, mg_ref, qa_ref, ckv_ref, akr_ref, gq_ref, gk_ref, gv_ref) = refs[pos:]
    x = x_ref[...]
    d = x.shape[-1]
    sh1 = mod_ref[0:1, :]
    sc1 = mod_ref[1:2, :]
    h = (_rms(x, d) * g1_ref[...]) * (1.0 + sc1) + sh1
    hb = h.astype(BF16)

    def proj(c0, width):
        return _dot(hb, w_ref[:, c0:c0 + width])

    aq = _rms(proj(_AQ0, A_Q_LORA), A_Q_LORA) * gql_ref[...]
    qa = _dot(aq.astype(BF16), wuq_ref[...])
    for hd in range(A_HEADS):
        qh = _rms(qa[:, hd * LANES:(hd + 1) * LANES], A_QK) * aqn_ref[...]
        if rotary:
            qh = _rope(qh, ca_ref[...], sa_ref[...], A_ROPE // 2, LANES)
        qa_ref[:, hd * LANES:(hd + 1) * LANES] = (qh * (A_QK ** -0.5 * LOG2E)).astype(qa_ref.dtype)
    ckv_ref[...] = _rms(proj(_AKV0, A_KV_LORA), A_KV_LORA) * gkvl_ref[...]
    akr_ref[...] = proj(_AKR0, LANES)

    for c0 in range(0, G_HEADS * G_HD, LANES):
        gq = proj(_GQ0 + c0, LANES)
        gq = gq * lax.rsqrt(_head_pair_ms(gq) + EPS) * gqn_ref[...]
        if rotary:
            gq = _rope(gq, cg_ref[...], sg_ref[...], G_HD // 2, G_HD)
        gq_ref[:, c0:c0 + LANES] = (gq * (G_HD ** -0.5 * LOG2E)).astype(gq_ref.dtype)
    gk = proj(_GK0, G_KV_HEADS * G_HD)
    gk = gk * lax.rsqrt(_head_pair_ms(gk) + EPS) * gkn_ref[...]
    if rotary:
        gk = _rope(gk, cg_ref[...], sg_ref[...], G_HD // 2, G_HD)
    gk_ref[...] = gk
    gv_ref[...] = proj(_GV0, G_KV_HEADS * G_HD)

    hw = M_HEADS * M_DK
    qkv_ref[:, 0:hw] = proj(_MQ0, hw).astype(qkv_ref.dtype)
    qkv_ref[:, hw:2 * hw] = (proj(_MK0, hw) * (M_DK ** -0.5)).astype(qkv_ref.dtype)
    qkv_ref[:, 2 * hw:3 * hw] = proj(_MV0, hw).astype(qkv_ref.dtype)
    og_ref[...] = _sigmoid(proj(_MO0, hw))
    mg_ref[...] = proj(_MG0, LANES) + bmg_ref[...]
    gates_ref[...] = _sigmoid(proj(_GATE0, _GATE1 - _GATE0))


def _inproj_call(x, mod, lw, mod_row0, rows_per_mod, rope, seq):
    n, d = x.shape
    tm = TOKEN_BLOCK
    hw = M_HEADS * M_DK

    def mod_idx(i):
        return (mod_row0 + (i * tm) // rows_per_mod, 0, 0)

    def rows(width):
        return pl.BlockSpec((tm, width), lambda i: (i, 0))

    in_specs = [
        rows(d),
        pl.BlockSpec((None, N_MOD, d), mod_idx),
        _resident((1, d)),
        _resident((d, _WIN_COLS)),
        _resident((1, LANES)),
        _resident((1, A_Q_LORA)),
        _resident((A_Q_LORA, A_HEADS * LANES)),
        _resident((1, A_KV_LORA)),
        _resident((1, LANES)),
        _resident((1, LANES)),
        _resident((1, LANES)),
    ]
    args = [x, mod, lw["norm1_g"], lw["w_in"], lw["b_mgate"], lw["a_qlora_g"], lw["w_uq"], lw["a_kvlora_g"],
            lw["g_knorm_g"], lw["a_qnorm_g"], lw["g_qnorm_g"]]
    if rope is not None:
        blocks_per_seq = seq // tm
        in_specs += [pl.BlockSpec((tm, LANES), lambda i: (i % blocks_per_seq, 0))] * 4
        args += [rope[0][0], rope[0][1], rope[1][0], rope[1][1]]
    out_widths = [3 * d, 3 * hw, hw, LANES, A_HEADS * LANES, A_KV_LORA, LANES, G_HEADS * G_HD,
                  G_KV_HEADS * G_HD, G_KV_HEADS * G_HD]
    out_dtypes = [F32, BF16, F32, F32, BF16, F32, F32, BF16, F32, F32]
    return pl.pallas_call(
        functools.partial(_inproj_kernel, rotary=rope is not None),
        out_shape=[jax.ShapeDtypeStruct((n, w), t) for w, t in zip(out_widths, out_dtypes)],
        grid=(n // tm,),
        in_specs=in_specs,
        out_specs=[rows(w) for w in out_widths],
        compiler_params=_params(1),
        name="in_proj",
    )(*args)


def _mlstm_kernel(*refs, n_chunks, heads, has_init, emit_state):
    q_ref, k_ref, v_ref, og_ref, gcol_ref, grow_ref, gn_ref = refs[:7]
    pos = 7
    if has_init:
        c0_ref, n0_ref, m0_ref = refs[pos:pos + 3]
        pos += 3
    y_ref = refs[pos]
    pos += 1
    if emit_state:
        cf_ref, nf_ref, mf_ref = refs[pos:pos + 3]
        pos += 3
    (pcol_scr, lfc_scr, prow_scr, lfr_scr, u_scr, nu_scr, gm_scr, tot_scr, cs_scr, ns_scr, ms_scr,
     c_scr, n_scr, m_scr) = refs[pos:]

    ln = MLSTM_CHUNK
    row = lax.broadcasted_iota(jnp.int32, (ln, ln), 0)
    col = lax.broadcasted_iota(jnp.int32, (ln, ln), 1)
    lower = col <= row
    upper = col >= row
    tril = jnp.where(lower, 1.0, 0.0).astype(BF16)
    triu = jnp.where(upper, 1.0, 0.0).astype(BF16)
    ones = jnp.ones((ln, ln), BF16)

    def gate_lanes(hh, d):
        return 2 * d * heads + hh, (2 * d + 1) * heads + hh

    def summaries(c, carry):
        r0 = pl.multiple_of(c * ln, ln)
        g = gcol_ref[pl.ds(r0, ln), :] * LOG2E
        lf = _log_sigmoid(gcol_ref[pl.ds(r0, ln), :]) * LOG2E
        pc = _dot01_left(tril, lf)
        pcol_scr[pl.ds(r0, ln), :] = pc
        lfc_scr[pl.ds(r0, ln), :] = lf
        lfr = _log_sigmoid(grow_ref[c]) * LOG2E
        prow_scr[c] = _dot01(lfr, triu)
        lfr_scr[c] = lfr
        tot_all = pc[ln - 1:ln, :]
        for hh in range(heads):
            k = k_ref[pl.ds(r0, ln), hh * LANES:(hh + 1) * LANES].astype(F32)
            v = v_ref[pl.ds(r0, ln), hh * LANES:(hh + 1) * LANES]
            for d in range(2):
                ji, jf = gate_lanes(hh, d)
                tot = tot_all[:, jf:jf + 1]
                b_col = pc[:, jf:jf + 1] if d == 0 else tot - pc[:, jf:jf + 1] + lf[:, jf:jf + 1]
                gg = jnp.broadcast_to(tot - b_col + g[:, ji:ji + 1], (ln, ln))
                gmax = jnp.max(gg, axis=0, keepdims=True)
                kw = k * jnp.exp2(gg - gmax)
                u_scr[hh, d, c] = _dot(kw.T.astype(BF16), v)
                nu_scr[hh, d, c] = jnp.sum(kw, axis=0, keepdims=True)
                gm_scr[hh, d, c] = gmax
                tot_scr[hh, d, c] = jnp.broadcast_to(tot, (1, ln))
        return carry

    lax.fori_loop(0, n_chunks, summaries, 0)

    for hh in range(heads):
        for d in range(2):
            if has_init:
                c_scr[hh, d] = c0_ref[d, hh]
                n_scr[hh, d] = n0_ref[d, hh]
                m_scr[hh, d] = m0_ref[d, hh] * LOG2E
            else:
                c_scr[hh, d] = jnp.zeros((M_DK, M_DV), F32)
                n_scr[hh, d] = jnp.zeros((1, M_DK), F32)
                m_scr[hh, d] = jnp.zeros((1, ln), F32)

    def scan(j, carry):
        for hh in range(heads):
            for d in range(2):
                c = j if d == 0 else n_chunks - 1 - j
                cst = c_scr[hh, d]
                nst = n_scr[hh, d]
                mst = m_scr[hh, d]
                cs_scr[hh, d, c] = cst.astype(BF16)
                ns_scr[hh, d, c] = nst
                ms_scr[hh, d, c] = mst
                gmax = gm_scr[hh, d, c]
                total = tot_scr[hh, d, c] + mst
                m_new = jnp.maximum(total, gmax)
                decay = jnp.exp2(total - m_new)
                scale = jnp.exp2(gmax - m_new)
                c_scr[hh, d] = cst * decay + u_scr[hh, d, c] * scale
                n_scr[hh, d] = nst * decay + nu_scr[hh, d, c] * scale
                m_scr[hh, d] = m_new
        return carry

    lax.fori_loop(0, n_chunks, scan, 0)

    def readout(c, carry):
        r0 = pl.multiple_of(c * ln, ln)
        pc = pcol_scr[pl.ds(r0, ln), :]
        lf = lfc_scr[pl.ds(r0, ln), :]
        gr = grow_ref[c] * LOG2E
        pr = prow_scr[c]
        lfr = lfr_scr[c]
        tot_all = pc[ln - 1:ln, :]
        for hh in range(heads):
            q = q_ref[pl.ds(r0, ln), hh * LANES:(hh + 1) * LANES]
            k = k_ref[pl.ds(r0, ln), hh * LANES:(hh + 1) * LANES]
            v = v_ref[pl.ds(r0, ln), hh * LANES:(hh + 1) * LANES]
            qk = _dot_nt(q, k)
            hsum = None
            for d in range(2):
                ji, jf = gate_lanes(hh, d)
                if d == 0:
                    b_col = pc[:, jf:jf + 1]
                    r_row = gr[ji:ji + 1, :] - pr[jf:jf + 1, :]
                    mask = lower
                else:
                    tot = tot_all[:, jf:jf + 1]
                    b_col = tot - pc[:, jf:jf + 1] + lf[:, jf:jf + 1]
                    r_row = gr[ji:ji + 1, :] - (tot - pr[jf:jf + 1, :] + lfr[jf:jf + 1, :])
                    mask = upper
                b_colb = jnp.broadcast_to(b_col, (ln, ln))
                d_log = jnp.where(mask, b_colb + r_row, -jnp.inf)
                dmax = jnp.broadcast_to(jnp.max(d_log, axis=-1, keepdims=True), (ln, ln))
                sb = (qk * jnp.exp2(d_log - dmax)).astype(BF16)
                num_loc = _dot(sb, v)
                den_loc = _dot(sb, ones)
                qc = _dot(q, cs_scr[hh, d, c])
                qn = _dot_nt(q, jnp.broadcast_to(ns_scr[hh, d, c], (ln, M_DK)).astype(BF16))
                inter = b_colb + ms_scr[hh, d, c]
                m_t = jnp.maximum(inter, dmax)
                w_inter = jnp.exp2(inter - m_t)
                w_loc = jnp.exp2(dmax - m_t)
                num = qc * w_inter + num_loc * w_loc
                den = qn * w_inter + den_loc * w_loc
                hc = num / jnp.maximum(jnp.abs(den), jnp.exp2(-m_t))
                hsum = hc if hsum is None else hsum + hc
            y_ref[pl.ds(r0, ln), hh * LANES:(hh + 1) * LANES] = (
                _rms(hsum, M_DV) * gn_ref[...] * og_ref[pl.ds(r0, ln), hh * LANES:(hh + 1) * LANES])
        return carry

    lax.fori_loop(0, n_chunks, readout, 0)

    if emit_state:
        for hh in range(heads):
            for d in range(2):
                cf_ref[d, hh] = c_scr[hh, d]
                nf_ref[d, hh] = n_scr[hh, d]
                mf_ref[d, hh] = m_scr[hh, d] * (1.0 / LOG2E)


def _mlstm_call(qkv, og, mg, lw, batch, seq, init, emit_state):
    n = batch * seq
    ln = MLSTM_CHUNK
    nc = seq // ln
    h = M_HEADS
    hs = MLSTM_HEADS_PER_STEP
    ng = h // hs
    g4 = mg[:, :4 * h].reshape(n, 4, ng, hs).transpose(2, 0, 1, 3).reshape(ng, n, 4 * hs)
    gcol = jnp.pad(g4, ((0, 0), (0, 0), (0, LANES - 4 * hs)))
    grow = jnp.pad(g4.reshape(ng, batch, nc, ln, 4 * hs).transpose(0, 1, 2, 4, 3),
                   ((0, 0),) * 3 + ((0, 4 * h - 4 * hs), (0, 0)))

    def tok(colblock):
        return pl.BlockSpec((seq, hs * LANES), lambda b, hg: (b, colblock(hg)))

    in_specs = [
        tok(lambda hg: hg), tok(lambda hg: ng + hg), tok(lambda hg: 2 * ng + hg),
        tok(lambda hg: hg),
        pl.BlockSpec((None, seq, LANES), lambda b, hg: (hg, b, 0)),
        pl.BlockSpec((None, None, nc, 4 * h, ln), lambda b, hg: (hg, b, 0, 0, 0)),
        pl.BlockSpec((1, M_DV), lambda b, hg: (0, 0)),
    ]
    args = [qkv, qkv, qkv, og, gcol, grow, lw["m_norm_g"]]
    state_specs = [
        pl.BlockSpec((None, 2, hs, M_DK, M_DV), lambda b, hg: (b, 0, hg, 0, 0)),
        pl.BlockSpec((None, 2, hs, 1, M_DK), lambda b, hg: (b, 0, hg, 0, 0)),
        pl.BlockSpec((None, 2, hs, 1, ln), lambda b, hg: (b, 0, hg, 0, 0)),
    ]
    if init is not None:
        c0, n0, m0 = init
        in_specs += state_specs
        args += [c0, n0.reshape(batch, 2, h, 1, M_DK),
                 jnp.broadcast_to(m0[..., None, None], (batch, 2, h, 1, ln))]
    out_shape = [jax.ShapeDtypeStruct((n, h * M_DV), F32)]
    out_specs = [tok(lambda hg: hg)]
    if emit_state:
        out_shape += [
            jax.ShapeDtypeStruct((batch, 2, h, M_DK, M_DV), F32),
            jax.ShapeDtypeStruct((batch, 2, h, 1, M_DK), F32),
            jax.ShapeDtypeStruct((batch, 2, h, 1, ln), F32),
        ]
        out_specs += state_specs
    per = (hs, 2, nc)
    outs = pl.pallas_call(
        functools.partial(_mlstm_kernel, n_chunks=nc, heads=hs, has_init=init is not None,
                          emit_state=emit_state),
        out_shape=out_shape,
        grid=(batch, ng),
        in_specs=in_specs,
        out_specs=out_specs,
        scratch_shapes=[
            pltpu.VMEM((seq, LANES), F32), pltpu.VMEM((seq, LANES), F32),
            pltpu.VMEM((nc, 4 * h, ln), F32), pltpu.VMEM((nc, 4 * h, ln), F32),
            pltpu.VMEM(per + (M_DK, M_DV), F32), pltpu.VMEM(per + (1, M_DK), F32),
            pltpu.VMEM(per + (1, ln), F32), pltpu.VMEM(per + (1, ln), F32),
            pltpu.VMEM(per + (M_DK, M_DV), BF16), pltpu.VMEM(per + (1, M_DK), F32), pltpu.VMEM(per + (1, ln), F32),
            pltpu.VMEM((hs, 2, M_DK, M_DV), F32), pltpu.VMEM((hs, 2, 1, M_DK), F32), pltpu.VMEM((hs, 2, 1, ln), F32),
        ],
        compiler_params=_params(2),
        name="mlstm",
    )(*args)
    if emit_state:
        ym, cf, nf, mf = outs
        return ym, (cf, nf[:, :, :, 0, :], mf[:, :, :, 0, 0])
    return outs[0], None


HEAD_V = 64


def _flash_heads(qs, kv_chunk, n_keys, o_ref):
    m = [None] * len(qs)
    acc = [None] * len(qs)
    for c0 in range(0, n_keys, KEY_BLOCK):
        c1 = min(c0 + KEY_BLOCK, n_keys)
        for e, qb in enumerate(qs):
            kc, vc = kv_chunk(e, c0, c1)
            s = _dot_nt(qb, kc)
            mc = jnp.max(s, axis=-1, keepdims=True)
            if c0 == 0:
                m[e] = mc
                acc[e] = _dot(jnp.exp2(s - mc).astype(BF16), vc)
            else:
                m_new = jnp.maximum(m[e], mc)
                alpha = jnp.exp2(m[e] - m_new)
                acc[e] = acc[e] * alpha + _dot(jnp.exp2(s - m_new).astype(BF16), vc)
                m[e] = m_new
    outs = [a / pltpu.roll(a, HEAD_V, 1) for a in acc]
    low = _lane(outs[0].shape) < HEAD_V
    for e in range(0, len(outs), 2):
        o_ref[:, (e // 2) * LANES:(e // 2 + 1) * LANES] = jnp.where(low, outs[e], pltpu.roll(outs[e + 1], HEAD_V, 1))


def _mla_kernel(*refs, past, rotary, heads):
    q_ref, ckv_ref, akr_ref = refs[:3]
    pos = 3
    if past:
        ckvc_ref, krc_ref = refs[pos:pos + 2]
        pos += 2
    w_ref, kg_ref = refs[pos:pos + 2]
    pos += 2
    if rotary:
        ck_ref, sk_ref = refs[pos:pos + 2]
        pos += 2
    o_ref, kbuf, vbuf = refs[pos:]
    seq = ckv_ref.shape[0]

    @pl.when(pl.program_id(2) == 0)
    def _():
        for e in range(heads):
            w_h = w_ref[:, e * LANES:(e + 1) * LANES]

            def keys(ckv, kr):
                kv = _dot(ckv.astype(BF16), w_h)
                is_v = _lane(kv.shape) < A_V
                kcat = jnp.where(is_v, kr, kv)
                return jnp.where(is_v, kv, 1.0), _rms(kcat, A_QK) * kg_ref[...]

            v1, kn = keys(ckv_ref[...], akr_ref[...])
            if rotary:
                kn = _rope(kn, ck_ref[...], sk_ref[...], A_ROPE // 2, LANES)
            kbuf[e, past:past + seq, :] = kn.astype(BF16)
            vbuf[e, past:past + seq, :] = v1.astype(BF16)
            if past:
                v1c, knc = keys(ckvc_ref[...], krc_ref[...])
                kbuf[e, 0:past, :] = knc.astype(BF16)
                vbuf[e, 0:past, :] = v1c.astype(BF16)

    qs = [q_ref[:, e * LANES:(e + 1) * LANES] for e in range(heads)]
    _flash_heads(qs, lambda e, c0, c1: (kbuf[e, c0:c1, :], vbuf[e, c0:c1, :]), past + seq, o_ref)


def _mla_call(qa, ckv, akr, lw, batch, seq, cache, rope_tabs):
    n = batch * seq
    tq = min(Q_BLOCK, seq)
    nq = seq // tq
    past = 0 if cache is None else cache[0].shape[1]
    heads = MLA_HEADS_PER_STEP if nq > 1 else A_HEADS
    in_specs = [
        pl.BlockSpec((tq, heads * LANES), lambda b, hg, qi: (b * nq + qi, hg)),
        pl.BlockSpec((seq, LANES), lambda b, hg, qi: (b, 0)),
        pl.BlockSpec((seq, LANES), lambda b, hg, qi: (b, 0)),
    ]
    args = [qa, ckv, akr]
    if past:
        in_specs += [pl.BlockSpec((None, past, LANES), lambda b, hg, qi: (b, 0, 0))] * 2
        args += [cache[0], cache[1]]
    in_specs += [
        pl.BlockSpec((A_KV_LORA, heads * LANES), lambda b, hg, qi: (0, hg)),
        pl.BlockSpec((1, LANES), lambda b, hg, qi: (0, 0)),
    ]
    args += [lw["w_ukv"], lw["a_knorm_g"]]
    if rope_tabs is not None:
        cos, sin = rope_tabs
        in_specs += [pl.BlockSpec((seq, LANES), lambda b, hg, qi: (0, 0))] * 2
        args += [cos, sin]
    return pl.pallas_call(
        functools.partial(_mla_kernel, past=past, rotary=rope_tabs is not None, heads=heads),
        out_shape=jax.ShapeDtypeStruct((n, A_HEADS * A_V), F32),
        grid=(batch, A_HEADS // heads, nq),
        in_specs=in_specs,
        out_specs=pl.BlockSpec((tq, heads * A_V), lambda b, hg, qi: (b * nq + qi, hg)),
        scratch_shapes=[pltpu.VMEM((heads, past + seq, LANES), BF16),
                        pltpu.VMEM((heads, past + seq, LANES), BF16)],
        compiler_params=_params(3),
        name="mla_attn",
    )(*args)


def _gqa_kernel(*refs, past, groups):
    q_ref, k_ref, v_ref = refs[:3]
    pos = 3
    if past:
        kc_ref, vc_ref = refs[pos:pos + 2]
        pos += 2
    o_ref, kbuf, vbuf = refs[pos:]
    seq = k_ref.shape[0]
    qw = G_GROUP * G_HD

    @pl.when(pl.program_id(2) == 0)
    def _():
        for gi in range(groups):
            grp = pl.program_id(1) * groups + gi

            def low_half(x):
                return jnp.where(grp == 0, x, pltpu.roll(x, G_HD, 1))

            def fill(r0, k, v):
                low = _lane(k.shape) < G_HD
                k_lo = low_half(k)
                rows = k.shape[0]
                kbuf[gi, r0:r0 + rows, :] = jnp.where(low, k_lo, pltpu.roll(k_lo, G_HD, 1)).astype(BF16)
                vbuf[gi, r0:r0 + rows, :] = jnp.where(low, low_half(v), 1.0).astype(BF16)

            fill(past, k_ref[...], v_ref[...])
            if past:
                fill(0, kc_ref[...], vc_ref[...])

    qs = []
    for gi in range(groups):
        for j in range(G_GROUP):
            col = q_ref[:, gi * qw + (j // 2) * LANES:gi * qw + (j // 2 + 1) * LANES]
            keep = (_lane(col.shape) < G_HD) == (j % 2 == 0)
            qs.append(jnp.where(keep, col, jnp.zeros_like(col)))
    _flash_heads(qs, lambda e, c0, c1: (kbuf[e // G_GROUP, c0:c1, :], vbuf[e // G_GROUP, c0:c1, :]),
                 past + seq, o_ref)


def _gqa_call(gq, gk, gv, batch, seq, cache):
    n = batch * seq
    tq = min(Q_BLOCK, seq)
    nq = seq // tq
    past = 0 if cache is None else cache[0].shape[1]
    groups = GQA_GROUPS_PER_STEP if nq > 1 else G_KV_HEADS
    qw = G_GROUP * G_HD
    kvw = G_KV_HEADS * G_HD
    in_specs = [
        pl.BlockSpec((tq, groups * qw), lambda b, g, qi: (b * nq + qi, g)),
        pl.BlockSpec((seq, kvw), lambda b, g, qi: (b, 0)),
        pl.BlockSpec((seq, kvw), lambda b, g, qi: (b, 0)),
    ]
    args = [gq, gk, gv]
    if past:
        in_specs += [pl.BlockSpec((None, past, kvw), lambda b, g, qi: (b, 0, 0))] * 2
        args += [cache[0], cache[1]]
    return pl.pallas_call(
        functools.partial(_gqa_kernel, past=past, groups=groups),
        out_shape=jax.ShapeDtypeStruct((n, G_HEADS * G_HD), F32),
        grid=(batch, G_KV_HEADS // groups, nq),
        in_specs=in_specs,
        out_specs=pl.BlockSpec((tq, groups * qw), lambda b, g, qi: (b * nq + qi, g)),
        scratch_shapes=[pltpu.VMEM((groups, past + seq, LANES), BF16),
                        pltpu.VMEM((groups, past + seq, LANES), BF16)],
        compiler_params=_params(3),
        name="gqa_attn",
    )(*args)


def _merge_ffn_kernel(x_ref, ym_ref, ya_ref, yg_ref, gates_ref, mod_ref, g2_ref, wb_ref, wo_ref, wfi_ref,
                      wfo_ref, o_ref, *, ff_chunk):
    x = x_ref[...]
    d = x.shape[-1]
    mixed = None
    for i, y_ref in enumerate((ym_ref, ya_ref, yg_ref)):
        br = _dot(y_ref[...].astype(BF16), wb_ref[i]) * gates_ref[:, i * d:(i + 1) * d]
        mixed = br if mixed is None else mixed + br
    gt1 = mod_ref[2:3, :]
    x1 = x + gt1 * _dot(mixed.astype(BF16), wo_ref[...])

    sh2 = mod_ref[3:4, :]
    sc2 = mod_ref[4:5, :]
    gt2 = mod_ref[5:6, :]
    h2 = ((_rms(x1, d) * g2_ref[...]) * (1.0 + sc2) + sh2).astype(BF16)
    d_ff = wfo_ref.shape[0]
    acc = None
    for c0 in range(0, d_ff, ff_chunk):
        ug = _dot(h2, wfi_ref[:, c0:c0 + ff_chunk])
        uv = _dot(h2, wfi_ref[:, d_ff + c0:d_ff + c0 + ff_chunk])
        act = (ug * _sigmoid(ug) * uv).astype(BF16)
        part = _dot(act, wfo_ref[c0:c0 + ff_chunk, :])
        acc = part if acc is None else acc + part
    o_ref[...] = x1 + gt2 * acc


def _merge_ffn_call(x, ym, ya, yg, gates, mod, lw, mod_row0, rows_per_mod):
    n, d = x.shape
    tm = TOKEN_BLOCK
    d_ff = lw["w_ffn_out"].shape[0]
    ff_chunk = d_ff // 2 if (d_ff // 2) % LANES == 0 else d_ff

    def mod_idx(i):
        return (mod_row0 + (i * tm) // rows_per_mod, 0, 0)

    def rows(width):
        return pl.BlockSpec((tm, width), lambda i: (i, 0))

    return pl.pallas_call(
        functools.partial(_merge_ffn_kernel, ff_chunk=ff_chunk),
        out_shape=jax.ShapeDtypeStruct((n, d), F32),
        grid=(n // tm,),
        in_specs=[
            rows(d), rows(BRANCH_WIDTH), rows(BRANCH_WIDTH), rows(BRANCH_WIDTH), rows(N_BRANCH * d),
            pl.BlockSpec((None, N_MOD, d), mod_idx),
            _resident((1, d)),
            _resident((N_BRANCH, BRANCH_WIDTH, d)),
            _resident((d, d)),
            _resident((d, 2 * d_ff)),
            _resident((d_ff, d)),
        ],
        out_specs=rows(d),
        compiler_params=_params(1),
        name="merge_ffn",
    )(x, ym, ya, yg, gates, mod, lw["norm2_g"], lw["w_branch"], lw["w_out"], lw["w_ffn_in"], lw["w_ffn_out"])


def _pad_cols(w, width):
    return jnp.pad(w, ((0, 0), (0, width - w.shape[1])))


def _mla_q_layout(a):
    lead = a.shape[:-1]
    a = a.reshape(lead + (A_HEADS, A_QK))
    z = jnp.zeros(lead + (A_HEADS, LANES - A_QK), a.dtype)
    return jnp.concatenate([a[..., A_NOPE:], z, a[..., :A_NOPE]], axis=-1).reshape(lead + (A_HEADS * LANES,))


def _layer_weights(l, w_in, b_mgate, norm1_g, m_norm_g, a_qlora_g, a_kvlora_g, w_uq, w_ukv, a_qnorm_g,
                   a_knorm_g, g_qnorm_g, g_knorm_g, w_branch, w_out, norm2_g, w_ffn_in, w_ffn_out):
    d = w_in.shape[1]
    hw = M_HEADS * M_DK
    sizes = (N_BRANCH * d, hw, hw, hw, hw, 4 * M_HEADS, A_Q_LORA, A_KV_LORA, A_ROPE, G_HEADS * G_HD,
             G_KV_HEADS * G_HD, G_KV_HEADS * G_HD)
    splits = np.cumsum(sizes)[:-1].tolist()
    (wg, wmq, wmk, wmv, wmo, wmg, waq, wakv, wakr, wgq, wgk, wgv) = jnp.split(w_in[l], splits, axis=1)
    w_in_p = jnp.concatenate(
        [wg, wmq, wmk, wmv, wmo, _pad_cols(wmg, LANES), waq, wakv, _pad_cols(wakr, LANES), wgq, wgk, wgv],
        axis=1).astype(BF16)
    ukv = w_ukv[l].reshape(A_KV_LORA, A_HEADS, A_NOPE + A_V)
    ukv = jnp.concatenate([ukv[..., A_NOPE:], ukv[..., :A_NOPE]], axis=-1).reshape(A_KV_LORA, A_HEADS * LANES)
    qg = _mla_q_layout(jnp.tile(a_qnorm_g[l], A_HEADS)[None, :])[:, :LANES]
    kg = _mla_q_layout(jnp.tile(a_knorm_g[l], A_HEADS)[None, :])[:, :LANES]
    return dict(
        w_in=w_in_p,
        b_mgate=_pad_cols(b_mgate[l][None, :], LANES),
        norm1_g=norm1_g[l][None, :],
        m_norm_g=m_norm_g[l][None, :],
        a_qlora_g=a_qlora_g[l][None, :],
        a_kvlora_g=a_kvlora_g[l][None, :],
        w_uq=_mla_q_layout(w_uq[l]).astype(BF16),
        w_ukv=ukv.astype(BF16),
        a_qnorm_g=qg,
        a_knorm_g=kg,
        g_qnorm_g=jnp.tile(g_qnorm_g[l], LANES // G_HD)[None, :],
        g_knorm_g=jnp.tile(g_knorm_g[l], G_KV_HEADS)[None, :],
        w_branch=w_branch[l].astype(BF16),
        w_out=w_out[l].astype(BF16),
        norm2_g=norm2_g[l][None, :],
        w_ffn_in=w_ffn_in[l].astype(BF16),
        w_ffn_out=w_ffn_out[l].astype(BF16),
    )


def _axial_angles(seq, rot_dim):
    n_freq = rot_dim // 4
    freqs = ROPE_BASE ** (-jnp.arange(n_freq, dtype=F32) / n_freq)
    t = jnp.arange(seq)
    row = (t // GRID_W).astype(F32)
    col = (t % GRID_W).astype(F32)
    return jnp.concatenate([row[:, None] * freqs, col[:, None] * freqs], axis=-1)


def _rope_tables(seq):
    ang = _axial_angles(seq, A_ROPE)
    one = jnp.ones((seq, LANES - A_ROPE), F32)
    mla_cos = jnp.concatenate([jnp.cos(ang), jnp.cos(ang), one], axis=-1)
    mla_sin = jnp.concatenate([-jnp.sin(ang), jnp.sin(ang), 0.0 * one], axis=-1)
    ang = _axial_angles(seq, G_HD)
    cos = jnp.concatenate([jnp.cos(ang), jnp.cos(ang)], axis=-1)
    sin = jnp.concatenate([-jnp.sin(ang), jnp.sin(ang)], axis=-1)
    gqa = (jnp.tile(cos, (1, LANES // G_HD)), jnp.tile(sin, (1, LANES // G_HD)))
    return (mla_cos, mla_sin), gqa


def _layer(x, mod, lw, batch, seq, mod_row0, rows_per_mod, ctx, rope):
    (gates, qkv, og, mg, qa, ckv, akr, gq, gk, gv) = _inproj_call(x, mod, lw, mod_row0, rows_per_mod, rope, seq)
    if ctx is None:
        ym, state = _mlstm_call(qkv, og, mg, lw, batch, seq, None, True)
        ya = _mla_call(qa, ckv, akr, lw, batch, seq, None, None)
        yg = _gqa_call(gq, gk, gv, batch, seq, None)
        new_ctx = dict(state=state, ckv=ckv, kr=akr[:, :A_ROPE], gk=gk, gv=gv)
    else:
        ym, _ = _mlstm_call(qkv, og, mg, lw, batch, seq, ctx["mlstm"], False)
        ya = _mla_call(qa, ckv, akr, lw, batch, seq, ctx["mla"], rope[0])
        yg = _gqa_call(gq, gk, gv, batch, seq, ctx["gqa"])
        new_ctx = None
    x = _merge_ffn_call(x, ym, ya, yg, gates, mod, lw, mod_row0, rows_per_mod)
    return x, new_ctx


def kernel(x_prompt, x_sample, state_mlstm_C, state_mlstm_n, state_mlstm_m, cache_mla_ckv, cache_mla_krope,
           cache_gqa_k, cache_gqa_v, c, c_ctx, w_mod, b_mod, norm1_g, w_in, b_mgate, m_norm_g, a_qlora_g,
           a_kvlora_g, w_uq, w_ukv, a_qnorm_g, a_knorm_g, g_qnorm_g, g_knorm_g, w_branch, w_out, norm2_g,
           w_ffn_in, w_ffn_out):
    batch, seq, d = x_prompt.shape
    dbatch, dseq, _ = x_sample.shape
    depth = w_in.shape[0]
    past = cache_mla_ckv.shape[2]

    n_rows = -(-(1 + dbatch) // SUBLANES) * SUBLANES
    cond = jnp.concatenate([c_ctx[None, :], c, jnp.zeros((n_rows - 1 - dbatch, d), F32)], axis=0)
    mod_all = _mod_call(cond, w_mod, b_mod).reshape(depth, n_rows, N_MOD, d)

    rope = _rope_tables(dseq)
    xp = x_prompt.reshape(batch * seq, d)
    xs = x_sample.reshape(dbatch * dseq, d)
    ctx_layers = []
    for l in range(depth):
        lw = _layer_weights(l, w_in, b_mgate, norm1_g, m_norm_g, a_qlora_g, a_kvlora_g, w_uq, w_ukv,
                            a_qnorm_g, a_knorm_g, g_qnorm_g, g_knorm_g, w_branch, w_out, norm2_g, w_ffn_in,
                            w_ffn_out)
        xp, st = _layer(xp, mod_all[l], lw, batch, seq, 0, batch * seq, None, None)
        ctx_layers.append(st)
        ctx = dict(
            mlstm=(state_mlstm_C[:, l], state_mlstm_n[:, l], state_mlstm_m[:, l]),
            mla=(cache_mla_ckv[:, l], _pad_cols(cache_mla_krope[:, l].reshape(dbatch * past, A_ROPE), LANES)
                 .reshape(dbatch, past, LANES)),
            gqa=(cache_gqa_k[:, l].reshape(dbatch, past, G_KV_HEADS * G_HD),
                 cache_gqa_v[:, l].reshape(dbatch, past, G_KV_HEADS * G_HD)),
        )
        xs, _ = _layer(xs, mod_all[l], lw, dbatch, dseq, 1, dseq, ctx, rope)

    def stack(fn):
        return jnp.stack([fn(s) for s in ctx_layers], axis=1)

    new_c = stack(lambda s: s["state"][0])
    new_n = stack(lambda s: s["state"][1])
    new_m = stack(lambda s: s["state"][2])
    new_ckv = stack(lambda s: s["ckv"].reshape(batch, seq, A_KV_LORA))
    new_kr = stack(lambda s: s["kr"].reshape(batch, seq, A_ROPE))
    new_gk = stack(lambda s: s["gk"].reshape(batch, seq, G_KV_HEADS, G_HD))
    new_gv = stack(lambda s: s["gv"].reshape(batch, seq, G_KV_HEADS, G_HD))
    return (xp.reshape(batch, seq, d), xs.reshape(dbatch, dseq, d), new_c, new_n, new_m, new_ckv, new_kr,
            new_gk, new_gv)
```

```python
import functools

import numpy as np
import jax
import jax.numpy as jnp
from jax import lax
from jax.experimental import pallas as pl
from jax.experimental.pallas import tpu as pltpu

F32 = jnp.float32
BF16 = jnp.bfloat16

LANES = 128
SUBLANES = 8
VMEM_LIMIT_BYTES = 56 * 1024 * 1024

EPS = 1e-6
ROPE_BASE = 10000.0
GRID_W = 64

M_HEADS = 4
M_DK = 128
M_DV = 128
A_HEADS = 8
A_NOPE = 64
A_ROPE = 32
A_QK = A_NOPE + A_ROPE
A_V = 64
A_Q_LORA = 256
A_KV_LORA = 128
G_HEADS = 8
G_KV_HEADS = 2
G_GROUP = G_HEADS // G_KV_HEADS
G_HD = 64
N_BRANCH = 3
BRANCH_WIDTH = 512
N_MOD = 6

TOKEN_BLOCK = 256
Q_BLOCK = 256
KEY_BLOCK = 256
MLA_HEADS_PER_STEP = 4
GQA_GROUPS_PER_STEP = 1
MLSTM_CHUNK = 128
MLSTM_HEADS_PER_STEP = 4
LOG2E = 1.4426950408889634


def _params(n_axes):
    return pltpu.CompilerParams(dimension_semantics=("arbitrary",) * n_axes,
                                vmem_limit_bytes=VMEM_LIMIT_BYTES)


def _resident(shape):
    nd = len(shape)
    return pl.BlockSpec(shape, lambda *_: (0,) * nd, pipeline_mode=pl.Buffered(1))


def _lane(shape, axis=None):
    return lax.broadcasted_iota(jnp.int32, shape, len(shape) - 1 if axis is None else axis)


def _dot(a, b):
    return jnp.dot(a, b, preferred_element_type=F32)


def _dot_nt(a, b):
    return lax.dot_general(a, b, (((1,), (1,)), ((), ())), preferred_element_type=F32)


def _split3(a):
    hi = a.astype(BF16)
    r1 = a - hi.astype(F32)
    mid = r1.astype(BF16)
    lo = (r1 - mid.astype(F32)).astype(BF16)
    return hi, mid, lo


def _dot01(a, m01):
    hi, mid, lo = _split3(a)
    return _dot(hi, m01) + _dot(mid, m01) + _dot(lo, m01)


def _dot01_left(m01, a):
    hi, mid, lo = _split3(a)
    return _dot(m01, hi) + _dot(m01, mid) + _dot(m01, lo)


def _sigmoid(x):
    return 1.0 / (1.0 + jnp.exp(-x))


def _log_sigmoid(x):
    return jnp.minimum(x, 0.0) - jnp.log(1.0 + jnp.exp(-jnp.abs(x)))


def _rms(x, width):
    ms = jnp.sum(x * x, axis=-1, keepdims=True) * (1.0 / width)
    return x * lax.rsqrt(ms + EPS)


def _rope(x, cos, sin_signed, half, period):
    n = x.shape[-1]
    first = (_lane(x.shape) % period) < half
    swapped = jnp.where(first, pltpu.roll(x, n - half, x.ndim - 1), pltpu.roll(x, half, x.ndim - 1))
    return x * cos + swapped * sin_signed


def _mod_kernel(c_ref, w_ref, b_ref, o_ref):
    c = c_ref[...]
    a = c * _sigmoid(c)
    hi, mid, lo = _split3(a)
    w = w_ref[...]
    whi = w.astype(BF16)
    wlo = (w - whi.astype(F32)).astype(BF16)
    acc = _dot(hi, whi) + _dot(mid, whi) + _dot(hi, wlo) + _dot(lo, whi) + _dot(mid, wlo)
    o_ref[...] = acc + b_ref[...]


def _mod_call(cond, w_mod, b_mod):
    depth, d, n = w_mod.shape
    rows = cond.shape[0]
    tn = 1536
    return pl.pallas_call(
        _mod_kernel,
        out_shape=jax.ShapeDtypeStruct((depth, rows, n), F32),
        grid=(depth, n // tn),
        in_specs=[
            pl.BlockSpec((rows, d), lambda l, j: (0, 0)),
            pl.BlockSpec((None, d, tn), lambda l, j: (l, 0, j)),
            pl.BlockSpec((None, 1, tn), lambda l, j: (l, 0, j)),
        ],
        out_specs=pl.BlockSpec((None, rows, tn), lambda l, j: (l, 0, j)),
        compiler_params=_params(2),
        name="adaln_mod",
    )(cond, w_mod, b_mod.reshape(depth, 1, n))


_GATE0, _GATE1 = 0, 3072
_MQ0 = 3072
_MK0 = 3584
_MV0 = 4096
_MO0 = 4608
_MG0 = 5120
_AQ0 = 5248
_AKV0 = 5504
_AKR0 = 5632
_GQ0 = 5760
_GK0 = 6272
_GV0 = 6400
_WIN_COLS = 6528


def _head_pair_ms(x):
    low = (_lane(x.shape) % LANES) < G_HD
    sq = x * x
    cols = []
    for c0 in range(0, x.shape[-1], LANES):
        s = sq[:, c0:c0 + LANES]
        lo = jnp.sum(jnp.where(low[:, c0:c0 + LANES], s, 0.0), axis=-1, keepdims=True)
        hi = jnp.sum(jnp.where(low[:, c0:c0 + LANES], 0.0, s), axis=-1, keepdims=True)
        cols.append(jnp.where(low[:, c0:c0 + LANES], lo, hi))
    ms = cols[0] if len(cols) == 1 else jnp.concatenate(cols, axis=-1)
    return ms * (1.0 / G_HD)


def _inproj_kernel(*refs, rotary):
    (x_ref, mod_ref, g1_ref, w_ref, bmg_ref, gql_ref, wuq_ref, gkvl_ref, gkn_ref, aqn_ref, gqn_ref) = refs[:11]
    pos = 11
    if rotary:
        ca_ref, sa_ref, cg_ref, sg_ref = refs[pos:pos + 4]
        pos += 4
    (gates_ref, qkv_ref, og_ref, mg_ref, qa_ref, ckv_ref, akr_ref, gq_ref, gk_ref, gv_ref) = refs[pos:]
    x = x_ref[...]
    d = x.shape[-1]
    sh1 = mod_ref[0:1, :]
    sc1 = mod_ref[1:2, :]
    h = (_rms(x, d) * g1_ref[...]) * (1.0 + sc1) + sh1
    hb = h.astype(BF16)

    def proj(c0, width):
        return _dot(hb, w_ref[:, c0:c0 + width])

    aq = _rms(proj(_AQ0, A_Q_LORA), A_Q_LORA) * gql_ref[...]
    qa = _dot(aq.astype(BF16), wuq_ref[...])
    for hd in range(A_HEADS):
        qh = _rms(qa[:, hd * LANES:(hd + 1) * LANES], A_QK) * aqn_ref[...]
        if rotary:
            qh = _rope(qh, ca_ref[...], sa_ref[...], A_ROPE // 2, LANES)
        qa_ref[:, hd * LANES:(hd + 1) * LANES] = (qh * (A_QK ** -0.5 * LOG2E)).astype(qa_ref.dtype)
    ckv_ref[...] = _rms(proj(_AKV0, A_KV_LORA), A_KV_LORA) * gkvl_ref[...]
    akr_ref[...] = proj(_AKR0, LANES)

    for c0 in range(0, G_HEADS * G_HD, LANES):
        gq = proj(_GQ0 + c0, LANES)
        gq = gq * lax.rsqrt(_head_pair_ms(gq) + EPS) * gqn_ref[...]
        if rotary:
            gq = _rope(gq, cg_ref[...], sg_ref[...], G_HD // 2, G_HD)
        gq_ref[:, c0:c0 + LANES] = (gq * (G_HD ** -0.5 * LOG2E)).astype(gq_ref.dtype)
    gk = proj(_GK0, G_KV_HEADS * G_HD)
    gk = gk * lax.rsqrt(_head_pair_ms(gk) + EPS) * gkn_ref[...]
    if rotary:
        gk = _rope(gk, cg_ref[...], sg_ref[...], G_HD // 2, G_HD)
    gk_ref[...] = gk
    gv_ref[...] = proj(_GV0, G_KV_HEADS * G_HD)

    hw = M_HEADS * M_DK
    qkv_ref[:, 0:hw] = proj(_MQ0, hw).astype(qkv_ref.dtype)
    qkv_ref[:, hw:2 * hw] = (proj(_MK0, hw) * (M_DK ** -0.5)).astype(qkv_ref.dtype)
    qkv_ref[:, 2 * hw:3 * hw] = proj(_MV0, hw).astype(qkv_ref.dtype)
    og_ref[...] = _sigmoid(proj(_MO0, hw)).astype(og_ref.dtype)
    mg_ref[...] = proj(_MG0, LANES) + bmg_ref[...]
    gates_ref[...] = _sigmoid(proj(_GATE0, _GATE1 - _GATE0)).astype(gates_ref.dtype)


def _inproj_call(x, mod, lw, mod_row0, rows_per_mod, rope, seq):
    n, d = x.shape
    tm = TOKEN_BLOCK
    hw = M_HEADS * M_DK

    def mod_idx(i):
        return (mod_row0 + (i * tm) // rows_per_mod, 0, 0)

    def rows(width):
        return pl.BlockSpec((tm, width), lambda i: (i, 0))

    in_specs = [
        rows(d),
        pl.BlockSpec((None, N_MOD, d), mod_idx),
        _resident((1, d)),
        _resident((d, _WIN_COLS)),
        _resident((1, LANES)),
        _resident((1, A_Q_LORA)),
        _resident((A_Q_LORA, A_HEADS * LANES)),
        _resident((1, A_KV_LORA)),
        _resident((1, LANES)),
        _resident((1, LANES)),
        _resident((1, LANES)),
    ]
    args = [x, mod, lw["norm1_g"], lw["w_in"], lw["b_mgate"], lw["a_qlora_g"], lw["w_uq"], lw["a_kvlora_g"],
            lw["g_knorm_g"], lw["a_qnorm_g"], lw["g_qnorm_g"]]
    if rope is not None:
        blocks_per_seq = seq // tm
        in_specs += [pl.BlockSpec((tm, LANES), lambda i: (i % blocks_per_seq, 0))] * 4
        args += [rope[0][0], rope[0][1], rope[1][0], rope[1][1]]
    out_widths = [3 * d, 3 * hw, hw, LANES, A_HEADS * LANES, A_KV_LORA, LANES, G_HEADS * G_HD,
                  G_KV_HEADS * G_HD, G_KV_HEADS * G_HD]
    out_dtypes = [BF16, BF16, BF16, F32, BF16, F32, F32, BF16, F32, F32]
    return pl.pallas_call(
        functools.partial(_inproj_kernel, rotary=rope is not None),
        out_shape=[jax.ShapeDtypeStruct((n, w), t) for w, t in zip(out_widths, out_dtypes)],
        grid=(n // tm,),
        in_specs=in_specs,
        out_specs=[rows(w) for w in out_widths],
        compiler_params=_params(1),
        name="in_proj",
    )(*args)


def _mlstm_kernel(*refs, n_chunks, heads, has_init, emit_state):
    q_ref, k_ref, v_ref, og_ref, gcol_ref, grow_ref, gn_ref = refs[:7]
    pos = 7
    if has_init:
        c0_ref, n0_ref, m0_ref = refs[pos:pos + 3]
        pos += 3
    y_ref = refs[pos]
    pos += 1
    if emit_state:
        cf_ref, nf_ref, mf_ref = refs[pos:pos + 3]
        pos += 3
    (pcol_scr, lfc_scr, prow_scr, lfr_scr, u_scr, nu_scr, gm_scr, tot_scr, cs_scr, ns_scr, ms_scr,
     c_scr, n_scr, m_scr) = refs[pos:]

    ln = MLSTM_CHUNK
    row = lax.broadcasted_iota(jnp.int32, (ln, ln), 0)
    col = lax.broadcasted_iota(jnp.int32, (ln, ln), 1)
    lower = col <= row
    upper = col >= row
    tril = jnp.where(lower, 1.0, 0.0).astype(BF16)
    triu = jnp.where(upper, 1.0, 0.0).astype(BF16)
    ones = jnp.ones((ln, ln), BF16)

    def gate_lanes(hh, d):
        return 2 * d * heads + hh, (2 * d + 1) * heads + hh

    def summaries(c, carry):
        r0 = pl.multiple_of(c * ln, ln)
        g = gcol_ref[pl.ds(r0, ln), :] * LOG2E
        lf = _log_sigmoid(gcol_ref[pl.ds(r0, ln), :]) * LOG2E
        pc = _dot01_left(tril, lf)
        pcol_scr[pl.ds(r0, ln), :] = pc
        lfc_scr[pl.ds(r0, ln), :] = lf
        lfr = _log_sigmoid(grow_ref[c]) * LOG2E
        prow_scr[c] = _dot01(lfr, triu)
        lfr_scr[c] = lfr
        tot_all = pc[ln - 1:ln, :]
        for hh in range(heads):
            k = k_ref[pl.ds(r0, ln), hh * LANES:(hh + 1) * LANES].astype(F32)
            v = v_ref[pl.ds(r0, ln), hh * LANES:(hh + 1) * LANES]
            for d in range(2):
                ji, jf = gate_lanes(hh, d)
                tot = tot_all[:, jf:jf + 1]
                b_col = pc[:, jf:jf + 1] if d == 0 else tot - pc[:, jf:jf + 1] + lf[:, jf:jf + 1]
                gg = jnp.broadcast_to(tot - b_col + g[:, ji:ji + 1], (ln, ln))
                gmax = jnp.max(gg, axis=0, keepdims=True)
                kw = k * jnp.exp2(gg - gmax)
                u_scr[hh, d, c] = _dot(kw.T.astype(BF16), v)
                nu_scr[hh, d, c] = jnp.sum(kw, axis=0, keepdims=True)
                gm_scr[hh, d, c] = gmax
                tot_scr[hh, d, c] = jnp.broadcast_to(tot, (1, ln))
        return carry

    lax.fori_loop(0, n_chunks, summaries, 0)

    for hh in range(heads):
        for d in range(2):
            if has_init:
                c_scr[hh, d] = c0_ref[d, hh]
                n_scr[hh, d] = n0_ref[d, hh]
                m_scr[hh, d] = m0_ref[d, hh] * LOG2E
            else:
                c_scr[hh, d] = jnp.zeros((M_DK, M_DV), F32)
                n_scr[hh, d] = jnp.zeros((1, M_DK), F32)
                m_scr[hh, d] = jnp.zeros((1, ln), F32)

    def scan(j, carry):
        for hh in range(heads):
            for d in range(2):
                c = j if d == 0 else n_chunks - 1 - j
                cst = c_scr[hh, d]
                nst = n_scr[hh, d]
                mst = m_scr[hh, d]
                cs_scr[hh, d, c] = cst.astype(BF16)
                ns_scr[hh, d, c] = nst
                ms_scr[hh, d, c] = mst
                gmax = gm_scr[hh, d, c]
                total = tot_scr[hh, d, c] + mst
                m_new = jnp.maximum(total, gmax)
                decay = jnp.exp2(total - m_new)
                scale = jnp.exp2(gmax - m_new)
                c_scr[hh, d] = cst * decay + u_scr[hh, d, c] * scale
                n_scr[hh, d] = nst * decay + nu_scr[hh, d, c] * scale
                m_scr[hh, d] = m_new
        return carry

    lax.fori_loop(0, n_chunks, scan, 0)

    def readout(c, carry):
        r0 = pl.multiple_of(c * ln, ln)
        pc = pcol_scr[pl.ds(r0, ln), :]
        lf = lfc_scr[pl.ds(r0, ln), :]
        gr = grow_ref[c] * LOG2E
        pr = prow_scr[c]
        lfr = lfr_scr[c]
        tot_all = pc[ln - 1:ln, :]
        for hh in range(heads):
            q = q_ref[pl.ds(r0, ln), hh * LANES:(hh + 1) * LANES]
            k = k_ref[pl.ds(r0, ln), hh * LANES:(hh + 1) * LANES]
            v = v_ref[pl.ds(r0, ln), hh * LANES:(hh + 1) * LANES]
            qk = _dot_nt(q, k)
            hsum = None
            for d in range(2):
                ji, jf = gate_lanes(hh, d)
                if d == 0:
                    b_col = pc[:, jf:jf + 1]
                    r_row = gr[ji:ji + 1, :] - pr[jf:jf + 1, :]
                    mask = lower
                else:
                    tot = tot_all[:, jf:jf + 1]
                    b_col = tot - pc[:, jf:jf + 1] + lf[:, jf:jf + 1]
                    r_row = gr[ji:ji + 1, :] - (tot - pr[jf:jf + 1, :] + lfr[jf:jf + 1, :])
                    mask = upper
                b_colb = jnp.broadcast_to(b_col, (ln, ln))
                d_log = jnp.where(mask, b_colb + r_row, -jnp.inf)
                dmax = jnp.broadcast_to(jnp.max(d_log, axis=-1, keepdims=True), (ln, ln))
                sb = (qk * jnp.exp2(d_log - dmax)).astype(BF16)
                num_loc = _dot(sb, v)
                den_loc = _dot(sb, ones)
                qc = _dot(q, cs_scr[hh, d, c])
                qn = _dot_nt(q, jnp.broadcast_to(ns_scr[hh, d, c], (ln, M_DK)).astype(BF16))
                inter = b_colb + ms_scr[hh, d, c]
                m_t = jnp.maximum(inter, dmax)
                w_inter = jnp.exp2(inter - m_t)
                w_loc = jnp.exp2(dmax - m_t)
                num = qc * w_inter + num_loc * w_loc
                den = qn * w_inter + den_loc * w_loc
                hc = num / jnp.maximum(jnp.abs(den), jnp.exp2(-m_t))
                hsum = hc if hsum is None else hsum + hc
            y_ref[pl.ds(r0, ln), hh * LANES:(hh + 1) * LANES] = (
                _rms(hsum, M_DV) * gn_ref[...] * og_ref[pl.ds(r0, ln), hh * LANES:(hh + 1) * LANES]
            ).astype(y_ref.dtype)
        return carry

    lax.fori_loop(0, n_chunks, readout, 0)

    if emit_state:
        for hh in range(heads):
            for d in range(2):
                cf_ref[d, hh] = c_scr[hh, d]
                nf_ref[d, hh] = n_scr[hh, d]
                mf_ref[d, hh] = m_scr[hh, d] * (1.0 / LOG2E)


def _mlstm_call(qkv, og, mg, lw, batch, seq, init, emit_state):
    n = batch * seq
    ln = MLSTM_CHUNK
    nc = seq // ln
    h = M_HEADS
    hs = MLSTM_HEADS_PER_STEP
    ng = h // hs
    g4 = mg[:, :4 * h].reshape(n, 4, ng, hs).transpose(2, 0, 1, 3).reshape(ng, n, 4 * hs)
    gcol = jnp.pad(g4, ((0, 0), (0, 0), (0, LANES - 4 * hs)))
    grow = jnp.pad(g4.reshape(ng, batch, nc, ln, 4 * hs).transpose(0, 1, 2, 4, 3),
                   ((0, 0),) * 3 + ((0, 4 * h - 4 * hs), (0, 0)))

    def tok(colblock):
        return pl.BlockSpec((seq, hs * LANES), lambda b, hg: (b, colblock(hg)))

    in_specs = [
        tok(lambda hg: hg), tok(lambda hg: ng + hg), tok(lambda hg: 2 * ng + hg),
        tok(lambda hg: hg),
        pl.BlockSpec((None, seq, LANES), lambda b, hg: (hg, b, 0)),
        pl.BlockSpec((None, None, nc, 4 * h, ln), lambda b, hg: (hg, b, 0, 0, 0)),
        pl.BlockSpec((1, M_DV), lambda b, hg: (0, 0)),
    ]
    args = [qkv, qkv, qkv, og, gcol, grow, lw["m_norm_g"]]
    state_specs = [
        pl.BlockSpec((None, 2, hs, M_DK, M_DV), lambda b, hg: (b, 0, hg, 0, 0)),
        pl.BlockSpec((None, 2, hs, 1, M_DK), lambda b, hg: (b, 0, hg, 0, 0)),
        pl.BlockSpec((None, 2, hs, 1, ln), lambda b, hg: (b, 0, hg, 0, 0)),
    ]
    if init is not None:
        c0, n0, m0 = init
        in_specs += state_specs
        args += [c0, n0.reshape(batch, 2, h, 1, M_DK),
                 jnp.broadcast_to(m0[..., None, None], (batch, 2, h, 1, ln))]
    out_shape = [jax.ShapeDtypeStruct((n, h * M_DV), BF16)]
    out_specs = [tok(lambda hg: hg)]
    if emit_state:
        out_shape += [
            jax.ShapeDtypeStruct((batch, 2, h, M_DK, M_DV), F32),
            jax.ShapeDtypeStruct((batch, 2, h, 1, M_DK), F32),
            jax.ShapeDtypeStruct((batch, 2, h, 1, ln), F32),
        ]
        out_specs += state_specs
    per = (hs, 2, nc)
    outs = pl.pallas_call(
        functools.partial(_mlstm_kernel, n_chunks=nc, heads=hs, has_init=init is not None,
                          emit_state=emit_state),
        out_shape=out_shape,
        grid=(batch, ng),
        in_specs=in_specs,
        out_specs=out_specs,
        scratch_shapes=[
            pltpu.VMEM((seq, LANES), F32), pltpu.VMEM((seq, LANES), F32),
            pltpu.VMEM((nc, 4 * h, ln), F32), pltpu.VMEM((nc, 4 * h, ln), F32),
            pltpu.VMEM(per + (M_DK, M_DV), F32), pltpu.VMEM(per + (1, M_DK), F32),
            pltpu.VMEM(per + (1, ln), F32), pltpu.VMEM(per + (1, ln), F32),
            pltpu.VMEM(per + (M_DK, M_DV), BF16), pltpu.VMEM(per + (1, M_DK), F32), pltpu.VMEM(per + (1, ln), F32),
            pltpu.VMEM((hs, 2, M_DK, M_DV), F32), pltpu.VMEM((hs, 2, 1, M_DK), F32), pltpu.VMEM((hs, 2, 1, ln), F32),
        ],
        compiler_params=_params(2),
        name="mlstm",
    )(*args)
    if emit_state:
        ym, cf, nf, mf = outs
        return ym, (cf, nf[:, :, :, 0, :], mf[:, :, :, 0, 0])
    return outs[0], None


HEAD_V = 64
VT_ROWS = 80


def _ones_row(v_t):
    return jnp.where(lax.broadcasted_iota(jnp.int32, v_t.shape, 0) == HEAD_V, 1.0, v_t)


def _flash_heads(qs, keys_of, vt_chunk, n_keys, o_ref):
    s_ts = [_dot_nt(keys_of(e), qb) for e, qb in enumerate(qs)]
    m = [None] * len(qs)
    acc = [None] * len(qs)
    for c0 in range(0, n_keys, KEY_BLOCK):
        c1 = min(c0 + KEY_BLOCK, n_keys)
        for e in range(len(qs)):
            s = s_ts[e][c0:c1, :]
            mc = jnp.max(s, axis=0, keepdims=True)
            if c0 == 0:
                m[e] = mc
                acc[e] = _dot(vt_chunk(e, c0, c1), jnp.exp2(s - mc).astype(BF16))
            else:
                m_new = jnp.maximum(m[e], mc)
                alpha = jnp.exp2(m[e] - m_new)
                acc[e] = acc[e] * alpha + _dot(vt_chunk(e, c0, c1), jnp.exp2(s - m_new).astype(BF16))
                m[e] = m_new
    outs = [a[0:HEAD_V, :] / a[HEAD_V:HEAD_V + 1, :] for a in acc]
    for e in range(0, len(outs), 2):
        pair = jnp.concatenate([outs[e], outs[e + 1]], axis=0)
        o_ref[:, (e // 2) * LANES:(e // 2 + 1) * LANES] = pair.T.astype(o_ref.dtype)


def _mla_kernel(*refs, past, rotary, heads):
    q_ref, ckv_ref, akr_ref = refs[:3]
    pos = 3
    if past:
        ckvc_ref, krc_ref = refs[pos:pos + 2]
        pos += 2
    wk_ref, wvt_ref, kg_ref = refs[pos:pos + 3]
    pos += 3
    if rotary:
        ck_ref, sk_ref = refs[pos:pos + 2]
        pos += 2
    o_ref, kbuf, vbuf = refs[pos:]
    seq = ckv_ref.shape[0]

    @pl.when(pl.program_id(2) == 0)
    def _():
        for e in range(heads):
            wk_h = wk_ref[:, e * LANES:(e + 1) * LANES]
            wvt_h = wvt_ref[e * VT_ROWS:(e + 1) * VT_ROWS, :]

            def keys(ckv, kr):
                cb = ckv.astype(BF16)
                kcat = _dot(cb, wk_h) + kr
                return _ones_row(_dot_nt(wvt_h, cb)), _rms(kcat, A_QK) * kg_ref[...]

            v_t, kn = keys(ckv_ref[...], akr_ref[...])
            if rotary:
                kn = _rope(kn, ck_ref[...], sk_ref[...], A_ROPE // 2, LANES)
            kbuf[e, past:past + seq, :] = kn.astype(BF16)
            vbuf[e, :, past:past + seq] = v_t.astype(BF16)
            if past:
                v_tc, knc = keys(ckvc_ref[...], krc_ref[...])
                kbuf[e, 0:past, :] = knc.astype(BF16)
                vbuf[e, :, 0:past] = v_tc.astype(BF16)

    qs = [q_ref[:, e * LANES:(e + 1) * LANES] for e in range(heads)]
    _flash_heads(qs, lambda e: kbuf[e], lambda e, c0, c1: vbuf[e, :, c0:c1], past + seq, o_ref)


def _mla_call(qa, ckv, akr, lw, batch, seq, cache, rope_tabs):
    n = batch * seq
    tq = min(Q_BLOCK, seq)
    nq = seq // tq
    past = 0 if cache is None else cache[0].shape[1]
    heads = MLA_HEADS_PER_STEP if nq > 1 else A_HEADS
    in_specs = [
        pl.BlockSpec((tq, heads * LANES), lambda b, hg, qi: (b * nq + qi, hg)),
        pl.BlockSpec((seq, LANES), lambda b, hg, qi: (b, 0)),
        pl.BlockSpec((seq, LANES), lambda b, hg, qi: (b, 0)),
    ]
    args = [qa, ckv, akr]
    if past:
        in_specs += [pl.BlockSpec((None, past, LANES), lambda b, hg, qi: (b, 0, 0))] * 2
        args += [cache[0], cache[1]]
    in_specs += [
        pl.BlockSpec((A_KV_LORA, heads * LANES), lambda b, hg, qi: (0, hg)),
        pl.BlockSpec((heads * VT_ROWS, A_KV_LORA), lambda b, hg, qi: (hg, 0)),
        pl.BlockSpec((1, LANES), lambda b, hg, qi: (0, 0)),
    ]
    args += [lw["w_uk"], lw["w_uvt"], lw["a_knorm_g"]]
    if rope_tabs is not None:
        cos, sin = rope_tabs
        in_specs += [pl.BlockSpec((seq, LANES), lambda b, hg, qi: (0, 0))] * 2
        args += [cos, sin]
    return pl.pallas_call(
        functools.partial(_mla_kernel, past=past, rotary=rope_tabs is not None, heads=heads),
        out_shape=jax.ShapeDtypeStruct((n, A_HEADS * A_V), BF16),
        grid=(batch, A_HEADS // heads, nq),
        in_specs=in_specs,
        out_specs=pl.BlockSpec((tq, heads * A_V), lambda b, hg, qi: (b * nq + qi, hg)),
        scratch_shapes=[pltpu.VMEM((heads, past + seq, LANES), BF16),
                        pltpu.VMEM((heads, VT_ROWS, past + seq), BF16)],
        compiler_params=_params(3),
        name="mla_attn",
    )(*args)


def _gqa_kernel(*refs, past, groups):
    q_ref, k_ref, v_ref = refs[:3]
    pos = 3
    if past:
        kc_ref, vc_ref = refs[pos:pos + 2]
        pos += 2
    o_ref, kbuf, vbuf = refs[pos:]
    seq = k_ref.shape[0]
    qw = G_GROUP * G_HD

    @pl.when(pl.program_id(2) == 0)
    def _():
        for gi in range(groups):
            grp = pl.program_id(1) * groups + gi

            r = lax.broadcasted_iota(jnp.int32, (VT_ROWS, LANES), 0)
            c = lax.broadcasted_iota(jnp.int32, (VT_ROWS, LANES), 1)
            pick = jnp.where((c == r + grp * G_HD) & (r < G_HD), 1.0, 0.0).astype(BF16)

            def fill(r0, k, v):
                low = _lane(k.shape) < G_HD
                k_lo = jnp.where(grp == 0, k, pltpu.roll(k, G_HD, 1))
                rows = k.shape[0]
                kbuf[gi, r0:r0 + rows, :] = jnp.where(low, k_lo, pltpu.roll(k_lo, G_HD, 1)).astype(BF16)
                vbuf[gi, :, r0:r0 + rows] = _ones_row(_dot_nt(pick, v.astype(BF16))).astype(BF16)

            fill(past, k_ref[...], v_ref[...])
            if past:
                fill(0, kc_ref[...], vc_ref[...])

    qs = []
    for gi in range(groups):
        for j in range(G_GROUP):
            col = q_ref[:, gi * qw + (j // 2) * LANES:gi * qw + (j // 2 + 1) * LANES]
            keep = (_lane(col.shape) < G_HD) == (j % 2 == 0)
            qs.append(jnp.where(keep, col, jnp.zeros_like(col)))
    _flash_heads(qs, lambda e: kbuf[e // G_GROUP], lambda e, c0, c1: vbuf[e // G_GROUP, :, c0:c1],
                 past + seq, o_ref)


def _gqa_call(gq, gk, gv, batch, seq, cache):
    n = batch * seq
    tq = min(Q_BLOCK, seq)
    nq = seq // tq
    past = 0 if cache is None else cache[0].shape[1]
    groups = GQA_GROUPS_PER_STEP if nq > 1 else G_KV_HEADS
    qw = G_GROUP * G_HD
    kvw = G_KV_HEADS * G_HD
    in_specs = [
        pl.BlockSpec((tq, groups * qw), lambda b, g, qi: (b * nq + qi, g)),
        pl.BlockSpec((seq, kvw), lambda b, g, qi: (b, 0)),
        pl.BlockSpec((seq, kvw), lambda b, g, qi: (b, 0)),
    ]
    args = [gq, gk, gv]
    if past:
        in_specs += [pl.BlockSpec((None, past, kvw), lambda b, g, qi: (b, 0, 0))] * 2
        args += [cache[0], cache[1]]
    return pl.pallas_call(
        functools.partial(_gqa_kernel, past=past, groups=groups),
        out_shape=jax.ShapeDtypeStruct((n, G_HEADS * G_HD), BF16),
        grid=(batch, G_KV_HEADS // groups, nq),
        in_specs=in_specs,
        out_specs=pl.BlockSpec((tq, groups * qw), lambda b, g, qi: (b * nq + qi, g)),
        scratch_shapes=[pltpu.VMEM((groups, past + seq, LANES), BF16),
                        pltpu.VMEM((groups, VT_ROWS, past + seq), BF16)],
        compiler_params=_params(3),
        name="gqa_attn",
    )(*args)


def _merge_ffn_kernel(x_ref, ym_ref, ya_ref, yg_ref, gates_ref, mod_ref, g2_ref, wb_ref, wo_ref, wfi_ref,
                      wfo_ref, o_ref, *, ff_chunk):
    x = x_ref[...]
    d = x.shape[-1]
    mixed = None
    for i, y_ref in enumerate((ym_ref, ya_ref, yg_ref)):
        br = _dot(y_ref[...].astype(BF16), wb_ref[i]) * gates_ref[:, i * d:(i + 1) * d]
        mixed = br if mixed is None else mixed + br
    gt1 = mod_ref[2:3, :]
    x1 = x + gt1 * _dot(mixed.astype(BF16), wo_ref[...])

    sh2 = mod_ref[3:4, :]
    sc2 = mod_ref[4:5, :]
    gt2 = mod_ref[5:6, :]
    h2 = ((_rms(x1, d) * g2_ref[...]) * (1.0 + sc2) + sh2).astype(BF16)
    d_ff = wfo_ref.shape[0]
    acc = None
    for c0 in range(0, d_ff, ff_chunk):
        ug = _dot(h2, wfi_ref[:, c0:c0 + ff_chunk])
        uv = _dot(h2, wfi_ref[:, d_ff + c0:d_ff + c0 + ff_chunk])
        act = (ug * _sigmoid(ug) * uv).astype(BF16)
        part = _dot(act, wfo_ref[c0:c0 + ff_chunk, :])
        acc = part if acc is None else acc + part
    o_ref[...] = x1 + gt2 * acc


def _merge_ffn_call(x, ym, ya, yg, gates, mod, lw, mod_row0, rows_per_mod):
    n, d = x.shape
    tm = TOKEN_BLOCK
    d_ff = lw["w_ffn_out"].shape[0]
    ff_chunk = d_ff // 2 if (d_ff // 2) % LANES == 0 else d_ff

    def mod_idx(i):
        return (mod_row0 + (i * tm) // rows_per_mod, 0, 0)

    def rows(width):
        return pl.BlockSpec((tm, width), lambda i: (i, 0))

    return pl.pallas_call(
        functools.partial(_merge_ffn_kernel, ff_chunk=ff_chunk),
        out_shape=jax.ShapeDtypeStruct((n, d), F32),
        grid=(n // tm,),
        in_specs=[
            rows(d), rows(BRANCH_WIDTH), rows(BRANCH_WIDTH), rows(BRANCH_WIDTH), rows(N_BRANCH * d),
            pl.BlockSpec((None, N_MOD, d), mod_idx),
            _resident((1, d)),
            _resident((N_BRANCH, BRANCH_WIDTH, d)),
            _resident((d, d)),
            _resident((d, 2 * d_ff)),
            _resident((d_ff, d)),
        ],
        out_specs=rows(d),
        compiler_params=_params(1),
        name="merge_ffn",
    )(x, ym, ya, yg, gates, mod, lw["norm2_g"], lw["w_branch"], lw["w_out"], lw["w_ffn_in"], lw["w_ffn_out"])


def _pad_cols(w, width):
    return jnp.pad(w, ((0, 0), (0, width - w.shape[1])))


def _mla_q_layout(a):
    lead = a.shape[:-1]
    a = a.reshape(lead + (A_HEADS, A_QK))
    z = jnp.zeros(lead + (A_HEADS, LANES - A_QK), a.dtype)
    return jnp.concatenate([a[..., A_NOPE:], z, a[..., :A_NOPE]], axis=-1).reshape(lead + (A_HEADS * LANES,))


def _layer_weights(l, w_in, b_mgate, norm1_g, m_norm_g, a_qlora_g, a_kvlora_g, w_uq, w_ukv, a_qnorm_g,
                   a_knorm_g, g_qnorm_g, g_knorm_g, w_branch, w_out, norm2_g, w_ffn_in, w_ffn_out):
    d = w_in.shape[1]
    hw = M_HEADS * M_DK
    sizes = (N_BRANCH * d, hw, hw, hw, hw, 4 * M_HEADS, A_Q_LORA, A_KV_LORA, A_ROPE, G_HEADS * G_HD,
             G_KV_HEADS * G_HD, G_KV_HEADS * G_HD)
    splits = np.cumsum(sizes)[:-1].tolist()
    (wg, wmq, wmk, wmv, wmo, wmg, waq, wakv, wakr, wgq, wgk, wgv) = jnp.split(w_in[l], splits, axis=1)
    w_in_p = jnp.concatenate(
        [wg, wmq, wmk, wmv, wmo, _pad_cols(wmg, LANES), waq, wakv, _pad_cols(wakr, LANES), wgq, wgk, wgv],
        axis=1).astype(BF16)
    ukv = w_ukv[l].reshape(A_KV_LORA, A_HEADS, A_NOPE + A_V)
    uk = jnp.pad(ukv[..., :A_NOPE], ((0, 0), (0, 0), (LANES - A_NOPE, 0))).reshape(A_KV_LORA, A_HEADS * LANES)
    uvt = jnp.pad(ukv[..., A_NOPE:].transpose(1, 2, 0), ((0, 0), (0, VT_ROWS - A_V), (0, 0)))
    uvt = uvt.reshape(A_HEADS * VT_ROWS, A_KV_LORA)
    qg = _mla_q_layout(jnp.tile(a_qnorm_g[l], A_HEADS)[None, :])[:, :LANES]
    kg = _mla_q_layout(jnp.tile(a_knorm_g[l], A_HEADS)[None, :])[:, :LANES]
    return dict(
        w_in=w_in_p,
        b_mgate=_pad_cols(b_mgate[l][None, :], LANES),
        norm1_g=norm1_g[l][None, :],
        m_norm_g=m_norm_g[l][None, :],
        a_qlora_g=a_qlora_g[l][None, :],
        a_kvlora_g=a_kvlora_g[l][None, :],
        w_uq=_mla_q_layout(w_uq[l]).astype(BF16),
        w_uk=uk.astype(BF16),
        w_uvt=uvt.astype(BF16),
        a_qnorm_g=qg,
        a_knorm_g=kg,
        g_qnorm_g=jnp.tile(g_qnorm_g[l], LANES // G_HD)[None, :],
        g_knorm_g=jnp.tile(g_knorm_g[l], G_KV_HEADS)[None, :],
        w_branch=w_branch[l].astype(BF16),
        w_out=w_out[l].astype(BF16),
        norm2_g=norm2_g[l][None, :],
        w_ffn_in=w_ffn_in[l].astype(BF16),
        w_ffn_out=w_ffn_out[l].astype(BF16),
    )


def _axial_angles(seq, rot_dim):
    n_freq = rot_dim // 4
    freqs = ROPE_BASE ** (-jnp.arange(n_freq, dtype=F32) / n_freq)
    t = jnp.arange(seq)
    row = (t // GRID_W).astype(F32)
    col = (t % GRID_W).astype(F32)
    return jnp.concatenate([row[:, None] * freqs, col[:, None] * freqs], axis=-1)


def _rope_tables(seq):
    ang = _axial_angles(seq, A_ROPE)
    one = jnp.ones((seq, LANES - A_ROPE), F32)
    mla_cos = jnp.concatenate([jnp.cos(ang), jnp.cos(ang), one], axis=-1)
    mla_sin = jnp.concatenate([-jnp.sin(ang), jnp.sin(ang), 0.0 * one], axis=-1)
    ang = _axial_angles(seq, G_HD)
    cos = jnp.concatenate([jnp.cos(ang), jnp.cos(ang)], axis=-1)
    sin = jnp.concatenate([-jnp.sin(ang), jnp.sin(ang)], axis=-1)
    gqa = (jnp.tile(cos, (1, LANES // G_HD)), jnp.tile(sin, (1, LANES // G_HD)))
    return (mla_cos, mla_sin), gqa


def _layer(x, mod, lw, batch, seq, mod_row0, rows_per_mod, ctx, rope):
    (gates, qkv, og, mg, qa, ckv, akr, gq, gk, gv) = _inproj_call(x, mod, lw, mod_row0, rows_per_mod, rope, seq)
    if ctx is None:
        ym, state = _mlstm_call(qkv, og, mg, lw, batch, seq, None, True)
        ya = _mla_call(qa, ckv, akr, lw, batch, seq, None, None)
        yg = _gqa_call(gq, gk, gv, batch, seq, None)
        new_ctx = dict(state=state, ckv=ckv, kr=akr[:, :A_ROPE], gk=gk, gv=gv)
    else:
        ym, _ = _mlstm_call(qkv, og, mg, lw, batch, seq, ctx["mlstm"], False)
        ya = _mla_call(qa, ckv, akr, lw, batch, seq, ctx["mla"], rope[0])
        yg = _gqa_call(gq, gk, gv, batch, seq, ctx["gqa"])
        new_ctx = None
    x = _merge_ffn_call(x, ym, ya, yg, gates, mod, lw, mod_row0, rows_per_mod)
    return x, new_ctx


def kernel(x_prompt, x_sample, state_mlstm_C, state_mlstm_n, state_mlstm_m, cache_mla_ckv, cache_mla_krope,
           cache_gqa_k, cache_gqa_v, c, c_ctx, w_mod, b_mod, norm1_g, w_in, b_mgate, m_norm_g, a_qlora_g,
           a_kvlora_g, w_uq, w_ukv, a_qnorm_g, a_knorm_g, g_qnorm_g, g_knorm_g, w_branch, w_out, norm2_g,
           w_ffn_in, w_ffn_out):
    batch, seq, d = x_prompt.shape
    dbatch, dseq, _ = x_sample.shape
    depth = w_in.shape[0]
    past = cache_mla_ckv.shape[2]

    n_rows = -(-(1 + dbatch) // SUBLANES) * SUBLANES
    cond = jnp.concatenate([c_ctx[None, :], c, jnp.zeros((n_rows - 1 - dbatch, d), F32)], axis=0)
    mod_all = _mod_call(cond, w_mod, b_mod).reshape(depth, n_rows, N_MOD, d)

    rope = _rope_tables(dseq)
    xp = x_prompt.reshape(batch * seq, d)
    xs = x_sample.reshape(dbatch * dseq, d)
    ctx_layers = []
    for l in range(depth):
        lw = _layer_weights(l, w_in, b_mgate, norm1_g, m_norm_g, a_qlora_g, a_kvlora_g, w_uq, w_ukv,
                            a_qnorm_g, a_knorm_g, g_qnorm_g, g_knorm_g, w_branch, w_out, norm2_g, w_ffn_in,
                            w_ffn_out)
        xp, st = _layer(xp, mod_all[l], lw, batch, seq, 0, batch * seq, None, None)
        ctx_layers.append(st)
        ctx = dict(
            mlstm=(state_mlstm_C[:, l], state_mlstm_n[:, l], state_mlstm_m[:, l]),
            mla=(cache_mla_ckv[:, l], _pad_cols(cache_mla_krope[:, l].reshape(dbatch * past, A_ROPE), LANES)
                 .reshape(dbatch, past, LANES)),
            gqa=(cache_gqa_k[:, l].reshape(dbatch, past, G_KV_HEADS * G_HD),
                 cache_gqa_v[:, l].reshape(dbatch, past, G_KV_HEADS * G_HD)),
        )
        xs, _ = _layer(xs, mod_all[l], lw, dbatch, dseq, 1, dseq, ctx, rope)

    def stack(fn):
        return jnp.stack([fn(s) for s in ctx_layers], axis=1)

    new_c = stack(lambda s: s["state"][0])
    new_n = stack(lambda s: s["state"][1])
    new_m = stack(lambda s: s["state"][2])
    new_ckv = stack(lambda s: s["ckv"].reshape(batch, seq, A_KV_LORA))
    new_kr = stack(lambda s: s["kr"].reshape(batch, seq, A_ROPE))
    new_gk = stack(lambda s: s["gk"].reshape(batch, seq, G_KV_HEADS, G_HD))
    new_gv = stack(lambda s: s["gv"].reshape(batch, seq, G_KV_HEADS, G_HD))
    return (xp.reshape(batch, seq, d), xs.reshape(dbatch, dseq, d), new_c, new_n, new_m, new_ckv, new_kr,
            new_gk, new_gv)
```

```python
import functools

import numpy as np
import jax
import jax.numpy as jnp
from jax import lax
from jax.experimental import pallas as pl
from jax.experimental.pallas import tpu as pltpu

F32 = jnp.float32
BF16 = jnp.bfloat16

LANES = 128
SUBLANES = 8
VMEM_LIMIT_BYTES = 56 * 1024 * 1024

EPS = 1e-6
ROPE_BASE = 10000.0
GRID_W = 64

M_HEADS = 4
M_DK = 128
M_DV = 128
A_HEADS = 8
A_NOPE = 64
A_ROPE = 32
A_QK = A_NOPE + A_ROPE
A_V = 64
A_Q_LORA = 256
A_KV_LORA = 128
G_HEADS = 8
G_KV_HEADS = 2
G_GROUP = G_HEADS // G_KV_HEADS
G_HD = 64
N_BRANCH = 3
BRANCH_WIDTH = 512
N_MOD = 6

TOKEN_BLOCK = 512
Q_BLOCK = 256
KEY_BLOCK = 256
MLA_HEADS_PER_STEP = 4
GQA_GROUPS_PER_STEP = 2
MLSTM_CHUNK = 128
MLSTM_HEADS_PER_STEP = 4
LOG2E = 1.4426950408889634


def _params(n_axes):
    return pltpu.CompilerParams(dimension_semantics=("arbitrary",) * n_axes,
                                vmem_limit_bytes=VMEM_LIMIT_BYTES)


def _resident(shape):
    nd = len(shape)
    return pl.BlockSpec(shape, lambda *_: (0,) * nd, pipeline_mode=pl.Buffered(1))


def _lane(shape, axis=None):
    return lax.broadcasted_iota(jnp.int32, shape, len(shape) - 1 if axis is None else axis)


def _dot(a, b):
    return jnp.dot(a, b, preferred_element_type=F32)


def _dot_nt(a, b):
    return lax.dot_general(a, b, (((1,), (1,)), ((), ())), preferred_element_type=F32)


def _split3(a):
    hi = a.astype(BF16)
    r1 = a - hi.astype(F32)
    mid = r1.astype(BF16)
    lo = (r1 - mid.astype(F32)).astype(BF16)
    return hi, mid, lo


def _dot01(a, m01):
    hi, mid, lo = _split3(a)
    return _dot(hi, m01) + _dot(mid, m01) + _dot(lo, m01)


def _dot01_left(m01, a):
    hi, mid, lo = _split3(a)
    return _dot(m01, hi) + _dot(m01, mid) + _dot(m01, lo)


def _sigmoid(x):
    return 1.0 / (1.0 + jnp.exp(-x))


def _log_sigmoid(x):
    return jnp.minimum(x, 0.0) - jnp.log(1.0 + jnp.exp(-jnp.abs(x)))


def _rms(x, width):
    ms = jnp.sum(x * x, axis=-1, keepdims=True) * (1.0 / width)
    return x * lax.rsqrt(ms + EPS)


def _rope(x, cos, sin_signed, half, period):
    n = x.shape[-1]
    first = (_lane(x.shape) % period) < half
    swapped = jnp.where(first, pltpu.roll(x, n - half, x.ndim - 1), pltpu.roll(x, half, x.ndim - 1))
    return x * cos + swapped * sin_signed


def _mod_kernel(c_ref, w_ref, b_ref, o_ref):
    c = c_ref[...]
    a = c * _sigmoid(c)
    hi, mid, lo = _split3(a)
    w = w_ref[...]
    whi = w.astype(BF16)
    wlo = (w - whi.astype(F32)).astype(BF16)
    acc = _dot(hi, whi) + _dot(mid, whi) + _dot(hi, wlo) + _dot(lo, whi) + _dot(mid, wlo)
    o_ref[...] = acc + b_ref[...]


def _mod_call(cond, w_mod, b_mod):
    depth, d, n = w_mod.shape
    rows = cond.shape[0]
    tn = 1536
    return pl.pallas_call(
        _mod_kernel,
        out_shape=jax.ShapeDtypeStruct((depth, rows, n), F32),
        grid=(depth, n // tn),
        in_specs=[
            pl.BlockSpec((rows, d), lambda l, j: (0, 0)),
            pl.BlockSpec((None, d, tn), lambda l, j: (l, 0, j)),
            pl.BlockSpec((None, 1, tn), lambda l, j: (l, 0, j)),
        ],
        out_specs=pl.BlockSpec((None, rows, tn), lambda l, j: (l, 0, j)),
        compiler_params=_params(2),
        name="adaln_mod",
    )(cond, w_mod, b_mod.reshape(depth, 1, n))


MXU_COLS = 256
_GATE0, _GATE1 = 0, 3072
_MQ0 = 3072
_MK0 = 3584
_MV0 = 4096
_MO0 = 4608
_AQ0 = 5120
_AKV0 = 5376
_MISC0 = 5504
_GQ0 = 5632
_GK0 = 6144
_GV0 = 6272
_WIN_COLS = 6400
MG_LANE0 = A_ROPE


def _head_pair_ms(x):
    low = (_lane(x.shape) % LANES) < G_HD
    sq = x * x
    cols = []
    for c0 in range(0, x.shape[-1], LANES):
        s = sq[:, c0:c0 + LANES]
        lo = jnp.sum(jnp.where(low[:, c0:c0 + LANES], s, 0.0), axis=-1, keepdims=True)
        hi = jnp.sum(jnp.where(low[:, c0:c0 + LANES], 0.0, s), axis=-1, keepdims=True)
        cols.append(jnp.where(low[:, c0:c0 + LANES], lo, hi))
    ms = cols[0] if len(cols) == 1 else jnp.concatenate(cols, axis=-1)
    return ms * (1.0 / G_HD)


def _inproj_kernel(*refs, rotary):
    (x_ref, mod_ref, g1_ref, w_ref, bmg_ref, gql_ref, wuq_ref, gkvl_ref, gkn_ref, aqn_ref, gqn_ref) = refs[:11]
    pos = 11
    if rotary:
        ca_ref, sa_ref, cg_ref, sg_ref = refs[pos:pos + 4]
        pos += 4
    (gates_ref, qkv_ref, og_ref, mg_ref, qa_ref, ckv_ref, akr_ref, gq_ref, gk_ref, gv_ref) = refs[pos:]
    x = x_ref[...]
    d = x.shape[-1]
    sh1 = mod_ref[0:1, :]
    sc1 = mod_ref[1:2, :]
    h = (_rms(x, d) * g1_ref[...]) * (1.0 + sc1) + sh1
    hb = h.astype(BF16)

    def proj(c0, width):
        return _dot(hb, w_ref[:, c0:c0 + width])

    aq = _rms(proj(_AQ0, A_Q_LORA), A_Q_LORA) * gql_ref[...]
    qa = _dot(aq.astype(BF16), wuq_ref[...])
    for hd in range(A_HEADS):
        qh = _rms(qa[:, hd * LANES:(hd + 1) * LANES], A_QK) * aqn_ref[...]
        if rotary:
            qh = _rope(qh, ca_ref[...], sa_ref[...], A_ROPE // 2, LANES)
        qa_ref[:, hd * LANES:(hd + 1) * LANES] = (qh * (A_QK ** -0.5 * LOG2E)).astype(qa_ref.dtype)
    akv_misc = proj(_AKV0, A_KV_LORA + LANES)
    ckv_ref[...] = _rms(akv_misc[:, :A_KV_LORA], A_KV_LORA) * gkvl_ref[...]
    misc = akv_misc[:, A_KV_LORA:]
    akr_ref[...] = jnp.where(_lane(misc.shape) < A_ROPE, misc, 0.0)
    mg_ref[...] = misc + bmg_ref[...]

    gq_all = proj(_GQ0, G_HEADS * G_HD)
    for c0 in range(0, G_HEADS * G_HD, LANES):
        gq = gq_all[:, c0:c0 + LANES]
        gq = gq * lax.rsqrt(_head_pair_ms(gq) + EPS) * gqn_ref[...]
        if rotary:
            gq = _rope(gq, cg_ref[...], sg_ref[...], G_HD // 2, G_HD)
        gq_ref[:, c0:c0 + LANES] = (gq * (G_HD ** -0.5 * LOG2E)).astype(gq_ref.dtype)
    kvw = G_KV_HEADS * G_HD
    gkv = proj(_GK0, 2 * kvw)
    gk = gkv[:, :kvw]
    gk = gk * lax.rsqrt(_head_pair_ms(gk) + EPS) * gkn_ref[...]
    if rotary:
        gk = _rope(gk, cg_ref[...], sg_ref[...], G_HD // 2, G_HD)
    gk_ref[...] = gk
    gv_ref[...] = gkv[:, kvw:]

    hw = M_HEADS * M_DK
    qkv_ref[:, 0:hw] = proj(_MQ0, hw).astype(qkv_ref.dtype)
    qkv_ref[:, hw:2 * hw] = (proj(_MK0, hw) * (M_DK ** -0.5)).astype(qkv_ref.dtype)
    qkv_ref[:, 2 * hw:3 * hw] = proj(_MV0, hw).astype(qkv_ref.dtype)
    og_ref[...] = _sigmoid(proj(_MO0, hw)).astype(og_ref.dtype)
    gates_ref[...] = _sigmoid(proj(_GATE0, _GATE1 - _GATE0)).astype(gates_ref.dtype)


def _inproj_call(x, mod, lw, mod_row0, rows_per_mod, rope, seq):
    n, d = x.shape
    tm = TOKEN_BLOCK
    hw = M_HEADS * M_DK

    def mod_idx(i):
        return (mod_row0 + (i * tm) // rows_per_mod, 0, 0)

    def rows(width):
        return pl.BlockSpec((tm, width), lambda i: (i, 0))

    in_specs = [
        rows(d),
        pl.BlockSpec((None, N_MOD, d), mod_idx),
        _resident((1, d)),
        _resident((d, _WIN_COLS)),
        _resident((1, LANES)),
        _resident((1, A_Q_LORA)),
        _resident((A_Q_LORA, A_HEADS * LANES)),
        _resident((1, A_KV_LORA)),
        _resident((1, LANES)),
        _resident((1, LANES)),
        _resident((1, LANES)),
    ]
    args = [x, mod, lw["norm1_g"], lw["w_in"], lw["b_mgate"], lw["a_qlora_g"], lw["w_uq"], lw["a_kvlora_g"],
            lw["g_knorm_g"], lw["a_qnorm_g"], lw["g_qnorm_g"]]
    if rope is not None:
        blocks_per_seq = seq // tm
        in_specs += [pl.BlockSpec((tm, LANES), lambda i: (i % blocks_per_seq, 0))] * 4
        args += [rope[0][0], rope[0][1], rope[1][0], rope[1][1]]
    out_widths = [3 * d, 3 * hw, hw, LANES, A_HEADS * LANES, A_KV_LORA, LANES, G_HEADS * G_HD,
                  G_KV_HEADS * G_HD, G_KV_HEADS * G_HD]
    out_dtypes = [BF16, BF16, BF16, F32, BF16, F32, F32, BF16, F32, F32]
    return pl.pallas_call(
        functools.partial(_inproj_kernel, rotary=rope is not None),
        out_shape=[jax.ShapeDtypeStruct((n, w), t) for w, t in zip(out_widths, out_dtypes)],
        grid=(n // tm,),
        in_specs=in_specs,
        out_specs=[rows(w) for w in out_widths],
        compiler_params=_params(1),
        name="in_proj",
    )(*args)


def _mlstm_kernel(*refs, n_chunks, heads, has_init, emit_state):
    q_ref, k_ref, v_ref, og_ref, gcol_ref, grow_ref, gn_ref = refs[:7]
    pos = 7
    if has_init:
        c0_ref, n0_ref, m0_ref = refs[pos:pos + 3]
        pos += 3
    y_ref = refs[pos]
    pos += 1
    if emit_state:
        cf_ref, nf_ref, mf_ref = refs[pos:pos + 3]
        pos += 3
    (pcol_scr, lfc_scr, prow_scr, lfr_scr, u_scr, nu_scr, gm_scr, tot_scr, cs_scr, ns_scr, ms_scr,
     c_scr, n_scr, m_scr) = refs[pos:]

    ln = MLSTM_CHUNK
    row = lax.broadcasted_iota(jnp.int32, (ln, ln), 0)
    col = lax.broadcasted_iota(jnp.int32, (ln, ln), 1)
    lower = col <= row
    upper = col >= row
    tril = jnp.where(lower, 1.0, 0.0).astype(BF16)
    triu = jnp.where(upper, 1.0, 0.0).astype(BF16)
    ones = jnp.ones((ln, ln), BF16)

    def gate_lanes(hh, d):
        return 2 * d * heads + hh, (2 * d + 1) * heads + hh

    def summaries(c, carry):
        r0 = pl.multiple_of(c * ln, ln)
        g = gcol_ref[pl.ds(r0, ln), :] * LOG2E
        lf = _log_sigmoid(gcol_ref[pl.ds(r0, ln), :]) * LOG2E
        pc = _dot01_left(tril, lf)
        pcol_scr[pl.ds(r0, ln), :] = pc
        lfc_scr[pl.ds(r0, ln), :] = lf
        lfr = _log_sigmoid(grow_ref[c]) * LOG2E
        prow_scr[c] = _dot01(lfr, triu)
        lfr_scr[c] = lfr
        tot_all = pc[ln - 1:ln, :]
        for hh in range(heads):
            k = k_ref[pl.ds(r0, ln), hh * LANES:(hh + 1) * LANES].astype(F32)
            v = v_ref[pl.ds(r0, ln), hh * LANES:(hh + 1) * LANES]
            for d in range(2):
                ji, jf = gate_lanes(hh, d)
                tot = tot_all[:, jf:jf + 1]
                b_col = pc[:, jf:jf + 1] if d == 0 else tot - pc[:, jf:jf + 1] + lf[:, jf:jf + 1]
                gg = jnp.broadcast_to(tot - b_col + g[:, ji:ji + 1], (ln, ln))
                gmax = jnp.max(gg, axis=0, keepdims=True)
                kw = k * jnp.exp2(gg - gmax)
                u_scr[hh, d, c] = _dot(kw.T.astype(BF16), v)
                nu_scr[hh, d, c] = jnp.sum(kw, axis=0, keepdims=True)
                gm_scr[hh, d, c] = gmax
                tot_scr[hh, d, c] = jnp.broadcast_to(tot, (1, ln))
        return carry

    lax.fori_loop(0, n_chunks, summaries, 0)

    for hh in range(heads):
        for d in range(2):
            if has_init:
                c_scr[hh, d] = c0_ref[d, hh]
                n_scr[hh, d] = n0_ref[d, hh]
                m_scr[hh, d] = m0_ref[d, hh] * LOG2E
            else:
                c_scr[hh, d] = jnp.zeros((M_DK, M_DV), F32)
                n_scr[hh, d] = jnp.zeros((1, M_DK), F32)
                m_scr[hh, d] = jnp.zeros((1, ln), F32)

    def scan(j, carry):
        for hh in range(heads):
            for d in range(2):
                c = j if d == 0 else n_chunks - 1 - j
                cst = c_scr[hh, d]
                nst = n_scr[hh, d]
                mst = m_scr[hh, d]
                cs_scr[hh, d, c] = cst.astype(BF16)
                ns_scr[hh, d, c] = nst
                ms_scr[hh, d, c] = mst
                gmax = gm_scr[hh, d, c]
                total = tot_scr[hh, d, c] + mst
                m_new = jnp.maximum(total, gmax)
                decay = jnp.exp2(total - m_new)
                scale = jnp.exp2(gmax - m_new)
                c_scr[hh, d] = cst * decay + u_scr[hh, d, c] * scale
                n_scr[hh, d] = nst * decay + nu_scr[hh, d, c] * scale
                m_scr[hh, d] = m_new
        return carry

    lax.fori_loop(0, n_chunks, scan, 0)

    def readout(c, carry):
        r0 = pl.multiple_of(c * ln, ln)
        pc = pcol_scr[pl.ds(r0, ln), :]
        lf = lfc_scr[pl.ds(r0, ln), :]
        gr = grow_ref[c] * LOG2E
        pr = prow_scr[c]
        lfr = lfr_scr[c]
        tot_all = pc[ln - 1:ln, :]
        for hh in range(heads):
            q = q_ref[pl.ds(r0, ln), hh * LANES:(hh + 1) * LANES]
            k = k_ref[pl.ds(r0, ln), hh * LANES:(hh + 1) * LANES]
            v = v_ref[pl.ds(r0, ln), hh * LANES:(hh + 1) * LANES]
            qk = _dot_nt(q, k)
            hsum = None
            for d in range(2):
                ji, jf = gate_lanes(hh, d)
                if d == 0:
                    b_col = pc[:, jf:jf + 1]
                    r_row = gr[ji:ji + 1, :] - pr[jf:jf + 1, :]
                    mask = lower
                else:
                    tot = tot_all[:, jf:jf + 1]
                    b_col = tot - pc[:, jf:jf + 1] + lf[:, jf:jf + 1]
                    r_row = gr[ji:ji + 1, :] - (tot - pr[jf:jf + 1, :] + lfr[jf:jf + 1, :])
                    mask = upper
                b_colb = jnp.broadcast_to(b_col, (ln, ln))
                d_log = jnp.where(mask, b_colb + r_row, -jnp.inf)
                dmax = jnp.broadcast_to(jnp.max(d_log, axis=-1, keepdims=True), (ln, ln))
                sb = (qk * jnp.exp2(d_log - dmax)).astype(BF16)
                num_loc = _dot(sb, v)
                den_loc = _dot(sb, ones)
                qc = _dot(q, cs_scr[hh, d, c])
                qn = _dot_nt(q, jnp.broadcast_to(ns_scr[hh, d, c], (ln, M_DK)).astype(BF16))
                inter = b_colb + ms_scr[hh, d, c]
                m_t = jnp.maximum(inter, dmax)
                w_inter = jnp.exp2(inter - m_t)
                w_loc = jnp.exp2(dmax - m_t)
                num = qc * w_inter + num_loc * w_loc
                den = qn * w_inter + den_loc * w_loc
                hc = num / jnp.maximum(jnp.abs(den), jnp.exp2(-m_t))
                hsum = hc if hsum is None else hsum + hc
            y_ref[pl.ds(r0, ln), hh * LANES:(hh + 1) * LANES] = (
                _rms(hsum, M_DV) * gn_ref[...] * og_ref[pl.ds(r0, ln), hh * LANES:(hh + 1) * LANES]
            ).astype(y_ref.dtype)
        return carry

    lax.fori_loop(0, n_chunks, readout, 0)

    if emit_state:
        for hh in range(heads):
            for d in range(2):
                cf_ref[d, hh] = c_scr[hh, d]
                nf_ref[d, hh] = n_scr[hh, d]
                mf_ref[d, hh] = m_scr[hh, d] * (1.0 / LOG2E)


def _mlstm_call(qkv, og, mg, lw, batch, seq, init, emit_state):
    n = batch * seq
    ln = MLSTM_CHUNK
    nc = seq // ln
    h = M_HEADS
    hs = MLSTM_HEADS_PER_STEP
    ng = h // hs
    g4 = mg[:, MG_LANE0:MG_LANE0 + 4 * h].reshape(n, 4, ng, hs).transpose(2, 0, 1, 3).reshape(ng, n, 4 * hs)
    gcol = jnp.pad(g4, ((0, 0), (0, 0), (0, LANES - 4 * hs)))
    grow = jnp.pad(g4.reshape(ng, batch, nc, ln, 4 * hs).transpose(0, 1, 2, 4, 3),
                   ((0, 0),) * 3 + ((0, 4 * h - 4 * hs), (0, 0)))

    def tok(colblock):
        return pl.BlockSpec((seq, hs * LANES), lambda b, hg: (b, colblock(hg)))

    in_specs = [
        tok(lambda hg: hg), tok(lambda hg: ng + hg), tok(lambda hg: 2 * ng + hg),
        tok(lambda hg: hg),
        pl.BlockSpec((None, seq, LANES), lambda b, hg: (hg, b, 0)),
        pl.BlockSpec((None, None, nc, 4 * h, ln), lambda b, hg: (hg, b, 0, 0, 0)),
        pl.BlockSpec((1, M_DV), lambda b, hg: (0, 0)),
    ]
    args = [qkv, qkv, qkv, og, gcol, grow, lw["m_norm_g"]]
    state_specs = [
        pl.BlockSpec((None, 2, hs, M_DK, M_DV), lambda b, hg: (b, 0, hg, 0, 0)),
        pl.BlockSpec((None, 2, hs, 1, M_DK), lambda b, hg: (b, 0, hg, 0, 0)),
        pl.BlockSpec((None, 2, hs, 1, ln), lambda b, hg: (b, 0, hg, 0, 0)),
    ]
    if init is not None:
        c0, n0, m0 = init
        in_specs += state_specs
        args += [c0, n0.reshape(batch, 2, h, 1, M_DK),
                 jnp.broadcast_to(m0[..., None, None], (batch, 2, h, 1, ln))]
    out_shape = [jax.ShapeDtypeStruct((n, h * M_DV), BF16)]
    out_specs = [tok(lambda hg: hg)]
    if emit_state:
        out_shape += [
            jax.ShapeDtypeStruct((batch, 2, h, M_DK, M_DV), F32),
            jax.ShapeDtypeStruct((batch, 2, h, 1, M_DK), F32),
            jax.ShapeDtypeStruct((batch, 2, h, 1, ln), F32),
        ]
        out_specs += state_specs
    per = (hs, 2, nc)
    outs = pl.pallas_call(
        functools.partial(_mlstm_kernel, n_chunks=nc, heads=hs, has_init=init is not None,
                          emit_state=emit_state),
        out_shape=out_shape,
        grid=(batch, ng),
        in_specs=in_specs,
        out_specs=out_specs,
        scratch_shapes=[
            pltpu.VMEM((seq, LANES), F32), pltpu.VMEM((seq, LANES), F32),
            pltpu.VMEM((nc, 4 * h, ln), F32), pltpu.VMEM((nc, 4 * h, ln), F32),
            pltpu.VMEM(per + (M_DK, M_DV), F32), pltpu.VMEM(per + (1, M_DK), F32),
            pltpu.VMEM(per + (1, ln), F32), pltpu.VMEM(per + (1, ln), F32),
            pltpu.VMEM(per + (M_DK, M_DV), BF16), pltpu.VMEM(per + (1, M_DK), F32), pltpu.VMEM(per + (1, ln), F32),
            pltpu.VMEM((hs, 2, M_DK, M_DV), F32), pltpu.VMEM((hs, 2, 1, M_DK), F32), pltpu.VMEM((hs, 2, 1, ln), F32),
        ],
        compiler_params=_params(2),
        name="mlstm",
    )(*args)
    if emit_state:
        ym, cf, nf, mf = outs
        return ym, (cf, nf[:, :, :, 0, :], mf[:, :, :, 0, 0])
    return outs[0], None


HEAD_V = 64
VT_ROWS = 80


def _ones_row(v_t):
    return jnp.where(lax.broadcasted_iota(jnp.int32, v_t.shape, 0) == HEAD_V, 1.0, v_t)


def _flash_heads(qs, keys_of, vt_chunk, n_keys, o_ref):
    s_ts = [_dot_nt(keys_of(e), qb) for e, qb in enumerate(qs)]
    m = [None] * len(qs)
    acc = [None] * len(qs)
    for c0 in range(0, n_keys, KEY_BLOCK):
        c1 = min(c0 + KEY_BLOCK, n_keys)
        for e in range(len(qs)):
            s = s_ts[e][c0:c1, :]
            mc = jnp.max(s, axis=0, keepdims=True)
            if c0 == 0:
                m[e] = mc
                acc[e] = _dot(vt_chunk(e, c0, c1), jnp.exp2(s - mc).astype(BF16))
            else:
                m_new = jnp.maximum(m[e], mc)
                alpha = jnp.exp2(m[e] - m_new)
                acc[e] = acc[e] * alpha + _dot(vt_chunk(e, c0, c1), jnp.exp2(s - m_new).astype(BF16))
                m[e] = m_new
    outs = [a[0:HEAD_V, :] / a[HEAD_V:HEAD_V + 1, :] for a in acc]
    for e in range(0, len(outs), 2):
        pair = jnp.concatenate([outs[e], outs[e + 1]], axis=0)
        o_ref[:, (e // 2) * LANES:(e // 2 + 1) * LANES] = pair.T.astype(o_ref.dtype)


def _mla_kernel(*refs, past, rotary, heads):
    q_ref, ckv_ref, akr_ref = refs[:3]
    pos = 3
    if past:
        ckvc_ref, krc_ref = refs[pos:pos + 2]
        pos += 2
    wk_ref, wvt_ref, kg_ref = refs[pos:pos + 3]
    pos += 3
    if rotary:
        ck_ref, sk_ref = refs[pos:pos + 2]
        pos += 2
    o_ref, kbuf, vbuf = refs[pos:]
    seq = ckv_ref.shape[0]

    @pl.when(pl.program_id(2) == 0)
    def _():
        for e in range(heads):
            wk_h = wk_ref[:, e * LANES:(e + 1) * LANES]
            wvt_h = wvt_ref[e * VT_ROWS:(e + 1) * VT_ROWS, :]

            def keys(ckv, kr):
                cb = ckv.astype(BF16)
                kcat = _dot(cb, wk_h) + kr
                return _ones_row(_dot_nt(wvt_h, cb)), _rms(kcat, A_QK) * kg_ref[...]

            v_t, kn = keys(ckv_ref[...], akr_ref[...])
            if rotary:
                kn = _rope(kn, ck_ref[...], sk_ref[...], A_ROPE // 2, LANES)
            kbuf[e, past:past + seq, :] = kn.astype(BF16)
            vbuf[e, :, past:past + seq] = v_t.astype(BF16)
            if past:
                v_tc, knc = keys(ckvc_ref[...], krc_ref[...])
                kbuf[e, 0:past, :] = knc.astype(BF16)
                vbuf[e, :, 0:past] = v_tc.astype(BF16)

    qs = [q_ref[:, e * LANES:(e + 1) * LANES] for e in range(heads)]
    _flash_heads(qs, lambda e: kbuf[e], lambda e, c0, c1: vbuf[e, :, c0:c1], past + seq, o_ref)


def _mla_call(qa, ckv, akr, lw, batch, seq, cache, rope_tabs):
    n = batch * seq
    tq = min(Q_BLOCK, seq)
    nq = seq // tq
    past = 0 if cache is None else cache[0].shape[1]
    heads = MLA_HEADS_PER_STEP if nq > 1 else A_HEADS
    in_specs = [
        pl.BlockSpec((tq, heads * LANES), lambda b, hg, qi: (b * nq + qi, hg)),
        pl.BlockSpec((seq, LANES), lambda b, hg, qi: (b, 0)),
        pl.BlockSpec((seq, LANES), lambda b, hg, qi: (b, 0)),
    ]
    args = [qa, ckv, akr]
    if past:
        in_specs += [pl.BlockSpec((None, past, LANES), lambda b, hg, qi: (b, 0, 0))] * 2
        args += [cache[0], cache[1]]
    in_specs += [
        pl.BlockSpec((A_KV_LORA, heads * LANES), lambda b, hg, qi: (0, hg)),
        pl.BlockSpec((heads * VT_ROWS, A_KV_LORA), lambda b, hg, qi: (hg, 0)),
        pl.BlockSpec((1, LANES), lambda b, hg, qi: (0, 0)),
    ]
    args += [lw["w_uk"], lw["w_uvt"], lw["a_knorm_g"]]
    if rope_tabs is not None:
        cos, sin = rope_tabs
        in_specs += [pl.BlockSpec((seq, LANES), lambda b, hg, qi: (0, 0))] * 2
        args += [cos, sin]
    return pl.pallas_call(
        functools.partial(_mla_kernel, past=past, rotary=rope_tabs is not None, heads=heads),
        out_shape=jax.ShapeDtypeStruct((n, A_HEADS * A_V), BF16),
        grid=(batch, A_HEADS // heads, nq),
        in_specs=in_specs,
        out_specs=pl.BlockSpec((tq, heads * A_V), lambda b, hg, qi: (b * nq + qi, hg)),
        scratch_shapes=[pltpu.VMEM((heads, past + seq, LANES), BF16),
                        pltpu.VMEM((heads, VT_ROWS, past + seq), BF16)],
        compiler_params=_params(3),
        name="mla_attn",
    )(*args)


def _gqa_kernel(*refs, past, groups):
    q_ref, k_ref, v_ref = refs[:3]
    pos = 3
    if past:
        kc_ref, vc_ref = refs[pos:pos + 2]
        pos += 2
    o_ref, kbuf, vbuf = refs[pos:]
    seq = k_ref.shape[0]
    qw = G_GROUP * G_HD

    @pl.when(pl.program_id(2) == 0)
    def _():
        for gi in range(groups):
            grp = pl.program_id(1) * groups + gi

            r = lax.broadcasted_iota(jnp.int32, (VT_ROWS, LANES), 0)
            c = lax.broadcasted_iota(jnp.int32, (VT_ROWS, LANES), 1)
            pick = jnp.where((c == r + grp * G_HD) & (r < G_HD), 1.0, 0.0).astype(BF16)

            def fill(r0, k, v):
                low = _lane(k.shape) < G_HD
                k_lo = jnp.where(grp == 0, k, pltpu.roll(k, G_HD, 1))
                rows = k.shape[0]
                kbuf[gi, r0:r0 + rows, :] = jnp.where(low, k_lo, pltpu.roll(k_lo, G_HD, 1)).astype(BF16)
                vbuf[gi, :, r0:r0 + rows] = _ones_row(_dot_nt(pick, v.astype(BF16))).astype(BF16)

            fill(past, k_ref[...], v_ref[...])
            if past:
                fill(0, kc_ref[...], vc_ref[...])

    qs = []
    for gi in range(groups):
        for j in range(G_GROUP):
            col = q_ref[:, gi * qw + (j // 2) * LANES:gi * qw + (j // 2 + 1) * LANES]
            keep = (_lane(col.shape) < G_HD) == (j % 2 == 0)
            qs.append(jnp.where(keep, col, jnp.zeros_like(col)))
    _flash_heads(qs, lambda e: kbuf[e // G_GROUP], lambda e, c0, c1: vbuf[e // G_GROUP, :, c0:c1],
                 past + seq, o_ref)


def _gqa_call(gq, gk, gv, batch, seq, cache):
    n = batch * seq
    tq = min(Q_BLOCK, seq)
    nq = seq // tq
    past = 0 if cache is None else cache[0].shape[1]
    groups = GQA_GROUPS_PER_STEP if nq > 1 else G_KV_HEADS
    qw = G_GROUP * G_HD
    kvw = G_KV_HEADS * G_HD
    in_specs = [
        pl.BlockSpec((tq, groups * qw), lambda b, g, qi: (b * nq + qi, g)),
        pl.BlockSpec((seq, kvw), lambda b, g, qi: (b, 0)),
        pl.BlockSpec((seq, kvw), lambda b, g, qi: (b, 0)),
    ]
    args = [gq, gk, gv]
    if past:
        in_specs += [pl.BlockSpec((None, past, kvw), lambda b, g, qi: (b, 0, 0))] * 2
        args += [cache[0], cache[1]]
    return pl.pallas_call(
        functools.partial(_gqa_kernel, past=past, groups=groups),
        out_shape=jax.ShapeDtypeStruct((n, G_HEADS * G_HD), BF16),
        grid=(batch, G_KV_HEADS // groups, nq),
        in_specs=in_specs,
        out_specs=pl.BlockSpec((tq, groups * qw), lambda b, g, qi: (b * nq + qi, g)),
        scratch_shapes=[pltpu.VMEM((groups, past + seq, LANES), BF16),
                        pltpu.VMEM((groups, VT_ROWS, past + seq), BF16)],
        compiler_params=_params(3),
        name="gqa_attn",
    )(*args)


def _merge_ffn_kernel(x_ref, ym_ref, ya_ref, yg_ref, gates_ref, mod_ref, g2_ref, wb_ref, wo_ref, wfi_ref,
                      wfo_ref, o_ref, *, ff_chunks):
    x = x_ref[...]
    d = x.shape[-1]
    mixed = None
    for i, y_ref in enumerate((ym_ref, ya_ref, yg_ref)):
        br = _dot(y_ref[...].astype(BF16), wb_ref[i]) * gates_ref[:, i * d:(i + 1) * d]
        mixed = br if mixed is None else mixed + br
    gt1 = mod_ref[2:3, :]
    x1 = x + gt1 * _dot(mixed.astype(BF16), wo_ref[...])

    sh2 = mod_ref[3:4, :]
    sc2 = mod_ref[4:5, :]
    gt2 = mod_ref[5:6, :]
    h2 = ((_rms(x1, d) * g2_ref[...]) * (1.0 + sc2) + sh2).astype(BF16)
    d_ff = wfo_ref.shape[0]
    acc = None
    for c0, c1 in ff_chunks:
        ug = _dot(h2, wfi_ref[:, c0:c1])
        uv = _dot(h2, wfi_ref[:, d_ff + c0:d_ff + c1])
        act = (ug * _sigmoid(ug) * uv).astype(BF16)
        part = _dot(act, wfo_ref[c0:c1, :])
        acc = part if acc is None else acc + part
    o_ref[...] = x1 + gt2 * acc


def _merge_ffn_call(x, ym, ya, yg, gates, mod, lw, mod_row0, rows_per_mod):
    n, d = x.shape
    tm = TOKEN_BLOCK
    d_ff = lw["w_ffn_out"].shape[0]
    split = -(-(d_ff // 2) // MXU_COLS) * MXU_COLS
    ff_chunks = ((0, split), (split, d_ff)) if 0 < split < d_ff else ((0, d_ff),)

    def mod_idx(i):
        return (mod_row0 + (i * tm) // rows_per_mod, 0, 0)

    def rows(width):
        return pl.BlockSpec((tm, width), lambda i: (i, 0))

    return pl.pallas_call(
        functools.partial(_merge_ffn_kernel, ff_chunks=ff_chunks),
        out_shape=jax.ShapeDtypeStruct((n, d), F32),
        grid=(n // tm,),
        in_specs=[
            rows(d), rows(BRANCH_WIDTH), rows(BRANCH_WIDTH), rows(BRANCH_WIDTH), rows(N_BRANCH * d),
            pl.BlockSpec((None, N_MOD, d), mod_idx),
            _resident((1, d)),
            _resident((N_BRANCH, BRANCH_WIDTH, d)),
            _resident((d, d)),
            _resident((d, 2 * d_ff)),
            _resident((d_ff, d)),
        ],
        out_specs=rows(d),
        compiler_params=_params(1),
        name="merge_ffn",
    )(x, ym, ya, yg, gates, mod, lw["norm2_g"], lw["w_branch"], lw["w_out"], lw["w_ffn_in"], lw["w_ffn_out"])


def _pad_cols(w, width):
    return jnp.pad(w, ((0, 0), (0, width - w.shape[1])))


def _mla_q_layout(a):
    lead = a.shape[:-1]
    a = a.reshape(lead + (A_HEADS, A_QK))
    z = jnp.zeros(lead + (A_HEADS, LANES - A_QK), a.dtype)
    return jnp.concatenate([a[..., A_NOPE:], z, a[..., :A_NOPE]], axis=-1).reshape(lead + (A_HEADS * LANES,))


def _layer_weights(l, w_in, b_mgate, norm1_g, m_norm_g, a_qlora_g, a_kvlora_g, w_uq, w_ukv, a_qnorm_g,
                   a_knorm_g, g_qnorm_g, g_knorm_g, w_branch, w_out, norm2_g, w_ffn_in, w_ffn_out):
    d = w_in.shape[1]
    hw = M_HEADS * M_DK
    sizes = (N_BRANCH * d, hw, hw, hw, hw, 4 * M_HEADS, A_Q_LORA, A_KV_LORA, A_ROPE, G_HEADS * G_HD,
             G_KV_HEADS * G_HD, G_KV_HEADS * G_HD)
    splits = np.cumsum(sizes)[:-1].tolist()
    (wg, wmq, wmk, wmv, wmo, wmg, waq, wakv, wakr, wgq, wgk, wgv) = jnp.split(w_in[l], splits, axis=1)
    misc = _pad_cols(jnp.concatenate([wakr, wmg], axis=1), LANES)
    w_in_p = jnp.concatenate([wg, wmq, wmk, wmv, wmo, waq, wakv, misc, wgq, wgk, wgv], axis=1).astype(BF16)
    ukv = w_ukv[l].reshape(A_KV_LORA, A_HEADS, A_NOPE + A_V)
    uk = jnp.pad(ukv[..., :A_NOPE], ((0, 0), (0, 0), (LANES - A_NOPE, 0))).reshape(A_KV_LORA, A_HEADS * LANES)
    uvt = jnp.pad(ukv[..., A_NOPE:].transpose(1, 2, 0), ((0, 0), (0, VT_ROWS - A_V), (0, 0)))
    uvt = uvt.reshape(A_HEADS * VT_ROWS, A_KV_LORA)
    qg = _mla_q_layout(jnp.tile(a_qnorm_g[l], A_HEADS)[None, :])[:, :LANES]
    kg = _mla_q_layout(jnp.tile(a_knorm_g[l], A_HEADS)[None, :])[:, :LANES]
    return dict(
        w_in=w_in_p,
        b_mgate=jnp.pad(b_mgate[l][None, :], ((0, 0), (MG_LANE0, LANES - MG_LANE0 - 4 * M_HEADS))),
        norm1_g=norm1_g[l][None, :],
        m_norm_g=m_norm_g[l][None, :],
        a_qlora_g=a_qlora_g[l][None, :],
        a_kvlora_g=a_kvlora_g[l][None, :],
        w_uq=_mla_q_layout(w_uq[l]).astype(BF16),
        w_uk=uk.astype(BF16),
        w_uvt=uvt.astype(BF16),
        a_qnorm_g=qg,
        a_knorm_g=kg,
        g_qnorm_g=jnp.tile(g_qnorm_g[l], LANES // G_HD)[None, :],
        g_knorm_g=jnp.tile(g_knorm_g[l], G_KV_HEADS)[None, :],
        w_branch=w_branch[l].astype(BF16),
        w_out=w_out[l].astype(BF16),
        norm2_g=norm2_g[l][None, :],
        w_ffn_in=w_ffn_in[l].astype(BF16),
        w_ffn_out=w_ffn_out[l].astype(BF16),
    )


def _axial_angles(seq, rot_dim):
    n_freq = rot_dim // 4
    freqs = ROPE_BASE ** (-jnp.arange(n_freq, dtype=F32) / n_freq)
    t = jnp.arange(seq)
    row = (t // GRID_W).astype(F32)
    col = (t % GRID_W).astype(F32)
    return jnp.concatenate([row[:, None] * freqs, col[:, None] * freqs], axis=-1)


def _rope_tables(seq):
    ang = _axial_angles(seq, A_ROPE)
    one = jnp.ones((seq, LANES - A_ROPE), F32)
    mla_cos = jnp.concatenate([jnp.cos(ang), jnp.cos(ang), one], axis=-1)
    mla_sin = jnp.concatenate([-jnp.sin(ang), jnp.sin(ang), 0.0 * one], axis=-1)
    ang = _axial_angles(seq, G_HD)
    cos = jnp.concatenate([jnp.cos(ang), jnp.cos(ang)], axis=-1)
    sin = jnp.concatenate([-jnp.sin(ang), jnp.sin(ang)], axis=-1)
    gqa = (jnp.tile(cos, (1, LANES // G_HD)), jnp.tile(sin, (1, LANES // G_HD)))
    return (mla_cos, mla_sin), gqa


def _layer(x, mod, lw, batch, seq, mod_row0, rows_per_mod, ctx, rope):
    (gates, qkv, og, mg, qa, ckv, akr, gq, gk, gv) = _inproj_call(x, mod, lw, mod_row0, rows_per_mod, rope, seq)
    if ctx is None:
        ym, state = _mlstm_call(qkv, og, mg, lw, batch, seq, None, True)
        ya = _mla_call(qa, ckv, akr, lw, batch, seq, None, None)
        yg = _gqa_call(gq, gk, gv, batch, seq, None)
        new_ctx = dict(state=state, ckv=ckv, kr=akr[:, :A_ROPE], gk=gk, gv=gv)
    else:
        ym, _ = _mlstm_call(qkv, og, mg, lw, batch, seq, ctx["mlstm"], False)
        ya = _mla_call(qa, ckv, akr, lw, batch, seq, ctx["mla"], rope[0])
        yg = _gqa_call(gq, gk, gv, batch, seq, ctx["gqa"])
        new_ctx = None
    x = _merge_ffn_call(x, ym, ya, yg, gates, mod, lw, mod_row0, rows_per_mod)
    return x, new_ctx


def kernel(x_prompt, x_sample, state_mlstm_C, state_mlstm_n, state_mlstm_m, cache_mla_ckv, cache_mla_krope,
           cache_gqa_k, cache_gqa_v, c, c_ctx, w_mod, b_mod, norm1_g, w_in, b_mgate, m_norm_g, a_qlora_g,
           a_kvlora_g, w_uq, w_ukv, a_qnorm_g, a_knorm_g, g_qnorm_g, g_knorm_g, w_branch, w_out, norm2_g,
           w_ffn_in, w_ffn_out):
    batch, seq, d = x_prompt.shape
    dbatch, dseq, _ = x_sample.shape
    depth = w_in.shape[0]
    past = cache_mla_ckv.shape[2]

    n_rows = -(-(1 + dbatch) // SUBLANES) * SUBLANES
    cond = jnp.concatenate([c_ctx[None, :], c, jnp.zeros((n_rows - 1 - dbatch, d), F32)], axis=0)
    mod_all = _mod_call(cond, w_mod, b_mod).reshape(depth, n_rows, N_MOD, d)

    rope = _rope_tables(dseq)
    xp = x_prompt.reshape(batch * seq, d)
    xs = x_sample.reshape(dbatch * dseq, d)
    ctx_layers = []
    for l in range(depth):
        lw = _layer_weights(l, w_in, b_mgate, norm1_g, m_norm_g, a_qlora_g, a_kvlora_g, w_uq, w_ukv,
                            a_qnorm_g, a_knorm_g, g_qnorm_g, g_knorm_g, w_branch, w_out, norm2_g, w_ffn_in,
                            w_ffn_out)
        xp, st = _layer(xp, mod_all[l], lw, batch, seq, 0, batch * seq, None, None)
        ctx_layers.append(st)
        ctx = dict(
            mlstm=(state_mlstm_C[:, l], state_mlstm_n[:, l], state_mlstm_m[:, l]),
            mla=(cache_mla_ckv[:, l], _pad_cols(cache_mla_krope[:, l].reshape(dbatch * past, A_ROPE), LANES)
                 .reshape(dbatch, past, LANES)),
            gqa=(cache_gqa_k[:, l].reshape(dbatch, past, G_KV_HEADS * G_HD),
                 cache_gqa_v[:, l].reshape(dbatch, past, G_KV_HEADS * G_HD)),
        )
        xs, _ = _layer(xs, mod_all[l], lw, dbatch, dseq, 1, dseq, ctx, rope)

    def stack(fn):
        return jnp.stack([fn(s) for s in ctx_layers], axis=1)

    new_c = stack(lambda s: s["state"][0])
    new_n = stack(lambda s: s["state"][1])
    new_m = stack(lambda s: s["state"][2])
    new_ckv = stack(lambda s: s["ckv"].reshape(batch, seq, A_KV_LORA))
    new_kr = stack(lambda s: s["kr"].reshape(batch, seq, A_ROPE))
    new_gk = stack(lambda s: s["gk"].reshape(batch, seq, G_KV_HEADS, G_HD))
    new_gv = stack(lambda s: s["gv"].reshape(batch, seq, G_KV_HEADS, G_HD))
    return (xp.reshape(batch, seq, d), xs.reshape(dbatch, dseq, d), new_c, new_n, new_m, new_ckv, new_kr,
            new_gk, new_gv)
```

```python
import functools

import numpy as np
import jax
import jax.numpy as jnp
from jax import lax
from jax.experimental import pallas as pl
from jax.experimental.pallas import tpu as pltpu

F32 = jnp.float32
BF16 = jnp.bfloat16

LANES = 128
SUBLANES = 8
VMEM_LIMIT_BYTES = 56 * 1024 * 1024

EPS = 1e-6
ROPE_BASE = 10000.0
GRID_W = 64

M_HEADS = 4
M_DK = 128
M_DV = 128
A_HEADS = 8
A_NOPE = 64
A_ROPE = 32
A_QK = A_NOPE + A_ROPE
A_V = 64
A_Q_LORA = 256
A_KV_LORA = 128
G_HEADS = 8
G_KV_HEADS = 2
G_GROUP = G_HEADS // G_KV_HEADS
G_HD = 64
N_BRANCH = 3
BRANCH_WIDTH = 512
N_MOD = 6

TOKEN_BLOCK = 512
Q_BLOCK = 256
KEY_BLOCK = 256
MLA_HEADS_PER_STEP = 4
GQA_GROUPS_PER_STEP = 2
MLSTM_CHUNK = 128
MLSTM_HEADS_PER_STEP = 4
LOG2E = 1.4426950408889634


def _params(n_axes):
    return pltpu.CompilerParams(dimension_semantics=("arbitrary",) * n_axes,
                                vmem_limit_bytes=VMEM_LIMIT_BYTES)


def _resident(shape):
    nd = len(shape)
    return pl.BlockSpec(shape, lambda *_: (0,) * nd, pipeline_mode=pl.Buffered(1))


def _lane(shape, axis=None):
    return lax.broadcasted_iota(jnp.int32, shape, len(shape) - 1 if axis is None else axis)


def _dot(a, b):
    return jnp.dot(a, b, preferred_element_type=F32)


def _dot_nt(a, b):
    return lax.dot_general(a, b, (((1,), (1,)), ((), ())), preferred_element_type=F32)


def _split3(a):
    hi = a.astype(BF16)
    r1 = a - hi.astype(F32)
    mid = r1.astype(BF16)
    lo = (r1 - mid.astype(F32)).astype(BF16)
    return hi, mid, lo


def _dot01(a, m01):
    hi, mid, lo = _split3(a)
    return _dot(hi, m01) + _dot(mid, m01) + _dot(lo, m01)


def _dot01_left(m01, a):
    hi, mid, lo = _split3(a)
    return _dot(m01, hi) + _dot(m01, mid) + _dot(m01, lo)


def _sigmoid(x):
    return 0.5 * jnp.tanh(0.5 * x) + 0.5


def _log_sigmoid(x):
    return jnp.minimum(x, 0.0) - jnp.log(1.0 + jnp.exp(-jnp.abs(x)))


def _rms(x, width):
    ms = jnp.sum(x * x, axis=-1, keepdims=True) * (1.0 / width)
    return x * lax.rsqrt(ms + EPS)


def _rope(x, cos, sin_signed, half, period):
    n = x.shape[-1]
    first = (_lane(x.shape) % period) < half
    swapped = jnp.where(first, pltpu.roll(x, n - half, x.ndim - 1), pltpu.roll(x, half, x.ndim - 1))
    return x * cos + swapped * sin_signed


def _mod_kernel(c_ref, w_ref, b_ref, o_ref):
    c = c_ref[...]
    a = c * _sigmoid(c)
    hi, mid, lo = _split3(a)
    w = w_ref[...]
    whi = w.astype(BF16)
    wlo = (w - whi.astype(F32)).astype(BF16)
    acc = _dot(hi, whi) + _dot(mid, whi) + _dot(hi, wlo) + _dot(lo, whi) + _dot(mid, wlo)
    o_ref[...] = acc + b_ref[...]


def _mod_call(cond, w_mod, b_mod):
    depth, d, n = w_mod.shape
    rows = cond.shape[0]
    tn = 1536
    return pl.pallas_call(
        _mod_kernel,
        out_shape=jax.ShapeDtypeStruct((depth, rows, n), F32),
        grid=(depth, n // tn),
        in_specs=[
            pl.BlockSpec((rows, d), lambda l, j: (0, 0)),
            pl.BlockSpec((None, d, tn), lambda l, j: (l, 0, j)),
            pl.BlockSpec((None, 1, tn), lambda l, j: (l, 0, j)),
        ],
        out_specs=pl.BlockSpec((None, rows, tn), lambda l, j: (l, 0, j)),
        compiler_params=_params(2),
        name="adaln_mod",
    )(cond, w_mod, b_mod.reshape(depth, 1, n))


MXU_COLS = 256
_GATE0, _GATE1 = 0, 3072
_MQ0 = 3072
_MK0 = 3584
_MV0 = 4096
_MO0 = 4608
_AQ0 = 5120
_AKV0 = 5376
_MISC0 = 5504
_GQ0 = 5632
_GK0 = 6144
_GV0 = 6272
_WIN_COLS = 6400
MG_LANE0 = A_ROPE


def _head_pair_ms(x):
    low = (_lane(x.shape) % LANES) < G_HD
    sq = x * x
    cols = []
    for c0 in range(0, x.shape[-1], LANES):
        s = sq[:, c0:c0 + LANES]
        lo = jnp.sum(jnp.where(low[:, c0:c0 + LANES], s, 0.0), axis=-1, keepdims=True)
        hi = jnp.sum(jnp.where(low[:, c0:c0 + LANES], 0.0, s), axis=-1, keepdims=True)
        cols.append(jnp.where(low[:, c0:c0 + LANES], lo, hi))
    ms = cols[0] if len(cols) == 1 else jnp.concatenate(cols, axis=-1)
    return ms * (1.0 / G_HD)


def _inproj_kernel(*refs, rotary):
    (x_ref, mod_ref, g1_ref, w_ref, bmg_ref, gql_ref, wuq_ref, gkvl_ref, gkn_ref, aqn_ref, gqn_ref) = refs[:11]
    pos = 11
    if rotary:
        ca_ref, sa_ref, cg_ref, sg_ref = refs[pos:pos + 4]
        pos += 4
    (gates_ref, qkv_ref, og_ref, mg_ref, qa_ref, ckv_ref, akr_ref, gq_ref, gk_ref, gv_ref) = refs[pos:]
    x = x_ref[...]
    d = x.shape[-1]
    sh1 = mod_ref[0:1, :]
    sc1 = mod_ref[1:2, :]
    h = (_rms(x, d) * g1_ref[...]) * (1.0 + sc1) + sh1
    hb = h.astype(BF16)

    def proj(c0, width):
        return _dot(hb, w_ref[:, c0:c0 + width])

    aq = _rms(proj(_AQ0, A_Q_LORA), A_Q_LORA) * gql_ref[...]
    qa = _dot(aq.astype(BF16), wuq_ref[...])
    for hd in range(A_HEADS):
        qh = _rms(qa[:, hd * LANES:(hd + 1) * LANES], A_QK) * aqn_ref[...]
        if rotary:
            qh = _rope(qh, ca_ref[...], sa_ref[...], A_ROPE // 2, LANES)
        qa_ref[:, hd * LANES:(hd + 1) * LANES] = (qh * (A_QK ** -0.5 * LOG2E)).astype(qa_ref.dtype)
    akv_misc = proj(_AKV0, A_KV_LORA + LANES)
    ckv_ref[...] = _rms(akv_misc[:, :A_KV_LORA], A_KV_LORA) * gkvl_ref[...]
    misc = akv_misc[:, A_KV_LORA:]
    akr_ref[...] = jnp.where(_lane(misc.shape) < A_ROPE, misc, 0.0)
    mg_ref[...] = misc + bmg_ref[...]

    gq_all = proj(_GQ0, G_HEADS * G_HD)
    for c0 in range(0, G_HEADS * G_HD, LANES):
        gq = gq_all[:, c0:c0 + LANES]
        gq = gq * lax.rsqrt(_head_pair_ms(gq) + EPS) * gqn_ref[...]
        if rotary:
            gq = _rope(gq, cg_ref[...], sg_ref[...], G_HD // 2, G_HD)
        gq_ref[:, c0:c0 + LANES] = (gq * (G_HD ** -0.5 * LOG2E)).astype(gq_ref.dtype)
    kvw = G_KV_HEADS * G_HD
    gkv = proj(_GK0, 2 * kvw)
    gk = gkv[:, :kvw]
    gk = gk * lax.rsqrt(_head_pair_ms(gk) + EPS) * gkn_ref[...]
    if rotary:
        gk = _rope(gk, cg_ref[...], sg_ref[...], G_HD // 2, G_HD)
    gk_ref[...] = gk
    gv_ref[...] = gkv[:, kvw:]

    hw = M_HEADS * M_DK
    gates_ref[...] = _sigmoid(proj(_GATE0, _GATE1 - _GATE0)).astype(gates_ref.dtype)
    og_ref[...] = _sigmoid(proj(_MO0, hw)).astype(og_ref.dtype)
    qkv_ref[:, 0:hw] = proj(_MQ0, hw).astype(qkv_ref.dtype)
    qkv_ref[:, hw:2 * hw] = (proj(_MK0, hw) * (M_DK ** -0.5)).astype(qkv_ref.dtype)
    qkv_ref[:, 2 * hw:3 * hw] = proj(_MV0, hw).astype(qkv_ref.dtype)


def _inproj_call(x, mod, lw, mod_row0, rows_per_mod, rope, seq):
    n, d = x.shape
    tm = TOKEN_BLOCK
    hw = M_HEADS * M_DK

    def mod_idx(i):
        return (mod_row0 + (i * tm) // rows_per_mod, 0, 0)

    def rows(width):
        return pl.BlockSpec((tm, width), lambda i: (i, 0))

    in_specs = [
        rows(d),
        pl.BlockSpec((None, N_MOD, d), mod_idx),
        _resident((1, d)),
        _resident((d, _WIN_COLS)),
        _resident((1, LANES)),
        _resident((1, A_Q_LORA)),
        _resident((A_Q_LORA, A_HEADS * LANES)),
        _resident((1, A_KV_LORA)),
        _resident((1, LANES)),
        _resident((1, LANES)),
        _resident((1, LANES)),
    ]
    args = [x, mod, lw["norm1_g"], lw["w_in"], lw["b_mgate"], lw["a_qlora_g"], lw["w_uq"], lw["a_kvlora_g"],
            lw["g_knorm_g"], lw["a_qnorm_g"], lw["g_qnorm_g"]]
    if rope is not None:
        blocks_per_seq = seq // tm
        in_specs += [pl.BlockSpec((tm, LANES), lambda i: (i % blocks_per_seq, 0))] * 4
        args += [rope[0][0], rope[0][1], rope[1][0], rope[1][1]]
    out_widths = [3 * d, 3 * hw, hw, LANES, A_HEADS * LANES, A_KV_LORA, LANES, G_HEADS * G_HD,
                  G_KV_HEADS * G_HD, G_KV_HEADS * G_HD]
    out_dtypes = [BF16, BF16, BF16, F32, BF16, F32, F32, BF16, F32, F32]
    return pl.pallas_call(
        functools.partial(_inproj_kernel, rotary=rope is not None),
        out_shape=[jax.ShapeDtypeStruct((n, w), t) for w, t in zip(out_widths, out_dtypes)],
        grid=(n // tm,),
        in_specs=in_specs,
        out_specs=[rows(w) for w in out_widths],
        compiler_params=_params(1),
        name="in_proj",
    )(*args)


def _mlstm_kernel(*refs, n_chunks, heads, has_init, emit_state):
    q_ref, k_ref, v_ref, og_ref, gcol_ref, grow_ref, gn_ref = refs[:7]
    pos = 7
    if has_init:
        c0_ref, n0_ref, m0_ref = refs[pos:pos + 3]
        pos += 3
    y_ref = refs[pos]
    pos += 1
    if emit_state:
        cf_ref, nf_ref, mf_ref = refs[pos:pos + 3]
        pos += 3
    (pcol_scr, lfc_scr, prow_scr, lfr_scr, u_scr, nu_scr, gm_scr, tot_scr, cs_scr, ns_scr, ms_scr,
     c_scr, n_scr, m_scr) = refs[pos:]

    ln = MLSTM_CHUNK
    row = lax.broadcasted_iota(jnp.int32, (ln, ln), 0)
    col = lax.broadcasted_iota(jnp.int32, (ln, ln), 1)
    lower = col <= row
    upper = col >= row
    tril = jnp.where(lower, 1.0, 0.0).astype(BF16)
    triu = jnp.where(upper, 1.0, 0.0).astype(BF16)
    ones = jnp.ones((ln, ln), BF16)

    def gate_lanes(hh, d):
        return 2 * d * heads + hh, (2 * d + 1) * heads + hh

    def summaries(c, carry):
        r0 = pl.multiple_of(c * ln, ln)
        lf = _log_sigmoid(gcol_ref[pl.ds(r0, ln), :]) * LOG2E
        pcol_scr[pl.ds(r0, ln), :] = _dot01_left(tril, lf)
        lfc_scr[pl.ds(r0, ln), :] = lf
        gr = grow_ref[c] * LOG2E
        lfr = _log_sigmoid(grow_ref[c]) * LOG2E
        pr = _dot01(lfr, triu)
        prow_scr[c] = pr
        lfr_scr[c] = lfr
        for hh in range(heads):
            kb = k_ref[pl.ds(r0, ln), hh * LANES:(hh + 1) * LANES]
            k_t = kb.astype(F32).T
            v = v_ref[pl.ds(r0, ln), hh * LANES:(hh + 1) * LANES]
            for d in range(2):
                ji, jf = gate_lanes(hh, d)
                tot = pr[jf:jf + 1, ln - 1:ln]
                b_row = pr[jf:jf + 1, :] if d == 0 else tot - pr[jf:jf + 1, :] + lfr[jf:jf + 1, :]
                g_row = tot - b_row + gr[ji:ji + 1, :]
                gmax = jnp.max(g_row, axis=-1, keepdims=True)
                wg = jnp.exp2(g_row - gmax)
                u_scr[hh, d, c] = _dot((k_t * wg).astype(BF16), v)
                nu_scr[hh, d, c] = _dot(jnp.broadcast_to(wg, (SUBLANES, ln)).astype(BF16), kb)[0:1, :]
                gm_scr[hh, d, c] = jnp.broadcast_to(gmax, (1, ln))
                tot_scr[hh, d, c] = jnp.broadcast_to(tot, (1, ln))
        return carry

    lax.fori_loop(0, n_chunks, summaries, 0, unroll=min(2, n_chunks))

    for hh in range(heads):
        for d in range(2):
            if has_init:
                c_scr[hh, d] = c0_ref[d, hh]
                n_scr[hh, d] = n0_ref[d, hh]
                m_scr[hh, d] = m0_ref[d, hh] * LOG2E
            else:
                c_scr[hh, d] = jnp.zeros((M_DK, M_DV), F32)
                n_scr[hh, d] = jnp.zeros((1, M_DK), F32)
                m_scr[hh, d] = jnp.zeros((1, ln), F32)

    def scan(j, carry):
        for hh in range(heads):
            for d in range(2):
                c = j if d == 0 else n_chunks - 1 - j
                cst = c_scr[hh, d]
                nst = n_scr[hh, d]
                mst = m_scr[hh, d]
                cs_scr[hh, d, c] = cst.astype(BF16)
                ns_scr[hh, d, c] = nst
                ms_scr[hh, d, c] = mst
                gmax = gm_scr[hh, d, c]
                total = tot_scr[hh, d, c] + mst
                m_new = jnp.maximum(total, gmax)
                decay = jnp.exp2(total - m_new)
                scale = jnp.exp2(gmax - m_new)
                c_scr[hh, d] = cst * decay + u_scr[hh, d, c] * scale
                n_scr[hh, d] = nst * decay + nu_scr[hh, d, c] * scale
                m_scr[hh, d] = m_new
        return carry

    lax.fori_loop(0, n_chunks, scan, 0)

    def readout(c, carry):
        r0 = pl.multiple_of(c * ln, ln)
        pc = pcol_scr[pl.ds(r0, ln), :]
        lf = lfc_scr[pl.ds(r0, ln), :]
        gr = grow_ref[c] * LOG2E
        pr = prow_scr[c]
        lfr = lfr_scr[c]
        tot_all = pc[ln - 1:ln, :]
        for hh in range(heads):
            q = q_ref[pl.ds(r0, ln), hh * LANES:(hh + 1) * LANES]
            k = k_ref[pl.ds(r0, ln), hh * LANES:(hh + 1) * LANES]
            v = v_ref[pl.ds(r0, ln), hh * LANES:(hh + 1) * LANES]
            qk = _dot_nt(q, k)
            hsum = None
            for d in range(2):
                ji, jf = gate_lanes(hh, d)
                if d == 0:
                    b_col = pc[:, jf:jf + 1]
                    r_row = gr[ji:ji + 1, :] - pr[jf:jf + 1, :]
                    mask = lower
                else:
                    tot = tot_all[:, jf:jf + 1]
                    b_col = tot - pc[:, jf:jf + 1] + lf[:, jf:jf + 1]
                    r_row = gr[ji:ji + 1, :] - (tot - pr[jf:jf + 1, :] + lfr[jf:jf + 1, :])
                    mask = upper
                b_colb = jnp.broadcast_to(b_col, (ln, ln))
                d_log = jnp.where(mask, b_colb + r_row, -jnp.inf)
                dmax = jnp.broadcast_to(jnp.max(d_log, axis=-1, keepdims=True), (ln, ln))
                sb = (qk * jnp.exp2(d_log - dmax)).astype(BF16)
                num_loc = _dot(sb, v)
                den_loc = _dot(sb, ones)
                qc = _dot(q, cs_scr[hh, d, c])
                qn = _dot_nt(q, jnp.broadcast_to(ns_scr[hh, d, c], (ln, M_DK)).astype(BF16))
                inter = b_colb + ms_scr[hh, d, c]
                m_t = jnp.maximum(inter, dmax)
                w_inter = jnp.exp2(inter - m_t)
                w_loc = jnp.exp2(dmax - m_t)
                num = qc * w_inter + num_loc * w_loc
                den = qn * w_inter + den_loc * w_loc
                hc = num / jnp.maximum(jnp.abs(den), jnp.exp2(-m_t))
                hsum = hc if hsum is None else hsum + hc
            y_ref[pl.ds(r0, ln), hh * LANES:(hh + 1) * LANES] = (
                _rms(hsum, M_DV) * gn_ref[...] * og_ref[pl.ds(r0, ln), hh * LANES:(hh + 1) * LANES]
            ).astype(y_ref.dtype)
        return carry

    lax.fori_loop(0, n_chunks, readout, 0, unroll=min(2, n_chunks))

    if emit_state:
        for hh in range(heads):
            for d in range(2):
                cf_ref[d, hh] = c_scr[hh, d]
                nf_ref[d, hh] = n_scr[hh, d]
                mf_ref[d, hh] = m_scr[hh, d] * (1.0 / LOG2E)


def _mlstm_call(qkv, og, mg, lw, batch, seq, init, emit_state):
    n = batch * seq
    ln = MLSTM_CHUNK
    nc = seq // ln
    h = M_HEADS
    hs = MLSTM_HEADS_PER_STEP
    ng = h // hs
    g4 = mg[:, MG_LANE0:MG_LANE0 + 4 * h].reshape(n, 4, ng, hs).transpose(2, 0, 1, 3).reshape(ng, n, 4 * hs)
    gcol = jnp.pad(g4, ((0, 0), (0, 0), (0, LANES - 4 * hs)))
    grow = jnp.pad(g4.reshape(ng, batch, nc, ln, 4 * hs).transpose(0, 1, 2, 4, 3),
                   ((0, 0),) * 3 + ((0, 4 * h - 4 * hs), (0, 0)))

    def tok(colblock):
        return pl.BlockSpec((seq, hs * LANES), lambda b, hg: (b, colblock(hg)))

    in_specs = [
        tok(lambda hg: hg), tok(lambda hg: ng + hg), tok(lambda hg: 2 * ng + hg),
        tok(lambda hg: hg),
        pl.BlockSpec((None, seq, LANES), lambda b, hg: (hg, b, 0)),
        pl.BlockSpec((None, None, nc, 4 * h, ln), lambda b, hg: (hg, b, 0, 0, 0)),
        pl.BlockSpec((1, M_DV), lambda b, hg: (0, 0)),
    ]
    args = [qkv, qkv, qkv, og, gcol, grow, lw["m_norm_g"]]
    state_specs = [
        pl.BlockSpec((None, 2, hs, M_DK, M_DV), lambda b, hg: (b, 0, hg, 0, 0)),
        pl.BlockSpec((None, 2, hs, 1, M_DK), lambda b, hg: (b, 0, hg, 0, 0)),
        pl.BlockSpec((None, 2, hs, 1, ln), lambda b, hg: (b, 0, hg, 0, 0)),
    ]
    if init is not None:
        c0, n0, m0 = init
        in_specs += state_specs
        args += [c0, n0.reshape(batch, 2, h, 1, M_DK),
                 jnp.broadcast_to(m0[..., None, None], (batch, 2, h, 1, ln))]
    out_shape = [jax.ShapeDtypeStruct((n, h * M_DV), BF16)]
    out_specs = [tok(lambda hg: hg)]
    if emit_state:
        out_shape += [
            jax.ShapeDtypeStruct((batch, 2, h, M_DK, M_DV), F32),
            jax.ShapeDtypeStruct((batch, 2, h, 1, M_DK), F32),
            jax.ShapeDtypeStruct((batch, 2, h, 1, ln), F32),
        ]
        out_specs += state_specs
    per = (hs, 2, nc)
    outs = pl.pallas_call(
        functools.partial(_mlstm_kernel, n_chunks=nc, heads=hs, has_init=init is not None,
                          emit_state=emit_state),
        out_shape=out_shape,
        grid=(batch, ng),
        in_specs=in_specs,
        out_specs=out_specs,
        scratch_shapes=[
            pltpu.VMEM((seq, LANES), F32), pltpu.VMEM((seq, LANES), F32),
            pltpu.VMEM((nc, 4 * h, ln), F32), pltpu.VMEM((nc, 4 * h, ln), F32),
            pltpu.VMEM(per + (M_DK, M_DV), F32), pltpu.VMEM(per + (1, M_DK), F32),
            pltpu.VMEM(per + (1, ln), F32), pltpu.VMEM(per + (1, ln), F32),
            pltpu.VMEM(per + (M_DK, M_DV), BF16), pltpu.VMEM(per + (1, M_DK), F32), pltpu.VMEM(per + (1, ln), F32),
            pltpu.VMEM((hs, 2, M_DK, M_DV), F32), pltpu.VMEM((hs, 2, 1, M_DK), F32), pltpu.VMEM((hs, 2, 1, ln), F32),
        ],
        compiler_params=_params(2),
        name="mlstm",
    )(*args)
    if emit_state:
        ym, cf, nf, mf = outs
        return ym, (cf, nf[:, :, :, 0, :], mf[:, :, :, 0, 0])
    return outs[0], None


HEAD_V = 64
VT_ROWS = 80


def _ones_row(v_t):
    return jnp.where(lax.broadcasted_iota(jnp.int32, v_t.shape, 0) == HEAD_V, 1.0, v_t)


def _flash_heads(qs, keys_of, vt_chunk, n_keys, o_ref):
    s_ts = [_dot_nt(keys_of(e), qb) for e, qb in enumerate(qs)]
    m = [None] * len(qs)
    acc = [None] * len(qs)
    for c0 in range(0, n_keys, KEY_BLOCK):
        c1 = min(c0 + KEY_BLOCK, n_keys)
        for e in range(len(qs)):
            s = s_ts[e][c0:c1, :]
            mc = jnp.max(s, axis=0, keepdims=True)
            if c0 == 0:
                m[e] = mc
                acc[e] = _dot(vt_chunk(e, c0, c1), jnp.exp2(s - mc).astype(BF16))
            else:
                m_new = jnp.maximum(m[e], mc)
                alpha = jnp.exp2(m[e] - m_new)
                acc[e] = acc[e] * alpha + _dot(vt_chunk(e, c0, c1), jnp.exp2(s - m_new).astype(BF16))
                m[e] = m_new
    outs = [a[0:HEAD_V, :] / a[HEAD_V:HEAD_V + 1, :] for a in acc]
    for e in range(0, len(outs), 2):
        pair = jnp.concatenate([outs[e], outs[e + 1]], axis=0)
        o_ref[:, (e // 2) * LANES:(e // 2 + 1) * LANES] = pair.T.astype(o_ref.dtype)


def _mla_kernel(*refs, past, rotary, heads):
    q_ref, ckv_ref, akr_ref = refs[:3]
    pos = 3
    if past:
        ckvc_ref, krc_ref = refs[pos:pos + 2]
        pos += 2
    wk_ref, wvt_ref, kg_ref = refs[pos:pos + 3]
    pos += 3
    if rotary:
        ck_ref, sk_ref = refs[pos:pos + 2]
        pos += 2
    o_ref, kbuf, vbuf = refs[pos:]
    seq = ckv_ref.shape[0]

    @pl.when(pl.program_id(2) == 0)
    def _():
        for e in range(heads):
            wk_h = wk_ref[:, e * LANES:(e + 1) * LANES]
            wvt_h = wvt_ref[e * VT_ROWS:(e + 1) * VT_ROWS, :]

            def keys(ckv, kr):
                cb = ckv.astype(BF16)
                kcat = _dot(cb, wk_h) + kr
                return _ones_row(_dot_nt(wvt_h, cb)), _rms(kcat, A_QK) * kg_ref[...]

            v_t, kn = keys(ckv_ref[...], akr_ref[...])
            if rotary:
                kn = _rope(kn, ck_ref[...], sk_ref[...], A_ROPE // 2, LANES)
            kbuf[e, past:past + seq, :] = kn.astype(BF16)
            vbuf[e, :, past:past + seq] = v_t.astype(BF16)
            if past:
                v_tc, knc = keys(ckvc_ref[...], krc_ref[...])
                kbuf[e, 0:past, :] = knc.astype(BF16)
                vbuf[e, :, 0:past] = v_tc.astype(BF16)

    qs = [q_ref[:, e * LANES:(e + 1) * LANES] for e in range(heads)]
    _flash_heads(qs, lambda e: kbuf[e], lambda e, c0, c1: vbuf[e, :, c0:c1], past + seq, o_ref)


def _mla_call(qa, ckv, akr, lw, batch, seq, cache, rope_tabs):
    n = batch * seq
    tq = min(Q_BLOCK, seq)
    nq = seq // tq
    past = 0 if cache is None else cache[0].shape[1]
    heads = MLA_HEADS_PER_STEP if nq > 1 else A_HEADS
    in_specs = [
        pl.BlockSpec((tq, heads * LANES), lambda b, hg, qi: (b * nq + qi, hg)),
        pl.BlockSpec((seq, LANES), lambda b, hg, qi: (b, 0)),
        pl.BlockSpec((seq, LANES), lambda b, hg, qi: (b, 0)),
    ]
    args = [qa, ckv, akr]
    if past:
        in_specs += [pl.BlockSpec((None, past, LANES), lambda b, hg, qi: (b, 0, 0))] * 2
        args += [cache[0], cache[1]]
    in_specs += [
        pl.BlockSpec((A_KV_LORA, heads * LANES), lambda b, hg, qi: (0, hg)),
        pl.BlockSpec((heads * VT_ROWS, A_KV_LORA), lambda b, hg, qi: (hg, 0)),
        pl.BlockSpec((1, LANES), lambda b, hg, qi: (0, 0)),
    ]
    args += [lw["w_uk"], lw["w_uvt"], lw["a_knorm_g"]]
    if rope_tabs is not None:
        cos, sin = rope_tabs
        in_specs += [pl.BlockSpec((seq, LANES), lambda b, hg, qi: (0, 0))] * 2
        args += [cos, sin]
    return pl.pallas_call(
        functools.partial(_mla_kernel, past=past, rotary=rope_tabs is not None, heads=heads),
        out_shape=jax.ShapeDtypeStruct((n, A_HEADS * A_V), BF16),
        grid=(batch, A_HEADS // heads, nq),
        in_specs=in_specs,
        out_specs=pl.BlockSpec((tq, heads * A_V), lambda b, hg, qi: (b * nq + qi, hg)),
        scratch_shapes=[pltpu.VMEM((heads, past + seq, LANES), BF16),
                        pltpu.VMEM((heads, VT_ROWS, past + seq), BF16)],
        compiler_params=_params(3),
        name="mla_attn",
    )(*args)


def _gqa_kernel(*refs, past, groups):
    q_ref, k_ref, v_ref = refs[:3]
    pos = 3
    if past:
        kc_ref, vc_ref = refs[pos:pos + 2]
        pos += 2
    o_ref, kbuf, vbuf = refs[pos:]
    seq = k_ref.shape[0]
    qw = G_GROUP * G_HD

    @pl.when(pl.program_id(2) == 0)
    def _():
        for gi in range(groups):
            grp = pl.program_id(1) * groups + gi

            r = lax.broadcasted_iota(jnp.int32, (VT_ROWS, LANES), 0)
            c = lax.broadcasted_iota(jnp.int32, (VT_ROWS, LANES), 1)
            pick = jnp.where((c == r + grp * G_HD) & (r < G_HD), 1.0, 0.0).astype(BF16)

            def fill(r0, k, v):
                low = _lane(k.shape) < G_HD
                k_lo = jnp.where(grp == 0, k, pltpu.roll(k, G_HD, 1))
                rows = k.shape[0]
                kbuf[gi, r0:r0 + rows, :] = jnp.where(low, k_lo, pltpu.roll(k_lo, G_HD, 1)).astype(BF16)
                vbuf[gi, :, r0:r0 + rows] = _ones_row(_dot_nt(pick, v.astype(BF16))).astype(BF16)

            fill(past, k_ref[...], v_ref[...])
            if past:
                fill(0, kc_ref[...], vc_ref[...])

    qs = []
    for gi in range(groups):
        for j in range(G_GROUP):
            col = q_ref[:, gi * qw + (j // 2) * LANES:gi * qw + (j // 2 + 1) * LANES]
            keep = (_lane(col.shape) < G_HD) == (j % 2 == 0)
            qs.append(jnp.where(keep, col, jnp.zeros_like(col)))
    _flash_heads(qs, lambda e: kbuf[e // G_GROUP], lambda e, c0, c1: vbuf[e // G_GROUP, :, c0:c1],
                 past + seq, o_ref)


def _gqa_call(gq, gk, gv, batch, seq, cache):
    n = batch * seq
    tq = min(Q_BLOCK, seq)
    nq = seq // tq
    past = 0 if cache is None else cache[0].shape[1]
    groups = GQA_GROUPS_PER_STEP if nq > 1 else G_KV_HEADS
    qw = G_GROUP * G_HD
    kvw = G_KV_HEADS * G_HD
    in_specs = [
        pl.BlockSpec((tq, groups * qw), lambda b, g, qi: (b * nq + qi, g)),
        pl.BlockSpec((seq, kvw), lambda b, g, qi: (b, 0)),
        pl.BlockSpec((seq, kvw), lambda b, g, qi: (b, 0)),
    ]
    args = [gq, gk, gv]
    if past:
        in_specs += [pl.BlockSpec((None, past, kvw), lambda b, g, qi: (b, 0, 0))] * 2
        args += [cache[0], cache[1]]
    return pl.pallas_call(
        functools.partial(_gqa_kernel, past=past, groups=groups),
        out_shape=jax.ShapeDtypeStruct((n, G_HEADS * G_HD), BF16),
        grid=(batch, G_KV_HEADS // groups, nq),
        in_specs=in_specs,
        out_specs=pl.BlockSpec((tq, groups * qw), lambda b, g, qi: (b * nq + qi, g)),
        scratch_shapes=[pltpu.VMEM((groups, past + seq, LANES), BF16),
                        pltpu.VMEM((groups, VT_ROWS, past + seq), BF16)],
        compiler_params=_params(3),
        name="gqa_attn",
    )(*args)


def _merge_ffn_kernel(x_ref, ym_ref, ya_ref, yg_ref, gates_ref, mod_ref, g2_ref, wb_ref, wo_ref, wfi_ref,
                      wfo_ref, o_ref, *, ff_chunks):
    x = x_ref[...]
    d = x.shape[-1]
    mixed = None
    for i, y_ref in enumerate((ym_ref, ya_ref, yg_ref)):
        br = _dot(y_ref[...].astype(BF16), wb_ref[i]) * gates_ref[:, i * d:(i + 1) * d]
        mixed = br if mixed is None else mixed + br
    gt1 = mod_ref[2:3, :]
    x1 = x + gt1 * _dot(mixed.astype(BF16), wo_ref[...])

    sh2 = mod_ref[3:4, :]
    sc2 = mod_ref[4:5, :]
    gt2 = mod_ref[5:6, :]
    h2 = ((_rms(x1, d) * g2_ref[...]) * (1.0 + sc2) + sh2).astype(BF16)
    d_ff = wfo_ref.shape[0]
    acc = None
    for c0, c1 in ff_chunks:
        ug = _dot(h2, wfi_ref[:, c0:c1])
        uv = _dot(h2, wfi_ref[:, d_ff + c0:d_ff + c1])
        act = (ug * _sigmoid(ug) * uv).astype(BF16)
        part = _dot(act, wfo_ref[c0:c1, :])
        acc = part if acc is None else acc + part
    o_ref[...] = x1 + gt2 * acc


def _merge_ffn_call(x, ym, ya, yg, gates, mod, lw, mod_row0, rows_per_mod):
    n, d = x.shape
    tm = TOKEN_BLOCK
    d_ff = lw["w_ffn_out"].shape[0]
    split = -(-(d_ff // 2) // MXU_COLS) * MXU_COLS
    ff_chunks = ((0, split), (split, d_ff)) if 0 < split < d_ff else ((0, d_ff),)

    def mod_idx(i):
        return (mod_row0 + (i * tm) // rows_per_mod, 0, 0)

    def rows(width):
        return pl.BlockSpec((tm, width), lambda i: (i, 0))

    return pl.pallas_call(
        functools.partial(_merge_ffn_kernel, ff_chunks=ff_chunks),
        out_shape=jax.ShapeDtypeStruct((n, d), F32),
        grid=(n // tm,),
        in_specs=[
            rows(d), rows(BRANCH_WIDTH), rows(BRANCH_WIDTH), rows(BRANCH_WIDTH), rows(N_BRANCH * d),
            pl.BlockSpec((None, N_MOD, d), mod_idx),
            _resident((1, d)),
            _resident((N_BRANCH, BRANCH_WIDTH, d)),
            _resident((d, d)),
            _resident((d, 2 * d_ff)),
            _resident((d_ff, d)),
        ],
        out_specs=rows(d),
        compiler_params=_params(1),
        name="merge_ffn",
    )(x, ym, ya, yg, gates, mod, lw["norm2_g"], lw["w_branch"], lw["w_out"], lw["w_ffn_in"], lw["w_ffn_out"])


def _pad_cols(w, width):
    return jnp.pad(w, ((0, 0), (0, width - w.shape[1])))


def _mla_q_layout(a):
    lead = a.shape[:-1]
    a = a.reshape(lead + (A_HEADS, A_QK))
    z = jnp.zeros(lead + (A_HEADS, LANES - A_QK), a.dtype)
    return jnp.concatenate([a[..., A_NOPE:], z, a[..., :A_NOPE]], axis=-1).reshape(lead + (A_HEADS * LANES,))


def _layer_weights(l, w_in, b_mgate, norm1_g, m_norm_g, a_qlora_g, a_kvlora_g, w_uq, w_ukv, a_qnorm_g,
                   a_knorm_g, g_qnorm_g, g_knorm_g, w_branch, w_out, norm2_g, w_ffn_in, w_ffn_out):
    d = w_in.shape[1]
    hw = M_HEADS * M_DK
    sizes = (N_BRANCH * d, hw, hw, hw, hw, 4 * M_HEADS, A_Q_LORA, A_KV_LORA, A_ROPE, G_HEADS * G_HD,
             G_KV_HEADS * G_HD, G_KV_HEADS * G_HD)
    splits = np.cumsum(sizes)[:-1].tolist()
    (wg, wmq, wmk, wmv, wmo, wmg, waq, wakv, wakr, wgq, wgk, wgv) = jnp.split(w_in[l], splits, axis=1)
    misc = _pad_cols(jnp.concatenate([wakr, wmg], axis=1), LANES)
    w_in_p = jnp.concatenate([wg, wmq, wmk, wmv, wmo, waq, wakv, misc, wgq, wgk, wgv], axis=1).astype(BF16)
    ukv = w_ukv[l].reshape(A_KV_LORA, A_HEADS, A_NOPE + A_V)
    uk = jnp.pad(ukv[..., :A_NOPE], ((0, 0), (0, 0), (LANES - A_NOPE, 0))).reshape(A_KV_LORA, A_HEADS * LANES)
    uvt = jnp.pad(ukv[..., A_NOPE:].transpose(1, 2, 0), ((0, 0), (0, VT_ROWS - A_V), (0, 0)))
    uvt = uvt.reshape(A_HEADS * VT_ROWS, A_KV_LORA)
    qg = _mla_q_layout(jnp.tile(a_qnorm_g[l], A_HEADS)[None, :])[:, :LANES]
    kg = _mla_q_layout(jnp.tile(a_knorm_g[l], A_HEADS)[None, :])[:, :LANES]
    return dict(
        w_in=w_in_p,
        b_mgate=jnp.pad(b_mgate[l][None, :], ((0, 0), (MG_LANE0, LANES - MG_LANE0 - 4 * M_HEADS))),
        norm1_g=norm1_g[l][None, :],
        m_norm_g=m_norm_g[l][None, :],
        a_qlora_g=a_qlora_g[l][None, :],
        a_kvlora_g=a_kvlora_g[l][None, :],
        w_uq=_mla_q_layout(w_uq[l]).astype(BF16),
        w_uk=uk.astype(BF16),
        w_uvt=uvt.astype(BF16),
        a_qnorm_g=qg,
        a_knorm_g=kg,
        g_qnorm_g=jnp.tile(g_qnorm_g[l], LANES // G_HD)[None, :],
        g_knorm_g=jnp.tile(g_knorm_g[l], G_KV_HEADS)[None, :],
        w_branch=w_branch[l].astype(BF16),
        w_out=w_out[l].astype(BF16),
        norm2_g=norm2_g[l][None, :],
        w_ffn_in=w_ffn_in[l].astype(BF16),
        w_ffn_out=w_ffn_out[l].astype(BF16),
    )


def _axial_angles(seq, rot_dim):
    n_freq = rot_dim // 4
    freqs = ROPE_BASE ** (-jnp.arange(n_freq, dtype=F32) / n_freq)
    t = jnp.arange(seq)
    row = (t // GRID_W).astype(F32)
    col = (t % GRID_W).astype(F32)
    return jnp.concatenate([row[:, None] * freqs, col[:, None] * freqs], axis=-1)


def _rope_tables(seq):
    ang = _axial_angles(seq, A_ROPE)
    one = jnp.ones((seq, LANES - A_ROPE), F32)
    mla_cos = jnp.concatenate([jnp.cos(ang), jnp.cos(ang), one], axis=-1)
    mla_sin = jnp.concatenate([-jnp.sin(ang), jnp.sin(ang), 0.0 * one], axis=-1)
    ang = _axial_angles(seq, G_HD)
    cos = jnp.concatenate([jnp.cos(ang), jnp.cos(ang)], axis=-1)
    sin = jnp.concatenate([-jnp.sin(ang), jnp.sin(ang)], axis=-1)
    gqa = (jnp.tile(cos, (1, LANES // G_HD)), jnp.tile(sin, (1, LANES // G_HD)))
    return (mla_cos, mla_sin), gqa


def _layer(x, mod, lw, batch, seq, mod_row0, rows_per_mod, ctx, rope):
    (gates, qkv, og, mg, qa, ckv, akr, gq, gk, gv) = _inproj_call(x, mod, lw, mod_row0, rows_per_mod, rope, seq)
    if ctx is None:
        ym, state = _mlstm_call(qkv, og, mg, lw, batch, seq, None, True)
        ya = _mla_call(qa, ckv, akr, lw, batch, seq, None, None)
        yg = _gqa_call(gq, gk, gv, batch, seq, None)
        new_ctx = dict(state=state, ckv=ckv, kr=akr[:, :A_ROPE], gk=gk, gv=gv)
    else:
        ym, _ = _mlstm_call(qkv, og, mg, lw, batch, seq, ctx["mlstm"], False)
        ya = _mla_call(qa, ckv, akr, lw, batch, seq, ctx["mla"], rope[0])
        yg = _gqa_call(gq, gk, gv, batch, seq, ctx["gqa"])
        new_ctx = None
    x = _merge_ffn_call(x, ym, ya, yg, gates, mod, lw, mod_row0, rows_per_mod)
    return x, new_ctx


def kernel(x_prompt, x_sample, state_mlstm_C, state_mlstm_n, state_mlstm_m, cache_mla_ckv, cache_mla_krope,
           cache_gqa_k, cache_gqa_v, c, c_ctx, w_mod, b_mod, norm1_g, w_in, b_mgate, m_norm_g, a_qlora_g,
           a_kvlora_g, w_uq, w_ukv, a_qnorm_g, a_knorm_g, g_qnorm_g, g_knorm_g, w_branch, w_out, norm2_g,
           w_ffn_in, w_ffn_out):
    batch, seq, d = x_prompt.shape
    dbatch, dseq, _ = x_sample.shape
    depth = w_in.shape[0]
    past = cache_mla_ckv.shape[2]

    n_rows = -(-(1 + dbatch) // SUBLANES) * SUBLANES
    cond = jnp.concatenate([c_ctx[None, :], c, jnp.zeros((n_rows - 1 - dbatch, d), F32)], axis=0)
    mod_all = _mod_call(cond, w_mod, b_mod).reshape(depth, n_rows, N_MOD, d)

    rope = _rope_tables(dseq)
    xp = x_prompt.reshape(batch * seq, d)
    xs = x_sample.reshape(dbatch * dseq, d)
    ctx_layers = []
    for l in range(depth):
        lw = _layer_weights(l, w_in, b_mgate, norm1_g, m_norm_g, a_qlora_g, a_kvlora_g, w_uq, w_ukv,
                            a_qnorm_g, a_knorm_g, g_qnorm_g, g_knorm_g, w_branch, w_out, norm2_g, w_ffn_in,
                            w_ffn_out)
        xp, st = _layer(xp, mod_all[l], lw, batch, seq, 0, batch * seq, None, None)
        ctx_layers.append(st)
        ctx = dict(
            mlstm=(state_mlstm_C[:, l], state_mlstm_n[:, l], state_mlstm_m[:, l]),
            mla=(cache_mla_ckv[:, l], _pad_cols(cache_mla_krope[:, l].reshape(dbatch * past, A_ROPE), LANES)
                 .reshape(dbatch, past, LANES)),
            gqa=(cache_gqa_k[:, l].reshape(dbatch, past, G_KV_HEADS * G_HD),
                 cache_gqa_v[:, l].reshape(dbatch, past, G_KV_HEADS * G_HD)),
        )
        xs, _ = _layer(xs, mod_all[l], lw, dbatch, dseq, 1, dseq, ctx, rope)

    def stack(fn):
        return jnp.stack([fn(s) for s in ctx_layers], axis=1)

    new_c = stack(lambda s: s["state"][0])
    new_n = stack(lambda s: s["state"][1])
    new_m = stack(lambda s: s["state"][2])
    new_ckv = stack(lambda s: s["ckv"].reshape(batch, seq, A_KV_LORA))
    new_kr = stack(lambda s: s["kr"].reshape(batch, seq, A_ROPE))
    new_gk = stack(lambda s: s["gk"].reshape(batch, seq, G_KV_HEADS, G_HD))
    new_gv = stack(lambda s: s["gv"].reshape(batch, seq, G_KV_HEADS, G_HD))
    return (xp.reshape(batch, seq, d), xs.reshape(dbatch, dseq, d), new_c, new_n, new_m, new_ckv, new_kr,
            new_gk, new_gv)
```

```python
import functools

import numpy as np
import jax
import jax.numpy as jnp
from jax import lax
from jax.experimental import pallas as pl
from jax.experimental.pallas import tpu as pltpu

F32 = jnp.float32
BF16 = jnp.bfloat16

LANES = 128
SUBLANES = 8
VMEM_LIMIT_BYTES = 56 * 1024 * 1024

EPS = 1e-6
ROPE_BASE = 10000.0
GRID_W = 64

M_HEADS = 4
M_DK = 128
M_DV = 128
A_HEADS = 8
A_NOPE = 64
A_ROPE = 32
A_QK = A_NOPE + A_ROPE
A_V = 64
A_Q_LORA = 256
A_KV_LORA = 128
G_HEADS = 8
G_KV_HEADS = 2
G_GROUP = G_HEADS // G_KV_HEADS
G_HD = 64
N_BRANCH = 3
BRANCH_WIDTH = 512
N_MOD = 6

TOKEN_BLOCK = 512
Q_BLOCK = 256
KEY_BLOCK = 256
MLA_HEADS_PER_STEP = 4
GQA_GROUPS_PER_STEP = 2
MLSTM_CHUNK = 128
MLSTM_HEADS_PER_STEP = 4
LOG2E = 1.4426950408889634


def _params(n_axes):
    return pltpu.CompilerParams(dimension_semantics=("arbitrary",) * n_axes,
                                vmem_limit_bytes=VMEM_LIMIT_BYTES)


def _resident(shape, layer=None):
    nd = len(shape)
    if layer is None:
        return pl.BlockSpec(shape, lambda *_: (0,) * nd, pipeline_mode=pl.Buffered(1))
    return pl.BlockSpec((None,) + tuple(shape), lambda *_: (layer,) + (0,) * nd, pipeline_mode=pl.Buffered(1))


def _lane(shape, axis=None):
    return lax.broadcasted_iota(jnp.int32, shape, len(shape) - 1 if axis is None else axis)


def _dot(a, b):
    return jnp.dot(a, b, preferred_element_type=F32)


def _dot_nt(a, b):
    return lax.dot_general(a, b, (((1,), (1,)), ((), ())), preferred_element_type=F32)


def _split3(a):
    hi = a.astype(BF16)
    r1 = a - hi.astype(F32)
    mid = r1.astype(BF16)
    lo = (r1 - mid.astype(F32)).astype(BF16)
    return hi, mid, lo


def _dot01(a, m01):
    hi, mid, lo = _split3(a)
    return _dot(hi, m01) + _dot(mid, m01) + _dot(lo, m01)


def _dot01_left(m01, a):
    hi, mid, lo = _split3(a)
    return _dot(m01, hi) + _dot(m01, mid) + _dot(m01, lo)


def _sigmoid(x):
    return 0.5 * jnp.tanh(0.5 * x) + 0.5


def _log_sigmoid(x):
    return jnp.minimum(x, 0.0) - jnp.log(1.0 + jnp.exp(-jnp.abs(x)))


def _rms(x, width):
    ms = jnp.sum(x * x, axis=-1, keepdims=True) * (1.0 / width)
    return x * lax.rsqrt(ms + EPS)


def _rope(x, cos, sin_signed, half, period):
    n = x.shape[-1]
    first = (_lane(x.shape) % period) < half
    swapped = jnp.where(first, pltpu.roll(x, n - half, x.ndim - 1), pltpu.roll(x, half, x.ndim - 1))
    return x * cos + swapped * sin_signed


def _mod_kernel(c_ref, w_ref, b_ref, o_ref):
    c = c_ref[...]
    a = c * _sigmoid(c)
    o_ref[...] = _dot01(a, w_ref[...].astype(BF16)) + b_ref[...]


def _mod_call(cond, w_mod, b_mod):
    depth, d, n = w_mod.shape
    rows = cond.shape[0]
    tn = 1536
    return pl.pallas_call(
        _mod_kernel,
        out_shape=jax.ShapeDtypeStruct((depth, rows, n), F32),
        grid=(depth, n // tn),
        in_specs=[
            pl.BlockSpec((rows, d), lambda l, j: (0, 0)),
            pl.BlockSpec((None, d, tn), lambda l, j: (l, 0, j)),
            pl.BlockSpec((None, 1, tn), lambda l, j: (l, 0, j)),
        ],
        out_specs=pl.BlockSpec((None, rows, tn), lambda l, j: (l, 0, j)),
        compiler_params=_params(2),
        name="adaln_mod",
    )(cond, w_mod, b_mod.reshape(depth, 1, n))


MXU_COLS = 256
_GATE0, _GATE1 = 0, 3072
_MQ0 = 3072
_MK0 = 3584
_MV0 = 4096
_MO0 = 4608
_AQ0 = 5120
_AKV0 = 5376
_MISC0 = 5504
_GQ0 = 5632
_GK0 = 6144
_GV0 = 6272
_WIN_COLS = 6400
MG_LANE0 = A_ROPE


def _head_pair_ms(x):
    low = (_lane(x.shape) % LANES) < G_HD
    sq = x * x
    cols = []
    for c0 in range(0, x.shape[-1], LANES):
        s = sq[:, c0:c0 + LANES]
        lo = jnp.sum(jnp.where(low[:, c0:c0 + LANES], s, 0.0), axis=-1, keepdims=True)
        hi = jnp.sum(jnp.where(low[:, c0:c0 + LANES], 0.0, s), axis=-1, keepdims=True)
        cols.append(jnp.where(low[:, c0:c0 + LANES], lo, hi))
    ms = cols[0] if len(cols) == 1 else jnp.concatenate(cols, axis=-1)
    return ms * (1.0 / G_HD)


def _inproj_kernel(*refs, rotary):
    (x_ref, mod_ref, g1_ref, w_ref, bmg_ref, gql_ref, wuq_ref, gkvl_ref, gkn_ref, aqn_ref, gqn_ref) = refs[:11]
    pos = 11
    if rotary:
        ca_ref, sa_ref, cg_ref, sg_ref = refs[pos:pos + 4]
        pos += 4
    (gates_ref, qkv_ref, og_ref, mg_ref, qa_ref, ckv_ref, akr_ref, gq_ref, gk_ref, gv_ref) = refs[pos:]
    x = x_ref[...]
    d = x.shape[-1]
    sh1 = mod_ref[0:1, :]
    sc1 = mod_ref[1:2, :]
    h = (_rms(x, d) * g1_ref[...]) * (1.0 + sc1) + sh1
    hb = h.astype(BF16)

    def proj(c0, width):
        return _dot(hb, w_ref[:, c0:c0 + width])

    aq = _rms(proj(_AQ0, A_Q_LORA), A_Q_LORA) * gql_ref[...]
    qa = _dot(aq.astype(BF16), wuq_ref[...])
    for hd in range(A_HEADS):
        qh = _rms(qa[:, hd * LANES:(hd + 1) * LANES], A_QK) * aqn_ref[...]
        if rotary:
            qh = _rope(qh, ca_ref[...], sa_ref[...], A_ROPE // 2, LANES)
        qa_ref[:, hd * LANES:(hd + 1) * LANES] = (qh * (A_QK ** -0.5 * LOG2E)).astype(qa_ref.dtype)
    akv_misc = proj(_AKV0, A_KV_LORA + LANES)
    ckv_ref[...] = _rms(akv_misc[:, :A_KV_LORA], A_KV_LORA) * gkvl_ref[...]
    misc = akv_misc[:, A_KV_LORA:]
    akr_ref[...] = jnp.where(_lane(misc.shape) < A_ROPE, misc, 0.0)
    mg_ref[...] = misc + bmg_ref[...]

    gq_all = proj(_GQ0, G_HEADS * G_HD)
    for c0 in range(0, G_HEADS * G_HD, LANES):
        gq = gq_all[:, c0:c0 + LANES]
        gq = gq * lax.rsqrt(_head_pair_ms(gq) + EPS) * gqn_ref[...]
        if rotary:
            gq = _rope(gq, cg_ref[...], sg_ref[...], G_HD // 2, G_HD)
        gq_ref[:, c0:c0 + LANES] = (gq * (G_HD ** -0.5 * LOG2E)).astype(gq_ref.dtype)
    kvw = G_KV_HEADS * G_HD
    gkv = proj(_GK0, 2 * kvw)
    gk = gkv[:, :kvw]
    gk = gk * lax.rsqrt(_head_pair_ms(gk) + EPS) * gkn_ref[...]
    if rotary:
        gk = _rope(gk, cg_ref[...], sg_ref[...], G_HD // 2, G_HD)
    gk_ref[...] = gk
    gv_ref[...] = gkv[:, kvw:]

    hw = M_HEADS * M_DK
    gates_ref[...] = _sigmoid(proj(_GATE0, _GATE1 - _GATE0)).astype(gates_ref.dtype)
    og_ref[...] = _sigmoid(proj(_MO0, hw)).astype(og_ref.dtype)
    qkv_ref[:, 0:hw] = proj(_MQ0, hw).astype(qkv_ref.dtype)
    qkv_ref[:, hw:2 * hw] = (proj(_MK0, hw) * (M_DK ** -0.5)).astype(qkv_ref.dtype)
    qkv_ref[:, 2 * hw:3 * hw] = proj(_MV0, hw).astype(qkv_ref.dtype)


def _inproj_call(x, mod, lw, mod_row0, rows_per_mod, rope, seq):
    n, d = x.shape
    tm = TOKEN_BLOCK
    hw = M_HEADS * M_DK

    def mod_idx(i):
        return (mod_row0 + (i * tm) // rows_per_mod, 0, 0)

    def rows(width):
        return pl.BlockSpec((tm, width), lambda i: (i, 0))

    in_specs = [
        rows(d),
        pl.BlockSpec((None, N_MOD, d), mod_idx),
        _resident((1, d)),
        _resident((d, _WIN_COLS), lw["layer"]),
        _resident((1, LANES)),
        _resident((1, A_Q_LORA)),
        _resident((A_Q_LORA, A_HEADS * LANES), lw["layer"]),
        _resident((1, A_KV_LORA)),
        _resident((1, LANES)),
        _resident((1, LANES)),
        _resident((1, LANES)),
    ]
    args = [x, mod, lw["norm1_g"], lw["w_in"], lw["b_mgate"], lw["a_qlora_g"], lw["w_uq"], lw["a_kvlora_g"],
            lw["g_knorm_g"], lw["a_qnorm_g"], lw["g_qnorm_g"]]
    if rope is not None:
        blocks_per_seq = seq // tm
        in_specs += [pl.BlockSpec((tm, LANES), lambda i: (i % blocks_per_seq, 0))] * 4
        args += [rope[0][0], rope[0][1], rope[1][0], rope[1][1]]
    out_widths = [3 * d, 3 * hw, hw, LANES, A_HEADS * LANES, A_KV_LORA, LANES, G_HEADS * G_HD,
                  G_KV_HEADS * G_HD, G_KV_HEADS * G_HD]
    out_dtypes = [BF16, BF16, BF16, F32, BF16, F32, F32, BF16, F32, F32]
    return pl.pallas_call(
        functools.partial(_inproj_kernel, rotary=rope is not None),
        out_shape=[jax.ShapeDtypeStruct((n, w), t) for w, t in zip(out_widths, out_dtypes)],
        grid=(n // tm,),
        in_specs=in_specs,
        out_specs=[rows(w) for w in out_widths],
        compiler_params=_params(1),
        name="in_proj",
    )(*args)


def _mlstm_kernel(*refs, n_chunks, heads, has_init, emit_state):
    q_ref, k_ref, v_ref, og_ref, mg_ref, gn_ref = refs[:6]
    pos = 6
    if has_init:
        c0_ref, n0_ref, m0_ref = refs[pos:pos + 3]
        pos += 3
    y_ref = refs[pos]
    pos += 1
    if emit_state:
        cf_ref, nf_ref, mf_ref = refs[pos:pos + 3]
        pos += 3
    (pcol_scr, lfc_scr, grow_scr, prow_scr, lfr_scr, u_scr, nu_scr, gm_scr, tot_scr, cs_scr, ns_scr, ms_scr,
     c_scr, n_scr, m_scr) = refs[pos:]
    assert heads == M_HEADS
    n_gates = 4 * M_HEADS

    ln = MLSTM_CHUNK
    row = lax.broadcasted_iota(jnp.int32, (ln, ln), 0)
    col = lax.broadcasted_iota(jnp.int32, (ln, ln), 1)
    lower = col <= row
    upper = col >= row
    tril = jnp.where(lower, 1.0, 0.0).astype(BF16)
    triu = jnp.where(upper, 1.0, 0.0).astype(BF16)
    ones = jnp.ones((ln, ln), BF16)

    def gate_index(hh, d):
        return 2 * d * heads + hh, (2 * d + 1) * heads + hh

    def summaries(c, carry):
        r0 = pl.multiple_of(c * ln, ln)
        g = mg_ref[pl.ds(r0, ln), :]
        lf = _log_sigmoid(g) * LOG2E
        pcol_scr[pl.ds(r0, ln), :] = _dot01_left(tril, lf)
        lfc_scr[pl.ds(r0, ln), :] = lf
        g_rows = g.T[MG_LANE0:MG_LANE0 + n_gates, :]
        gr = g_rows * LOG2E
        lfr = _log_sigmoid(g_rows) * LOG2E
        pr = _dot01(lfr, triu)
        grow_scr[c] = gr
        prow_scr[c] = pr
        lfr_scr[c] = lfr
        for hh in range(heads):
            kb = k_ref[pl.ds(r0, ln), hh * LANES:(hh + 1) * LANES]
            k_t = kb.astype(F32).T
            v = v_ref[pl.ds(r0, ln), hh * LANES:(hh + 1) * LANES]
            for d in range(2):
                ji, jf = gate_index(hh, d)
                tot = pr[jf:jf + 1, ln - 1:ln]
                b_row = pr[jf:jf + 1, :] if d == 0 else tot - pr[jf:jf + 1, :] + lfr[jf:jf + 1, :]
                g_row = tot - b_row + gr[ji:ji + 1, :]
                gmax = jnp.max(g_row, axis=-1, keepdims=True)
                wg = jnp.exp2(g_row - gmax)
                u_scr[hh, d, c] = _dot((k_t * wg).astype(BF16), v)
                nu_scr[hh, d, c] = _dot(jnp.broadcast_to(wg, (SUBLANES, ln)).astype(BF16), kb)[0:1, :]
                gm_scr[hh, d, c] = jnp.broadcast_to(gmax, (1, ln))
                tot_scr[hh, d, c] = jnp.broadcast_to(tot, (1, ln))
        return carry

    lax.fori_loop(0, n_chunks, summaries, 0, unroll=min(2, n_chunks))

    for hh in range(heads):
        for d in range(2):
            if has_init:
                c_scr[hh, d] = c0_ref[d, hh]
                n_scr[hh, d] = n0_ref[d, hh]
                m_scr[hh, d] = m0_ref[d, hh] * LOG2E
            else:
                c_scr[hh, d] = jnp.zeros((M_DK, M_DV), F32)
                n_scr[hh, d] = jnp.zeros((1, M_DK), F32)
                m_scr[hh, d] = jnp.zeros((1, ln), F32)

    def scan(j, carry):
        for hh in range(heads):
            for d in range(2):
                c = j if d == 0 else n_chunks - 1 - j
                cst = c_scr[hh, d]
                nst = n_scr[hh, d]
                mst = m_scr[hh, d]
                cs_scr[hh, d, c] = cst.astype(BF16)
                ns_scr[hh, d, c] = nst
                ms_scr[hh, d, c] = mst
                gmax = gm_scr[hh, d, c]
                total = tot_scr[hh, d, c] + mst
                m_new = jnp.maximum(total, gmax)
                decay = jnp.exp2(total - m_new)
                scale = jnp.exp2(gmax - m_new)
                c_scr[hh, d] = cst * decay + u_scr[hh, d, c] * scale
                n_scr[hh, d] = nst * decay + nu_scr[hh, d, c] * scale
                m_scr[hh, d] = m_new
        return carry

    lax.fori_loop(0, n_chunks, scan, 0)

    def readout(c, carry):
        r0 = pl.multiple_of(c * ln, ln)
        pc = pcol_scr[pl.ds(r0, ln), :]
        lf = lfc_scr[pl.ds(r0, ln), :]
        gr = grow_scr[c]
        pr = prow_scr[c]
        lfr = lfr_scr[c]
        tot_all = pc[ln - 1:ln, :]
        for hh in range(heads):
            q = q_ref[pl.ds(r0, ln), hh * LANES:(hh + 1) * LANES]
            k = k_ref[pl.ds(r0, ln), hh * LANES:(hh + 1) * LANES]
            v = v_ref[pl.ds(r0, ln), hh * LANES:(hh + 1) * LANES]
            qk = _dot_nt(q, k)
            hsum = None
            for d in range(2):
                ji, jf = gate_index(hh, d)
                lane_f = MG_LANE0 + jf
                if d == 0:
                    b_col = pc[:, lane_f:lane_f + 1]
                    r_row = gr[ji:ji + 1, :] - pr[jf:jf + 1, :]
                    mask = lower
                else:
                    tot = tot_all[:, lane_f:lane_f + 1]
                    b_col = tot - pc[:, lane_f:lane_f + 1] + lf[:, lane_f:lane_f + 1]
                    r_row = gr[ji:ji + 1, :] - (tot - pr[jf:jf + 1, :] + lfr[jf:jf + 1, :])
                    mask = upper
                b_colb = jnp.broadcast_to(b_col, (ln, ln))
                d_log = jnp.where(mask, b_colb + r_row, -jnp.inf)
                dmax = jnp.broadcast_to(jnp.max(d_log, axis=-1, keepdims=True), (ln, ln))
                sb = (qk * jnp.exp2(d_log - dmax)).astype(BF16)
                num_loc = _dot(sb, v)
                den_loc = _dot(sb, ones)
                qc = _dot(q, cs_scr[hh, d, c])
                qn = _dot_nt(q, jnp.broadcast_to(ns_scr[hh, d, c], (ln, M_DK)).astype(BF16))
                inter = b_colb + ms_scr[hh, d, c]
                m_t = jnp.maximum(inter, dmax)
                w_inter = jnp.exp2(inter - m_t)
                w_loc = jnp.exp2(dmax - m_t)
                num = qc * w_inter + num_loc * w_loc
                den = qn * w_inter + den_loc * w_loc
                hc = num / jnp.maximum(jnp.abs(den), jnp.exp2(-m_t))
                hsum = hc if hsum is None else hsum + hc
            y_ref[pl.ds(r0, ln), hh * LANES:(hh + 1) * LANES] = (
                _rms(hsum, M_DV) * gn_ref[...] * og_ref[pl.ds(r0, ln), hh * LANES:(hh + 1) * LANES]
            ).astype(y_ref.dtype)
        return carry

    lax.fori_loop(0, n_chunks, readout, 0, unroll=min(2, n_chunks))

    if emit_state:
        for hh in range(heads):
            for d in range(2):
                cf_ref[d, hh] = c_scr[hh, d]
                nf_ref[d, hh] = n_scr[hh, d]
                mf_ref[d, hh] = m_scr[hh, d] * (1.0 / LOG2E)


def _mlstm_call(qkv, og, mg, lw, batch, seq, init, emit_state):
    n = batch * seq
    ln = MLSTM_CHUNK
    nc = seq // ln
    h = M_HEADS
    hs = MLSTM_HEADS_PER_STEP
    ng = h // hs

    def tok(colblock):
        return pl.BlockSpec((seq, hs * LANES), lambda b, hg: (b, colblock(hg)))

    in_specs = [
        tok(lambda hg: hg), tok(lambda hg: ng + hg), tok(lambda hg: 2 * ng + hg),
        tok(lambda hg: hg),
        pl.BlockSpec((seq, LANES), lambda b, hg: (b, 0)),
        pl.BlockSpec((1, M_DV), lambda b, hg: (0, 0)),
    ]
    args = [qkv, qkv, qkv, og, mg, lw["m_norm_g"]]
    state_specs = [
        pl.BlockSpec((None, 2, hs, M_DK, M_DV), lambda b, hg: (b, 0, hg, 0, 0)),
        pl.BlockSpec((None, 2, hs, 1, M_DK), lambda b, hg: (b, 0, hg, 0, 0)),
        pl.BlockSpec((None, 2, hs, 1, ln), lambda b, hg: (b, 0, hg, 0, 0)),
    ]
    if init is not None:
        c0, n0, m0 = init
        in_specs += state_specs
        args += [c0, n0.reshape(batch, 2, h, 1, M_DK),
                 jnp.broadcast_to(m0[..., None, None], (batch, 2, h, 1, ln))]
    out_shape = [jax.ShapeDtypeStruct((n, h * M_DV), BF16)]
    out_specs = [tok(lambda hg: hg)]
    if emit_state:
        out_shape += [
            jax.ShapeDtypeStruct((batch, 2, h, M_DK, M_DV), F32),
            jax.ShapeDtypeStruct((batch, 2, h, 1, M_DK), F32),
            jax.ShapeDtypeStruct((batch, 2, h, 1, ln), F32),
        ]
        out_specs += state_specs
    per = (hs, 2, nc)
    outs = pl.pallas_call(
        functools.partial(_mlstm_kernel, n_chunks=nc, heads=hs, has_init=init is not None,
                          emit_state=emit_state),
        out_shape=out_shape,
        grid=(batch, ng),
        in_specs=in_specs,
        out_specs=out_specs,
        scratch_shapes=[
            pltpu.VMEM((seq, LANES), F32), pltpu.VMEM((seq, LANES), F32),
            pltpu.VMEM((nc, 4 * h, ln), F32), pltpu.VMEM((nc, 4 * h, ln), F32), pltpu.VMEM((nc, 4 * h, ln), F32),
            pltpu.VMEM(per + (M_DK, M_DV), F32), pltpu.VMEM(per + (1, M_DK), F32),
            pltpu.VMEM(per + (1, ln), F32), pltpu.VMEM(per + (1, ln), F32),
            pltpu.VMEM(per + (M_DK, M_DV), BF16), pltpu.VMEM(per + (1, M_DK), F32), pltpu.VMEM(per + (1, ln), F32),
            pltpu.VMEM((hs, 2, M_DK, M_DV), F32), pltpu.VMEM((hs, 2, 1, M_DK), F32), pltpu.VMEM((hs, 2, 1, ln), F32),
        ],
        compiler_params=_params(2),
        name="mlstm",
    )(*args)
    if emit_state:
        ym, cf, nf, mf = outs
        return ym, (cf, nf[:, :, :, 0, :], mf[:, :, :, 0, 0])
    return outs[0], None


HEAD_V = 64
VT_ROWS = 80


def _ones_row(v_t):
    return jnp.where(lax.broadcasted_iota(jnp.int32, v_t.shape, 0) == HEAD_V, 1.0, v_t)


def _flash_heads(qs, keys_of, vt_chunk, n_keys, o_ref):
    s_ts = [_dot_nt(keys_of(e), qb) for e, qb in enumerate(qs)]
    m = [None] * len(qs)
    acc = [None] * len(qs)
    for c0 in range(0, n_keys, KEY_BLOCK):
        c1 = min(c0 + KEY_BLOCK, n_keys)
        for e in range(len(qs)):
            s = s_ts[e][c0:c1, :]
            mc = jnp.max(s, axis=0, keepdims=True)
            if c0 == 0:
                m[e] = mc
                acc[e] = _dot(vt_chunk(e, c0, c1), jnp.exp2(s - mc).astype(BF16))
            else:
                m_new = jnp.maximum(m[e], mc)
                alpha = jnp.exp2(m[e] - m_new)
                acc[e] = acc[e] * alpha + _dot(vt_chunk(e, c0, c1), jnp.exp2(s - m_new).astype(BF16))
                m[e] = m_new
    outs = [a[0:HEAD_V, :] / a[HEAD_V:HEAD_V + 1, :] for a in acc]
    for e in range(0, len(outs), 2):
        pair = jnp.concatenate([outs[e], outs[e + 1]], axis=0)
        o_ref[:, (e // 2) * LANES:(e // 2 + 1) * LANES] = pair.T.astype(o_ref.dtype)


def _mla_kernel(*refs, past, rotary, heads):
    q_ref, ckv_ref, akr_ref = refs[:3]
    pos = 3
    if past:
        ckvc_ref, krc_ref = refs[pos:pos + 2]
        pos += 2
    wk_ref, wvt_ref, kg_ref = refs[pos:pos + 3]
    pos += 3
    if rotary:
        ck_ref, sk_ref = refs[pos:pos + 2]
        pos += 2
    o_ref, kbuf, vbuf = refs[pos:]
    seq = ckv_ref.shape[0]

    @pl.when(pl.program_id(2) == 0)
    def _():
        for e in range(heads):
            wk_h = wk_ref[:, e * LANES:(e + 1) * LANES]
            wvt_h = wvt_ref[e * VT_ROWS:(e + 1) * VT_ROWS, :]

            def keys(ckv, kr):
                cb = ckv.astype(BF16)
                kcat = _dot(cb, wk_h) + kr
                return _ones_row(_dot_nt(wvt_h, cb)), _rms(kcat, A_QK) * kg_ref[...]

            v_t, kn = keys(ckv_ref[...], akr_ref[...])
            if rotary:
                kn = _rope(kn, ck_ref[...], sk_ref[...], A_ROPE // 2, LANES)
            kbuf[e, past:past + seq, :] = kn.astype(BF16)
            vbuf[e, :, past:past + seq] = v_t.astype(BF16)
            if past:
                v_tc, knc = keys(ckvc_ref[...], krc_ref[...])
                kbuf[e, 0:past, :] = knc.astype(BF16)
                vbuf[e, :, 0:past] = v_tc.astype(BF16)

    qs = [q_ref[:, e * LANES:(e + 1) * LANES] for e in range(heads)]
    _flash_heads(qs, lambda e: kbuf[e], lambda e, c0, c1: vbuf[e, :, c0:c1], past + seq, o_ref)


def _mla_call(qa, ckv, akr, lw, batch, seq, cache, rope_tabs):
    n = batch * seq
    tq = min(Q_BLOCK, seq)
    nq = seq // tq
    past = 0 if cache is None else cache[0].shape[1]
    heads = MLA_HEADS_PER_STEP if nq > 1 else A_HEADS
    in_specs = [
        pl.BlockSpec((tq, heads * LANES), lambda b, hg, qi: (b * nq + qi, hg)),
        pl.BlockSpec((seq, LANES), lambda b, hg, qi: (b, 0)),
        pl.BlockSpec((seq, LANES), lambda b, hg, qi: (b, 0)),
    ]
    args = [qa, ckv, akr]
    if past:
        in_specs += [pl.BlockSpec((None, past, LANES), lambda b, hg, qi: (b, 0, 0))] * 2
        args += [cache[0], cache[1]]
    layer = lw["layer"]
    in_specs += [
        pl.BlockSpec((None, A_KV_LORA, heads * LANES), lambda b, hg, qi: (layer, 0, hg)),
        pl.BlockSpec((None, heads * VT_ROWS, A_KV_LORA), lambda b, hg, qi: (layer, hg, 0)),
        pl.BlockSpec((1, LANES), lambda b, hg, qi: (0, 0)),
    ]
    args += [lw["w_uk"], lw["w_uvt"], lw["a_knorm_g"]]
    if rope_tabs is not None:
        cos, sin = rope_tabs
        in_specs += [pl.BlockSpec((seq, LANES), lambda b, hg, qi: (0, 0))] * 2
        args += [cos, sin]
    return pl.pallas_call(
        functools.partial(_mla_kernel, past=past, rotary=rope_tabs is not None, heads=heads),
        out_shape=jax.ShapeDtypeStruct((n, A_HEADS * A_V), BF16),
        grid=(batch, A_HEADS // heads, nq),
        in_specs=in_specs,
        out_specs=pl.BlockSpec((tq, heads * A_V), lambda b, hg, qi: (b * nq + qi, hg)),
        scratch_shapes=[pltpu.VMEM((heads, past + seq, LANES), BF16),
                        pltpu.VMEM((heads, VT_ROWS, past + seq), BF16)],
        compiler_params=_params(3),
        name="mla_attn",
    )(*args)


def _gqa_kernel(*refs, past, groups):
    q_ref, k_ref, v_ref = refs[:3]
    pos = 3
    if past:
        kc_ref, vc_ref = refs[pos:pos + 2]
        pos += 2
    o_ref, kbuf, vbuf = refs[pos:]
    seq = k_ref.shape[0]
    qw = G_GROUP * G_HD

    @pl.when(pl.program_id(2) == 0)
    def _():
        for gi in range(groups):
            grp = pl.program_id(1) * groups + gi

            r = lax.broadcasted_iota(jnp.int32, (VT_ROWS, LANES), 0)
            c = lax.broadcasted_iota(jnp.int32, (VT_ROWS, LANES), 1)
            pick = jnp.where((c == r + grp * G_HD) & (r < G_HD), 1.0, 0.0).astype(BF16)

            def fill(r0, k, v):
                low = _lane(k.shape) < G_HD
                k_lo = jnp.where(grp == 0, k, pltpu.roll(k, G_HD, 1))
                rows = k.shape[0]
                kbuf[gi, r0:r0 + rows, :] = jnp.where(low, k_lo, pltpu.roll(k_lo, G_HD, 1)).astype(BF16)
                vbuf[gi, :, r0:r0 + rows] = _ones_row(_dot_nt(pick, v.astype(BF16))).astype(BF16)

            fill(past, k_ref[...], v_ref[...])
            if past:
                fill(0, kc_ref[...], vc_ref[...])

    qs = []
    for gi in range(groups):
        for j in range(G_GROUP):
            col = q_ref[:, gi * qw + (j // 2) * LANES:gi * qw + (j // 2 + 1) * LANES]
            keep = (_lane(col.shape) < G_HD) == (j % 2 == 0)
            qs.append(jnp.where(keep, col, jnp.zeros_like(col)))
    _flash_heads(qs, lambda e: kbuf[e // G_GROUP], lambda e, c0, c1: vbuf[e // G_GROUP, :, c0:c1],
                 past + seq, o_ref)


def _gqa_call(gq, gk, gv, batch, seq, cache):
    n = batch * seq
    tq = min(Q_BLOCK, seq)
    nq = seq // tq
    past = 0 if cache is None else cache[0].shape[1]
    groups = GQA_GROUPS_PER_STEP if nq > 1 else G_KV_HEADS
    qw = G_GROUP * G_HD
    kvw = G_KV_HEADS * G_HD
    in_specs = [
        pl.BlockSpec((tq, groups * qw), lambda b, g, qi: (b * nq + qi, g)),
        pl.BlockSpec((seq, kvw), lambda b, g, qi: (b, 0)),
        pl.BlockSpec((seq, kvw), lambda b, g, qi: (b, 0)),
    ]
    args = [gq, gk, gv]
    if past:
        in_specs += [pl.BlockSpec((None, past, kvw), lambda b, g, qi: (b, 0, 0))] * 2
        args += [cache[0], cache[1]]
    return pl.pallas_call(
        functools.partial(_gqa_kernel, past=past, groups=groups),
        out_shape=jax.ShapeDtypeStruct((n, G_HEADS * G_HD), BF16),
        grid=(batch, G_KV_HEADS // groups, nq),
        in_specs=in_specs,
        out_specs=pl.BlockSpec((tq, groups * qw), lambda b, g, qi: (b * nq + qi, g)),
        scratch_shapes=[pltpu.VMEM((groups, past + seq, LANES), BF16),
                        pltpu.VMEM((groups, VT_ROWS, past + seq), BF16)],
        compiler_params=_params(3),
        name="gqa_attn",
    )(*args)


def _merge_ffn_kernel(x_ref, ym_ref, ya_ref, yg_ref, gates_ref, mod_ref, g2_ref, wb_ref, wo_ref, wfi_ref,
                      wfo_ref, o_ref, *, ff_chunks):
    x = x_ref[...]
    d = x.shape[-1]
    mixed = None
    for i, y_ref in enumerate((ym_ref, ya_ref, yg_ref)):
        br = _dot(y_ref[...].astype(BF16), wb_ref[i]) * gates_ref[:, i * d:(i + 1) * d]
        mixed = br if mixed is None else mixed + br
    gt1 = mod_ref[2:3, :]
    x1 = x + gt1 * _dot(mixed.astype(BF16), wo_ref[...])

    sh2 = mod_ref[3:4, :]
    sc2 = mod_ref[4:5, :]
    gt2 = mod_ref[5:6, :]
    h2 = ((_rms(x1, d) * g2_ref[...]) * (1.0 + sc2) + sh2).astype(BF16)
    d_ff = wfo_ref.shape[0]
    acc = None
    for c0, c1 in ff_chunks:
        ug = _dot(h2, wfi_ref[:, c0:c1])
        uv = _dot(h2, wfi_ref[:, d_ff + c0:d_ff + c1])
        act = (ug * _sigmoid(ug) * uv).astype(BF16)
        part = _dot(act, wfo_ref[c0:c1, :])
        acc = part if acc is None else acc + part
    o_ref[...] = x1 + gt2 * acc


def _merge_ffn_call(x, ym, ya, yg, gates, mod, lw, mod_row0, rows_per_mod):
    n, d = x.shape
    tm = TOKEN_BLOCK
    d_ff = lw["w_ffn_out"].shape[1]
    layer = lw["layer"]
    split = -(-(d_ff // 2) // MXU_COLS) * MXU_COLS
    ff_chunks = ((0, split), (split, d_ff)) if 0 < split < d_ff else ((0, d_ff),)

    def mod_idx(i):
        return (mod_row0 + (i * tm) // rows_per_mod, 0, 0)

    def rows(width):
        return pl.BlockSpec((tm, width), lambda i: (i, 0))

    return pl.pallas_call(
        functools.partial(_merge_ffn_kernel, ff_chunks=ff_chunks),
        out_shape=jax.ShapeDtypeStruct((n, d), F32),
        grid=(n // tm,),
        in_specs=[
            rows(d), rows(BRANCH_WIDTH), rows(BRANCH_WIDTH), rows(BRANCH_WIDTH), rows(N_BRANCH * d),
            pl.BlockSpec((None, N_MOD, d), mod_idx),
            _resident((1, d)),
            _resident((N_BRANCH, BRANCH_WIDTH, d), layer),
            _resident((d, d), layer),
            _resident((d, 2 * d_ff), layer),
            _resident((d_ff, d), layer),
        ],
        out_specs=rows(d),
        compiler_params=_params(1),
        name="merge_ffn",
    )(x, ym, ya, yg, gates, mod, lw["norm2_g"], lw["w_branch"], lw["w_out"], lw["w_ffn_in"], lw["w_ffn_out"])


def _pad_cols(w, width):
    return jnp.pad(w, ((0, 0), (0, width - w.shape[1])))


def _mla_q_layout(a):
    lead = a.shape[:-1]
    a = a.reshape(lead + (A_HEADS, A_QK))
    z = jnp.zeros(lead + (A_HEADS, LANES - A_QK), a.dtype)
    return jnp.concatenate([a[..., A_NOPE:], z, a[..., :A_NOPE]], axis=-1).reshape(lead + (A_HEADS * LANES,))


def _stacked_weights(w_in, w_uq, w_ukv, w_branch, w_out, w_ffn_in, w_ffn_out):
    depth, d, _ = w_in.shape
    hw = M_HEADS * M_DK
    sizes = (N_BRANCH * d, hw, hw, hw, hw, 4 * M_HEADS, A_Q_LORA, A_KV_LORA, A_ROPE, G_HEADS * G_HD,
             G_KV_HEADS * G_HD, G_KV_HEADS * G_HD)
    edges = np.concatenate([[0], np.cumsum(sizes)]).tolist()
    (wg, wmq, wmk, wmv, wmo, wmg, waq, wakv, wakr, wgq, wgk, wgv) = [
        w_in[:, :, a:b] for a, b in zip(edges[:-1], edges[1:])]
    misc_pad = jnp.zeros((depth, d, LANES - A_ROPE - 4 * M_HEADS), w_in.dtype)
    w_in_p = jnp.concatenate([wg, wmq, wmk, wmv, wmo, waq, wakv, wakr, wmg, misc_pad, wgq, wgk, wgv], axis=2)
    ukv = w_ukv.reshape(depth, A_KV_LORA, A_HEADS, A_NOPE + A_V)
    uk = jnp.pad(ukv[..., :A_NOPE], ((0, 0), (0, 0), (0, 0), (LANES - A_NOPE, 0)))
    uvt = jnp.pad(ukv[..., A_NOPE:].transpose(0, 2, 3, 1), ((0, 0), (0, 0), (0, VT_ROWS - A_V), (0, 0)))
    return dict(
        w_in=w_in_p.astype(BF16),
        w_uq=_mla_q_layout(w_uq).astype(BF16),
        w_uk=uk.reshape(depth, A_KV_LORA, A_HEADS * LANES).astype(BF16),
        w_uvt=uvt.reshape(depth, A_HEADS * VT_ROWS, A_KV_LORA).astype(BF16),
        w_branch=w_branch.astype(BF16),
        w_out=w_out.astype(BF16),
        w_ffn_in=w_ffn_in.astype(BF16),
        w_ffn_out=w_ffn_out.astype(BF16),
    )


def _layer_vectors(l, b_mgate, norm1_g, m_norm_g, a_qlora_g, a_kvlora_g, a_qnorm_g, a_knorm_g, g_qnorm_g,
                   g_knorm_g, norm2_g):
    return dict(
        layer=l,
        b_mgate=jnp.pad(b_mgate[l][None, :], ((0, 0), (MG_LANE0, LANES - MG_LANE0 - 4 * M_HEADS))),
        norm1_g=norm1_g[l][None, :],
        m_norm_g=m_norm_g[l][None, :],
        a_qlora_g=a_qlora_g[l][None, :],
        a_kvlora_g=a_kvlora_g[l][None, :],
        a_qnorm_g=_mla_q_layout(jnp.tile(a_qnorm_g[l], A_HEADS)[None, :])[:, :LANES],
        a_knorm_g=_mla_q_layout(jnp.tile(a_knorm_g[l], A_HEADS)[None, :])[:, :LANES],
        g_qnorm_g=jnp.tile(g_qnorm_g[l], LANES // G_HD)[None, :],
        g_knorm_g=jnp.tile(g_knorm_g[l], G_KV_HEADS)[None, :],
        norm2_g=norm2_g[l][None, :],
    )


def _axial_angles(seq, rot_dim):
    n_freq = rot_dim // 4
    freqs = ROPE_BASE ** (-jnp.arange(n_freq, dtype=F32) / n_freq)
    t = jnp.arange(seq)
    row = (t // GRID_W).astype(F32)
    col = (t % GRID_W).astype(F32)
    return jnp.concatenate([row[:, None] * freqs, col[:, None] * freqs], axis=-1)


def _rope_tables(seq):
    ang = _axial_angles(seq, A_ROPE)
    one = jnp.ones((seq, LANES - A_ROPE), F32)
    mla_cos = jnp.concatenate([jnp.cos(ang), jnp.cos(ang), one], axis=-1)
    mla_sin = jnp.concatenate([-jnp.sin(ang), jnp.sin(ang), 0.0 * one], axis=-1)
    ang = _axial_angles(seq, G_HD)
    cos = jnp.concatenate([jnp.cos(ang), jnp.cos(ang)], axis=-1)
    sin = jnp.concatenate([-jnp.sin(ang), jnp.sin(ang)], axis=-1)
    gqa = (jnp.tile(cos, (1, LANES // G_HD)), jnp.tile(sin, (1, LANES // G_HD)))
    return (mla_cos, mla_sin), gqa


def _layer(x, mod, lw, batch, seq, mod_row0, rows_per_mod, ctx, rope):
    (gates, qkv, og, mg, qa, ckv, akr, gq, gk, gv) = _inproj_call(x, mod, lw, mod_row0, rows_per_mod, rope, seq)
    if ctx is None:
        ym, state = _mlstm_call(qkv, og, mg, lw, batch, seq, None, True)
        ya = _mla_call(qa, ckv, akr, lw, batch, seq, None, None)
        yg = _gqa_call(gq, gk, gv, batch, seq, None)
        new_ctx = dict(state=state, ckv=ckv, kr=akr[:, :A_ROPE], gk=gk, gv=gv)
    else:
        ym, _ = _mlstm_call(qkv, og, mg, lw, batch, seq, ctx["mlstm"], False)
        ya = _mla_call(qa, ckv, akr, lw, batch, seq, ctx["mla"], rope[0])
        yg = _gqa_call(gq, gk, gv, batch, seq, ctx["gqa"])
        new_ctx = None
    x = _merge_ffn_call(x, ym, ya, yg, gates, mod, lw, mod_row0, rows_per_mod)
    return x, new_ctx


def kernel(x_prompt, x_sample, state_mlstm_C, state_mlstm_n, state_mlstm_m, cache_mla_ckv, cache_mla_krope,
           cache_gqa_k, cache_gqa_v, c, c_ctx, w_mod, b_mod, norm1_g, w_in, b_mgate, m_norm_g, a_qlora_g,
           a_kvlora_g, w_uq, w_ukv, a_qnorm_g, a_knorm_g, g_qnorm_g, g_knorm_g, w_branch, w_out, norm2_g,
           w_ffn_in, w_ffn_out):
    batch, seq, d = x_prompt.shape
    dbatch, dseq, _ = x_sample.shape
    depth = w_in.shape[0]
    past = cache_mla_ckv.shape[2]

    n_rows = -(-(1 + dbatch) // SUBLANES) * SUBLANES
    cond = jnp.concatenate([c_ctx[None, :], c, jnp.zeros((n_rows - 1 - dbatch, d), F32)], axis=0)
    mod_all = _mod_call(cond, w_mod, b_mod).reshape(depth, n_rows, N_MOD, d)

    rope = _rope_tables(dseq)
    xp = x_prompt.reshape(batch * seq, d)
    xs = x_sample.reshape(dbatch * dseq, d)
    ctx_layers = []
    stacked = _stacked_weights(w_in, w_uq, w_ukv, w_branch, w_out, w_ffn_in, w_ffn_out)
    for l in range(depth):
        lw = dict(stacked, **_layer_vectors(l, b_mgate, norm1_g, m_norm_g, a_qlora_g, a_kvlora_g, a_qnorm_g,
                                            a_knorm_g, g_qnorm_g, g_knorm_g, norm2_g))
        xp, st = _layer(xp, mod_all[l], lw, batch, seq, 0, batch * seq, None, None)
        ctx_layers.append(st)
        ctx = dict(
            mlstm=(state_mlstm_C[:, l], state_mlstm_n[:, l], state_mlstm_m[:, l]),
            mla=(cache_mla_ckv[:, l], _pad_cols(cache_mla_krope[:, l].reshape(dbatch * past, A_ROPE), LANES)
                 .reshape(dbatch, past, LANES)),
            gqa=(cache_gqa_k[:, l].reshape(dbatch, past, G_KV_HEADS * G_HD),
                 cache_gqa_v[:, l].reshape(dbatch, past, G_KV_HEADS * G_HD)),
        )
        xs, _ = _layer(xs, mod_all[l], lw, dbatch, dseq, 1, dseq, ctx, rope)

    def stack(fn):
        return jnp.stack([fn(s) for s in ctx_layers], axis=1)

    new_c = stack(lambda s: s["state"][0])
    new_n = stack(lambda s: s["state"][1])
    new_m = stack(lambda s: s["state"][2])
    new_ckv = stack(lambda s: s["ckv"].reshape(batch, seq, A_KV_LORA))
    new_kr = stack(lambda s: s["kr"].reshape(batch, seq, A_ROPE))
    new_gk = stack(lambda s: s["gk"].reshape(batch, seq, G_KV_HEADS, G_HD))
    new_gv = stack(lambda s: s["gv"].reshape(batch, seq, G_KV_HEADS, G_HD))
    return (xp.reshape(batch, seq, d), xs.reshape(dbatch, dseq, d), new_c, new_n, new_m, new_ckv, new_kr,
            new_gk, new_gv)
```

```python
import functools

import numpy as np
import jax
import jax.numpy as jnp
from jax import lax
from jax.experimental import pallas as pl
from jax.experimental.pallas import tpu as pltpu

F32 = jnp.float32
BF16 = jnp.bfloat16

LANES = 128
SUBLANES = 8
VMEM_LIMIT_BYTES = 56 * 1024 * 1024

EPS = 1e-6
ROPE_BASE = 10000.0
GRID_W = 64

M_HEADS = 4
M_DK = 128
M_DV = 128
A_HEADS = 8
A_NOPE = 64
A_ROPE = 32
A_QK = A_NOPE + A_ROPE
A_V = 64
A_Q_LORA = 256
A_KV_LORA = 128
G_HEADS = 8
G_KV_HEADS = 2
G_GROUP = G_HEADS // G_KV_HEADS
G_HD = 64
N_BRANCH = 3
BRANCH_WIDTH = 512
N_MOD = 6

TOKEN_BLOCK = 512
Q_BLOCK = 256
KEY_BLOCK = 256
MLA_HEADS_PER_STEP = 8
GQA_GROUPS_PER_STEP = 2
MLSTM_CHUNK = 128
MLSTM_HEADS_PER_STEP = 4
LOG2E = 1.4426950408889634


def _params(n_axes):
    return pltpu.CompilerParams(dimension_semantics=("arbitrary",) * n_axes,
                                vmem_limit_bytes=VMEM_LIMIT_BYTES)


def _resident(shape, layer=None):
    nd = len(shape)
    if layer is None:
        return pl.BlockSpec(shape, lambda *_: (0,) * nd, pipeline_mode=pl.Buffered(1))
    return pl.BlockSpec((None,) + tuple(shape), lambda *_: (layer,) + (0,) * nd, pipeline_mode=pl.Buffered(1))


def _lane(shape, axis=None):
    return lax.broadcasted_iota(jnp.int32, shape, len(shape) - 1 if axis is None else axis)


def _dot(a, b):
    return jnp.dot(a, b, preferred_element_type=F32)


def _dot_nt(a, b):
    return lax.dot_general(a, b, (((1,), (1,)), ((), ())), preferred_element_type=F32)


def _split3(a):
    hi = a.astype(BF16)
    r1 = a - hi.astype(F32)
    mid = r1.astype(BF16)
    lo = (r1 - mid.astype(F32)).astype(BF16)
    return hi, mid, lo


def _dot01(a, m01):
    hi, mid, lo = _split3(a)
    return _dot(hi, m01) + _dot(mid, m01) + _dot(lo, m01)


def _dot01_left(m01, a):
    hi, mid, lo = _split3(a)
    return _dot(m01, hi) + _dot(m01, mid) + _dot(m01, lo)


def _sigmoid(x):
    return 0.5 * jnp.tanh(0.5 * x) + 0.5


def _log_sigmoid(x):
    return jnp.minimum(x, 0.0) - jnp.log(1.0 + jnp.exp(-jnp.abs(x)))


def _rms(x, width):
    ms = jnp.sum(x * x, axis=-1, keepdims=True) * (1.0 / width)
    return x * lax.rsqrt(ms + EPS)


def _rope(x, cos, sin_signed, half, period):
    n = x.shape[-1]
    first = (_lane(x.shape) % period) < half
    swapped = jnp.where(first, pltpu.roll(x, n - half, x.ndim - 1), pltpu.roll(x, half, x.ndim - 1))
    return x * cos + swapped * sin_signed


def _mod_kernel(c_ref, w_ref, b_ref, o_ref):
    c = c_ref[...]
    a = c * _sigmoid(c)
    o_ref[...] = _dot01(a, w_ref[...].astype(BF16)) + b_ref[...]


def _mod_call(cond, w_mod, b_mod):
    depth, d, n = w_mod.shape
    rows = cond.shape[0]
    tn = 1536
    return pl.pallas_call(
        _mod_kernel,
        out_shape=jax.ShapeDtypeStruct((depth, rows, n), F32),
        grid=(depth, n // tn),
        in_specs=[
            pl.BlockSpec((rows, d), lambda l, j: (0, 0)),
            pl.BlockSpec((None, d, tn), lambda l, j: (l, 0, j)),
            pl.BlockSpec((None, 1, tn), lambda l, j: (l, 0, j)),
        ],
        out_specs=pl.BlockSpec((None, rows, tn), lambda l, j: (l, 0, j)),
        compiler_params=_params(2),
        name="adaln_mod",
    )(cond, w_mod, b_mod.reshape(depth, 1, n))


MXU_COLS = 256
_GATE0, _GATE1 = 0, 3072
_MQ0 = 3072
_MK0 = 3584
_MV0 = 4096
_MO0 = 4608
_AQ0 = 5120
_AKV0 = 5376
_MISC0 = 5504
_GQ0 = 5632
_GK0 = 6144
_GV0 = 6272
_WIN_COLS = 6400
MG_LANE0 = A_ROPE


def _head_pair_ms(x):
    low = (_lane(x.shape) % LANES) < G_HD
    sq = x * x
    cols = []
    for c0 in range(0, x.shape[-1], LANES):
        s = sq[:, c0:c0 + LANES]
        lo = jnp.sum(jnp.where(low[:, c0:c0 + LANES], s, 0.0), axis=-1, keepdims=True)
        hi = jnp.sum(jnp.where(low[:, c0:c0 + LANES], 0.0, s), axis=-1, keepdims=True)
        cols.append(jnp.where(low[:, c0:c0 + LANES], lo, hi))
    ms = cols[0] if len(cols) == 1 else jnp.concatenate(cols, axis=-1)
    return ms * (1.0 / G_HD)


def _inproj_kernel(*refs, rotary):
    (x_ref, mod_ref, g1_ref, w_ref, bmg_ref, gql_ref, wuq_ref, gkvl_ref, gkn_ref, aqn_ref, gqn_ref) = refs[:11]
    pos = 11
    if rotary:
        ca_ref, sa_ref, cg_ref, sg_ref = refs[pos:pos + 4]
        pos += 4
    (gates_ref, qkv_ref, og_ref, mg_ref, qa_ref, ckv_ref, akr_ref, gq_ref, gk_ref, gv_ref) = refs[pos:]
    x = x_ref[...]
    d = x.shape[-1]
    sh1 = mod_ref[0:1, :]
    sc1 = mod_ref[1:2, :]
    h = (_rms(x, d) * g1_ref[...]) * (1.0 + sc1) + sh1
    hb = h.astype(BF16)

    def proj(c0, width):
        return _dot(hb, w_ref[:, c0:c0 + width])

    aq = _rms(proj(_AQ0, A_Q_LORA), A_Q_LORA) * gql_ref[...]
    qa = _dot(aq.astype(BF16), wuq_ref[...])
    for hd in range(A_HEADS):
        qh = _rms(qa[:, hd * LANES:(hd + 1) * LANES], A_QK) * aqn_ref[...]
        if rotary:
            qh = _rope(qh, ca_ref[...], sa_ref[...], A_ROPE // 2, LANES)
        qa_ref[:, hd * LANES:(hd + 1) * LANES] = (qh * (A_QK ** -0.5 * LOG2E)).astype(qa_ref.dtype)
    akv_misc = proj(_AKV0, A_KV_LORA + LANES)
    ckv_ref[...] = _rms(akv_misc[:, :A_KV_LORA], A_KV_LORA) * gkvl_ref[...]
    misc = akv_misc[:, A_KV_LORA:]
    akr_ref[...] = jnp.where(_lane(misc.shape) < A_ROPE, misc, 0.0)
    mg_ref[...] = misc + bmg_ref[...]

    gq_all = proj(_GQ0, G_HEADS * G_HD)
    for c0 in range(0, G_HEADS * G_HD, LANES):
        gq = gq_all[:, c0:c0 + LANES]
        gq = gq * lax.rsqrt(_head_pair_ms(gq) + EPS) * gqn_ref[...]
        if rotary:
            gq = _rope(gq, cg_ref[...], sg_ref[...], G_HD // 2, G_HD)
        gq_ref[:, c0:c0 + LANES] = (gq * (G_HD ** -0.5 * LOG2E)).astype(gq_ref.dtype)
    kvw = G_KV_HEADS * G_HD
    gkv = proj(_GK0, 2 * kvw)
    gk = gkv[:, :kvw]
    gk = gk * lax.rsqrt(_head_pair_ms(gk) + EPS) * gkn_ref[...]
    if rotary:
        gk = _rope(gk, cg_ref[...], sg_ref[...], G_HD // 2, G_HD)
    gk_ref[...] = gk
    gv_ref[...] = gkv[:, kvw:]

    hw = M_HEADS * M_DK
    gates_ref[...] = _sigmoid(proj(_GATE0, _GATE1 - _GATE0)).astype(gates_ref.dtype)
    og_ref[...] = _sigmoid(proj(_MO0, hw)).astype(og_ref.dtype)
    qkv_ref[:, 0:hw] = proj(_MQ0, hw).astype(qkv_ref.dtype)
    qkv_ref[:, hw:2 * hw] = (proj(_MK0, hw) * (M_DK ** -0.5)).astype(qkv_ref.dtype)
    qkv_ref[:, 2 * hw:3 * hw] = proj(_MV0, hw).astype(qkv_ref.dtype)


def _inproj_call(x, mod, lw, mod_row0, rows_per_mod, rope, seq):
    n, d = x.shape
    tm = TOKEN_BLOCK
    hw = M_HEADS * M_DK

    def mod_idx(i):
        return (mod_row0 + (i * tm) // rows_per_mod, 0, 0)

    def rows(width):
        return pl.BlockSpec((tm, width), lambda i: (i, 0))

    in_specs = [
        rows(d),
        pl.BlockSpec((None, N_MOD, d), mod_idx),
        _resident((1, d)),
        _resident((d, _WIN_COLS), lw["layer"]),
        _resident((1, LANES)),
        _resident((1, A_Q_LORA)),
        _resident((A_Q_LORA, A_HEADS * LANES), lw["layer"]),
        _resident((1, A_KV_LORA)),
        _resident((1, LANES)),
        _resident((1, LANES)),
        _resident((1, LANES)),
    ]
    args = [x, mod, lw["norm1_g"], lw["w_in"], lw["b_mgate"], lw["a_qlora_g"], lw["w_uq"], lw["a_kvlora_g"],
            lw["g_knorm_g"], lw["a_qnorm_g"], lw["g_qnorm_g"]]
    if rope is not None:
        blocks_per_seq = seq // tm
        in_specs += [pl.BlockSpec((tm, LANES), lambda i: (i % blocks_per_seq, 0))] * 4
        args += [rope[0][0], rope[0][1], rope[1][0], rope[1][1]]
    out_widths = [3 * d, 3 * hw, hw, LANES, A_HEADS * LANES, A_KV_LORA, LANES, G_HEADS * G_HD,
                  G_KV_HEADS * G_HD, G_KV_HEADS * G_HD]
    out_dtypes = [BF16, BF16, BF16, F32, BF16, F32, F32, BF16, F32, F32]
    return pl.pallas_call(
        functools.partial(_inproj_kernel, rotary=rope is not None),
        out_shape=[jax.ShapeDtypeStruct((n, w), t) for w, t in zip(out_widths, out_dtypes)],
        grid=(n // tm,),
        in_specs=in_specs,
        out_specs=[rows(w) for w in out_widths],
        compiler_params=_params(1),
        name="in_proj",
    )(*args)


def _mlstm_kernel(*refs, n_chunks, heads, has_init, emit_state):
    q_ref, k_ref, v_ref, og_ref, mg_ref, gn_ref = refs[:6]
    pos = 6
    if has_init:
        c0_ref, n0_ref, m0_ref = refs[pos:pos + 3]
        pos += 3
    y_ref = refs[pos]
    pos += 1
    if emit_state:
        cf_ref, nf_ref, mf_ref = refs[pos:pos + 3]
        pos += 3
    (pcol_scr, lfc_scr, grow_scr, prow_scr, lfr_scr, u_scr, nu_scr, gm_scr, tot_scr, cs_scr, ns_scr, ms_scr,
     c_scr, n_scr, m_scr) = refs[pos:]
    assert heads == M_HEADS
    n_gates = 4 * M_HEADS

    ln = MLSTM_CHUNK
    row = lax.broadcasted_iota(jnp.int32, (ln, ln), 0)
    col = lax.broadcasted_iota(jnp.int32, (ln, ln), 1)
    lower = col <= row
    upper = col >= row
    tril = jnp.where(lower, 1.0, 0.0).astype(BF16)
    triu = jnp.where(upper, 1.0, 0.0).astype(BF16)
    ones = jnp.ones((ln, ln), BF16)

    def gate_index(hh, d):
        return 2 * d * heads + hh, (2 * d + 1) * heads + hh

    def summaries(c, carry):
        r0 = pl.multiple_of(c * ln, ln)
        g = mg_ref[pl.ds(r0, ln), :]
        lf = _log_sigmoid(g) * LOG2E
        pcol_scr[pl.ds(r0, ln), :] = _dot01_left(tril, lf)
        lfc_scr[pl.ds(r0, ln), :] = lf
        g_rows = g.T[MG_LANE0:MG_LANE0 + n_gates, :]
        gr = g_rows * LOG2E
        lfr = _log_sigmoid(g_rows) * LOG2E
        pr = _dot01(lfr, triu)
        grow_scr[c] = gr
        prow_scr[c] = pr
        lfr_scr[c] = lfr
        for hh in range(heads):
            kb = k_ref[pl.ds(r0, ln), hh * LANES:(hh + 1) * LANES]
            k_t = kb.astype(F32).T
            v = v_ref[pl.ds(r0, ln), hh * LANES:(hh + 1) * LANES]
            for d in range(2):
                ji, jf = gate_index(hh, d)
                tot = pr[jf:jf + 1, ln - 1:ln]
                b_row = pr[jf:jf + 1, :] if d == 0 else tot - pr[jf:jf + 1, :] + lfr[jf:jf + 1, :]
                g_row = tot - b_row + gr[ji:ji + 1, :]
                gmax = jnp.max(g_row, axis=-1, keepdims=True)
                wg = jnp.exp2(g_row - gmax)
                u_scr[hh, d, c] = _dot((k_t * wg).astype(BF16), v)
                nu_scr[hh, d, c] = _dot(jnp.broadcast_to(wg, (SUBLANES, ln)).astype(BF16), kb)[0:1, :]
                gm_scr[hh, d, c] = jnp.broadcast_to(gmax, (1, ln))
                tot_scr[hh, d, c] = jnp.broadcast_to(tot, (1, ln))
        return carry

    lax.fori_loop(0, n_chunks, summaries, 0, unroll=min(2, n_chunks))

    for hh in range(heads):
        for d in range(2):
            if has_init:
                c_scr[hh, d] = c0_ref[d, hh]
                n_scr[hh, d] = n0_ref[d, hh]
                m_scr[hh, d] = m0_ref[d, hh] * LOG2E
            else:
                c_scr[hh, d] = jnp.zeros((M_DK, M_DV), F32)
                n_scr[hh, d] = jnp.zeros((1, M_DK), F32)
                m_scr[hh, d] = jnp.zeros((1, ln), F32)

    def scan(j, carry):
        for hh in range(heads):
            for d in range(2):
                c = j if d == 0 else n_chunks - 1 - j
                cst = c_scr[hh, d]
                nst = n_scr[hh, d]
                mst = m_scr[hh, d]
                cs_scr[hh, d, c] = cst.astype(BF16)
                ns_scr[hh, d, c] = nst
                ms_scr[hh, d, c] = mst
                gmax = gm_scr[hh, d, c]
                total = tot_scr[hh, d, c] + mst
                m_new = jnp.maximum(total, gmax)
                decay = jnp.exp2(total - m_new)
                scale = jnp.exp2(gmax - m_new)
                c_scr[hh, d] = cst * decay + u_scr[hh, d, c] * scale
                n_scr[hh, d] = nst * decay + nu_scr[hh, d, c] * scale
                m_scr[hh, d] = m_new
        return carry

    lax.fori_loop(0, n_chunks, scan, 0)

    def readout(c, carry):
        r0 = pl.multiple_of(c * ln, ln)
        pc = pcol_scr[pl.ds(r0, ln), :]
        lf = lfc_scr[pl.ds(r0, ln), :]
        gr = grow_scr[c]
        pr = prow_scr[c]
        lfr = lfr_scr[c]
        tot_all = pc[ln - 1:ln, :]
        for hh in range(heads):
            q = q_ref[pl.ds(r0, ln), hh * LANES:(hh + 1) * LANES]
            k = k_ref[pl.ds(r0, ln), hh * LANES:(hh + 1) * LANES]
            v = v_ref[pl.ds(r0, ln), hh * LANES:(hh + 1) * LANES]
            qk = _dot_nt(q, k)
            hsum = None
            for d in range(2):
                ji, jf = gate_index(hh, d)
                lane_f = MG_LANE0 + jf
                if d == 0:
                    b_col = pc[:, lane_f:lane_f + 1]
                    r_row = gr[ji:ji + 1, :] - pr[jf:jf + 1, :]
                    mask = lower
                else:
                    tot = tot_all[:, lane_f:lane_f + 1]
                    b_col = tot - pc[:, lane_f:lane_f + 1] + lf[:, lane_f:lane_f + 1]
                    r_row = gr[ji:ji + 1, :] - (tot - pr[jf:jf + 1, :] + lfr[jf:jf + 1, :])
                    mask = upper
                b_colb = jnp.broadcast_to(b_col, (ln, ln))
                d_log = jnp.where(mask, b_colb + r_row, -jnp.inf)
                dmax = jnp.broadcast_to(jnp.max(d_log, axis=-1, keepdims=True), (ln, ln))
                sb = (qk * jnp.exp2(d_log - dmax)).astype(BF16)
                num_loc = _dot(sb, v)
                den_loc = _dot(sb, ones)
                qc = _dot(q, cs_scr[hh, d, c])
                qn = _dot_nt(q, jnp.broadcast_to(ns_scr[hh, d, c], (ln, M_DK)).astype(BF16))
                inter = b_colb + ms_scr[hh, d, c]
                m_t = jnp.maximum(inter, dmax)
                w_inter = jnp.exp2(inter - m_t)
                w_loc = jnp.exp2(dmax - m_t)
                num = qc * w_inter + num_loc * w_loc
                den = qn * w_inter + den_loc * w_loc
                hc = num / jnp.maximum(jnp.abs(den), jnp.exp2(-m_t))
                hsum = hc if hsum is None else hsum + hc
            y_ref[pl.ds(r0, ln), hh * LANES:(hh + 1) * LANES] = (
                _rms(hsum, M_DV) * gn_ref[...] * og_ref[pl.ds(r0, ln), hh * LANES:(hh + 1) * LANES]
            ).astype(y_ref.dtype)
        return carry

    lax.fori_loop(0, n_chunks, readout, 0, unroll=min(2, n_chunks))

    if emit_state:
        for hh in range(heads):
            for d in range(2):
                cf_ref[d, hh] = c_scr[hh, d]
                nf_ref[d, hh] = n_scr[hh, d]
                mf_ref[d, hh] = m_scr[hh, d] * (1.0 / LOG2E)


def _mlstm_call(qkv, og, mg, lw, batch, seq, init, emit_state):
    n = batch * seq
    ln = MLSTM_CHUNK
    nc = seq // ln
    h = M_HEADS
    hs = MLSTM_HEADS_PER_STEP
    ng = h // hs

    def tok(colblock):
        return pl.BlockSpec((seq, hs * LANES), lambda b, hg: (b, colblock(hg)))

    in_specs = [
        tok(lambda hg: hg), tok(lambda hg: ng + hg), tok(lambda hg: 2 * ng + hg),
        tok(lambda hg: hg),
        pl.BlockSpec((seq, LANES), lambda b, hg: (b, 0)),
        pl.BlockSpec((1, M_DV), lambda b, hg: (0, 0)),
    ]
    args = [qkv, qkv, qkv, og, mg, lw["m_norm_g"]]
    state_specs = [
        pl.BlockSpec((None, 2, hs, M_DK, M_DV), lambda b, hg: (b, 0, hg, 0, 0)),
        pl.BlockSpec((None, 2, hs, 1, M_DK), lambda b, hg: (b, 0, hg, 0, 0)),
        pl.BlockSpec((None, 2, hs, 1, ln), lambda b, hg: (b, 0, hg, 0, 0)),
    ]
    if init is not None:
        c0, n0, m0 = init
        in_specs += state_specs
        args += [c0, n0.reshape(batch, 2, h, 1, M_DK),
                 jnp.broadcast_to(m0[..., None, None], (batch, 2, h, 1, ln))]
    out_shape = [jax.ShapeDtypeStruct((n, h * M_DV), BF16)]
    out_specs = [tok(lambda hg: hg)]
    if emit_state:
        out_shape += [
            jax.ShapeDtypeStruct((batch, 2, h, M_DK, M_DV), F32),
            jax.ShapeDtypeStruct((batch, 2, h, 1, M_DK), F32),
            jax.ShapeDtypeStruct((batch, 2, h, 1, ln), F32),
        ]
        out_specs += state_specs
    per = (hs, 2, nc)
    outs = pl.pallas_call(
        functools.partial(_mlstm_kernel, n_chunks=nc, heads=hs, has_init=init is not None,
                          emit_state=emit_state),
        out_shape=out_shape,
        grid=(batch, ng),
        in_specs=in_specs,
        out_specs=out_specs,
        scratch_shapes=[
            pltpu.VMEM((seq, LANES), F32), pltpu.VMEM((seq, LANES), F32),
            pltpu.VMEM((nc, 4 * h, ln), F32), pltpu.VMEM((nc, 4 * h, ln), F32), pltpu.VMEM((nc, 4 * h, ln), F32),
            pltpu.VMEM(per + (M_DK, M_DV), F32), pltpu.VMEM(per + (1, M_DK), F32),
            pltpu.VMEM(per + (1, ln), F32), pltpu.VMEM(per + (1, ln), F32),
            pltpu.VMEM(per + (M_DK, M_DV), BF16), pltpu.VMEM(per + (1, M_DK), F32), pltpu.VMEM(per + (1, ln), F32),
            pltpu.VMEM((hs, 2, M_DK, M_DV), F32), pltpu.VMEM((hs, 2, 1, M_DK), F32), pltpu.VMEM((hs, 2, 1, ln), F32),
        ],
        compiler_params=_params(2),
        name="mlstm",
    )(*args)
    if emit_state:
        ym, cf, nf, mf = outs
        return ym, (cf, nf[:, :, :, 0, :], mf[:, :, :, 0, 0])
    return outs[0], None


HEAD_V = 64
VT_ROWS = 80


def _ones_row(v_t):
    return jnp.where(lax.broadcasted_iota(jnp.int32, v_t.shape, 0) == HEAD_V, 1.0, v_t)


def _flash_heads(qs, keys_of, vt_chunk, n_keys, o_ref):
    def values(e, s_t):
        m = acc = None
        for c0 in range(0, n_keys, KEY_BLOCK):
            c1 = min(c0 + KEY_BLOCK, n_keys)
            s = s_t[c0:c1, :]
            mc = jnp.max(s, axis=0, keepdims=True)
            if c0 == 0:
                m = mc
                acc = _dot(vt_chunk(e, c0, c1), jnp.exp2(s - mc).astype(BF16))
            else:
                m_new = jnp.maximum(m, mc)
                acc = acc * jnp.exp2(m - m_new) + _dot(vt_chunk(e, c0, c1), jnp.exp2(s - m_new).astype(BF16))
                m = m_new
        return acc[0:HEAD_V, :] / acc[HEAD_V:HEAD_V + 1, :]

    outs = []
    s_next = _dot_nt(keys_of(0), qs[0])
    for e in range(len(qs)):
        s_t = s_next
        if e + 1 < len(qs):
            s_next = _dot_nt(keys_of(e + 1), qs[e + 1])
        outs.append(values(e, s_t))
    for e in range(0, len(outs), 2):
        pair = jnp.concatenate([outs[e], outs[e + 1]], axis=0)
        o_ref[:, (e // 2) * LANES:(e // 2 + 1) * LANES] = pair.T.astype(o_ref.dtype)


def _mla_kernel(*refs, past, rotary, heads):
    q_ref, ckv_ref, akr_ref = refs[:3]
    pos = 3
    if past:
        ckvc_ref, krc_ref = refs[pos:pos + 2]
        pos += 2
    wk_ref, wvt_ref, kg_ref = refs[pos:pos + 3]
    pos += 3
    if rotary:
        ck_ref, sk_ref = refs[pos:pos + 2]
        pos += 2
    o_ref, kbuf, vbuf = refs[pos:]
    seq = ckv_ref.shape[0]

    @pl.when(pl.program_id(2) == 0)
    def _():
        for e in range(heads):
            wk_h = wk_ref[:, e * LANES:(e + 1) * LANES]
            wvt_h = wvt_ref[e * VT_ROWS:(e + 1) * VT_ROWS, :]

            def keys(ckv, kr):
                cb = ckv.astype(BF16)
                kcat = _dot(cb, wk_h) + kr
                return _ones_row(_dot_nt(wvt_h, cb)), _rms(kcat, A_QK) * kg_ref[...]

            v_t, kn = keys(ckv_ref[...], akr_ref[...])
            if rotary:
                kn = _rope(kn, ck_ref[...], sk_ref[...], A_ROPE // 2, LANES)
            kbuf[e, past:past + seq, :] = kn.astype(BF16)
            vbuf[e, :, past:past + seq] = v_t.astype(BF16)
            if past:
                v_tc, knc = keys(ckvc_ref[...], krc_ref[...])
                kbuf[e, 0:past, :] = knc.astype(BF16)
                vbuf[e, :, 0:past] = v_tc.astype(BF16)

    qs = [q_ref[:, e * LANES:(e + 1) * LANES] for e in range(heads)]
    _flash_heads(qs, lambda e: kbuf[e], lambda e, c0, c1: vbuf[e, :, c0:c1], past + seq, o_ref)


def _mla_call(qa, ckv, akr, lw, batch, seq, cache, rope_tabs):
    n = batch * seq
    tq = min(Q_BLOCK, seq)
    nq = seq // tq
    past = 0 if cache is None else cache[0].shape[1]
    heads = MLA_HEADS_PER_STEP if nq > 1 else A_HEADS
    in_specs = [
        pl.BlockSpec((tq, heads * LANES), lambda b, hg, qi: (b * nq + qi, hg)),
        pl.BlockSpec((seq, LANES), lambda b, hg, qi: (b, 0)),
        pl.BlockSpec((seq, LANES), lambda b, hg, qi: (b, 0)),
    ]
    args = [qa, ckv, akr]
    if past:
        in_specs += [pl.BlockSpec((None, past, LANES), lambda b, hg, qi: (b, 0, 0))] * 2
        args += [cache[0], cache[1]]
    layer = lw["layer"]
    in_specs += [
        pl.BlockSpec((None, A_KV_LORA, heads * LANES), lambda b, hg, qi: (layer, 0, hg)),
        pl.BlockSpec((None, heads * VT_ROWS, A_KV_LORA), lambda b, hg, qi: (layer, hg, 0)),
        pl.BlockSpec((1, LANES), lambda b, hg, qi: (0, 0)),
    ]
    args += [lw["w_uk"], lw["w_uvt"], lw["a_knorm_g"]]
    if rope_tabs is not None:
        cos, sin = rope_tabs
        in_specs += [pl.BlockSpec((seq, LANES), lambda b, hg, qi: (0, 0))] * 2
        args += [cos, sin]
    return pl.pallas_call(
        functools.partial(_mla_kernel, past=past, rotary=rope_tabs is not None, heads=heads),
        out_shape=jax.ShapeDtypeStruct((n, A_HEADS * A_V), BF16),
        grid=(batch, A_HEADS // heads, nq),
        in_specs=in_specs,
        out_specs=pl.BlockSpec((tq, heads * A_V), lambda b, hg, qi: (b * nq + qi, hg)),
        scratch_shapes=[pltpu.VMEM((heads, past + seq, LANES), BF16),
                        pltpu.VMEM((heads, VT_ROWS, past + seq), BF16)],
        compiler_params=_params(3),
        name="mla_attn",
    )(*args)


def _gqa_kernel(*refs, past, groups):
    q_ref, k_ref, v_ref = refs[:3]
    pos = 3
    if past:
        kc_ref, vc_ref = refs[pos:pos + 2]
        pos += 2
    o_ref, kbuf, vbuf = refs[pos:]
    seq = k_ref.shape[0]
    qw = G_GROUP * G_HD

    @pl.when(pl.program_id(2) == 0)
    def _():
        for gi in range(groups):
            grp = pl.program_id(1) * groups + gi

            r = lax.broadcasted_iota(jnp.int32, (VT_ROWS, LANES), 0)
            c = lax.broadcasted_iota(jnp.int32, (VT_ROWS, LANES), 1)
            pick = jnp.where((c == r + grp * G_HD) & (r < G_HD), 1.0, 0.0).astype(BF16)

            def fill(r0, k, v):
                low = _lane(k.shape) < G_HD
                k_lo = jnp.where(grp == 0, k, pltpu.roll(k, G_HD, 1))
                rows = k.shape[0]
                kbuf[gi, r0:r0 + rows, :] = jnp.where(low, k_lo, pltpu.roll(k_lo, G_HD, 1)).astype(BF16)
                vbuf[gi, :, r0:r0 + rows] = _ones_row(_dot_nt(pick, v.astype(BF16))).astype(BF16)

            fill(past, k_ref[...], v_ref[...])
            if past:
                fill(0, kc_ref[...], vc_ref[...])

    qs = []
    for gi in range(groups):
        for j in range(G_GROUP):
            col = q_ref[:, gi * qw + (j // 2) * LANES:gi * qw + (j // 2 + 1) * LANES]
            keep = (_lane(col.shape) < G_HD) == (j % 2 == 0)
            qs.append(jnp.where(keep, col, jnp.zeros_like(col)))
    _flash_heads(qs, lambda e: kbuf[e // G_GROUP], lambda e, c0, c1: vbuf[e // G_GROUP, :, c0:c1],
                 past + seq, o_ref)


def _gqa_call(gq, gk, gv, batch, seq, cache):
    n = batch * seq
    tq = min(Q_BLOCK, seq)
    nq = seq // tq
    past = 0 if cache is None else cache[0].shape[1]
    groups = GQA_GROUPS_PER_STEP if nq > 1 else G_KV_HEADS
    qw = G_GROUP * G_HD
    kvw = G_KV_HEADS * G_HD
    in_specs = [
        pl.BlockSpec((tq, groups * qw), lambda b, g, qi: (b * nq + qi, g)),
        pl.BlockSpec((seq, kvw), lambda b, g, qi: (b, 0)),
        pl.BlockSpec((seq, kvw), lambda b, g, qi: (b, 0)),
    ]
    args = [gq, gk, gv]
    if past:
        in_specs += [pl.BlockSpec((None, past, kvw), lambda b, g, qi: (b, 0, 0))] * 2
        args += [cache[0], cache[1]]
    return pl.pallas_call(
        functools.partial(_gqa_kernel, past=past, groups=groups),
        out_shape=jax.ShapeDtypeStruct((n, G_HEADS * G_HD), BF16),
        grid=(batch, G_KV_HEADS // groups, nq),
        in_specs=in_specs,
        out_specs=pl.BlockSpec((tq, groups * qw), lambda b, g, qi: (b * nq + qi, g)),
        scratch_shapes=[pltpu.VMEM((groups, past + seq, LANES), BF16),
                        pltpu.VMEM((groups, VT_ROWS, past + seq), BF16)],
        compiler_params=_params(3),
        name="gqa_attn",
    )(*args)


def _merge_ffn_kernel(x_ref, ym_ref, ya_ref, yg_ref, gates_ref, mod_ref, g2_ref, wb_ref, wo_ref, wfi_ref,
                      wfo_ref, o_ref, *, ff_chunks):
    x = x_ref[...]
    d = x.shape[-1]
    mixed = None
    for i, y_ref in enumerate((ym_ref, ya_ref, yg_ref)):
        br = _dot(y_ref[...].astype(BF16), wb_ref[i]) * gates_ref[:, i * d:(i + 1) * d]
        mixed = br if mixed is None else mixed + br
    gt1 = mod_ref[2:3, :]
    x1 = x + gt1 * _dot(mixed.astype(BF16), wo_ref[...])

    sh2 = mod_ref[3:4, :]
    sc2 = mod_ref[4:5, :]
    gt2 = mod_ref[5:6, :]
    h2 = ((_rms(x1, d) * g2_ref[...]) * (1.0 + sc2) + sh2).astype(BF16)
    d_ff = wfo_ref.shape[0]
    acc = None
    for c0, c1 in ff_chunks:
        ug = _dot(h2, wfi_ref[:, c0:c1])
        uv = _dot(h2, wfi_ref[:, d_ff + c0:d_ff + c1])
        act = (ug * _sigmoid(ug) * uv).astype(BF16)
        part = _dot(act, wfo_ref[c0:c1, :])
        acc = part if acc is None else acc + part
    o_ref[...] = x1 + gt2 * acc


def _merge_ffn_call(x, ym, ya, yg, gates, mod, lw, mod_row0, rows_per_mod):
    n, d = x.shape
    tm = TOKEN_BLOCK
    d_ff = lw["w_ffn_out"].shape[1]
    layer = lw["layer"]
    split = -(-(d_ff // 2) // MXU_COLS) * MXU_COLS
    ff_chunks = ((0, split), (split, d_ff)) if 0 < split < d_ff else ((0, d_ff),)

    def mod_idx(i):
        return (mod_row0 + (i * tm) // rows_per_mod, 0, 0)

    def rows(width):
        return pl.BlockSpec((tm, width), lambda i: (i, 0))

    return pl.pallas_call(
        functools.partial(_merge_ffn_kernel, ff_chunks=ff_chunks),
        out_shape=jax.ShapeDtypeStruct((n, d), F32),
        grid=(n // tm,),
        in_specs=[
            rows(d), rows(BRANCH_WIDTH), rows(BRANCH_WIDTH), rows(BRANCH_WIDTH), rows(N_BRANCH * d),
            pl.BlockSpec((None, N_MOD, d), mod_idx),
            _resident((1, d)),
            _resident((N_BRANCH, BRANCH_WIDTH, d), layer),
            _resident((d, d), layer),
            _resident((d, 2 * d_ff), layer),
            _resident((d_ff, d), layer),
        ],
        out_specs=rows(d),
        compiler_params=_params(1),
        name="merge_ffn",
    )(x, ym, ya, yg, gates, mod, lw["norm2_g"], lw["w_branch"], lw["w_out"], lw["w_ffn_in"], lw["w_ffn_out"])


def _pad_cols(w, width):
    return jnp.pad(w, ((0, 0), (0, width - w.shape[1])))


def _mla_q_layout(a):
    lead = a.shape[:-1]
    a = a.reshape(lead + (A_HEADS, A_QK))
    z = jnp.zeros(lead + (A_HEADS, LANES - A_QK), a.dtype)
    return jnp.concatenate([a[..., A_NOPE:], z, a[..., :A_NOPE]], axis=-1).reshape(lead + (A_HEADS * LANES,))


def _stacked_weights(w_in, w_uq, w_ukv, w_branch, w_out, w_ffn_in, w_ffn_out):
    depth, d, _ = w_in.shape
    hw = M_HEADS * M_DK
    sizes = (N_BRANCH * d, hw, hw, hw, hw, 4 * M_HEADS, A_Q_LORA, A_KV_LORA, A_ROPE, G_HEADS * G_HD,
             G_KV_HEADS * G_HD, G_KV_HEADS * G_HD)
    edges = np.concatenate([[0], np.cumsum(sizes)]).tolist()
    (wmg, waq, wakv, wakr, wgq, wgk, wgv) = [w_in[:, :, a:b] for a, b in zip(edges[5:-1], edges[6:])]
    misc_pad = jnp.zeros((depth, d, LANES - A_ROPE - 4 * M_HEADS), w_in.dtype)
    tail = jnp.concatenate([waq, wakv, wakr, wmg, misc_pad, wgq, wgk, wgv], axis=2).astype(BF16)
    w_in_p = jnp.pad(w_in, ((0, 0), (0, 0), (0, _WIN_COLS - w_in.shape[2]))).astype(BF16)
    w_in_p = lax.dynamic_update_slice(w_in_p, tail, (0, 0, _AQ0))
    ukv = w_ukv.reshape(depth, A_KV_LORA, A_HEADS, A_NOPE + A_V)
    uk = jnp.pad(ukv[..., :A_NOPE], ((0, 0), (0, 0), (0, 0), (LANES - A_NOPE, 0)))
    uvt = jnp.pad(ukv[..., A_NOPE:].transpose(0, 2, 3, 1), ((0, 0), (0, 0), (0, VT_ROWS - A_V), (0, 0)))
    return dict(
        w_in=w_in_p,
        w_uq=_mla_q_layout(w_uq).astype(BF16),
        w_uk=uk.reshape(depth, A_KV_LORA, A_HEADS * LANES).astype(BF16),
        w_uvt=uvt.reshape(depth, A_HEADS * VT_ROWS, A_KV_LORA).astype(BF16),
        w_branch=w_branch.astype(BF16),
        w_out=w_out.astype(BF16),
        w_ffn_in=w_ffn_in.astype(BF16),
        w_ffn_out=w_ffn_out.astype(BF16),
    )


def _layer_vectors(l, b_mgate, norm1_g, m_norm_g, a_qlora_g, a_kvlora_g, a_qnorm_g, a_knorm_g, g_qnorm_g,
                   g_knorm_g, norm2_g):
    return dict(
        layer=l,
        b_mgate=jnp.pad(b_mgate[l][None, :], ((0, 0), (MG_LANE0, LANES - MG_LANE0 - 4 * M_HEADS))),
        norm1_g=norm1_g[l][None, :],
        m_norm_g=m_norm_g[l][None, :],
        a_qlora_g=a_qlora_g[l][None, :],
        a_kvlora_g=a_kvlora_g[l][None, :],
        a_qnorm_g=_mla_q_layout(jnp.tile(a_qnorm_g[l], A_HEADS)[None, :])[:, :LANES],
        a_knorm_g=_mla_q_layout(jnp.tile(a_knorm_g[l], A_HEADS)[None, :])[:, :LANES],
        g_qnorm_g=jnp.tile(g_qnorm_g[l], LANES // G_HD)[None, :],
        g_knorm_g=jnp.tile(g_knorm_g[l], G_KV_HEADS)[None, :],
        norm2_g=norm2_g[l][None, :],
    )


def _axial_angles(seq, rot_dim):
    n_freq = rot_dim // 4
    freqs = ROPE_BASE ** (-jnp.arange(n_freq, dtype=F32) / n_freq)
    t = jnp.arange(seq)
    row = (t // GRID_W).astype(F32)
    col = (t % GRID_W).astype(F32)
    return jnp.concatenate([row[:, None] * freqs, col[:, None] * freqs], axis=-1)


def _rope_tables(seq):
    ang = _axial_angles(seq, A_ROPE)
    one = jnp.ones((seq, LANES - A_ROPE), F32)
    mla_cos = jnp.concatenate([jnp.cos(ang), jnp.cos(ang), one], axis=-1)
    mla_sin = jnp.concatenate([-jnp.sin(ang), jnp.sin(ang), 0.0 * one], axis=-1)
    ang = _axial_angles(seq, G_HD)
    cos = jnp.concatenate([jnp.cos(ang), jnp.cos(ang)], axis=-1)
    sin = jnp.concatenate([-jnp.sin(ang), jnp.sin(ang)], axis=-1)
    gqa = (jnp.tile(cos, (1, LANES // G_HD)), jnp.tile(sin, (1, LANES // G_HD)))
    return (mla_cos, mla_sin), gqa


def _layer(x, mod, lw, batch, seq, mod_row0, rows_per_mod, ctx, rope):
    (gates, qkv, og, mg, qa, ckv, akr, gq, gk, gv) = _inproj_call(x, mod, lw, mod_row0, rows_per_mod, rope, seq)
    if ctx is None:
        ym, state = _mlstm_call(qkv, og, mg, lw, batch, seq, None, True)
        ya = _mla_call(qa, ckv, akr, lw, batch, seq, None, None)
        yg = _gqa_call(gq, gk, gv, batch, seq, None)
        new_ctx = dict(state=state, ckv=ckv, kr=akr[:, :A_ROPE], gk=gk, gv=gv)
    else:
        ym, _ = _mlstm_call(qkv, og, mg, lw, batch, seq, ctx["mlstm"], False)
        ya = _mla_call(qa, ckv, akr, lw, batch, seq, ctx["mla"], rope[0])
        yg = _gqa_call(gq, gk, gv, batch, seq, ctx["gqa"])
        new_ctx = None
    x = _merge_ffn_call(x, ym, ya, yg, gates, mod, lw, mod_row0, rows_per_mod)
    return x, new_ctx


def kernel(x_prompt, x_sample, state_mlstm_C, state_mlstm_n, state_mlstm_m, cache_mla_ckv, cache_mla_krope,
           cache_gqa_k, cache_gqa_v, c, c_ctx, w_mod, b_mod, norm1_g, w_in, b_mgate, m_norm_g, a_qlora_g,
           a_kvlora_g, w_uq, w_ukv, a_qnorm_g, a_knorm_g, g_qnorm_g, g_knorm_g, w_branch, w_out, norm2_g,
           w_ffn_in, w_ffn_out):
    batch, seq, d = x_prompt.shape
    dbatch, dseq, _ = x_sample.shape
    depth = w_in.shape[0]
    past = cache_mla_ckv.shape[2]

    n_rows = -(-(1 + dbatch) // SUBLANES) * SUBLANES
    cond = jnp.concatenate([c_ctx[None, :], c, jnp.zeros((n_rows - 1 - dbatch, d), F32)], axis=0)
    mod_all = _mod_call(cond, w_mod, b_mod).reshape(depth, n_rows, N_MOD, d)

    rope = _rope_tables(dseq)
    xp = x_prompt.reshape(batch * seq, d)
    xs = x_sample.reshape(dbatch * dseq, d)
    ctx_layers = []
    stacked = _stacked_weights(w_in, w_uq, w_ukv, w_branch, w_out, w_ffn_in, w_ffn_out)
    for l in range(depth):
        lw = dict(stacked, **_layer_vectors(l, b_mgate, norm1_g, m_norm_g, a_qlora_g, a_kvlora_g, a_qnorm_g,
                                            a_knorm_g, g_qnorm_g, g_knorm_g, norm2_g))
        xp, st = _layer(xp, mod_all[l], lw, batch, seq, 0, batch * seq, None, None)
        ctx_layers.append(st)
        ctx = dict(
            mlstm=(state_mlstm_C[:, l], state_mlstm_n[:, l], state_mlstm_m[:, l]),
            mla=(cache_mla_ckv[:, l], _pad_cols(cache_mla_krope[:, l].reshape(dbatch * past, A_ROPE), LANES)
                 .reshape(dbatch, past, LANES)),
            gqa=(cache_gqa_k[:, l].reshape(dbatch, past, G_KV_HEADS * G_HD),
                 cache_gqa_v[:, l].reshape(dbatch, past, G_KV_HEADS * G_HD)),
        )
        xs, _ = _layer(xs, mod_all[l], lw, dbatch, dseq, 1, dseq, ctx, rope)

    def stack(fn):
        return jnp.stack([fn(s) for s in ctx_layers], axis=1)

    new_c = stack(lambda s: s["state"][0])
    new_n = stack(lambda s: s["state"][1])
    new_m = stack(lambda s: s["state"][2])
    new_ckv = stack(lambda s: s["ckv"].reshape(batch, seq, A_KV_LORA))
    new_kr = stack(lambda s: s["kr"].reshape(batch, seq, A_ROPE))
    new_gk = stack(lambda s: s["gk"].reshape(batch, seq, G_KV_HEADS, G_HD))
    new_gv = stack(lambda s: s["gv"].reshape(batch, seq, G_KV_HEADS, G_HD))
    return (xp.reshape(batch, seq, d), xs.reshape(dbatch, dseq, d), new_c, new_n, new_m, new_ckv, new_kr,
            new_gk, new_gv)
```

```python
import functools

import numpy as np
import jax
import jax.numpy as jnp
from jax import lax
from jax.experimental import pallas as pl
from jax.experimental.pallas import tpu as pltpu

F32 = jnp.float32
BF16 = jnp.bfloat16

LANES = 128
SUBLANES = 8
VMEM_LIMIT_BYTES = 56 * 1024 * 1024

EPS = 1e-6
ROPE_BASE = 10000.0
GRID_W = 64

M_HEADS = 4
M_DK = 128
M_DV = 128
A_HEADS = 8
A_NOPE = 64
A_ROPE = 32
A_QK = A_NOPE + A_ROPE
A_V = 64
A_Q_LORA = 256
A_KV_LORA = 128
G_HEADS = 8
G_KV_HEADS = 2
G_GROUP = G_HEADS // G_KV_HEADS
G_HD = 64
N_BRANCH = 3
BRANCH_WIDTH = 512
N_MOD = 6

TOKEN_BLOCK = 512
Q_BLOCK = 256
KEY_BLOCK = 256
MLA_HEADS_PER_STEP = 8
GQA_GROUPS_PER_STEP = 2
MLSTM_CHUNK = 128
MLSTM_HEADS_PER_STEP = 4
LOG2E = 1.4426950408889634


def _params(n_axes):
    return pltpu.CompilerParams(dimension_semantics=("arbitrary",) * n_axes,
                                vmem_limit_bytes=VMEM_LIMIT_BYTES)


def _resident(shape, layer=None):
    nd = len(shape)
    if layer is None:
        return pl.BlockSpec(shape, lambda *_: (0,) * nd, pipeline_mode=pl.Buffered(1))
    return pl.BlockSpec((None,) + tuple(shape), lambda *_: (layer,) + (0,) * nd, pipeline_mode=pl.Buffered(1))


def _lane(shape, axis=None):
    return lax.broadcasted_iota(jnp.int32, shape, len(shape) - 1 if axis is None else axis)


def _dot(a, b):
    return jnp.dot(a, b, preferred_element_type=F32)


def _dot_nt(a, b):
    return lax.dot_general(a, b, (((1,), (1,)), ((), ())), preferred_element_type=F32)


def _split3(a):
    hi = a.astype(BF16)
    r1 = a - hi.astype(F32)
    mid = r1.astype(BF16)
    lo = (r1 - mid.astype(F32)).astype(BF16)
    return hi, mid, lo


def _dot01(a, m01):
    hi, mid, lo = _split3(a)
    return _dot(hi, m01) + _dot(mid, m01) + _dot(lo, m01)


def _dot01_left(m01, a):
    hi, mid, lo = _split3(a)
    return _dot(m01, hi) + _dot(m01, mid) + _dot(m01, lo)


def _sigmoid(x):
    return 0.5 * jnp.tanh(0.5 * x) + 0.5


def _log_sigmoid(x):
    return jnp.minimum(x, 0.0) - jnp.log(1.0 + jnp.exp(-jnp.abs(x)))


def _rms(x, width):
    ms = jnp.sum(x * x, axis=-1, keepdims=True) * (1.0 / width)
    return x * lax.rsqrt(ms + EPS)


def _rope(x, cos, sin_signed, half, period):
    n = x.shape[-1]
    first = (_lane(x.shape) % period) < half
    swapped = jnp.where(first, pltpu.roll(x, n - half, x.ndim - 1), pltpu.roll(x, half, x.ndim - 1))
    return x * cos + swapped * sin_signed


def _mod_kernel(c_ref, w_ref, b_ref, o_ref):
    c = c_ref[...]
    a = c * _sigmoid(c)
    o_ref[...] = _dot01(a, w_ref[...].astype(BF16)) + b_ref[...]


def _mod_call(cond, w_mod, b_mod):
    depth, d, n = w_mod.shape
    rows = cond.shape[0]
    tn = 1536
    return pl.pallas_call(
        _mod_kernel,
        out_shape=jax.ShapeDtypeStruct((depth, rows, n), F32),
        grid=(depth, n // tn),
        in_specs=[
            pl.BlockSpec((rows, d), lambda l, j: (0, 0)),
            pl.BlockSpec((None, d, tn), lambda l, j: (l, 0, j)),
            pl.BlockSpec((None, 1, tn), lambda l, j: (l, 0, j)),
        ],
        out_specs=pl.BlockSpec((None, rows, tn), lambda l, j: (l, 0, j)),
        compiler_params=_params(2),
        name="adaln_mod",
    )(cond, w_mod, b_mod.reshape(depth, 1, n))


MXU_COLS = 256
_GATE0, _GATE1 = 0, 3072
_MQ0 = 3072
_MK0 = 3584
_MV0 = 4096
_MO0 = 4608
_AQ0 = 5120
_AKV0 = 5376
_MISC0 = 5504
_GQ0 = 5632
_GK0 = 6144
_GV0 = 6272
_WIN_COLS = 6400
MG_LANE0 = A_ROPE


def _head_pair_ms(x):
    low = (_lane(x.shape) % LANES) < G_HD
    sq = x * x
    cols = []
    for c0 in range(0, x.shape[-1], LANES):
        s = sq[:, c0:c0 + LANES]
        lo = jnp.sum(jnp.where(low[:, c0:c0 + LANES], s, 0.0), axis=-1, keepdims=True)
        hi = jnp.sum(jnp.where(low[:, c0:c0 + LANES], 0.0, s), axis=-1, keepdims=True)
        cols.append(jnp.where(low[:, c0:c0 + LANES], lo, hi))
    ms = cols[0] if len(cols) == 1 else jnp.concatenate(cols, axis=-1)
    return ms * (1.0 / G_HD)


def _inproj_kernel(*refs, rotary):
    (x_ref, mod_ref, g1_ref, w_ref, wt_ref, bmg_ref, gql_ref, wuq_ref, gkvl_ref, gkn_ref, aqn_ref,
     gqn_ref) = refs[:12]
    pos = 12
    if rotary:
        ca_ref, sa_ref, cg_ref, sg_ref = refs[pos:pos + 4]
        pos += 4
    (gates_ref, qkv_ref, og_ref, mg_ref, qa_ref, ckv_ref, akr_ref, gq_ref, gk_ref, gv_ref) = refs[pos:]
    x = x_ref[...]
    d = x.shape[-1]
    sh1 = mod_ref[0:1, :]
    sc1 = mod_ref[1:2, :]
    h = (_rms(x, d) * g1_ref[...]) * (1.0 + sc1) + sh1
    hb = h.astype(BF16)

    def proj(c0, width):
        if c0 < _AQ0:
            return _dot(hb, w_ref[:, c0:c0 + width])
        return _dot(hb, wt_ref[:, c0 - _AQ0:c0 - _AQ0 + width])

    aq = _rms(proj(_AQ0, A_Q_LORA), A_Q_LORA) * gql_ref[...]
    qa = _dot(aq.astype(BF16), wuq_ref[...])
    for hd in range(A_HEADS):
        qh = _rms(qa[:, hd * LANES:(hd + 1) * LANES], A_QK) * aqn_ref[...]
        if rotary:
            qh = _rope(qh, ca_ref[...], sa_ref[...], A_ROPE // 2, LANES)
        qa_ref[:, hd * LANES:(hd + 1) * LANES] = (qh * (A_QK ** -0.5 * LOG2E)).astype(qa_ref.dtype)
    akv_misc = proj(_AKV0, A_KV_LORA + LANES)
    ckv_ref[...] = _rms(akv_misc[:, :A_KV_LORA], A_KV_LORA) * gkvl_ref[...]
    misc = akv_misc[:, A_KV_LORA:]
    akr_ref[...] = jnp.where(_lane(misc.shape) < A_ROPE, misc, 0.0)
    mg_ref[...] = misc + bmg_ref[...]

    gq_all = proj(_GQ0, G_HEADS * G_HD)
    for c0 in range(0, G_HEADS * G_HD, LANES):
        gq = gq_all[:, c0:c0 + LANES]
        gq = gq * lax.rsqrt(_head_pair_ms(gq) + EPS) * gqn_ref[...]
        if rotary:
            gq = _rope(gq, cg_ref[...], sg_ref[...], G_HD // 2, G_HD)
        gq_ref[:, c0:c0 + LANES] = (gq * (G_HD ** -0.5 * LOG2E)).astype(gq_ref.dtype)
    kvw = G_KV_HEADS * G_HD
    gkv = proj(_GK0, 2 * kvw)
    gk = gkv[:, :kvw]
    gk = gk * lax.rsqrt(_head_pair_ms(gk) + EPS) * gkn_ref[...]
    if rotary:
        gk = _rope(gk, cg_ref[...], sg_ref[...], G_HD // 2, G_HD)
    gk_ref[...] = gk
    gv_ref[...] = gkv[:, kvw:]

    hw = M_HEADS * M_DK
    gates_ref[...] = _sigmoid(proj(_GATE0, _GATE1 - _GATE0)).astype(gates_ref.dtype)
    og_ref[...] = _sigmoid(proj(_MO0, hw)).astype(og_ref.dtype)
    qkv_ref[:, 0:hw] = proj(_MQ0, hw).astype(qkv_ref.dtype)
    qkv_ref[:, hw:2 * hw] = (proj(_MK0, hw) * (M_DK ** -0.5)).astype(qkv_ref.dtype)
    qkv_ref[:, 2 * hw:3 * hw] = proj(_MV0, hw).astype(qkv_ref.dtype)


def _inproj_call(x, mod, lw, mod_row0, rows_per_mod, rope, seq):
    n, d = x.shape
    tm = TOKEN_BLOCK
    hw = M_HEADS * M_DK

    def mod_idx(i):
        return (mod_row0 + (i * tm) // rows_per_mod, 0, 0)

    def rows(width):
        return pl.BlockSpec((tm, width), lambda i: (i, 0))

    in_specs = [
        rows(d),
        pl.BlockSpec((None, N_MOD, d), mod_idx),
        _resident((1, d)),
        _resident(lw["w_in"].shape[1:], lw["layer"]),
        _resident((d, _WIN_COLS - _AQ0), lw["layer"]),
        _resident((1, LANES)),
        _resident((1, A_Q_LORA)),
        _resident((A_Q_LORA, A_HEADS * LANES), lw["layer"]),
        _resident((1, A_KV_LORA)),
        _resident((1, LANES)),
        _resident((1, LANES)),
        _resident((1, LANES)),
    ]
    args = [x, mod, lw["norm1_g"], lw["w_in"], lw["w_in_tail"], lw["b_mgate"], lw["a_qlora_g"], lw["w_uq"],
            lw["a_kvlora_g"], lw["g_knorm_g"], lw["a_qnorm_g"], lw["g_qnorm_g"]]
    if rope is not None:
        blocks_per_seq = seq // tm
        in_specs += [pl.BlockSpec((tm, LANES), lambda i: (i % blocks_per_seq, 0))] * 4
        args += [rope[0][0], rope[0][1], rope[1][0], rope[1][1]]
    out_widths = [3 * d, 3 * hw, hw, LANES, A_HEADS * LANES, A_KV_LORA, LANES, G_HEADS * G_HD,
                  G_KV_HEADS * G_HD, G_KV_HEADS * G_HD]
    out_dtypes = [BF16, BF16, BF16, F32, BF16, F32, F32, BF16, F32, F32]
    return pl.pallas_call(
        functools.partial(_inproj_kernel, rotary=rope is not None),
        out_shape=[jax.ShapeDtypeStruct((n, w), t) for w, t in zip(out_widths, out_dtypes)],
        grid=(n // tm,),
        in_specs=in_specs,
        out_specs=[rows(w) for w in out_widths],
        compiler_params=_params(1),
        name="in_proj",
    )(*args)


def _mlstm_kernel(*refs, n_chunks, heads, has_init, emit_state):
    q_ref, k_ref, v_ref, og_ref, mg_ref, gn_ref = refs[:6]
    pos = 6
    if has_init:
        c0_ref, n0_ref, m0_ref = refs[pos:pos + 3]
        pos += 3
    y_ref = refs[pos]
    pos += 1
    if emit_state:
        cf_ref, nf_ref, mf_ref = refs[pos:pos + 3]
        pos += 3
    (pcol_scr, lfc_scr, grow_scr, prow_scr, lfr_scr, u_scr, nu_scr, gm_scr, tot_scr, cs_scr, ns_scr, ms_scr,
     c_scr, n_scr, m_scr) = refs[pos:]
    assert heads == M_HEADS
    n_gates = 4 * M_HEADS

    ln = MLSTM_CHUNK
    row = lax.broadcasted_iota(jnp.int32, (ln, ln), 0)
    col = lax.broadcasted_iota(jnp.int32, (ln, ln), 1)
    lower = col <= row
    upper = col >= row
    tril = jnp.where(lower, 1.0, 0.0).astype(BF16)
    triu = jnp.where(upper, 1.0, 0.0).astype(BF16)
    ones = jnp.ones((ln, ln), BF16)

    def gate_index(hh, d):
        return 2 * d * heads + hh, (2 * d + 1) * heads + hh

    def summaries(c, carry):
        r0 = pl.multiple_of(c * ln, ln)
        g = mg_ref[pl.ds(r0, ln), :]
        lf = _log_sigmoid(g) * LOG2E
        pcol_scr[pl.ds(r0, ln), :] = _dot01_left(tril, lf)
        lfc_scr[pl.ds(r0, ln), :] = lf
        g_rows = g.T[MG_LANE0:MG_LANE0 + n_gates, :]
        gr = g_rows * LOG2E
        lfr = _log_sigmoid(g_rows) * LOG2E
        pr = _dot01(lfr, triu)
        grow_scr[c] = gr
        prow_scr[c] = pr
        lfr_scr[c] = lfr
        for hh in range(heads):
            kb = k_ref[pl.ds(r0, ln), hh * LANES:(hh + 1) * LANES]
            k_t = kb.astype(F32).T
            v = v_ref[pl.ds(r0, ln), hh * LANES:(hh + 1) * LANES]
            for d in range(2):
                ji, jf = gate_index(hh, d)
                tot = pr[jf:jf + 1, ln - 1:ln]
                b_row = pr[jf:jf + 1, :] if d == 0 else tot - pr[jf:jf + 1, :] + lfr[jf:jf + 1, :]
                g_row = tot - b_row + gr[ji:ji + 1, :]
                gmax = jnp.max(g_row, axis=-1, keepdims=True)
                wg = jnp.exp2(g_row - gmax)
                u_scr[hh, d, c] = _dot((k_t * wg).astype(BF16), v)
                nu_scr[hh, d, c] = _dot(jnp.broadcast_to(wg, (SUBLANES, ln)).astype(BF16), kb)[0:1, :]
                gm_scr[hh, d, c] = jnp.broadcast_to(gmax, (1, ln))
                tot_scr[hh, d, c] = jnp.broadcast_to(tot, (1, ln))
        return carry

    lax.fori_loop(0, n_chunks, summaries, 0, unroll=min(2, n_chunks))

    for hh in range(heads):
        for d in range(2):
            if has_init:
                c_scr[hh, d] = c0_ref[d, hh]
                n_scr[hh, d] = n0_ref[d, hh]
                m_scr[hh, d] = m0_ref[d, hh] * LOG2E
            else:
                c_scr[hh, d] = jnp.zeros((M_DK, M_DV), F32)
                n_scr[hh, d] = jnp.zeros((1, M_DK), F32)
                m_scr[hh, d] = jnp.zeros((1, ln), F32)

    def scan(j, carry):
        for hh in range(heads):
            for d in range(2):
                c = j if d == 0 else n_chunks - 1 - j
                cst = c_scr[hh, d]
                nst = n_scr[hh, d]
                mst = m_scr[hh, d]
                cs_scr[hh, d, c] = cst.astype(BF16)
                ns_scr[hh, d, c] = nst
                ms_scr[hh, d, c] = mst
                gmax = gm_scr[hh, d, c]
                total = tot_scr[hh, d, c] + mst
                m_new = jnp.maximum(total, gmax)
                decay = jnp.exp2(total - m_new)
                scale = jnp.exp2(gmax - m_new)
                c_scr[hh, d] = cst * decay + u_scr[hh, d, c] * scale
                n_scr[hh, d] = nst * decay + nu_scr[hh, d, c] * scale
                m_scr[hh, d] = m_new
        return carry

    lax.fori_loop(0, n_chunks, scan, 0)

    def readout(c, carry):
        r0 = pl.multiple_of(c * ln, ln)
        pc = pcol_scr[pl.ds(r0, ln), :]
        lf = lfc_scr[pl.ds(r0, ln), :]
        gr = grow_scr[c]
        pr = prow_scr[c]
        lfr = lfr_scr[c]
        tot_all = pc[ln - 1:ln, :]
        for hh in range(heads):
            q = q_ref[pl.ds(r0, ln), hh * LANES:(hh + 1) * LANES]
            k = k_ref[pl.ds(r0, ln), hh * LANES:(hh + 1) * LANES]
            v = v_ref[pl.ds(r0, ln), hh * LANES:(hh + 1) * LANES]
            qk = _dot_nt(q, k)
            hsum = None
            for d in range(2):
                ji, jf = gate_index(hh, d)
                lane_f = MG_LANE0 + jf
                if d == 0:
                    b_col = pc[:, lane_f:lane_f + 1]
                    r_row = gr[ji:ji + 1, :] - pr[jf:jf + 1, :]
                    mask = lower
                else:
                    tot = tot_all[:, lane_f:lane_f + 1]
                    b_col = tot - pc[:, lane_f:lane_f + 1] + lf[:, lane_f:lane_f + 1]
                    r_row = gr[ji:ji + 1, :] - (tot - pr[jf:jf + 1, :] + lfr[jf:jf + 1, :])
                    mask = upper
                b_colb = jnp.broadcast_to(b_col, (ln, ln))
                d_log = jnp.where(mask, b_colb + r_row, -jnp.inf)
                dmax = jnp.broadcast_to(jnp.max(d_log, axis=-1, keepdims=True), (ln, ln))
                sb = (qk * jnp.exp2(d_log - dmax)).astype(BF16)
                num_loc = _dot(sb, v)
                den_loc = _dot(sb, ones)
                qc = _dot(q, cs_scr[hh, d, c])
                qn = _dot_nt(q, jnp.broadcast_to(ns_scr[hh, d, c], (ln, M_DK)).astype(BF16))
                inter = b_colb + ms_scr[hh, d, c]
                m_t = jnp.maximum(inter, dmax)
                w_inter = jnp.exp2(inter - m_t)
                w_loc = jnp.exp2(dmax - m_t)
                num = qc * w_inter + num_loc * w_loc
                den = qn * w_inter + den_loc * w_loc
                hc = num / jnp.maximum(jnp.abs(den), jnp.exp2(-m_t))
                hsum = hc if hsum is None else hsum + hc
            y_ref[pl.ds(r0, ln), hh * LANES:(hh + 1) * LANES] = (
                _rms(hsum, M_DV) * gn_ref[...] * og_ref[pl.ds(r0, ln), hh * LANES:(hh + 1) * LANES]
            ).astype(y_ref.dtype)
        return carry

    lax.fori_loop(0, n_chunks, readout, 0, unroll=min(2, n_chunks))

    if emit_state:
        for hh in range(heads):
            for d in range(2):
                cf_ref[d, hh] = c_scr[hh, d]
                nf_ref[d, hh] = n_scr[hh, d]
                mf_ref[d, hh] = m_scr[hh, d] * (1.0 / LOG2E)


def _mlstm_call(qkv, og, mg, lw, batch, seq, init, emit_state):
    n = batch * seq
    ln = MLSTM_CHUNK
    nc = seq // ln
    h = M_HEADS
    hs = MLSTM_HEADS_PER_STEP
    ng = h // hs

    def tok(colblock):
        return pl.BlockSpec((seq, hs * LANES), lambda b, hg: (b, colblock(hg)))

    in_specs = [
        tok(lambda hg: hg), tok(lambda hg: ng + hg), tok(lambda hg: 2 * ng + hg),
        tok(lambda hg: hg),
        pl.BlockSpec((seq, LANES), lambda b, hg: (b, 0)),
        pl.BlockSpec((1, M_DV), lambda b, hg: (0, 0)),
    ]
    args = [qkv, qkv, qkv, og, mg, lw["m_norm_g"]]
    state_specs = [
        pl.BlockSpec((None, 2, hs, M_DK, M_DV), lambda b, hg: (b, 0, hg, 0, 0)),
        pl.BlockSpec((None, 2, hs, 1, M_DK), lambda b, hg: (b, 0, hg, 0, 0)),
        pl.BlockSpec((None, 2, hs, 1, ln), lambda b, hg: (b, 0, hg, 0, 0)),
    ]
    if init is not None:
        c0, n0, m0 = init
        in_specs += state_specs
        args += [c0, n0.reshape(batch, 2, h, 1, M_DK),
                 jnp.broadcast_to(m0[..., None, None], (batch, 2, h, 1, ln))]
    out_shape = [jax.ShapeDtypeStruct((n, h * M_DV), BF16)]
    out_specs = [tok(lambda hg: hg)]
    if emit_state:
        out_shape += [
            jax.ShapeDtypeStruct((batch, 2, h, M_DK, M_DV), F32),
            jax.ShapeDtypeStruct((batch, 2, h, 1, M_DK), F32),
            jax.ShapeDtypeStruct((batch, 2, h, 1, ln), F32),
        ]
        out_specs += state_specs
    per = (hs, 2, nc)
    outs = pl.pallas_call(
        functools.partial(_mlstm_kernel, n_chunks=nc, heads=hs, has_init=init is not None,
                          emit_state=emit_state),
        out_shape=out_shape,
        grid=(batch, ng),
        in_specs=in_specs,
        out_specs=out_specs,
        scratch_shapes=[
            pltpu.VMEM((seq, LANES), F32), pltpu.VMEM((seq, LANES), F32),
            pltpu.VMEM((nc, 4 * h, ln), F32), pltpu.VMEM((nc, 4 * h, ln), F32), pltpu.VMEM((nc, 4 * h, ln), F32),
            pltpu.VMEM(per + (M_DK, M_DV), F32), pltpu.VMEM(per + (1, M_DK), F32),
            pltpu.VMEM(per + (1, ln), F32), pltpu.VMEM(per + (1, ln), F32),
            pltpu.VMEM(per + (M_DK, M_DV), BF16), pltpu.VMEM(per + (1, M_DK), F32), pltpu.VMEM(per + (1, ln), F32),
            pltpu.VMEM((hs, 2, M_DK, M_DV), F32), pltpu.VMEM((hs, 2, 1, M_DK), F32), pltpu.VMEM((hs, 2, 1, ln), F32),
        ],
        compiler_params=_params(2),
        name="mlstm",
    )(*args)
    if emit_state:
        ym, cf, nf, mf = outs
        return ym, (cf, nf[:, :, :, 0, :], mf[:, :, :, 0, 0])
    return outs[0], None


HEAD_V = 64
VT_ROWS = 80


def _ones_row(v_t):
    return jnp.where(lax.broadcasted_iota(jnp.int32, v_t.shape, 0) == HEAD_V, 1.0, v_t)


def _flash_heads(qs, keys_of, vt_chunk, n_keys, o_ref):
    def values(e, s_t):
        m = acc = None
        for c0 in range(0, n_keys, KEY_BLOCK):
            c1 = min(c0 + KEY_BLOCK, n_keys)
            s = s_t[c0:c1, :]
            mc = jnp.max(s, axis=0, keepdims=True)
            if c0 == 0:
                m = mc
                acc = _dot(vt_chunk(e, c0, c1), jnp.exp2(s - mc).astype(BF16))
            else:
                m_new = jnp.maximum(m, mc)
                acc = acc * jnp.exp2(m - m_new) + _dot(vt_chunk(e, c0, c1), jnp.exp2(s - m_new).astype(BF16))
                m = m_new
        return acc[0:HEAD_V, :] / acc[HEAD_V:HEAD_V + 1, :]

    if n_keys <= KEY_BLOCK:
        s_all = [_dot_nt(keys_of(e), qb) for e, qb in enumerate(qs)]
        outs = [values(e, s_t) for e, s_t in enumerate(s_all)]
    else:
        outs = []
        s_next = _dot_nt(keys_of(0), qs[0])
        for e in range(len(qs)):
            s_t = s_next
            if e + 1 < len(qs):
                s_next = _dot_nt(keys_of(e + 1), qs[e + 1])
            outs.append(values(e, s_t))
    for e in range(0, len(outs), 2):
        pair = jnp.concatenate([outs[e], outs[e + 1]], axis=0)
        o_ref[:, (e // 2) * LANES:(e // 2 + 1) * LANES] = pair.T.astype(o_ref.dtype)


def _mla_kernel(*refs, past, rotary, heads):
    q_ref, ckv_ref, akr_ref = refs[:3]
    pos = 3
    if past:
        ckvc_ref, krc_ref = refs[pos:pos + 2]
        pos += 2
    wk_ref, wvt_ref, kg_ref = refs[pos:pos + 3]
    pos += 3
    if rotary:
        ck_ref, sk_ref = refs[pos:pos + 2]
        pos += 2
    o_ref, kbuf, vbuf = refs[pos:]
    seq = ckv_ref.shape[0]

    @pl.when(pl.program_id(2) == 0)
    def _():
        for e in range(heads):
            wk_h = wk_ref[:, e * LANES:(e + 1) * LANES]
            wvt_h = wvt_ref[e * VT_ROWS:(e + 1) * VT_ROWS, :]

            def keys(ckv, kr):
                cb = ckv.astype(BF16)
                kcat = _dot(cb, wk_h) + kr
                return _ones_row(_dot_nt(wvt_h, cb)), _rms(kcat, A_QK) * kg_ref[...]

            v_t, kn = keys(ckv_ref[...], akr_ref[...])
            if rotary:
                kn = _rope(kn, ck_ref[...], sk_ref[...], A_ROPE // 2, LANES)
            kbuf[e, past:past + seq, :] = kn.astype(BF16)
            vbuf[e, :, past:past + seq] = v_t.astype(BF16)
            if past:
                v_tc, knc = keys(ckvc_ref[...], krc_ref[...])
                kbuf[e, 0:past, :] = knc.astype(BF16)
                vbuf[e, :, 0:past] = v_tc.astype(BF16)

    qs = [q_ref[:, e * LANES:(e + 1) * LANES] for e in range(heads)]
    _flash_heads(qs, lambda e: kbuf[e], lambda e, c0, c1: vbuf[e, :, c0:c1], past + seq, o_ref)


def _mla_call(qa, ckv, akr, lw, batch, seq, cache, rope_tabs):
    n = batch * seq
    tq = min(Q_BLOCK, seq)
    nq = seq // tq
    past = 0 if cache is None else cache[0].shape[1]
    heads = MLA_HEADS_PER_STEP if nq > 1 else A_HEADS
    in_specs = [
        pl.BlockSpec((tq, heads * LANES), lambda b, hg, qi: (b * nq + qi, hg)),
        pl.BlockSpec((seq, LANES), lambda b, hg, qi: (b, 0)),
        pl.BlockSpec((seq, LANES), lambda b, hg, qi: (b, 0)),
    ]
    args = [qa, ckv, akr]
    if past:
        in_specs += [pl.BlockSpec((None, past, LANES), lambda b, hg, qi: (b, 0, 0))] * 2
        args += [cache[0], cache[1]]
    layer = lw["layer"]
    in_specs += [
        pl.BlockSpec((None, A_KV_LORA, heads * LANES), lambda b, hg, qi: (layer, 0, hg)),
        pl.BlockSpec((None, heads * VT_ROWS, A_KV_LORA), lambda b, hg, qi: (layer, hg, 0)),
        pl.BlockSpec((1, LANES), lambda b, hg, qi: (0, 0)),
    ]
    args += [lw["w_uk"], lw["w_uvt"], lw["a_knorm_g"]]
    if rope_tabs is not None:
        cos, sin = rope_tabs
        in_specs += [pl.BlockSpec((seq, LANES), lambda b, hg, qi: (0, 0))] * 2
        args += [cos, sin]
    return pl.pallas_call(
        functools.partial(_mla_kernel, past=past, rotary=rope_tabs is not None, heads=heads),
        out_shape=jax.ShapeDtypeStruct((n, A_HEADS * A_V), BF16),
        grid=(batch, A_HEADS // heads, nq),
        in_specs=in_specs,
        out_specs=pl.BlockSpec((tq, heads * A_V), lambda b, hg, qi: (b * nq + qi, hg)),
        scratch_shapes=[pltpu.VMEM((heads, past + seq, LANES), BF16),
                        pltpu.VMEM((heads, VT_ROWS, past + seq), BF16)],
        compiler_params=_params(3),
        name="mla_attn",
    )(*args)


def _gqa_kernel(*refs, past, groups):
    q_ref, k_ref, v_ref = refs[:3]
    pos = 3
    if past:
        kc_ref, vc_ref = refs[pos:pos + 2]
        pos += 2
    o_ref, kbuf, vbuf = refs[pos:]
    seq = k_ref.shape[0]
    qw = G_GROUP * G_HD

    @pl.when(pl.program_id(2) == 0)
    def _():
        for gi in range(groups):
            grp = pl.program_id(1) * groups + gi

            r = lax.broadcasted_iota(jnp.int32, (VT_ROWS, LANES), 0)
            c = lax.broadcasted_iota(jnp.int32, (VT_ROWS, LANES), 1)
            pick = jnp.where((c == r + grp * G_HD) & (r < G_HD), 1.0, 0.0).astype(BF16)

            def fill(r0, k, v):
                low = _lane(k.shape) < G_HD
                k_lo = jnp.where(grp == 0, k, pltpu.roll(k, G_HD, 1))
                rows = k.shape[0]
                kbuf[gi, r0:r0 + rows, :] = jnp.where(low, k_lo, pltpu.roll(k_lo, G_HD, 1)).astype(BF16)
                vbuf[gi, :, r0:r0 + rows] = _ones_row(_dot_nt(pick, v.astype(BF16))).astype(BF16)

            fill(past, k_ref[...], v_ref[...])
            if past:
                fill(0, kc_ref[...], vc_ref[...])

    qs = []
    for gi in range(groups):
        for j in range(G_GROUP):
            col = q_ref[:, gi * qw + (j // 2) * LANES:gi * qw + (j // 2 + 1) * LANES]
            keep = (_lane(col.shape) < G_HD) == (j % 2 == 0)
            qs.append(jnp.where(keep, col, jnp.zeros_like(col)))
    _flash_heads(qs, lambda e: kbuf[e // G_GROUP], lambda e, c0, c1: vbuf[e // G_GROUP, :, c0:c1],
                 past + seq, o_ref)


def _gqa_call(gq, gk, gv, batch, seq, cache):
    n = batch * seq
    tq = min(Q_BLOCK, seq)
    nq = seq // tq
    past = 0 if cache is None else cache[0].shape[1]
    groups = GQA_GROUPS_PER_STEP if nq > 1 else G_KV_HEADS
    qw = G_GROUP * G_HD
    kvw = G_KV_HEADS * G_HD
    in_specs = [
        pl.BlockSpec((tq, groups * qw), lambda b, g, qi: (b * nq + qi, g)),
        pl.BlockSpec((seq, kvw), lambda b, g, qi: (b, 0)),
        pl.BlockSpec((seq, kvw), lambda b, g, qi: (b, 0)),
    ]
    args = [gq, gk, gv]
    if past:
        in_specs += [pl.BlockSpec((None, past, kvw), lambda b, g, qi: (b, 0, 0))] * 2
        args += [cache[0], cache[1]]
    return pl.pallas_call(
        functools.partial(_gqa_kernel, past=past, groups=groups),
        out_shape=jax.ShapeDtypeStruct((n, G_HEADS * G_HD), BF16),
        grid=(batch, G_KV_HEADS // groups, nq),
        in_specs=in_specs,
        out_specs=pl.BlockSpec((tq, groups * qw), lambda b, g, qi: (b * nq + qi, g)),
        scratch_shapes=[pltpu.VMEM((groups, past + seq, LANES), BF16),
                        pltpu.VMEM((groups, VT_ROWS, past + seq), BF16)],
        compiler_params=_params(3),
        name="gqa_attn",
    )(*args)


def _merge_ffn_kernel(x_ref, ym_ref, ya_ref, yg_ref, gates_ref, mod_ref, g2_ref, wb_ref, wo_ref, wfi_ref,
                      wfo_ref, o_ref, *, ff_chunks):
    x = x_ref[...]
    d = x.shape[-1]
    mixed = None
    for i, y_ref in enumerate((ym_ref, ya_ref, yg_ref)):
        br = _dot(y_ref[...].astype(BF16), wb_ref[i]) * gates_ref[:, i * d:(i + 1) * d]
        mixed = br if mixed is None else mixed + br
    gt1 = mod_ref[2:3, :]
    x1 = x + gt1 * _dot(mixed.astype(BF16), wo_ref[...])

    sh2 = mod_ref[3:4, :]
    sc2 = mod_ref[4:5, :]
    gt2 = mod_ref[5:6, :]
    h2 = ((_rms(x1, d) * g2_ref[...]) * (1.0 + sc2) + sh2).astype(BF16)
    d_ff = wfo_ref.shape[0]
    acc = None
    for c0, c1 in ff_chunks:
        ug = _dot(h2, wfi_ref[:, c0:c1])
        uv = _dot(h2, wfi_ref[:, d_ff + c0:d_ff + c1])
        act = (ug * _sigmoid(ug) * uv).astype(BF16)
        part = _dot(act, wfo_ref[c0:c1, :])
        acc = part if acc is None else acc + part
    o_ref[...] = x1 + gt2 * acc


def _merge_ffn_call(x, ym, ya, yg, gates, mod, lw, mod_row0, rows_per_mod):
    n, d = x.shape
    tm = TOKEN_BLOCK
    d_ff = lw["w_ffn_out"].shape[1]
    layer = lw["layer"]
    split = -(-(d_ff // 2) // MXU_COLS) * MXU_COLS
    ff_chunks = ((0, split), (split, d_ff)) if 0 < split < d_ff else ((0, d_ff),)

    def mod_idx(i):
        return (mod_row0 + (i * tm) // rows_per_mod, 0, 0)

    def rows(width):
        return pl.BlockSpec((tm, width), lambda i: (i, 0))

    return pl.pallas_call(
        functools.partial(_merge_ffn_kernel, ff_chunks=ff_chunks),
        out_shape=jax.ShapeDtypeStruct((n, d), F32),
        grid=(n // tm,),
        in_specs=[
            rows(d), rows(BRANCH_WIDTH), rows(BRANCH_WIDTH), rows(BRANCH_WIDTH), rows(N_BRANCH * d),
            pl.BlockSpec((None, N_MOD, d), mod_idx),
            _resident((1, d)),
            _resident((N_BRANCH, BRANCH_WIDTH, d), layer),
            _resident((d, d), layer),
            _resident((d, 2 * d_ff), layer),
            _resident((d_ff, d), layer),
        ],
        out_specs=rows(d),
        compiler_params=_params(1),
        name="merge_ffn",
    )(x, ym, ya, yg, gates, mod, lw["norm2_g"], lw["w_branch"], lw["w_out"], lw["w_ffn_in"], lw["w_ffn_out"])


def _pad_cols(w, width):
    return jnp.pad(w, ((0, 0), (0, width - w.shape[1])))


def _mla_q_layout(a):
    lead = a.shape[:-1]
    a = a.reshape(lead + (A_HEADS, A_QK))
    z = jnp.zeros(lead + (A_HEADS, LANES - A_QK), a.dtype)
    return jnp.concatenate([a[..., A_NOPE:], z, a[..., :A_NOPE]], axis=-1).reshape(lead + (A_HEADS * LANES,))


def _stacked_weights(w_in, w_uq, w_ukv, w_branch, w_out, w_ffn_in, w_ffn_out):
    depth, d, _ = w_in.shape
    hw = M_HEADS * M_DK
    sizes = (N_BRANCH * d, hw, hw, hw, hw, 4 * M_HEADS, A_Q_LORA, A_KV_LORA, A_ROPE, G_HEADS * G_HD,
             G_KV_HEADS * G_HD, G_KV_HEADS * G_HD)
    edges = np.concatenate([[0], np.cumsum(sizes)]).tolist()
    (wmg, waq, wakv, wakr, wgq, wgk, wgv) = [w_in[:, :, a:b] for a, b in zip(edges[5:-1], edges[6:])]
    misc_pad = jnp.zeros((depth, d, LANES - A_ROPE - 4 * M_HEADS), w_in.dtype)
    tail = jnp.concatenate([waq, wakv, wakr, wmg, misc_pad, wgq, wgk, wgv], axis=2).astype(BF16)
    ukv = w_ukv.reshape(depth, A_KV_LORA, A_HEADS, A_NOPE + A_V)
    uk = jnp.pad(ukv[..., :A_NOPE], ((0, 0), (0, 0), (0, 0), (LANES - A_NOPE, 0)))
    uvt = jnp.pad(ukv[..., A_NOPE:].transpose(0, 2, 3, 1), ((0, 0), (0, 0), (0, VT_ROWS - A_V), (0, 0)))
    return dict(
        w_in=w_in.astype(BF16),
        w_in_tail=tail,
        w_uq=_mla_q_layout(w_uq).astype(BF16),
        w_uk=uk.reshape(depth, A_KV_LORA, A_HEADS * LANES).astype(BF16),
        w_uvt=uvt.reshape(depth, A_HEADS * VT_ROWS, A_KV_LORA).astype(BF16),
        w_branch=w_branch.astype(BF16),
        w_out=w_out.astype(BF16),
        w_ffn_in=w_ffn_in.astype(BF16),
        w_ffn_out=w_ffn_out.astype(BF16),
    )


def _layer_vectors(l, b_mgate, norm1_g, m_norm_g, a_qlora_g, a_kvlora_g, a_qnorm_g, a_knorm_g, g_qnorm_g,
                   g_knorm_g, norm2_g):
    return dict(
        layer=l,
        b_mgate=jnp.pad(b_mgate[l][None, :], ((0, 0), (MG_LANE0, LANES - MG_LANE0 - 4 * M_HEADS))),
        norm1_g=norm1_g[l][None, :],
        m_norm_g=m_norm_g[l][None, :],
        a_qlora_g=a_qlora_g[l][None, :],
        a_kvlora_g=a_kvlora_g[l][None, :],
        a_qnorm_g=_mla_q_layout(jnp.tile(a_qnorm_g[l], A_HEADS)[None, :])[:, :LANES],
        a_knorm_g=_mla_q_layout(jnp.tile(a_knorm_g[l], A_HEADS)[None, :])[:, :LANES],
        g_qnorm_g=jnp.tile(g_qnorm_g[l], LANES // G_HD)[None, :],
        g_knorm_g=jnp.tile(g_knorm_g[l], G_KV_HEADS)[None, :],
        norm2_g=norm2_g[l][None, :],
    )


def _axial_angles(seq, rot_dim):
    n_freq = rot_dim // 4
    freqs = ROPE_BASE ** (-jnp.arange(n_freq, dtype=F32) / n_freq)
    t = jnp.arange(seq)
    row = (t // GRID_W).astype(F32)
    col = (t % GRID_W).astype(F32)
    return jnp.concatenate([row[:, None] * freqs, col[:, None] * freqs], axis=-1)


def _rope_tables(seq):
    ang = _axial_angles(seq, A_ROPE)
    one = jnp.ones((seq, LANES - A_ROPE), F32)
    mla_cos = jnp.concatenate([jnp.cos(ang), jnp.cos(ang), one], axis=-1)
    mla_sin = jnp.concatenate([-jnp.sin(ang), jnp.sin(ang), 0.0 * one], axis=-1)
    ang = _axial_angles(seq, G_HD)
    cos = jnp.concatenate([jnp.cos(ang), jnp.cos(ang)], axis=-1)
    sin = jnp.concatenate([-jnp.sin(ang), jnp.sin(ang)], axis=-1)
    gqa = (jnp.tile(cos, (1, LANES // G_HD)), jnp.tile(sin, (1, LANES // G_HD)))
    return (mla_cos, mla_sin), gqa


def _layer(x, mod, lw, batch, seq, mod_row0, rows_per_mod, ctx, rope):
    (gates, qkv, og, mg, qa, ckv, akr, gq, gk, gv) = _inproj_call(x, mod, lw, mod_row0, rows_per_mod, rope, seq)
    if ctx is None:
        ym, state = _mlstm_call(qkv, og, mg, lw, batch, seq, None, True)
        ya = _mla_call(qa, ckv, akr, lw, batch, seq, None, None)
        yg = _gqa_call(gq, gk, gv, batch, seq, None)
        new_ctx = dict(state=state, ckv=ckv, kr=akr[:, :A_ROPE], gk=gk, gv=gv)
    else:
        ym, _ = _mlstm_call(qkv, og, mg, lw, batch, seq, ctx["mlstm"], False)
        ya = _mla_call(qa, ckv, akr, lw, batch, seq, ctx["mla"], rope[0])
        yg = _gqa_call(gq, gk, gv, batch, seq, ctx["gqa"])
        new_ctx = None
    x = _merge_ffn_call(x, ym, ya, yg, gates, mod, lw, mod_row0, rows_per_mod)
    return x, new_ctx


def kernel(x_prompt, x_sample, state_mlstm_C, state_mlstm_n, state_mlstm_m, cache_mla_ckv, cache_mla_krope,
           cache_gqa_k, cache_gqa_v, c, c_ctx, w_mod, b_mod, norm1_g, w_in, b_mgate, m_norm_g, a_qlora_g,
           a_kvlora_g, w_uq, w_ukv, a_qnorm_g, a_knorm_g, g_qnorm_g, g_knorm_g, w_branch, w_out, norm2_g,
           w_ffn_in, w_ffn_out):
    batch, seq, d = x_prompt.shape
    dbatch, dseq, _ = x_sample.shape
    depth = w_in.shape[0]
    past = cache_mla_ckv.shape[2]

    n_rows = -(-(1 + dbatch) // SUBLANES) * SUBLANES
    cond = jnp.concatenate([c_ctx[None, :], c, jnp.zeros((n_rows - 1 - dbatch, d), F32)], axis=0)
    mod_all = _mod_call(cond, w_mod, b_mod).reshape(depth, n_rows, N_MOD, d)

    rope = _rope_tables(dseq)
    xp = x_prompt.reshape(batch * seq, d)
    xs = x_sample.reshape(dbatch * dseq, d)
    ctx_layers = []
    stacked = _stacked_weights(w_in, w_uq, w_ukv, w_branch, w_out, w_ffn_in, w_ffn_out)
    for l in range(depth):
        lw = dict(stacked, **_layer_vectors(l, b_mgate, norm1_g, m_norm_g, a_qlora_g, a_kvlora_g, a_qnorm_g,
                                            a_knorm_g, g_qnorm_g, g_knorm_g, norm2_g))
        xp, st = _layer(xp, mod_all[l], lw, batch, seq, 0, batch * seq, None, None)
        ctx_layers.append(st)
        ctx = dict(
            mlstm=(state_mlstm_C[:, l], state_mlstm_n[:, l], state_mlstm_m[:, l]),
            mla=(cache_mla_ckv[:, l], _pad_cols(cache_mla_krope[:, l].reshape(dbatch * past, A_ROPE), LANES)
                 .reshape(dbatch, past, LANES)),
            gqa=(cache_gqa_k[:, l].reshape(dbatch, past, G_KV_HEADS * G_HD),
                 cache_gqa_v[:, l].reshape(dbatch, past, G_KV_HEADS * G_HD)),
        )
        xs, _ = _layer(xs, mod_all[l], lw, dbatch, dseq, 1, dseq, ctx, rope)

    def stack(fn):
        return jnp.stack([fn(s) for s in ctx_layers], axis=1)

    new_c = stack(lambda s: s["state"][0])
    new_n = stack(lambda s: s["state"][1])
    new_m = stack(lambda s: s["state"][2])
    new_ckv = stack(lambda s: s["ckv"].reshape(batch, seq, A_KV_LORA))
    new_kr = stack(lambda s: s["kr"].reshape(batch, seq, A_ROPE))
    new_gk = stack(lambda s: s["gk"].reshape(batch, seq, G_KV_HEADS, G_HD))
    new_gv = stack(lambda s: s["gv"].reshape(batch, seq, G_KV_HEADS, G_HD))
    return (xp.reshape(batch, seq, d), xs.reshape(dbatch, dseq, d), new_c, new_n, new_m, new_ckv, new_kr,
            new_gk, new_gv)
```

```python
import functools

import numpy as np
import jax
import jax.numpy as jnp
from jax import lax
from jax.experimental import pallas as pl
from jax.experimental.pallas import tpu as pltpu

F32 = jnp.float32
BF16 = jnp.bfloat16

LANES = 128
SUBLANES = 8
VMEM_LIMIT_BYTES = 56 * 1024 * 1024

EPS = 1e-6
ROPE_BASE = 10000.0
GRID_W = 64

M_HEADS = 4
M_DK = 128
M_DV = 128
A_HEADS = 8
A_NOPE = 64
A_ROPE = 32
A_QK = A_NOPE + A_ROPE
A_V = 64
A_Q_LORA = 256
A_KV_LORA = 128
G_HEADS = 8
G_KV_HEADS = 2
G_GROUP = G_HEADS // G_KV_HEADS
G_HD = 64
N_BRANCH = 3
BRANCH_WIDTH = 512
N_MOD = 6

TOKEN_BLOCK = 512
Q_BLOCK = 256
KEY_BLOCK = 256
MLA_HEADS_PER_STEP = 8
GQA_GROUPS_PER_STEP = 2
MLSTM_CHUNK = 128
MLSTM_HEADS_PER_STEP = 4
LOG2E = 1.4426950408889634


def _params(n_axes):
    return pltpu.CompilerParams(dimension_semantics=("arbitrary",) * n_axes,
                                vmem_limit_bytes=VMEM_LIMIT_BYTES)


def _resident(shape, layer=None):
    nd = len(shape)
    if layer is None:
        return pl.BlockSpec(shape, lambda *_: (0,) * nd, pipeline_mode=pl.Buffered(1))
    return pl.BlockSpec((None,) + tuple(shape), lambda *_: (layer,) + (0,) * nd, pipeline_mode=pl.Buffered(1))


def _lane(shape, axis=None):
    return lax.broadcasted_iota(jnp.int32, shape, len(shape) - 1 if axis is None else axis)


def _dot(a, b):
    return jnp.dot(a, b, preferred_element_type=F32)


def _dot_nt(a, b):
    return lax.dot_general(a, b, (((1,), (1,)), ((), ())), preferred_element_type=F32)


def _split3(a):
    hi = a.astype(BF16)
    r1 = a - hi.astype(F32)
    mid = r1.astype(BF16)
    lo = (r1 - mid.astype(F32)).astype(BF16)
    return hi, mid, lo


def _dot01(a, m01):
    hi, mid, lo = _split3(a)
    return _dot(hi, m01) + _dot(mid, m01) + _dot(lo, m01)


def _dot01_left(m01, a):
    hi, mid, lo = _split3(a)
    return _dot(m01, hi) + _dot(m01, mid) + _dot(m01, lo)


def _sigmoid(x):
    return 0.5 * jnp.tanh(0.5 * x) + 0.5


def _log_sigmoid(x):
    return jnp.minimum(x, 0.0) - jnp.log(1.0 + jnp.exp(-jnp.abs(x)))


def _rms(x, width):
    ms = jnp.sum(x * x, axis=-1, keepdims=True) * (1.0 / width)
    return x * lax.rsqrt(ms + EPS)


def _rope(x, cos, sin_signed, half, period):
    n = x.shape[-1]
    first = (_lane(x.shape) % period) < half
    swapped = jnp.where(first, pltpu.roll(x, n - half, x.ndim - 1), pltpu.roll(x, half, x.ndim - 1))
    return x * cos + swapped * sin_signed


def _mod_kernel(c_ref, w_ref, b_ref, o_ref):
    c = c_ref[...]
    a = c * _sigmoid(c)
    o_ref[...] = _dot01(a, w_ref[...].astype(BF16)) + b_ref[...]


def _mod_call(cond, w_mod, b_mod):
    depth, d, n = w_mod.shape
    rows = cond.shape[0]
    tn = 1536
    return pl.pallas_call(
        _mod_kernel,
        out_shape=jax.ShapeDtypeStruct((depth, rows, n), F32),
        grid=(depth, n // tn),
        in_specs=[
            pl.BlockSpec((rows, d), lambda l, j: (0, 0)),
            pl.BlockSpec((None, d, tn), lambda l, j: (l, 0, j)),
            pl.BlockSpec((None, 1, tn), lambda l, j: (l, 0, j)),
        ],
        out_specs=pl.BlockSpec((None, rows, tn), lambda l, j: (l, 0, j)),
        compiler_params=_params(2),
        name="adaln_mod",
    )(cond, w_mod, b_mod.reshape(depth, 1, n))


MXU_COLS = 256
_GATE0, _GATE1 = 0, 3072
_MQ0 = 3072
_MK0 = 3584
_MV0 = 4096
_MO0 = 4608
_AQ0 = 5120
_AKV0 = 5376
_MISC0 = 5504
_GQ0 = 5632
_GK0 = 6144
_GV0 = 6272
_WIN_COLS = 6400
MG_LANE0 = A_ROPE


def _head_pair_ms(x):
    low = (_lane(x.shape) % LANES) < G_HD
    sq = x * x
    cols = []
    for c0 in range(0, x.shape[-1], LANES):
        s = sq[:, c0:c0 + LANES]
        lo = jnp.sum(jnp.where(low[:, c0:c0 + LANES], s, 0.0), axis=-1, keepdims=True)
        hi = jnp.sum(jnp.where(low[:, c0:c0 + LANES], 0.0, s), axis=-1, keepdims=True)
        cols.append(jnp.where(low[:, c0:c0 + LANES], lo, hi))
    ms = cols[0] if len(cols) == 1 else jnp.concatenate(cols, axis=-1)
    return ms * (1.0 / G_HD)


def _inproj_kernel(*refs, rotary):
    (x_ref, mod_ref, g1_ref, w_ref, wt_ref, bmg_ref, gql_ref, wuq_ref, gkvl_ref, gkn_ref, aqn_ref,
     gqn_ref) = refs[:12]
    pos = 12
    if rotary:
        ca_ref, sa_ref, cg_ref, sg_ref = refs[pos:pos + 4]
        pos += 4
    (gates_ref, qkv_ref, og_ref, mg_ref, qa_ref, ckv_ref, akr_ref, gq_ref, gk_ref, gv_ref) = refs[pos:]
    x = x_ref[...]
    d = x.shape[-1]
    sh1 = mod_ref[0:1, :]
    sc1 = mod_ref[1:2, :]
    h = (_rms(x, d) * g1_ref[...]) * (1.0 + sc1) + sh1
    hb = h.astype(BF16)

    def proj(c0, width):
        if c0 < _AQ0:
            return _dot(hb, w_ref[:, c0:c0 + width])
        return _dot(hb, wt_ref[:, c0 - _AQ0:c0 - _AQ0 + width])

    aq = _rms(proj(_AQ0, A_Q_LORA), A_Q_LORA) * gql_ref[...]
    qa = _dot(aq.astype(BF16), wuq_ref[...])
    for hd in range(A_HEADS):
        qh = _rms(qa[:, hd * LANES:(hd + 1) * LANES], A_QK) * aqn_ref[...]
        if rotary:
            qh = _rope(qh, ca_ref[...], sa_ref[...], A_ROPE // 2, LANES)
        qa_ref[:, hd * LANES:(hd + 1) * LANES] = (qh * (A_QK ** -0.5 * LOG2E)).astype(qa_ref.dtype)
    akv_misc = proj(_AKV0, A_KV_LORA + LANES)
    ckv_ref[...] = _rms(akv_misc[:, :A_KV_LORA], A_KV_LORA) * gkvl_ref[...]
    misc = akv_misc[:, A_KV_LORA:]
    akr_ref[...] = jnp.where(_lane(misc.shape) < A_ROPE, misc, 0.0)
    mg_ref[...] = misc + bmg_ref[...]

    gq_all = proj(_GQ0, G_HEADS * G_HD)
    for c0 in range(0, G_HEADS * G_HD, LANES):
        gq = gq_all[:, c0:c0 + LANES]
        gq = gq * lax.rsqrt(_head_pair_ms(gq) + EPS) * gqn_ref[...]
        if rotary:
            gq = _rope(gq, cg_ref[...], sg_ref[...], G_HD // 2, G_HD)
        gq_ref[:, c0:c0 + LANES] = (gq * (G_HD ** -0.5 * LOG2E)).astype(gq_ref.dtype)
    kvw = G_KV_HEADS * G_HD
    gkv = proj(_GK0, 2 * kvw)
    gk = gkv[:, :kvw]
    gk = gk * lax.rsqrt(_head_pair_ms(gk) + EPS) * gkn_ref[...]
    if rotary:
        gk = _rope(gk, cg_ref[...], sg_ref[...], G_HD // 2, G_HD)
    gk_ref[...] = gk
    gv_ref[...] = gkv[:, kvw:]

    hw = M_HEADS * M_DK
    gates_ref[...] = _sigmoid(proj(_GATE0, _GATE1 - _GATE0)).astype(gates_ref.dtype)
    og_ref[...] = _sigmoid(proj(_MO0, hw)).astype(og_ref.dtype)
    qkv_ref[:, 0:hw] = proj(_MQ0, hw).astype(qkv_ref.dtype)
    qkv_ref[:, hw:2 * hw] = (proj(_MK0, hw) * (M_DK ** -0.5)).astype(qkv_ref.dtype)
    qkv_ref[:, 2 * hw:3 * hw] = proj(_MV0, hw).astype(qkv_ref.dtype)


def _inproj_call(x, mod, lw, mod_row0, rows_per_mod, rope, seq):
    n, d = x.shape
    tm = TOKEN_BLOCK
    hw = M_HEADS * M_DK

    def mod_idx(i):
        return (mod_row0 + (i * tm) // rows_per_mod, 0, 0)

    def rows(width):
        return pl.BlockSpec((tm, width), lambda i: (i, 0))

    in_specs = [
        rows(d),
        pl.BlockSpec((None, N_MOD, d), mod_idx),
        _resident((1, d)),
        _resident(lw["w_in"].shape[1:], lw["layer"]),
        _resident((d, _WIN_COLS - _AQ0), lw["layer"]),
        _resident((1, LANES)),
        _resident((1, A_Q_LORA)),
        _resident((A_Q_LORA, A_HEADS * LANES), lw["layer"]),
        _resident((1, A_KV_LORA)),
        _resident((1, LANES)),
        _resident((1, LANES)),
        _resident((1, LANES)),
    ]
    args = [x, mod, lw["norm1_g"], lw["w_in"], lw["w_in_tail"], lw["b_mgate"], lw["a_qlora_g"], lw["w_uq"],
            lw["a_kvlora_g"], lw["g_knorm_g"], lw["a_qnorm_g"], lw["g_qnorm_g"]]
    if rope is not None:
        blocks_per_seq = seq // tm
        in_specs += [pl.BlockSpec((tm, LANES), lambda i: (i % blocks_per_seq, 0))] * 4
        args += [rope[0][0], rope[0][1], rope[1][0], rope[1][1]]
    out_widths = [3 * d, 3 * hw, hw, LANES, A_HEADS * LANES, A_KV_LORA, LANES, G_HEADS * G_HD,
                  G_KV_HEADS * G_HD, G_KV_HEADS * G_HD]
    out_dtypes = [BF16, BF16, BF16, F32, BF16, F32, F32, BF16, F32, F32]
    return pl.pallas_call(
        functools.partial(_inproj_kernel, rotary=rope is not None),
        out_shape=[jax.ShapeDtypeStruct((n, w), t) for w, t in zip(out_widths, out_dtypes)],
        grid=(n // tm,),
        in_specs=in_specs,
        out_specs=[rows(w) for w in out_widths],
        compiler_params=_params(1),
        name="in_proj",
    )(*args)


def _mlstm_kernel(*refs, n_chunks, heads, has_init, emit_state):
    q_ref, k_ref, v_ref, og_ref, mg_ref, gn_ref = refs[:6]
    pos = 6
    if has_init:
        c0_ref, n0_ref, m0_ref = refs[pos:pos + 3]
        pos += 3
    y_ref = refs[pos]
    pos += 1
    if emit_state:
        cf_ref, nf_ref, mf_ref = refs[pos:pos + 3]
        pos += 3
    (pcol_scr, lfc_scr, grow_scr, prow_scr, lfr_scr, u_scr, nu_scr, gm_scr, tot_scr, cs_scr, ns_scr, ms_scr,
     c_scr, n_scr, m_scr) = refs[pos:]
    assert heads == M_HEADS
    n_gates = 4 * M_HEADS

    ln = MLSTM_CHUNK
    row = lax.broadcasted_iota(jnp.int32, (ln, ln), 0)
    col = lax.broadcasted_iota(jnp.int32, (ln, ln), 1)
    lower = col <= row
    upper = col >= row
    tril = jnp.where(lower, 1.0, 0.0).astype(BF16)
    triu = jnp.where(upper, 1.0, 0.0).astype(BF16)
    ones = jnp.ones((ln, ln), BF16)

    def gate_index(hh, d):
        return 2 * d * heads + hh, (2 * d + 1) * heads + hh

    def summaries(c, carry):
        r0 = pl.multiple_of(c * ln, ln)
        g = mg_ref[pl.ds(r0, ln), :]
        lf = _log_sigmoid(g) * LOG2E
        pcol_scr[pl.ds(r0, ln), :] = _dot01_left(tril, lf)
        lfc_scr[pl.ds(r0, ln), :] = lf
        g_rows = g.T[MG_LANE0:MG_LANE0 + n_gates, :]
        gr = g_rows * LOG2E
        lfr = _log_sigmoid(g_rows) * LOG2E
        pr = _dot01(lfr, triu)
        grow_scr[c] = gr
        prow_scr[c] = pr
        lfr_scr[c] = lfr
        for hh in range(heads):
            kb = k_ref[pl.ds(r0, ln), hh * LANES:(hh + 1) * LANES]
            k_t = kb.astype(F32).T
            v = v_ref[pl.ds(r0, ln), hh * LANES:(hh + 1) * LANES]
            for d in range(2):
                ji, jf = gate_index(hh, d)
                tot = pr[jf:jf + 1, ln - 1:ln]
                b_row = pr[jf:jf + 1, :] if d == 0 else tot - pr[jf:jf + 1, :] + lfr[jf:jf + 1, :]
                g_row = tot - b_row + gr[ji:ji + 1, :]
                gmax = jnp.max(g_row, axis=-1, keepdims=True)
                wg = jnp.exp2(g_row - gmax)
                u_scr[hh, d, c] = _dot((k_t * wg).astype(BF16), v)
                nu_scr[hh, d, c] = _dot(jnp.broadcast_to(wg, (SUBLANES, ln)).astype(BF16), kb)[0:1, :]
                gm_scr[hh, d, c] = jnp.broadcast_to(gmax, (1, ln))
                tot_scr[hh, d, c] = jnp.broadcast_to(tot, (1, ln))
        return carry

    lax.fori_loop(0, n_chunks, summaries, 0, unroll=min(2, n_chunks))

    for hh in range(heads):
        for d in range(2):
            if has_init:
                c_scr[hh, d] = c0_ref[d, hh]
                n_scr[hh, d] = n0_ref[d, hh]
                m_scr[hh, d] = m0_ref[d, hh] * LOG2E
            else:
                c_scr[hh, d] = jnp.zeros((M_DK, M_DV), F32)
                n_scr[hh, d] = jnp.zeros((1, M_DK), F32)
                m_scr[hh, d] = jnp.zeros((1, ln), F32)

    def scan(j, carry):
        for hh in range(heads):
            for d in range(2):
                c = j if d == 0 else n_chunks - 1 - j
                cst = c_scr[hh, d]
                nst = n_scr[hh, d]
                mst = m_scr[hh, d]
                cs_scr[hh, d, c] = cst.astype(BF16)
                ns_scr[hh, d, c] = nst
                ms_scr[hh, d, c] = mst
                gmax = gm_scr[hh, d, c]
                total = tot_scr[hh, d, c] + mst
                m_new = jnp.maximum(total, gmax)
                decay = jnp.exp2(total - m_new)
                scale = jnp.exp2(gmax - m_new)
                c_scr[hh, d] = cst * decay + u_scr[hh, d, c] * scale
                n_scr[hh, d] = nst * decay + nu_scr[hh, d, c] * scale
                m_scr[hh, d] = m_new
        return carry

    lax.fori_loop(0, n_chunks, scan, 0)

    def readout(c, carry):
        r0 = pl.multiple_of(c * ln, ln)
        pc = pcol_scr[pl.ds(r0, ln), :]
        lf = lfc_scr[pl.ds(r0, ln), :]
        gr = grow_scr[c]
        pr = prow_scr[c]
        lfr = lfr_scr[c]
        tot_all = pc[ln - 1:ln, :]
        for hh in range(heads):
            q = q_ref[pl.ds(r0, ln), hh * LANES:(hh + 1) * LANES]
            k = k_ref[pl.ds(r0, ln), hh * LANES:(hh + 1) * LANES]
            v = v_ref[pl.ds(r0, ln), hh * LANES:(hh + 1) * LANES]
            qk = _dot_nt(q, k)
            hsum = None
            for d in range(2):
                ji, jf = gate_index(hh, d)
                lane_f = MG_LANE0 + jf
                if d == 0:
                    b_col = pc[:, lane_f:lane_f + 1]
                    r_row = gr[ji:ji + 1, :] - pr[jf:jf + 1, :]
                    mask = lower
                else:
                    tot = tot_all[:, lane_f:lane_f + 1]
                    b_col = tot - pc[:, lane_f:lane_f + 1] + lf[:, lane_f:lane_f + 1]
                    r_row = gr[ji:ji + 1, :] - (tot - pr[jf:jf + 1, :] + lfr[jf:jf + 1, :])
                    mask = upper
                b_colb = jnp.broadcast_to(b_col, (ln, ln))
                d_log = jnp.where(mask, b_colb + r_row, -jnp.inf)
                dmax = jnp.broadcast_to(jnp.max(d_log, axis=-1, keepdims=True), (ln, ln))
                inter = b_colb + ms_scr[hh, d, c]
                m_t = jnp.maximum(inter, dmax)
                sb = (qk * jnp.exp2(d_log - m_t)).astype(BF16)
                w_inter = jnp.exp2(inter - m_t)
                qc = _dot(q, cs_scr[hh, d, c])
                qn = _dot_nt(q, jnp.broadcast_to(ns_scr[hh, d, c], (ln, M_DK)).astype(BF16))
                num = qc * w_inter + _dot(sb, v)
                den = qn * w_inter + _dot(sb, ones)
                hc = num / jnp.maximum(jnp.abs(den), jnp.exp2(-m_t))
                hsum = hc if hsum is None else hsum + hc
            y_ref[pl.ds(r0, ln), hh * LANES:(hh + 1) * LANES] = (
                _rms(hsum, M_DV) * gn_ref[...] * og_ref[pl.ds(r0, ln), hh * LANES:(hh + 1) * LANES]
            ).astype(y_ref.dtype)
        return carry

    lax.fori_loop(0, n_chunks, readout, 0, unroll=min(2, n_chunks))

    if emit_state:
        for hh in range(heads):
            for d in range(2):
                cf_ref[d, hh] = c_scr[hh, d]
                nf_ref[d, hh] = n_scr[hh, d]
                mf_ref[d, hh] = m_scr[hh, d] * (1.0 / LOG2E)


def _mlstm_call(qkv, og, mg, lw, batch, seq, init, emit_state):
    n = batch * seq
    ln = MLSTM_CHUNK
    nc = seq // ln
    h = M_HEADS
    hs = MLSTM_HEADS_PER_STEP
    ng = h // hs

    def tok(colblock):
        return pl.BlockSpec((seq, hs * LANES), lambda b, hg: (b, colblock(hg)))

    in_specs = [
        tok(lambda hg: hg), tok(lambda hg: ng + hg), tok(lambda hg: 2 * ng + hg),
        tok(lambda hg: hg),
        pl.BlockSpec((seq, LANES), lambda b, hg: (b, 0)),
        pl.BlockSpec((1, M_DV), lambda b, hg: (0, 0)),
    ]
    args = [qkv, qkv, qkv, og, mg, lw["m_norm_g"]]
    state_specs = [
        pl.BlockSpec((None, 2, hs, M_DK, M_DV), lambda b, hg: (b, 0, hg, 0, 0)),
        pl.BlockSpec((None, 2, hs, 1, M_DK), lambda b, hg: (b, 0, hg, 0, 0)),
        pl.BlockSpec((None, 2, hs, 1, ln), lambda b, hg: (b, 0, hg, 0, 0)),
    ]
    if init is not None:
        c_all, n_all, m_all, layer = init
        depth = c_all.shape[1]
        in_specs += [
            pl.BlockSpec((None, None, 2, hs, M_DK, M_DV), lambda b, hg: (b, layer, 0, hg, 0, 0)),
            pl.BlockSpec((None, None, 2, hs, 1, M_DK), lambda b, hg: (b, layer, 0, hg, 0, 0)),
            pl.BlockSpec((None, None, 2, hs, 1, ln), lambda b, hg: (b, layer, 0, hg, 0, 0)),
        ]
        args += [c_all, n_all.reshape(batch, depth, 2, h, 1, M_DK),
                 jnp.broadcast_to(m_all[..., None, None], (batch, depth, 2, h, 1, ln))]
    out_shape = [jax.ShapeDtypeStruct((n, h * M_DV), BF16)]
    out_specs = [tok(lambda hg: hg)]
    if emit_state:
        out_shape += [
            jax.ShapeDtypeStruct((batch, 2, h, M_DK, M_DV), F32),
            jax.ShapeDtypeStruct((batch, 2, h, 1, M_DK), F32),
            jax.ShapeDtypeStruct((batch, 2, h, 1, ln), F32),
        ]
        out_specs += state_specs
    per = (hs, 2, nc)
    outs = pl.pallas_call(
        functools.partial(_mlstm_kernel, n_chunks=nc, heads=hs, has_init=init is not None,
                          emit_state=emit_state),
        out_shape=out_shape,
        grid=(batch, ng),
        in_specs=in_specs,
        out_specs=out_specs,
        scratch_shapes=[
            pltpu.VMEM((seq, LANES), F32), pltpu.VMEM((seq, LANES), F32),
            pltpu.VMEM((nc, 4 * h, ln), F32), pltpu.VMEM((nc, 4 * h, ln), F32), pltpu.VMEM((nc, 4 * h, ln), F32),
            pltpu.VMEM(per + (M_DK, M_DV), F32), pltpu.VMEM(per + (1, M_DK), F32),
            pltpu.VMEM(per + (1, ln), F32), pltpu.VMEM(per + (1, ln), F32),
            pltpu.VMEM(per + (M_DK, M_DV), BF16), pltpu.VMEM(per + (1, M_DK), F32), pltpu.VMEM(per + (1, ln), F32),
            pltpu.VMEM((hs, 2, M_DK, M_DV), F32), pltpu.VMEM((hs, 2, 1, M_DK), F32), pltpu.VMEM((hs, 2, 1, ln), F32),
        ],
        compiler_params=_params(2),
        name="mlstm",
    )(*args)
    if emit_state:
        ym, cf, nf, mf = outs
        return ym, (cf, nf[:, :, :, 0, :], mf[:, :, :, 0, 0])
    return outs[0], None


HEAD_V = 64
VT_ROWS = 80


def _ones_row(v_t):
    return jnp.where(lax.broadcasted_iota(jnp.int32, v_t.shape, 0) == HEAD_V, 1.0, v_t)


def _flash_heads(qs, keys_of, vt_chunk, n_keys, o_ref):
    def values(e, s_t):
        m = acc = None
        for c0 in range(0, n_keys, KEY_BLOCK):
            c1 = min(c0 + KEY_BLOCK, n_keys)
            s = s_t[c0:c1, :]
            mc = jnp.max(s, axis=0, keepdims=True)
            if c0 == 0:
                m = mc
                acc = _dot(vt_chunk(e, c0, c1), jnp.exp2(s - mc).astype(BF16))
            else:
                m_new = jnp.maximum(m, mc)
                acc = acc * jnp.exp2(m - m_new) + _dot(vt_chunk(e, c0, c1), jnp.exp2(s - m_new).astype(BF16))
                m = m_new
        return acc[0:HEAD_V, :] / acc[HEAD_V:HEAD_V + 1, :]

    if n_keys <= KEY_BLOCK:
        s_all = [_dot_nt(keys_of(e), qb) for e, qb in enumerate(qs)]
        outs = [values(e, s_t) for e, s_t in enumerate(s_all)]
    else:
        outs = []
        s_next = _dot_nt(keys_of(0), qs[0])
        for e in range(len(qs)):
            s_t = s_next
            if e + 1 < len(qs):
                s_next = _dot_nt(keys_of(e + 1), qs[e + 1])
            outs.append(values(e, s_t))
    for e in range(0, len(outs), 2):
        pair = jnp.concatenate([outs[e], outs[e + 1]], axis=0)
        o_ref[:, (e // 2) * LANES:(e // 2 + 1) * LANES] = pair.T.astype(o_ref.dtype)


def _mla_kernel(*refs, past, rotary, heads):
    q_ref, ckv_ref, akr_ref = refs[:3]
    pos = 3
    if past:
        ckvc_ref, krc_ref = refs[pos:pos + 2]
        pos += 2
    wk_ref, wvt_ref, kg_ref = refs[pos:pos + 3]
    pos += 3
    if rotary:
        ck_ref, sk_ref = refs[pos:pos + 2]
        pos += 2
    o_ref, kbuf, vbuf = refs[pos:]
    seq = ckv_ref.shape[0]

    @pl.when(pl.program_id(2) == 0)
    def _():
        for e in range(heads):
            wk_h = wk_ref[:, e * LANES:(e + 1) * LANES]
            wvt_h = wvt_ref[e * VT_ROWS:(e + 1) * VT_ROWS, :]

            def keys(ckv, kr):
                cb = ckv.astype(BF16)
                kcat = _dot(cb, wk_h) + kr
                return _ones_row(_dot_nt(wvt_h, cb)), _rms(kcat, A_QK) * kg_ref[...]

            v_t, kn = keys(ckv_ref[...], akr_ref[...])
            if rotary:
                kn = _rope(kn, ck_ref[...], sk_ref[...], A_ROPE // 2, LANES)
            kbuf[e, past:past + seq, :] = kn.astype(BF16)
            vbuf[e, :, past:past + seq] = v_t.astype(BF16)
            if past:
                v_tc, knc = keys(ckvc_ref[...], krc_ref[...])
                kbuf[e, 0:past, :] = knc.astype(BF16)
                vbuf[e, :, 0:past] = v_tc.astype(BF16)

    qs = [q_ref[:, e * LANES:(e + 1) * LANES] for e in range(heads)]
    _flash_heads(qs, lambda e: kbuf[e], lambda e, c0, c1: vbuf[e, :, c0:c1], past + seq, o_ref)


def _mla_call(qa, ckv, akr, lw, batch, seq, cache, rope_tabs):
    n = batch * seq
    tq = min(Q_BLOCK, seq)
    nq = seq // tq
    past = 0 if cache is None else cache[0].shape[2]
    heads = MLA_HEADS_PER_STEP if nq > 1 else A_HEADS
    in_specs = [
        pl.BlockSpec((tq, heads * LANES), lambda b, hg, qi: (b * nq + qi, hg)),
        pl.BlockSpec((seq, LANES), lambda b, hg, qi: (b, 0)),
        pl.BlockSpec((seq, LANES), lambda b, hg, qi: (b, 0)),
    ]
    args = [qa, ckv, akr]
    layer = lw["layer"]
    if past:
        in_specs += [pl.BlockSpec((None, None, past, LANES), lambda b, hg, qi: (b, layer, 0, 0))] * 2
        args += [cache[0], cache[1]]
    in_specs += [
        pl.BlockSpec((None, A_KV_LORA, heads * LANES), lambda b, hg, qi: (layer, 0, hg)),
        pl.BlockSpec((None, heads * VT_ROWS, A_KV_LORA), lambda b, hg, qi: (layer, hg, 0)),
        pl.BlockSpec((1, LANES), lambda b, hg, qi: (0, 0)),
    ]
    args += [lw["w_uk"], lw["w_uvt"], lw["a_knorm_g"]]
    if rope_tabs is not None:
        cos, sin = rope_tabs
        in_specs += [pl.BlockSpec((seq, LANES), lambda b, hg, qi: (0, 0))] * 2
        args += [cos, sin]
    return pl.pallas_call(
        functools.partial(_mla_kernel, past=past, rotary=rope_tabs is not None, heads=heads),
        out_shape=jax.ShapeDtypeStruct((n, A_HEADS * A_V), BF16),
        grid=(batch, A_HEADS // heads, nq),
        in_specs=in_specs,
        out_specs=pl.BlockSpec((tq, heads * A_V), lambda b, hg, qi: (b * nq + qi, hg)),
        scratch_shapes=[pltpu.VMEM((heads, past + seq, LANES), BF16),
                        pltpu.VMEM((heads, VT_ROWS, past + seq), BF16)],
        compiler_params=_params(3),
        name="mla_attn",
    )(*args)


def _gqa_kernel(*refs, past, groups):
    q_ref, k_ref, v_ref = refs[:3]
    pos = 3
    if past:
        kc_ref, vc_ref = refs[pos:pos + 2]
        pos += 2
    o_ref, kbuf, vbuf = refs[pos:]
    seq = k_ref.shape[0]
    qw = G_GROUP * G_HD

    @pl.when(pl.program_id(2) == 0)
    def _():
        for gi in range(groups):
            grp = pl.program_id(1) * groups + gi

            r = lax.broadcasted_iota(jnp.int32, (VT_ROWS, LANES), 0)
            c = lax.broadcasted_iota(jnp.int32, (VT_ROWS, LANES), 1)
            pick = jnp.where((c == r + grp * G_HD) & (r < G_HD), 1.0, 0.0).astype(BF16)

            def fill(r0, k, v):
                low = _lane(k.shape) < G_HD
                k_lo = jnp.where(grp == 0, k, pltpu.roll(k, G_HD, 1))
                rows = k.shape[0]
                kbuf[gi, r0:r0 + rows, :] = jnp.where(low, k_lo, pltpu.roll(k_lo, G_HD, 1)).astype(BF16)
                vbuf[gi, :, r0:r0 + rows] = _ones_row(_dot_nt(pick, v.astype(BF16))).astype(BF16)

            fill(past, k_ref[...], v_ref[...])
            if past:
                fill(0, kc_ref[...], vc_ref[...])

    qs = []
    for gi in range(groups):
        for j in range(G_GROUP):
            col = q_ref[:, gi * qw + (j // 2) * LANES:gi * qw + (j // 2 + 1) * LANES]
            keep = (_lane(col.shape) < G_HD) == (j % 2 == 0)
            qs.append(jnp.where(keep, col, jnp.zeros_like(col)))
    _flash_heads(qs, lambda e: kbuf[e // G_GROUP], lambda e, c0, c1: vbuf[e // G_GROUP, :, c0:c1],
                 past + seq, o_ref)


def _gqa_call(gq, gk, gv, batch, seq, cache):
    n = batch * seq
    tq = min(Q_BLOCK, seq)
    nq = seq // tq
    past = 0 if cache is None else cache[0].shape[2]
    groups = GQA_GROUPS_PER_STEP if nq > 1 else G_KV_HEADS
    qw = G_GROUP * G_HD
    kvw = G_KV_HEADS * G_HD
    in_specs = [
        pl.BlockSpec((tq, groups * qw), lambda b, g, qi: (b * nq + qi, g)),
        pl.BlockSpec((seq, kvw), lambda b, g, qi: (b, 0)),
        pl.BlockSpec((seq, kvw), lambda b, g, qi: (b, 0)),
    ]
    args = [gq, gk, gv]
    if past:
        layer = cache[2]
        in_specs += [pl.BlockSpec((None, None, past, kvw), lambda b, g, qi: (b, layer, 0, 0))] * 2
        args += [cache[0], cache[1]]
    return pl.pallas_call(
        functools.partial(_gqa_kernel, past=past, groups=groups),
        out_shape=jax.ShapeDtypeStruct((n, G_HEADS * G_HD), BF16),
        grid=(batch, G_KV_HEADS // groups, nq),
        in_specs=in_specs,
        out_specs=pl.BlockSpec((tq, groups * qw), lambda b, g, qi: (b * nq + qi, g)),
        scratch_shapes=[pltpu.VMEM((groups, past + seq, LANES), BF16),
                        pltpu.VMEM((groups, VT_ROWS, past + seq), BF16)],
        compiler_params=_params(3),
        name="gqa_attn",
    )(*args)


def _merge_ffn_kernel(x_ref, ym_ref, ya_ref, yg_ref, gates_ref, mod_ref, g2_ref, wb_ref, wo_ref, wfi_ref,
                      wfo_ref, o_ref, *, ff_chunks):
    x = x_ref[...]
    d = x.shape[-1]
    mixed = None
    for i, y_ref in enumerate((ym_ref, ya_ref, yg_ref)):
        br = _dot(y_ref[...].astype(BF16), wb_ref[i]) * gates_ref[:, i * d:(i + 1) * d]
        mixed = br if mixed is None else mixed + br
    gt1 = mod_ref[2:3, :]
    x1 = x + gt1 * _dot(mixed.astype(BF16), wo_ref[...])

    sh2 = mod_ref[3:4, :]
    sc2 = mod_ref[4:5, :]
    gt2 = mod_ref[5:6, :]
    h2 = ((_rms(x1, d) * g2_ref[...]) * (1.0 + sc2) + sh2).astype(BF16)
    d_ff = wfo_ref.shape[0]
    acc = None
    for c0, c1 in ff_chunks:
        ug = _dot(h2, wfi_ref[:, c0:c1])
        uv = _dot(h2, wfi_ref[:, d_ff + c0:d_ff + c1])
        act = (ug * _sigmoid(ug) * uv).astype(BF16)
        part = _dot(act, wfo_ref[c0:c1, :])
        acc = part if acc is None else acc + part
    o_ref[...] = x1 + gt2 * acc


def _merge_ffn_call(x, ym, ya, yg, gates, mod, lw, mod_row0, rows_per_mod):
    n, d = x.shape
    tm = TOKEN_BLOCK
    d_ff = lw["w_ffn_out"].shape[1]
    layer = lw["layer"]
    split = -(-(d_ff // 2) // MXU_COLS) * MXU_COLS
    ff_chunks = ((0, split), (split, d_ff)) if 0 < split < d_ff else ((0, d_ff),)

    def mod_idx(i):
        return (mod_row0 + (i * tm) // rows_per_mod, 0, 0)

    def rows(width):
        return pl.BlockSpec((tm, width), lambda i: (i, 0))

    return pl.pallas_call(
        functools.partial(_merge_ffn_kernel, ff_chunks=ff_chunks),
        out_shape=jax.ShapeDtypeStruct((n, d), F32),
        grid=(n // tm,),
        in_specs=[
            rows(d), rows(BRANCH_WIDTH), rows(BRANCH_WIDTH), rows(BRANCH_WIDTH), rows(N_BRANCH * d),
            pl.BlockSpec((None, N_MOD, d), mod_idx),
            _resident((1, d)),
            _resident((N_BRANCH, BRANCH_WIDTH, d), layer),
            _resident((d, d), layer),
            _resident((d, 2 * d_ff), layer),
            _resident((d_ff, d), layer),
        ],
        out_specs=rows(d),
        compiler_params=_params(1),
        name="merge_ffn",
    )(x, ym, ya, yg, gates, mod, lw["norm2_g"], lw["w_branch"], lw["w_out"], lw["w_ffn_in"], lw["w_ffn_out"])


def _mla_q_layout(a):
    lead = a.shape[:-1]
    a = a.reshape(lead + (A_HEADS, A_QK))
    z = jnp.zeros(lead + (A_HEADS, LANES - A_QK), a.dtype)
    return jnp.concatenate([a[..., A_NOPE:], z, a[..., :A_NOPE]], axis=-1).reshape(lead + (A_HEADS * LANES,))


def _stacked_weights(w_in, w_uq, w_ukv, w_branch, w_out, w_ffn_in, w_ffn_out):
    depth, d, _ = w_in.shape
    hw = M_HEADS * M_DK
    sizes = (N_BRANCH * d, hw, hw, hw, hw, 4 * M_HEADS, A_Q_LORA, A_KV_LORA, A_ROPE, G_HEADS * G_HD,
             G_KV_HEADS * G_HD, G_KV_HEADS * G_HD)
    edges = np.concatenate([[0], np.cumsum(sizes)]).tolist()
    (wmg, waq, wakv, wakr, wgq, wgk, wgv) = [w_in[:, :, a:b] for a, b in zip(edges[5:-1], edges[6:])]
    misc_pad = jnp.zeros((depth, d, LANES - A_ROPE - 4 * M_HEADS), w_in.dtype)
    tail = jnp.concatenate([waq, wakv, wakr, wmg, misc_pad, wgq, wgk, wgv], axis=2).astype(BF16)
    ukv = w_ukv.reshape(depth, A_KV_LORA, A_HEADS, A_NOPE + A_V)
    uk = jnp.pad(ukv[..., :A_NOPE], ((0, 0), (0, 0), (0, 0), (LANES - A_NOPE, 0)))
    uvt = jnp.pad(ukv[..., A_NOPE:].transpose(0, 2, 3, 1), ((0, 0), (0, 0), (0, VT_ROWS - A_V), (0, 0)))
    return dict(
        w_in=w_in.astype(BF16),
        w_in_tail=tail,
        w_uq=_mla_q_layout(w_uq).astype(BF16),
        w_uk=uk.reshape(depth, A_KV_LORA, A_HEADS * LANES).astype(BF16),
        w_uvt=uvt.reshape(depth, A_HEADS * VT_ROWS, A_KV_LORA).astype(BF16),
        w_branch=w_branch.astype(BF16),
        w_out=w_out.astype(BF16),
        w_ffn_in=w_ffn_in.astype(BF16),
        w_ffn_out=w_ffn_out.astype(BF16),
    )


def _layer_vectors(l, b_mgate, norm1_g, m_norm_g, a_qlora_g, a_kvlora_g, a_qnorm_g, a_knorm_g, g_qnorm_g,
                   g_knorm_g, norm2_g):
    return dict(
        layer=l,
        b_mgate=jnp.pad(b_mgate[l][None, :], ((0, 0), (MG_LANE0, LANES - MG_LANE0 - 4 * M_HEADS))),
        norm1_g=norm1_g[l][None, :],
        m_norm_g=m_norm_g[l][None, :],
        a_qlora_g=a_qlora_g[l][None, :],
        a_kvlora_g=a_kvlora_g[l][None, :],
        a_qnorm_g=_mla_q_layout(jnp.tile(a_qnorm_g[l], A_HEADS)[None, :])[:, :LANES],
        a_knorm_g=_mla_q_layout(jnp.tile(a_knorm_g[l], A_HEADS)[None, :])[:, :LANES],
        g_qnorm_g=jnp.tile(g_qnorm_g[l], LANES // G_HD)[None, :],
        g_knorm_g=jnp.tile(g_knorm_g[l], G_KV_HEADS)[None, :],
        norm2_g=norm2_g[l][None, :],
    )


def _axial_angles(seq, rot_dim):
    n_freq = rot_dim // 4
    freqs = ROPE_BASE ** (-jnp.arange(n_freq, dtype=F32) / n_freq)
    t = jnp.arange(seq)
    row = (t // GRID_W).astype(F32)
    col = (t % GRID_W).astype(F32)
    return jnp.concatenate([row[:, None] * freqs, col[:, None] * freqs], axis=-1)


def _rope_tables(seq):
    ang = _axial_angles(seq, A_ROPE)
    one = jnp.ones((seq, LANES - A_ROPE), F32)
    mla_cos = jnp.concatenate([jnp.cos(ang), jnp.cos(ang), one], axis=-1)
    mla_sin = jnp.concatenate([-jnp.sin(ang), jnp.sin(ang), 0.0 * one], axis=-1)
    ang = _axial_angles(seq, G_HD)
    cos = jnp.concatenate([jnp.cos(ang), jnp.cos(ang)], axis=-1)
    sin = jnp.concatenate([-jnp.sin(ang), jnp.sin(ang)], axis=-1)
    gqa = (jnp.tile(cos, (1, LANES // G_HD)), jnp.tile(sin, (1, LANES // G_HD)))
    return (mla_cos, mla_sin), gqa


def _layer(x, mod, lw, batch, seq, mod_row0, rows_per_mod, ctx, rope):
    (gates, qkv, og, mg, qa, ckv, akr, gq, gk, gv) = _inproj_call(x, mod, lw, mod_row0, rows_per_mod, rope, seq)
    if ctx is None:
        ym, state = _mlstm_call(qkv, og, mg, lw, batch, seq, None, True)
        ya = _mla_call(qa, ckv, akr, lw, batch, seq, None, None)
        yg = _gqa_call(gq, gk, gv, batch, seq, None)
        new_ctx = dict(state=state, ckv=ckv, kr=akr[:, :A_ROPE], gk=gk, gv=gv)
    else:
        ym, _ = _mlstm_call(qkv, og, mg, lw, batch, seq, ctx["mlstm"], False)
        ya = _mla_call(qa, ckv, akr, lw, batch, seq, ctx["mla"], rope[0])
        yg = _gqa_call(gq, gk, gv, batch, seq, ctx["gqa"])
        new_ctx = None
    x = _merge_ffn_call(x, ym, ya, yg, gates, mod, lw, mod_row0, rows_per_mod)
    return x, new_ctx


def kernel(x_prompt, x_sample, state_mlstm_C, state_mlstm_n, state_mlstm_m, cache_mla_ckv, cache_mla_krope,
           cache_gqa_k, cache_gqa_v, c, c_ctx, w_mod, b_mod, norm1_g, w_in, b_mgate, m_norm_g, a_qlora_g,
           a_kvlora_g, w_uq, w_ukv, a_qnorm_g, a_knorm_g, g_qnorm_g, g_knorm_g, w_branch, w_out, norm2_g,
           w_ffn_in, w_ffn_out):
    batch, seq, d = x_prompt.shape
    dbatch, dseq, _ = x_sample.shape
    depth = w_in.shape[0]
    past = cache_mla_ckv.shape[2]

    n_rows = -(-(1 + dbatch) // SUBLANES) * SUBLANES
    cond = jnp.concatenate([c_ctx[None, :], c, jnp.zeros((n_rows - 1 - dbatch, d), F32)], axis=0)
    mod_all = _mod_call(cond, w_mod, b_mod).reshape(depth, n_rows, N_MOD, d)

    rope = _rope_tables(dseq)
    xp = x_prompt.reshape(batch * seq, d)
    xs = x_sample.reshape(dbatch * dseq, d)
    ctx_layers = []
    krope_cache = jnp.pad(cache_mla_krope, ((0, 0), (0, 0), (0, 0), (0, LANES - A_ROPE)))
    gqa_k_cache = cache_gqa_k.reshape(dbatch, depth, past, G_KV_HEADS * G_HD)
    gqa_v_cache = cache_gqa_v.reshape(dbatch, depth, past, G_KV_HEADS * G_HD)
    stacked = _stacked_weights(w_in, w_uq, w_ukv, w_branch, w_out, w_ffn_in, w_ffn_out)
    for l in range(depth):
        lw = dict(stacked, **_layer_vectors(l, b_mgate, norm1_g, m_norm_g, a_qlora_g, a_kvlora_g, a_qnorm_g,
                                            a_knorm_g, g_qnorm_g, g_knorm_g, norm2_g))
        xp, st = _layer(xp, mod_all[l], lw, batch, seq, 0, batch * seq, None, None)
        ctx_layers.append(st)
        ctx = dict(mlstm=(state_mlstm_C, state_mlstm_n, state_mlstm_m, l), mla=(cache_mla_ckv, krope_cache),
                   gqa=(gqa_k_cache, gqa_v_cache, l))
        xs, _ = _layer(xs, mod_all[l], lw, dbatch, dseq, 1, dseq, ctx, rope)

    def stack(fn):
        return jnp.stack([fn(s) for s in ctx_layers], axis=1)

    new_c = stack(lambda s: s["state"][0])
    new_n = stack(lambda s: s["state"][1])
    new_m = stack(lambda s: s["state"][2])
    new_ckv = stack(lambda s: s["ckv"].reshape(batch, seq, A_KV_LORA))
    new_kr = stack(lambda s: s["kr"].reshape(batch, seq, A_ROPE))
    new_gk = stack(lambda s: s["gk"].reshape(batch, seq, G_KV_HEADS, G_HD))
    new_gv = stack(lambda s: s["gv"].reshape(batch, seq, G_KV_HEADS, G_HD))
    return (xp.reshape(batch, seq, d), xs.reshape(dbatch, dseq, d), new_c, new_n, new_m, new_ckv, new_kr,
            new_gk, new_gv)
```

```python
import functools

import numpy as np
import jax
import jax.numpy as jnp
from jax import lax
from jax.experimental import pallas as pl
from jax.experimental.pallas import tpu as pltpu

F32 = jnp.float32
BF16 = jnp.bfloat16

LANES = 128
SUBLANES = 8
VMEM_LIMIT_BYTES = 56 * 1024 * 1024

EPS = 1e-6
ROPE_BASE = 10000.0
GRID_W = 64

M_HEADS = 4
M_DK = 128
M_DV = 128
A_HEADS = 8
A_NOPE = 64
A_ROPE = 32
A_QK = A_NOPE + A_ROPE
A_V = 64
A_Q_LORA = 256
A_KV_LORA = 128
G_HEADS = 8
G_KV_HEADS = 2
G_GROUP = G_HEADS // G_KV_HEADS
G_HD = 64
N_BRANCH = 3
BRANCH_WIDTH = 512
N_MOD = 6

TOKEN_BLOCK = 512
Q_BLOCK = 256
KEY_BLOCK = 256
MLA_HEADS_PER_STEP = 8
GQA_GROUPS_PER_STEP = 2
MLSTM_CHUNK = 128
MLSTM_HEADS_PER_STEP = 4
LOG2E = 1.4426950408889634


def _params(n_axes):
    return pltpu.CompilerParams(dimension_semantics=("arbitrary",) * n_axes,
                                vmem_limit_bytes=VMEM_LIMIT_BYTES)


def _resident(shape, layer=None):
    nd = len(shape)
    if layer is None:
        return pl.BlockSpec(shape, lambda *_: (0,) * nd, pipeline_mode=pl.Buffered(1))
    return pl.BlockSpec((None,) + tuple(shape), lambda *_: (layer,) + (0,) * nd, pipeline_mode=pl.Buffered(1))


def _lane(shape, axis=None):
    return lax.broadcasted_iota(jnp.int32, shape, len(shape) - 1 if axis is None else axis)


def _dot(a, b):
    return jnp.dot(a, b, preferred_element_type=F32)


def _dot_nt(a, b):
    return lax.dot_general(a, b, (((1,), (1,)), ((), ())), preferred_element_type=F32)


def _split3(a):
    hi = a.astype(BF16)
    r1 = a - hi.astype(F32)
    mid = r1.astype(BF16)
    lo = (r1 - mid.astype(F32)).astype(BF16)
    return hi, mid, lo


def _dot01(a, m01):
    hi, mid, lo = _split3(a)
    return _dot(hi, m01) + _dot(mid, m01) + _dot(lo, m01)


def _dot01_left(m01, a):
    hi, mid, lo = _split3(a)
    return _dot(m01, hi) + _dot(m01, mid) + _dot(m01, lo)


def _sigmoid(x):
    return 0.5 * jnp.tanh(0.5 * x) + 0.5


def _log_sigmoid(x):
    return jnp.minimum(x, 0.0) - jnp.log(1.0 + jnp.exp(-jnp.abs(x)))


def _rms(x, width):
    ms = jnp.sum(x * x, axis=-1, keepdims=True) * (1.0 / width)
    return x * lax.rsqrt(ms + EPS)


def _rope(x, cos, sin_signed, half, period):
    n = x.shape[-1]
    first = (_lane(x.shape) % period) < half
    swapped = jnp.where(first, pltpu.roll(x, n - half, x.ndim - 1), pltpu.roll(x, half, x.ndim - 1))
    return x * cos + swapped * sin_signed


def _mod_kernel(c_ref, w_ref, b_ref, o_ref):
    c = c_ref[...]
    a = c * _sigmoid(c)
    o_ref[...] = _dot01(a, w_ref[...].astype(BF16)) + b_ref[...]


def _mod_call(cond, w_mod, b_mod):
    depth, d, n = w_mod.shape
    rows = cond.shape[0]
    tn = 1536
    return pl.pallas_call(
        _mod_kernel,
        out_shape=jax.ShapeDtypeStruct((depth, rows, n), F32),
        grid=(depth, n // tn),
        in_specs=[
            pl.BlockSpec((rows, d), lambda l, j: (0, 0)),
            pl.BlockSpec((None, d, tn), lambda l, j: (l, 0, j)),
            pl.BlockSpec((None, 1, tn), lambda l, j: (l, 0, j)),
        ],
        out_specs=pl.BlockSpec((None, rows, tn), lambda l, j: (l, 0, j)),
        compiler_params=_params(2),
        name="adaln_mod",
    )(cond, w_mod, b_mod.reshape(depth, 1, n))


MXU_COLS = 256
_GATE0, _GATE1 = 0, 3072
_MQ0 = 3072
_MK0 = 3584
_MV0 = 4096
_MO0 = 4608
_AQ0 = 5120
_AKV0 = 5376
_MISC0 = 5504
_GQ0 = 5632
_GK0 = 6144
_GV0 = 6272
_WIN_COLS = 6400
MG_LANE0 = A_ROPE
KR_SS_LANE = 64


def _head_pair_ms(x):
    low = (_lane(x.shape) % LANES) < G_HD
    sq = x * x
    cols = []
    for c0 in range(0, x.shape[-1], LANES):
        s = sq[:, c0:c0 + LANES]
        lo = jnp.sum(jnp.where(low[:, c0:c0 + LANES], s, 0.0), axis=-1, keepdims=True)
        hi = jnp.sum(jnp.where(low[:, c0:c0 + LANES], 0.0, s), axis=-1, keepdims=True)
        cols.append(jnp.where(low[:, c0:c0 + LANES], lo, hi))
    ms = cols[0] if len(cols) == 1 else jnp.concatenate(cols, axis=-1)
    return ms * (1.0 / G_HD)


def _inproj_kernel(*refs, rotary):
    (x_ref, mod_ref, g1_ref, w_ref, wt_ref, bmg_ref, gql_ref, wuq_ref, gkvl_ref, gkn_ref, aqn_ref,
     gqn_ref, akn_ref) = refs[:13]
    pos = 13
    if rotary:
        ca_ref, sa_ref, cg_ref, sg_ref = refs[pos:pos + 4]
        pos += 4
    (gates_ref, qkv_ref, og_ref, mg_ref, qa_ref, ckv_ref, akr_ref, gq_ref, gk_ref, gv_ref) = refs[pos:]
    x = x_ref[...]
    d = x.shape[-1]
    sh1 = mod_ref[0:1, :]
    sc1 = mod_ref[1:2, :]
    h = (_rms(x, d) * g1_ref[...]) * (1.0 + sc1) + sh1
    hb = h.astype(BF16)

    def proj(c0, width):
        if c0 < _AQ0:
            return _dot(hb, w_ref[:, c0:c0 + width])
        return _dot(hb, wt_ref[:, c0 - _AQ0:c0 - _AQ0 + width])

    aq = _rms(proj(_AQ0, A_Q_LORA), A_Q_LORA) * gql_ref[...]
    qa = _dot(aq.astype(BF16), wuq_ref[...])
    for hd in range(A_HEADS):
        qh = _rms(qa[:, hd * LANES:(hd + 1) * LANES], A_QK) * aqn_ref[...]
        if rotary:
            qh = _rope(qh, ca_ref[...], sa_ref[...], A_ROPE // 2, LANES)
        qa_ref[:, hd * LANES:(hd + 1) * LANES] = (qh * (A_QK ** -0.5 * LOG2E)).astype(qa_ref.dtype)
    akv_misc = proj(_AKV0, A_KV_LORA + LANES)
    ckv_ref[...] = _rms(akv_misc[:, :A_KV_LORA], A_KV_LORA) * gkvl_ref[...]
    misc = akv_misc[:, A_KV_LORA:]
    kr = jnp.where(_lane(misc.shape) < A_ROPE, misc, 0.0)
    mg = misc + bmg_ref[...]
    if rotary:
        akr_ref[...] = _rope(kr * akn_ref[...], ca_ref[...], sa_ref[...], A_ROPE // 2, LANES)
        mg = jnp.where(_lane(mg.shape) == KR_SS_LANE, jnp.sum(kr * kr, axis=-1, keepdims=True), mg)
    else:
        akr_ref[...] = kr
    mg_ref[...] = mg

    gq_all = proj(_GQ0, G_HEADS * G_HD)
    for c0 in range(0, G_HEADS * G_HD, LANES):
        gq = gq_all[:, c0:c0 + LANES]
        gq = gq * lax.rsqrt(_head_pair_ms(gq) + EPS) * gqn_ref[...]
        if rotary:
            gq = _rope(gq, cg_ref[...], sg_ref[...], G_HD // 2, G_HD)
        gq_ref[:, c0:c0 + LANES] = (gq * (G_HD ** -0.5 * LOG2E)).astype(gq_ref.dtype)
    kvw = G_KV_HEADS * G_HD
    gkv = proj(_GK0, 2 * kvw)
    gk = gkv[:, :kvw]
    gk = gk * lax.rsqrt(_head_pair_ms(gk) + EPS) * gkn_ref[...]
    if rotary:
        gk = _rope(gk, cg_ref[...], sg_ref[...], G_HD // 2, G_HD)
    gk_ref[...] = gk
    gv_ref[...] = gkv[:, kvw:]

    hw = M_HEADS * M_DK
    gates_ref[...] = _sigmoid(proj(_GATE0, _GATE1 - _GATE0)).astype(gates_ref.dtype)
    og_ref[...] = _sigmoid(proj(_MO0, hw)).astype(og_ref.dtype)
    qkv_ref[:, 0:hw] = proj(_MQ0, hw).astype(qkv_ref.dtype)
    qkv_ref[:, hw:2 * hw] = (proj(_MK0, hw) * (M_DK ** -0.5)).astype(qkv_ref.dtype)
    qkv_ref[:, 2 * hw:3 * hw] = proj(_MV0, hw).astype(qkv_ref.dtype)


def _inproj_call(x, mod, lw, mod_row0, rows_per_mod, rope, seq):
    n, d = x.shape
    tm = TOKEN_BLOCK
    hw = M_HEADS * M_DK

    def mod_idx(i):
        return (mod_row0 + (i * tm) // rows_per_mod, 0, 0)

    def rows(width):
        return pl.BlockSpec((tm, width), lambda i: (i, 0))

    in_specs = [
        rows(d),
        pl.BlockSpec((None, N_MOD, d), mod_idx),
        _resident((1, d)),
        _resident(lw["w_in"].shape[1:], lw["layer"]),
        _resident((d, _WIN_COLS - _AQ0), lw["layer"]),
        _resident((1, LANES)),
        _resident((1, A_Q_LORA)),
        _resident((A_Q_LORA, A_HEADS * LANES), lw["layer"]),
        _resident((1, A_KV_LORA)),
        _resident((1, LANES)),
        _resident((1, LANES)),
        _resident((1, LANES)),
        _resident((1, LANES)),
    ]
    args = [x, mod, lw["norm1_g"], lw["w_in"], lw["w_in_tail"], lw["b_mgate"], lw["a_qlora_g"], lw["w_uq"],
            lw["a_kvlora_g"], lw["g_knorm_g"], lw["a_qnorm_g"], lw["g_qnorm_g"], lw["a_knorm_g"]]
    if rope is not None:
        blocks_per_seq = seq // tm
        in_specs += [pl.BlockSpec((tm, LANES), lambda i: (i % blocks_per_seq, 0))] * 4
        args += [rope[0][0], rope[0][1], rope[1][0], rope[1][1]]
    out_widths = [3 * d, 3 * hw, hw, LANES, A_HEADS * LANES, A_KV_LORA, LANES, G_HEADS * G_HD,
                  G_KV_HEADS * G_HD, G_KV_HEADS * G_HD]
    out_dtypes = [BF16, BF16, BF16, F32, BF16, F32, F32, BF16, F32, F32]
    return pl.pallas_call(
        functools.partial(_inproj_kernel, rotary=rope is not None),
        out_shape=[jax.ShapeDtypeStruct((n, w), t) for w, t in zip(out_widths, out_dtypes)],
        grid=(n // tm,),
        in_specs=in_specs,
        out_specs=[rows(w) for w in out_widths],
        compiler_params=_params(1),
        name="in_proj",
    )(*args)


def _mlstm_kernel(*refs, n_chunks, heads, has_init, emit_state):
    q_ref, k_ref, v_ref, og_ref, mg_ref, gn_ref = refs[:6]
    pos = 6
    if has_init:
        c0_ref, n0_ref, m0_ref = refs[pos:pos + 3]
        pos += 3
    y_ref = refs[pos]
    pos += 1
    if emit_state:
        cf_ref, nf_ref, mf_ref = refs[pos:pos + 3]
        pos += 3
    (pcol_scr, lfc_scr, grow_scr, prow_scr, lfr_scr, u_scr, nu_scr, gm_scr, tot_scr, cs_scr, ns_scr, ms_scr,
     c_scr, n_scr, m_scr) = refs[pos:]
    assert heads == M_HEADS
    n_gates = 4 * M_HEADS

    ln = MLSTM_CHUNK
    row = lax.broadcasted_iota(jnp.int32, (ln, ln), 0)
    col = lax.broadcasted_iota(jnp.int32, (ln, ln), 1)
    lower = col <= row
    upper = col >= row
    tril = jnp.where(lower, 1.0, 0.0).astype(BF16)
    triu = jnp.where(upper, 1.0, 0.0).astype(BF16)
    ones = jnp.ones((ln, ln), BF16)

    def gate_index(hh, d):
        return 2 * d * heads + hh, (2 * d + 1) * heads + hh

    def summaries(c, carry):
        r0 = pl.multiple_of(c * ln, ln)
        g = mg_ref[pl.ds(r0, ln), :]
        lf = _log_sigmoid(g) * LOG2E
        pcol_scr[pl.ds(r0, ln), :] = _dot01_left(tril, lf)
        lfc_scr[pl.ds(r0, ln), :] = lf
        g_rows = g.T[MG_LANE0:MG_LANE0 + n_gates, :]
        gr = g_rows * LOG2E
        lfr = _log_sigmoid(g_rows) * LOG2E
        pr = _dot01(lfr, triu)
        grow_scr[c] = gr
        prow_scr[c] = pr
        lfr_scr[c] = lfr
        for hh in range(heads):
            kb = k_ref[pl.ds(r0, ln), hh * LANES:(hh + 1) * LANES]
            k_t = kb.astype(F32).T
            v = v_ref[pl.ds(r0, ln), hh * LANES:(hh + 1) * LANES]
            for d in range(2):
                ji, jf = gate_index(hh, d)
                tot = pr[jf:jf + 1, ln - 1:ln]
                b_row = pr[jf:jf + 1, :] if d == 0 else tot - pr[jf:jf + 1, :] + lfr[jf:jf + 1, :]
                g_row = tot - b_row + gr[ji:ji + 1, :]
                gmax = jnp.max(g_row, axis=-1, keepdims=True)
                wg = jnp.exp2(g_row - gmax)
                u_scr[hh, d, c] = _dot((k_t * wg).astype(BF16), v)
                nu_scr[hh, d, c] = _dot(jnp.broadcast_to(wg, (SUBLANES, ln)).astype(BF16), kb)[0:1, :]
                gm_scr[hh, d, c] = jnp.broadcast_to(gmax, (1, ln))
                tot_scr[hh, d, c] = jnp.broadcast_to(tot, (1, ln))
        return carry

    lax.fori_loop(0, n_chunks, summaries, 0, unroll=min(2, n_chunks))

    for hh in range(heads):
        for d in range(2):
            if has_init:
                c_scr[hh, d] = c0_ref[d, hh]
                n_scr[hh, d] = n0_ref[d, hh]
                m_scr[hh, d] = m0_ref[d, hh] * LOG2E
            else:
                c_scr[hh, d] = jnp.zeros((M_DK, M_DV), F32)
                n_scr[hh, d] = jnp.zeros((1, M_DK), F32)
                m_scr[hh, d] = jnp.zeros((1, ln), F32)

    def scan(j, carry):
        for hh in range(heads):
            for d in range(2):
                c = j if d == 0 else n_chunks - 1 - j
                cst = c_scr[hh, d]
                nst = n_scr[hh, d]
                mst = m_scr[hh, d]
                cs_scr[hh, d, c] = cst.astype(BF16)
                ns_scr[hh, d, c] = nst
                ms_scr[hh, d, c] = mst
                gmax = gm_scr[hh, d, c]
                total = tot_scr[hh, d, c] + mst
                m_new = jnp.maximum(total, gmax)
                decay = jnp.exp2(total - m_new)
                scale = jnp.exp2(gmax - m_new)
                c_scr[hh, d] = cst * decay + u_scr[hh, d, c] * scale
                n_scr[hh, d] = nst * decay + nu_scr[hh, d, c] * scale
                m_scr[hh, d] = m_new
        return carry

    lax.fori_loop(0, n_chunks, scan, 0)

    def readout(c, carry):
        r0 = pl.multiple_of(c * ln, ln)
        pc = pcol_scr[pl.ds(r0, ln), :]
        lf = lfc_scr[pl.ds(r0, ln), :]
        gr = grow_scr[c]
        pr = prow_scr[c]
        lfr = lfr_scr[c]
        tot_all = pc[ln - 1:ln, :]
        for hh in range(heads):
            q = q_ref[pl.ds(r0, ln), hh * LANES:(hh + 1) * LANES]
            k = k_ref[pl.ds(r0, ln), hh * LANES:(hh + 1) * LANES]
            v = v_ref[pl.ds(r0, ln), hh * LANES:(hh + 1) * LANES]
            qk = _dot_nt(q, k)
            hsum = None
            for d in range(2):
                ji, jf = gate_index(hh, d)
                lane_f = MG_LANE0 + jf
                if d == 0:
                    b_col = pc[:, lane_f:lane_f + 1]
                    r_row = gr[ji:ji + 1, :] - pr[jf:jf + 1, :]
                    mask = lower
                else:
                    tot = tot_all[:, lane_f:lane_f + 1]
                    b_col = tot - pc[:, lane_f:lane_f + 1] + lf[:, lane_f:lane_f + 1]
                    r_row = gr[ji:ji + 1, :] - (tot - pr[jf:jf + 1, :] + lfr[jf:jf + 1, :])
                    mask = upper
                b_colb = jnp.broadcast_to(b_col, (ln, ln))
                d_log = jnp.where(mask, b_colb + r_row, -jnp.inf)
                dmax = jnp.broadcast_to(jnp.max(d_log, axis=-1, keepdims=True), (ln, ln))
                inter = b_colb + ms_scr[hh, d, c]
                m_t = jnp.maximum(inter, dmax)
                sb = (qk * jnp.exp2(d_log - m_t)).astype(BF16)
                w_inter = jnp.exp2(inter - m_t)
                qc = _dot(q, cs_scr[hh, d, c])
                qn = _dot_nt(q, jnp.broadcast_to(ns_scr[hh, d, c], (ln, M_DK)).astype(BF16))
                num = qc * w_inter + _dot(sb, v)
                den = qn * w_inter + _dot(sb, ones)
                hc = num / jnp.maximum(jnp.abs(den), jnp.exp2(-m_t))
                hsum = hc if hsum is None else hsum + hc
            y_ref[pl.ds(r0, ln), hh * LANES:(hh + 1) * LANES] = (
                _rms(hsum, M_DV) * gn_ref[...] * og_ref[pl.ds(r0, ln), hh * LANES:(hh + 1) * LANES]
            ).astype(y_ref.dtype)
        return carry

    lax.fori_loop(0, n_chunks, readout, 0, unroll=min(2, n_chunks))

    if emit_state:
        for hh in range(heads):
            for d in range(2):
                cf_ref[d, hh] = c_scr[hh, d]
                nf_ref[d, hh] = n_scr[hh, d]
                mf_ref[d, hh] = m_scr[hh, d] * (1.0 / LOG2E)


def _mlstm_call(qkv, og, mg, lw, batch, seq, init, emit_state):
    n = batch * seq
    ln = MLSTM_CHUNK
    nc = seq // ln
    h = M_HEADS
    hs = MLSTM_HEADS_PER_STEP
    ng = h // hs

    def tok(colblock):
        return pl.BlockSpec((seq, hs * LANES), lambda b, hg: (b, colblock(hg)))

    in_specs = [
        tok(lambda hg: hg), tok(lambda hg: ng + hg), tok(lambda hg: 2 * ng + hg),
        tok(lambda hg: hg),
        pl.BlockSpec((seq, LANES), lambda b, hg: (b, 0)),
        pl.BlockSpec((1, M_DV), lambda b, hg: (0, 0)),
    ]
    args = [qkv, qkv, qkv, og, mg, lw["m_norm_g"]]
    state_specs = [
        pl.BlockSpec((None, 2, hs, M_DK, M_DV), lambda b, hg: (b, 0, hg, 0, 0)),
        pl.BlockSpec((None, 2, hs, 1, M_DK), lambda b, hg: (b, 0, hg, 0, 0)),
        pl.BlockSpec((None, 2, hs, 1, ln), lambda b, hg: (b, 0, hg, 0, 0)),
    ]
    if init is not None:
        c_all, n_all, m_all, layer = init
        depth = c_all.shape[1]
        in_specs += [
            pl.BlockSpec((None, None, 2, hs, M_DK, M_DV), lambda b, hg: (b, layer, 0, hg, 0, 0)),
            pl.BlockSpec((None, None, 2, hs, 1, M_DK), lambda b, hg: (b, layer, 0, hg, 0, 0)),
            pl.BlockSpec((None, None, 2, hs, 1, ln), lambda b, hg: (b, layer, 0, hg, 0, 0)),
        ]
        args += [c_all, n_all.reshape(batch, depth, 2, h, 1, M_DK),
                 jnp.broadcast_to(m_all[..., None, None], (batch, depth, 2, h, 1, ln))]
    out_shape = [jax.ShapeDtypeStruct((n, h * M_DV), BF16)]
    out_specs = [tok(lambda hg: hg)]
    if emit_state:
        out_shape += [
            jax.ShapeDtypeStruct((batch, 2, h, M_DK, M_DV), F32),
            jax.ShapeDtypeStruct((batch, 2, h, 1, M_DK), F32),
            jax.ShapeDtypeStruct((batch, 2, h, 1, ln), F32),
        ]
        out_specs += state_specs
    per = (hs, 2, nc)
    outs = pl.pallas_call(
        functools.partial(_mlstm_kernel, n_chunks=nc, heads=hs, has_init=init is not None,
                          emit_state=emit_state),
        out_shape=out_shape,
        grid=(batch, ng),
        in_specs=in_specs,
        out_specs=out_specs,
        scratch_shapes=[
            pltpu.VMEM((seq, LANES), F32), pltpu.VMEM((seq, LANES), F32),
            pltpu.VMEM((nc, 4 * h, ln), F32), pltpu.VMEM((nc, 4 * h, ln), F32), pltpu.VMEM((nc, 4 * h, ln), F32),
            pltpu.VMEM(per + (M_DK, M_DV), F32), pltpu.VMEM(per + (1, M_DK), F32),
            pltpu.VMEM(per + (1, ln), F32), pltpu.VMEM(per + (1, ln), F32),
            pltpu.VMEM(per + (M_DK, M_DV), BF16), pltpu.VMEM(per + (1, M_DK), F32), pltpu.VMEM(per + (1, ln), F32),
            pltpu.VMEM((hs, 2, M_DK, M_DV), F32), pltpu.VMEM((hs, 2, 1, M_DK), F32), pltpu.VMEM((hs, 2, 1, ln), F32),
        ],
        compiler_params=_params(2),
        name="mlstm",
    )(*args)
    if emit_state:
        ym, cf, nf, mf = outs
        return ym, (cf, nf[:, :, :, 0, :], mf[:, :, :, 0, 0])
    return outs[0], None


HEAD_V = 64
VT_ROWS = 80


def _ones_row(v_t):
    return jnp.where(lax.broadcasted_iota(jnp.int32, v_t.shape, 0) == HEAD_V, 1.0, v_t)


def _flash_heads(qs, keys_of, vt_chunk, n_keys, o_ref):
    def values(e, s_t):
        m = acc = None
        for c0 in range(0, n_keys, KEY_BLOCK):
            c1 = min(c0 + KEY_BLOCK, n_keys)
            s = s_t[c0:c1, :]
            mc = jnp.max(s, axis=0, keepdims=True)
            if c0 == 0:
                m = mc
                acc = _dot(vt_chunk(e, c0, c1), jnp.exp2(s - mc).astype(BF16))
            else:
                m_new = jnp.maximum(m, mc)
                acc = acc * jnp.exp2(m - m_new) + _dot(vt_chunk(e, c0, c1), jnp.exp2(s - m_new).astype(BF16))
                m = m_new
        return acc[0:HEAD_V, :] / acc[HEAD_V:HEAD_V + 1, :]

    if n_keys <= KEY_BLOCK:
        s_all = [_dot_nt(keys_of(e), qb) for e, qb in enumerate(qs)]
        outs = [values(e, s_t) for e, s_t in enumerate(s_all)]
    else:
        outs = []
        s_next = _dot_nt(keys_of(0), qs[0])
        for e in range(len(qs)):
            s_t = s_next
            if e + 1 < len(qs):
                s_next = _dot_nt(keys_of(e + 1), qs[e + 1])
            outs.append(values(e, s_t))
    for e in range(0, len(outs), 2):
        pair = jnp.concatenate([outs[e], outs[e + 1]], axis=0)
        o_ref[:, (e // 2) * LANES:(e // 2 + 1) * LANES] = pair.T.astype(o_ref.dtype)


def _mla_kernel(*refs, past, rotary, heads):
    q_ref, ckv_ref, akr_ref = refs[:3]
    pos = 3
    if rotary:
        mg_ref = refs[pos]
        pos += 1
    if past:
        ckvc_ref, krc_ref = refs[pos:pos + 2]
        pos += 2
    wk_ref, wvt_ref, kg_ref = refs[pos:pos + 3]
    pos += 3
    o_ref, kbuf, vbuf = refs[pos:]
    seq = ckv_ref.shape[0]

    @pl.when(pl.program_id(2) == 0)
    def _():
        for e in range(heads):
            wk_h = wk_ref[:, e * LANES:(e + 1) * LANES]
            wvt_h = wvt_ref[e * VT_ROWS:(e + 1) * VT_ROWS, :]

            def keys(ckv, kr, kr_ss):
                cb = ckv.astype(BF16)
                k_nope = _dot(cb, wk_h)
                ss = jnp.sum(k_nope * k_nope, axis=-1, keepdims=True)
                if kr_ss is None:
                    ss = ss + jnp.sum(kr * kr, axis=-1, keepdims=True)
                    kcat = (k_nope + kr) * kg_ref[...]
                else:
                    ss = ss + kr_ss
                    kcat = k_nope * kg_ref[...] + kr
                return _ones_row(_dot_nt(wvt_h, cb)), kcat * lax.rsqrt(ss * (1.0 / A_QK) + EPS)

            kr_ss = mg_ref[:, KR_SS_LANE:KR_SS_LANE + 1] if rotary else None
            v_t, kn = keys(ckv_ref[...], akr_ref[...], kr_ss)
            kbuf[e, past:past + seq, :] = kn.astype(BF16)
            vbuf[e, :, past:past + seq] = v_t.astype(BF16)
            if past:
                v_tc, knc = keys(ckvc_ref[...], krc_ref[...], None)
                kbuf[e, 0:past, :] = knc.astype(BF16)
                vbuf[e, :, 0:past] = v_tc.astype(BF16)

    qs = [q_ref[:, e * LANES:(e + 1) * LANES] for e in range(heads)]
    _flash_heads(qs, lambda e: kbuf[e], lambda e, c0, c1: vbuf[e, :, c0:c1], past + seq, o_ref)


def _mla_call(qa, ckv, akr, mg, lw, batch, seq, cache, rotary):
    n = batch * seq
    tq = min(Q_BLOCK, seq)
    nq = seq // tq
    past = 0 if cache is None else cache[0].shape[2]
    heads = MLA_HEADS_PER_STEP if nq > 1 else A_HEADS
    in_specs = [
        pl.BlockSpec((tq, heads * LANES), lambda b, hg, qi: (b * nq + qi, hg)),
        pl.BlockSpec((seq, LANES), lambda b, hg, qi: (b, 0)),
        pl.BlockSpec((seq, LANES), lambda b, hg, qi: (b, 0)),
    ]
    args = [qa, ckv, akr]
    if rotary:
        in_specs += [pl.BlockSpec((seq, LANES), lambda b, hg, qi: (b, 0))]
        args += [mg]
    layer = lw["layer"]
    if past:
        in_specs += [pl.BlockSpec((None, None, past, LANES), lambda b, hg, qi: (b, layer, 0, 0))] * 2
        args += [cache[0], cache[1]]
    in_specs += [
        pl.BlockSpec((None, A_KV_LORA, heads * LANES), lambda b, hg, qi: (layer, 0, hg)),
        pl.BlockSpec((None, heads * VT_ROWS, A_KV_LORA), lambda b, hg, qi: (layer, hg, 0)),
        pl.BlockSpec((1, LANES), lambda b, hg, qi: (0, 0)),
    ]
    args += [lw["w_uk"], lw["w_uvt"], lw["a_knorm_g"]]
    return pl.pallas_call(
        functools.partial(_mla_kernel, past=past, rotary=rotary, heads=heads),
        out_shape=jax.ShapeDtypeStruct((n, A_HEADS * A_V), BF16),
        grid=(batch, A_HEADS // heads, nq),
        in_specs=in_specs,
        out_specs=pl.BlockSpec((tq, heads * A_V), lambda b, hg, qi: (b * nq + qi, hg)),
        scratch_shapes=[pltpu.VMEM((heads, past + seq, LANES), BF16),
                        pltpu.VMEM((heads, VT_ROWS, past + seq), BF16)],
        compiler_params=_params(3),
        name="mla_attn",
    )(*args)


def _gqa_kernel(*refs, past, groups):
    q_ref, k_ref, v_ref = refs[:3]
    pos = 3
    if past:
        kc_ref, vc_ref = refs[pos:pos + 2]
        pos += 2
    o_ref, kbuf, vbuf = refs[pos:]
    seq = k_ref.shape[0]
    qw = G_GROUP * G_HD

    @pl.when(pl.program_id(2) == 0)
    def _():
        for gi in range(groups):
            grp = pl.program_id(1) * groups + gi

            r = lax.broadcasted_iota(jnp.int32, (VT_ROWS, LANES), 0)
            c = lax.broadcasted_iota(jnp.int32, (VT_ROWS, LANES), 1)
            pick = jnp.where((c == r + grp * G_HD) & (r < G_HD), 1.0, 0.0).astype(BF16)

            def fill(r0, k, v):
                low = _lane(k.shape) < G_HD
                k_lo = jnp.where(grp == 0, k, pltpu.roll(k, G_HD, 1))
                rows = k.shape[0]
                kbuf[gi, r0:r0 + rows, :] = jnp.where(low, k_lo, pltpu.roll(k_lo, G_HD, 1)).astype(BF16)
                vbuf[gi, :, r0:r0 + rows] = _ones_row(_dot_nt(pick, v.astype(BF16))).astype(BF16)

            fill(past, k_ref[...], v_ref[...])
            if past:
                fill(0, kc_ref[...], vc_ref[...])

    qs = []
    for gi in range(groups):
        for j in range(G_GROUP):
            col = q_ref[:, gi * qw + (j // 2) * LANES:gi * qw + (j // 2 + 1) * LANES]
            keep = (_lane(col.shape) < G_HD) == (j % 2 == 0)
            qs.append(jnp.where(keep, col, jnp.zeros_like(col)))
    _flash_heads(qs, lambda e: kbuf[e // G_GROUP], lambda e, c0, c1: vbuf[e // G_GROUP, :, c0:c1],
                 past + seq, o_ref)


def _gqa_call(gq, gk, gv, batch, seq, cache):
    n = batch * seq
    tq = min(Q_BLOCK, seq)
    nq = seq // tq
    past = 0 if cache is None else cache[0].shape[2]
    groups = GQA_GROUPS_PER_STEP if nq > 1 else G_KV_HEADS
    qw = G_GROUP * G_HD
    kvw = G_KV_HEADS * G_HD
    in_specs = [
        pl.BlockSpec((tq, groups * qw), lambda b, g, qi: (b * nq + qi, g)),
        pl.BlockSpec((seq, kvw), lambda b, g, qi: (b, 0)),
        pl.BlockSpec((seq, kvw), lambda b, g, qi: (b, 0)),
    ]
    args = [gq, gk, gv]
    if past:
        layer = cache[2]
        in_specs += [pl.BlockSpec((None, None, past, kvw), lambda b, g, qi: (b, layer, 0, 0))] * 2
        args += [cache[0], cache[1]]
    return pl.pallas_call(
        functools.partial(_gqa_kernel, past=past, groups=groups),
        out_shape=jax.ShapeDtypeStruct((n, G_HEADS * G_HD), BF16),
        grid=(batch, G_KV_HEADS // groups, nq),
        in_specs=in_specs,
        out_specs=pl.BlockSpec((tq, groups * qw), lambda b, g, qi: (b * nq + qi, g)),
        scratch_shapes=[pltpu.VMEM((groups, past + seq, LANES), BF16),
                        pltpu.VMEM((groups, VT_ROWS, past + seq), BF16)],
        compiler_params=_params(3),
        name="gqa_attn",
    )(*args)


def _merge_ffn_kernel(x_ref, ym_ref, ya_ref, yg_ref, gates_ref, mod_ref, g2_ref, wb_ref, wo_ref, wfi_ref,
                      wfo_ref, o_ref, *, ff_chunks):
    x = x_ref[...]
    d = x.shape[-1]
    mixed = None
    for i, y_ref in enumerate((ym_ref, ya_ref, yg_ref)):
        br = _dot(y_ref[...].astype(BF16), wb_ref[i]) * gates_ref[:, i * d:(i + 1) * d]
        mixed = br if mixed is None else mixed + br
    gt1 = mod_ref[2:3, :]
    x1 = x + gt1 * _dot(mixed.astype(BF16), wo_ref[...])

    sh2 = mod_ref[3:4, :]
    sc2 = mod_ref[4:5, :]
    gt2 = mod_ref[5:6, :]
    h2 = ((_rms(x1, d) * g2_ref[...]) * (1.0 + sc2) + sh2).astype(BF16)
    d_ff = wfo_ref.shape[0]
    acc = None
    for c0, c1 in ff_chunks:
        ug = _dot(h2, wfi_ref[:, c0:c1])
        uv = _dot(h2, wfi_ref[:, d_ff + c0:d_ff + c1])
        act = (ug * _sigmoid(ug) * uv).astype(BF16)
        part = _dot(act, wfo_ref[c0:c1, :])
        acc = part if acc is None else acc + part
    o_ref[...] = x1 + gt2 * acc


def _merge_ffn_call(x, ym, ya, yg, gates, mod, lw, mod_row0, rows_per_mod):
    n, d = x.shape
    tm = TOKEN_BLOCK
    d_ff = lw["w_ffn_out"].shape[1]
    layer = lw["layer"]
    split = -(-(d_ff // 2) // MXU_COLS) * MXU_COLS
    ff_chunks = ((0, split), (split, d_ff)) if 0 < split < d_ff else ((0, d_ff),)

    def mod_idx(i):
        return (mod_row0 + (i * tm) // rows_per_mod, 0, 0)

    def rows(width):
        return pl.BlockSpec((tm, width), lambda i: (i, 0))

    return pl.pallas_call(
        functools.partial(_merge_ffn_kernel, ff_chunks=ff_chunks),
        out_shape=jax.ShapeDtypeStruct((n, d), F32),
        grid=(n // tm,),
        in_specs=[
            rows(d), rows(BRANCH_WIDTH), rows(BRANCH_WIDTH), rows(BRANCH_WIDTH), rows(N_BRANCH * d),
            pl.BlockSpec((None, N_MOD, d), mod_idx),
            _resident((1, d)),
            _resident((N_BRANCH, BRANCH_WIDTH, d), layer),
            _resident((d, d), layer),
            _resident((d, 2 * d_ff), layer),
            _resident((d_ff, d), layer),
        ],
        out_specs=rows(d),
        compiler_params=_params(1),
        name="merge_ffn",
    )(x, ym, ya, yg, gates, mod, lw["norm2_g"], lw["w_branch"], lw["w_out"], lw["w_ffn_in"], lw["w_ffn_out"])


def _mla_q_layout(a):
    lead = a.shape[:-1]
    a = a.reshape(lead + (A_HEADS, A_QK))
    z = jnp.zeros(lead + (A_HEADS, LANES - A_QK), a.dtype)
    return jnp.concatenate([a[..., A_NOPE:], z, a[..., :A_NOPE]], axis=-1).reshape(lead + (A_HEADS * LANES,))


def _stacked_weights(w_in, w_uq, w_ukv, w_branch, w_out, w_ffn_in, w_ffn_out):
    depth, d, _ = w_in.shape
    hw = M_HEADS * M_DK
    sizes = (N_BRANCH * d, hw, hw, hw, hw, 4 * M_HEADS, A_Q_LORA, A_KV_LORA, A_ROPE, G_HEADS * G_HD,
             G_KV_HEADS * G_HD, G_KV_HEADS * G_HD)
    edges = np.concatenate([[0], np.cumsum(sizes)]).tolist()
    (wmg, waq, wakv, wakr, wgq, wgk, wgv) = [w_in[:, :, a:b] for a, b in zip(edges[5:-1], edges[6:])]
    misc_pad = jnp.zeros((depth, d, LANES - A_ROPE - 4 * M_HEADS), w_in.dtype)
    tail = jnp.concatenate([waq, wakv, wakr, wmg, misc_pad, wgq, wgk, wgv], axis=2).astype(BF16)
    ukv = w_ukv.reshape(depth, A_KV_LORA, A_HEADS, A_NOPE + A_V)
    uk = jnp.pad(ukv[..., :A_NOPE], ((0, 0), (0, 0), (0, 0), (LANES - A_NOPE, 0)))
    uvt = jnp.pad(ukv[..., A_NOPE:].transpose(0, 2, 3, 1), ((0, 0), (0, 0), (0, VT_ROWS - A_V), (0, 0)))
    return dict(
        w_in=w_in.astype(BF16),
        w_in_tail=tail,
        w_uq=_mla_q_layout(w_uq).astype(BF16),
        w_uk=uk.reshape(depth, A_KV_LORA, A_HEADS * LANES).astype(BF16),
        w_uvt=uvt.reshape(depth, A_HEADS * VT_ROWS, A_KV_LORA).astype(BF16),
        w_branch=w_branch.astype(BF16),
        w_out=w_out.astype(BF16),
        w_ffn_in=w_ffn_in.astype(BF16),
        w_ffn_out=w_ffn_out.astype(BF16),
    )


def _layer_vectors(l, b_mgate, norm1_g, m_norm_g, a_qlora_g, a_kvlora_g, a_qnorm_g, a_knorm_g, g_qnorm_g,
                   g_knorm_g, norm2_g):
    return dict(
        layer=l,
        b_mgate=jnp.pad(b_mgate[l][None, :], ((0, 0), (MG_LANE0, LANES - MG_LANE0 - 4 * M_HEADS))),
        norm1_g=norm1_g[l][None, :],
        m_norm_g=m_norm_g[l][None, :],
        a_qlora_g=a_qlora_g[l][None, :],
        a_kvlora_g=a_kvlora_g[l][None, :],
        a_qnorm_g=_mla_q_layout(jnp.tile(a_qnorm_g[l], A_HEADS)[None, :])[:, :LANES],
        a_knorm_g=_mla_q_layout(jnp.tile(a_knorm_g[l], A_HEADS)[None, :])[:, :LANES],
        g_qnorm_g=jnp.tile(g_qnorm_g[l], LANES // G_HD)[None, :],
        g_knorm_g=jnp.tile(g_knorm_g[l], G_KV_HEADS)[None, :],
        norm2_g=norm2_g[l][None, :],
    )


def _axial_angles(seq, rot_dim):
    n_freq = rot_dim // 4
    freqs = ROPE_BASE ** (-jnp.arange(n_freq, dtype=F32) / n_freq)
    t = jnp.arange(seq)
    row = (t // GRID_W).astype(F32)
    col = (t % GRID_W).astype(F32)
    return jnp.concatenate([row[:, None] * freqs, col[:, None] * freqs], axis=-1)


def _rope_tables(seq):
    ang = _axial_angles(seq, A_ROPE)
    one = jnp.ones((seq, LANES - A_ROPE), F32)
    mla_cos = jnp.concatenate([jnp.cos(ang), jnp.cos(ang), one], axis=-1)
    mla_sin = jnp.concatenate([-jnp.sin(ang), jnp.sin(ang), 0.0 * one], axis=-1)
    ang = _axial_angles(seq, G_HD)
    cos = jnp.concatenate([jnp.cos(ang), jnp.cos(ang)], axis=-1)
    sin = jnp.concatenate([-jnp.sin(ang), jnp.sin(ang)], axis=-1)
    gqa = (jnp.tile(cos, (1, LANES // G_HD)), jnp.tile(sin, (1, LANES // G_HD)))
    return (mla_cos, mla_sin), gqa


def _layer(x, mod, lw, batch, seq, mod_row0, rows_per_mod, ctx, rope):
    (gates, qkv, og, mg, qa, ckv, akr, gq, gk, gv) = _inproj_call(x, mod, lw, mod_row0, rows_per_mod, rope, seq)
    if ctx is None:
        ym, state = _mlstm_call(qkv, og, mg, lw, batch, seq, None, True)
        ya = _mla_call(qa, ckv, akr, mg, lw, batch, seq, None, False)
        yg = _gqa_call(gq, gk, gv, batch, seq, None)
        new_ctx = dict(state=state, ckv=ckv, kr=akr[:, :A_ROPE], gk=gk, gv=gv)
    else:
        ym, _ = _mlstm_call(qkv, og, mg, lw, batch, seq, ctx["mlstm"], False)
        ya = _mla_call(qa, ckv, akr, mg, lw, batch, seq, ctx["mla"], True)
        yg = _gqa_call(gq, gk, gv, batch, seq, ctx["gqa"])
        new_ctx = None
    x = _merge_ffn_call(x, ym, ya, yg, gates, mod, lw, mod_row0, rows_per_mod)
    return x, new_ctx


def kernel(x_prompt, x_sample, state_mlstm_C, state_mlstm_n, state_mlstm_m, cache_mla_ckv, cache_mla_krope,
           cache_gqa_k, cache_gqa_v, c, c_ctx, w_mod, b_mod, norm1_g, w_in, b_mgate, m_norm_g, a_qlora_g,
           a_kvlora_g, w_uq, w_ukv, a_qnorm_g, a_knorm_g, g_qnorm_g, g_knorm_g, w_branch, w_out, norm2_g,
           w_ffn_in, w_ffn_out):
    batch, seq, d = x_prompt.shape
    dbatch, dseq, _ = x_sample.shape
    depth = w_in.shape[0]
    past = cache_mla_ckv.shape[2]

    n_rows = -(-(1 + dbatch) // SUBLANES) * SUBLANES
    cond = jnp.concatenate([c_ctx[None, :], c, jnp.zeros((n_rows - 1 - dbatch, d), F32)], axis=0)
    mod_all = _mod_call(cond, w_mod, b_mod).reshape(depth, n_rows, N_MOD, d)

    rope = _rope_tables(dseq)
    xp = x_prompt.reshape(batch * seq, d)
    xs = x_sample.reshape(dbatch * dseq, d)
    ctx_layers = []
    krope_cache = jnp.pad(cache_mla_krope, ((0, 0), (0, 0), (0, 0), (0, LANES - A_ROPE)))
    gqa_k_cache = cache_gqa_k.reshape(dbatch, depth, past, G_KV_HEADS * G_HD)
    gqa_v_cache = cache_gqa_v.reshape(dbatch, depth, past, G_KV_HEADS * G_HD)
    stacked = _stacked_weights(w_in, w_uq, w_ukv, w_branch, w_out, w_ffn_in, w_ffn_out)
    for l in range(depth):
        lw = dict(stacked, **_layer_vectors(l, b_mgate, norm1_g, m_norm_g, a_qlora_g, a_kvlora_g, a_qnorm_g,
                                            a_knorm_g, g_qnorm_g, g_knorm_g, norm2_g))
        xp, st = _layer(xp, mod_all[l], lw, batch, seq, 0, batch * seq, None, None)
        ctx_layers.append(st)
        ctx = dict(mlstm=(state_mlstm_C, state_mlstm_n, state_mlstm_m, l), mla=(cache_mla_ckv, krope_cache),
                   gqa=(gqa_k_cache, gqa_v_cache, l))
        xs, _ = _layer(xs, mod_all[l], lw, dbatch, dseq, 1, dseq, ctx, rope)

    def stack(fn):
        return jnp.stack([fn(s) for s in ctx_layers], axis=1)

    new_c = stack(lambda s: s["state"][0])
    new_n = stack(lambda s: s["state"][1])
    new_m = stack(lambda s: s["state"][2])
    new_ckv = stack(lambda s: s["ckv"].reshape(batch, seq, A_KV_LORA))
    new_kr = stack(lambda s: s["kr"].reshape(batch, seq, A_ROPE))
    new_gk = stack(lambda s: s["gk"].reshape(batch, seq, G_KV_HEADS, G_HD))
    new_gv = stack(lambda s: s["gv"].reshape(batch, seq, G_KV_HEADS, G_HD))
    return (xp.reshape(batch, seq, d), xs.reshape(dbatch, dseq, d), new_c, new_n, new_m, new_ckv, new_kr,
            new_gk, new_gv)
```

```python
import functools

import numpy as np
import jax
import jax.numpy as jnp
from jax import lax
from jax.experimental import pallas as pl
from jax.experimental.pallas import tpu as pltpu

F32 = jnp.float32
BF16 = jnp.bfloat16

LANES = 128
SUBLANES = 8
VMEM_LIMIT_BYTES = 56 * 1024 * 1024

EPS = 1e-6
ROPE_BASE = 10000.0
GRID_W = 64

M_HEADS = 4
M_DK = 128
M_DV = 128
A_HEADS = 8
A_NOPE = 64
A_ROPE = 32
A_QK = A_NOPE + A_ROPE
A_V = 64
A_Q_LORA = 256
A_KV_LORA = 128
G_HEADS = 8
G_KV_HEADS = 2
G_GROUP = G_HEADS // G_KV_HEADS
G_HD = 64
N_BRANCH = 3
BRANCH_WIDTH = 512
N_MOD = 6

TOKEN_BLOCK = 512
Q_BLOCK = 256
KEY_BLOCK = 256
MLA_HEADS_PER_STEP = 8
GQA_GROUPS_PER_STEP = 2
MLSTM_CHUNK = 128
MLSTM_HEADS_PER_STEP = 4
LOG2E = 1.4426950408889634


def _params(n_axes):
    return pltpu.CompilerParams(dimension_semantics=("arbitrary",) * n_axes,
                                vmem_limit_bytes=VMEM_LIMIT_BYTES)


def _resident(shape, layer=None):
    nd = len(shape)
    if layer is None:
        return pl.BlockSpec(shape, lambda *_: (0,) * nd, pipeline_mode=pl.Buffered(1))
    return pl.BlockSpec((None,) + tuple(shape), lambda *_: (layer,) + (0,) * nd, pipeline_mode=pl.Buffered(1))


def _lane(shape, axis=None):
    return lax.broadcasted_iota(jnp.int32, shape, len(shape) - 1 if axis is None else axis)


def _dot(a, b):
    return jnp.dot(a, b, preferred_element_type=F32)


def _dot_nt(a, b):
    return lax.dot_general(a, b, (((1,), (1,)), ((), ())), preferred_element_type=F32)


def _split3(a):
    hi = a.astype(BF16)
    r1 = a - hi.astype(F32)
    mid = r1.astype(BF16)
    lo = (r1 - mid.astype(F32)).astype(BF16)
    return hi, mid, lo


def _dot01(a, m01):
    hi, mid, lo = _split3(a)
    return _dot(hi, m01) + _dot(mid, m01) + _dot(lo, m01)


def _dot01_left(m01, a):
    hi, mid, lo = _split3(a)
    return _dot(m01, hi) + _dot(m01, mid) + _dot(m01, lo)


def _sigmoid(x):
    return 0.5 * jnp.tanh(0.5 * x) + 0.5


def _log_sigmoid(x):
    return jnp.minimum(x, 0.0) - jnp.log(1.0 + jnp.exp(-jnp.abs(x)))


def _rms(x, width):
    ms = jnp.sum(x * x, axis=-1, keepdims=True) * (1.0 / width)
    return x * lax.rsqrt(ms + EPS)


def _rope(x, cos, sin_signed, half, period):
    n = x.shape[-1]
    first = (_lane(x.shape) % period) < half
    swapped = jnp.where(first, pltpu.roll(x, n - half, x.ndim - 1), pltpu.roll(x, half, x.ndim - 1))
    return x * cos + swapped * sin_signed


def _mod_kernel(c_ref, w_ref, b_ref, o_ref):
    c = c_ref[...]
    a = c * _sigmoid(c)
    o_ref[...] = _dot01(a, w_ref[...].astype(BF16)) + b_ref[...]


def _mod_call(cond, w_mod, b_mod):
    depth, d, n = w_mod.shape
    rows = cond.shape[0]
    tn = 1536
    return pl.pallas_call(
        _mod_kernel,
        out_shape=jax.ShapeDtypeStruct((depth, rows, n), F32),
        grid=(depth, n // tn),
        in_specs=[
            pl.BlockSpec((rows, d), lambda l, j: (0, 0)),
            pl.BlockSpec((None, d, tn), lambda l, j: (l, 0, j)),
            pl.BlockSpec((None, 1, tn), lambda l, j: (l, 0, j)),
        ],
        out_specs=pl.BlockSpec((None, rows, tn), lambda l, j: (l, 0, j)),
        compiler_params=_params(2),
        name="adaln_mod",
    )(cond, w_mod, b_mod.reshape(depth, 1, n))


MXU_COLS = 256
_GATE0, _GATE1 = 0, 3072
_MQ0 = 3072
_MK0 = 3584
_MV0 = 4096
_MO0 = 4608
_AQ0 = 5120
_AKV0 = 5376
_MISC0 = 5504
_GQ0 = 5632
_GK0 = 6144
_GV0 = 6272
_WIN_COLS = 6400
MG_LANE0 = A_ROPE
KR_SS_LANE = 64


def _head_pair_ms(x):
    low = (_lane(x.shape) % LANES) < G_HD
    sq = x * x
    cols = []
    for c0 in range(0, x.shape[-1], LANES):
        s = sq[:, c0:c0 + LANES]
        lo = jnp.sum(jnp.where(low[:, c0:c0 + LANES], s, 0.0), axis=-1, keepdims=True)
        hi = jnp.sum(jnp.where(low[:, c0:c0 + LANES], 0.0, s), axis=-1, keepdims=True)
        cols.append(jnp.where(low[:, c0:c0 + LANES], lo, hi))
    ms = cols[0] if len(cols) == 1 else jnp.concatenate(cols, axis=-1)
    return ms * (1.0 / G_HD)


def _inproj_kernel(*refs, rotary):
    (x_ref, mod_ref, g1_ref, w_ref, wt_ref, bmg_ref, gql_ref, wuq_ref, gkvl_ref, gkn_ref, aqn_ref,
     gqn_ref, akn_ref) = refs[:13]
    pos = 13
    if rotary:
        ca_ref, sa_ref, cg_ref, sg_ref = refs[pos:pos + 4]
        pos += 4
    (gates_ref, qkv_ref, og_ref, mg_ref, qa_ref, ckv_ref, akr_ref, gq_ref, gk_ref, gv_ref) = refs[pos:]
    x = x_ref[...]
    d = x.shape[-1]
    sh1 = mod_ref[0:1, :]
    sc1 = mod_ref[1:2, :]
    h = (_rms(x, d) * g1_ref[...]) * (1.0 + sc1) + sh1
    hb = h.astype(BF16)

    def proj(c0, width):
        if c0 < _AQ0:
            return _dot(hb, w_ref[:, c0:c0 + width])
        return _dot(hb, wt_ref[:, c0 - _AQ0:c0 - _AQ0 + width])

    aq = _rms(proj(_AQ0, A_Q_LORA), A_Q_LORA) * gql_ref[...]
    qa = _dot(aq.astype(BF16), wuq_ref[...])
    for hd in range(A_HEADS):
        qh = _rms(qa[:, hd * LANES:(hd + 1) * LANES], A_QK) * aqn_ref[...]
        if rotary:
            qh = _rope(qh, ca_ref[...], sa_ref[...], A_ROPE // 2, LANES)
        qa_ref[:, hd * LANES:(hd + 1) * LANES] = (qh * (A_QK ** -0.5 * LOG2E)).astype(qa_ref.dtype)
    akv_misc = proj(_AKV0, A_KV_LORA + LANES)
    ckv_ref[...] = _rms(akv_misc[:, :A_KV_LORA], A_KV_LORA) * gkvl_ref[...]
    misc = akv_misc[:, A_KV_LORA:]
    kr = jnp.where(_lane(misc.shape) < A_ROPE, misc, 0.0)
    mg = misc + bmg_ref[...]
    if rotary:
        akr_ref[...] = _rope(kr * akn_ref[...], ca_ref[...], sa_ref[...], A_ROPE // 2, LANES)
        mg = jnp.where(_lane(mg.shape) == KR_SS_LANE, jnp.sum(kr * kr, axis=-1, keepdims=True), mg)
    else:
        akr_ref[...] = kr
    mg_ref[...] = mg

    gq_all = proj(_GQ0, G_HEADS * G_HD)
    for c0 in range(0, G_HEADS * G_HD, LANES):
        gq = gq_all[:, c0:c0 + LANES]
        gq = gq * lax.rsqrt(_head_pair_ms(gq) + EPS) * gqn_ref[...]
        if rotary:
            gq = _rope(gq, cg_ref[...], sg_ref[...], G_HD // 2, G_HD)
        gq_ref[:, c0:c0 + LANES] = (gq * (G_HD ** -0.5 * LOG2E)).astype(gq_ref.dtype)
    kvw = G_KV_HEADS * G_HD
    gkv = proj(_GK0, 2 * kvw)
    gk = gkv[:, :kvw]
    gk = gk * lax.rsqrt(_head_pair_ms(gk) + EPS) * gkn_ref[...]
    if rotary:
        gk = _rope(gk, cg_ref[...], sg_ref[...], G_HD // 2, G_HD)
    gk_ref[...] = gk
    gv_ref[...] = gkv[:, kvw:]

    hw = M_HEADS * M_DK
    gates_ref[...] = _sigmoid(proj(_GATE0, _GATE1 - _GATE0)).astype(gates_ref.dtype)
    og_ref[...] = _sigmoid(proj(_MO0, hw)).astype(og_ref.dtype)
    qkv_ref[:, 0:hw] = proj(_MQ0, hw).astype(qkv_ref.dtype)
    qkv_ref[:, hw:2 * hw] = (proj(_MK0, hw) * (M_DK ** -0.5)).astype(qkv_ref.dtype)
    qkv_ref[:, 2 * hw:3 * hw] = proj(_MV0, hw).astype(qkv_ref.dtype)


def _inproj_call(x, mod, lw, mod_row0, rows_per_mod, rope, seq):
    n, d = x.shape
    tm = TOKEN_BLOCK
    hw = M_HEADS * M_DK

    def mod_idx(i):
        return (mod_row0 + (i * tm) // rows_per_mod, 0, 0)

    def rows(width):
        return pl.BlockSpec((tm, width), lambda i: (i, 0))

    in_specs = [
        rows(d),
        pl.BlockSpec((None, N_MOD, d), mod_idx),
        _resident((1, d)),
        _resident(lw["w_in"].shape[1:], lw["layer"]),
        _resident((d, _WIN_COLS - _AQ0), lw["layer"]),
        _resident((1, LANES)),
        _resident((1, A_Q_LORA)),
        _resident((A_Q_LORA, A_HEADS * LANES), lw["layer"]),
        _resident((1, A_KV_LORA)),
        _resident((1, LANES)),
        _resident((1, LANES)),
        _resident((1, LANES)),
        _resident((1, LANES)),
    ]
    args = [x, mod, lw["norm1_g"], lw["w_in"], lw["w_in_tail"], lw["b_mgate"], lw["a_qlora_g"], lw["w_uq"],
            lw["a_kvlora_g"], lw["g_knorm_g"], lw["a_qnorm_g"], lw["g_qnorm_g"], lw["a_knorm_g"]]
    if rope is not None:
        blocks_per_seq = seq // tm
        in_specs += [pl.BlockSpec((tm, LANES), lambda i: (i % blocks_per_seq, 0))] * 4
        args += [rope[0][0], rope[0][1], rope[1][0], rope[1][1]]
    out_widths = [3 * d, 3 * hw, hw, LANES, A_HEADS * LANES, A_KV_LORA, LANES, G_HEADS * G_HD,
                  G_KV_HEADS * G_HD, G_KV_HEADS * G_HD]
    out_dtypes = [BF16, BF16, BF16, F32, BF16, F32, F32, BF16, F32, F32]
    return pl.pallas_call(
        functools.partial(_inproj_kernel, rotary=rope is not None),
        out_shape=[jax.ShapeDtypeStruct((n, w), t) for w, t in zip(out_widths, out_dtypes)],
        grid=(n // tm,),
        in_specs=in_specs,
        out_specs=[rows(w) for w in out_widths],
        compiler_params=_params(1),
        name="in_proj",
    )(*args)


def _mlstm_kernel(*refs, n_chunks, heads, has_init, emit_state):
    q_ref, k_ref, v_ref, og_ref, mg_ref, gn_ref = refs[:6]
    pos = 6
    if has_init:
        c0_ref, n0_ref, m0_ref = refs[pos:pos + 3]
        pos += 3
    y_ref = refs[pos]
    pos += 1
    if emit_state:
        cf_ref, nf_ref, mf_ref = refs[pos:pos + 3]
        pos += 3
    (pcol_scr, lfc_scr, grow_scr, prow_scr, lfr_scr, u_scr, nu_scr, gm_scr, tot_scr, cs_scr, ns_scr, ms_scr,
     c_scr, n_scr, m_scr) = refs[pos:]
    assert heads == M_HEADS
    n_gates = 4 * M_HEADS

    ln = MLSTM_CHUNK
    row = lax.broadcasted_iota(jnp.int32, (ln, ln), 0)
    col = lax.broadcasted_iota(jnp.int32, (ln, ln), 1)
    lower = col <= row
    upper = col >= row
    tril = jnp.where(lower, 1.0, 0.0).astype(BF16)
    triu = jnp.where(upper, 1.0, 0.0).astype(BF16)
    ones = jnp.ones((ln, ln), BF16)

    def gate_index(hh, d):
        return 2 * d * heads + hh, (2 * d + 1) * heads + hh

    def summaries(c, carry):
        r0 = pl.multiple_of(c * ln, ln)
        g = mg_ref[pl.ds(r0, ln), :]
        lf = _log_sigmoid(g) * LOG2E
        pcol_scr[pl.ds(r0, ln), :] = _dot01_left(tril, lf)
        lfc_scr[pl.ds(r0, ln), :] = lf
        g_rows = g.T[MG_LANE0:MG_LANE0 + n_gates, :]
        gr = g_rows * LOG2E
        lfr = _log_sigmoid(g_rows) * LOG2E
        pr = _dot01(lfr, triu)
        grow_scr[c] = gr
        prow_scr[c] = pr
        lfr_scr[c] = lfr
        for hh in range(heads):
            kb = k_ref[pl.ds(r0, ln), hh * LANES:(hh + 1) * LANES]
            k_t = kb.astype(F32).T
            v = v_ref[pl.ds(r0, ln), hh * LANES:(hh + 1) * LANES]
            for d in range(2):
                ji, jf = gate_index(hh, d)
                tot = pr[jf:jf + 1, ln - 1:ln]
                b_row = pr[jf:jf + 1, :] if d == 0 else tot - pr[jf:jf + 1, :] + lfr[jf:jf + 1, :]
                g_row = tot - b_row + gr[ji:ji + 1, :]
                gmax = jnp.max(g_row, axis=-1, keepdims=True)
                wg = jnp.exp2(g_row - gmax)
                u_scr[hh, d, c] = _dot((k_t * wg).astype(BF16), v)
                nu_scr[hh, d, c] = _dot(jnp.broadcast_to(wg, (SUBLANES, ln)).astype(BF16), kb)[0:1, :]
                gm_scr[hh, d, c] = jnp.broadcast_to(gmax, (1, ln))
                tot_scr[hh, d, c] = jnp.broadcast_to(tot, (1, ln))
        return carry

    lax.fori_loop(0, n_chunks, summaries, 0, unroll=min(2, n_chunks))

    for hh in range(heads):
        for d in range(2):
            if has_init:
                c_scr[hh, d] = c0_ref[d, hh]
                n_scr[hh, d] = n0_ref[d, hh]
                m_scr[hh, d] = m0_ref[d, hh] * LOG2E
            else:
                c_scr[hh, d] = jnp.zeros((M_DK, M_DV), F32)
                n_scr[hh, d] = jnp.zeros((1, M_DK), F32)
                m_scr[hh, d] = jnp.zeros((1, ln), F32)

    def scan(j, carry):
        for hh in range(heads):
            for d in range(2):
                c = j if d == 0 else n_chunks - 1 - j
                cst = c_scr[hh, d]
                nst = n_scr[hh, d]
                mst = m_scr[hh, d]
                cs_scr[hh, d, c] = cst.astype(BF16)
                ns_scr[hh, d, c] = nst
                ms_scr[hh, d, c] = mst
                gmax = gm_scr[hh, d, c]
                total = tot_scr[hh, d, c] + mst
                m_new = jnp.maximum(total, gmax)
                decay = jnp.exp2(total - m_new)
                scale = jnp.exp2(gmax - m_new)
                c_scr[hh, d] = cst * decay + u_scr[hh, d, c] * scale
                n_scr[hh, d] = nst * decay + nu_scr[hh, d, c] * scale
                m_scr[hh, d] = m_new
        return carry

    lax.fori_loop(0, n_chunks, scan, 0)

    def readout(c, carry):
        r0 = pl.multiple_of(c * ln, ln)
        pc = pcol_scr[pl.ds(r0, ln), :]
        lf = lfc_scr[pl.ds(r0, ln), :]
        gr = grow_scr[c]
        pr = prow_scr[c]
        lfr = lfr_scr[c]
        tot_all = pc[ln - 1:ln, :]
        for hh in range(heads):
            q = q_ref[pl.ds(r0, ln), hh * LANES:(hh + 1) * LANES]
            k = k_ref[pl.ds(r0, ln), hh * LANES:(hh + 1) * LANES]
            v = v_ref[pl.ds(r0, ln), hh * LANES:(hh + 1) * LANES]
            qk = _dot_nt(q, k)
            hsum = None
            for d in range(2):
                ji, jf = gate_index(hh, d)
                lane_f = MG_LANE0 + jf
                if d == 0:
                    b_col = pc[:, lane_f:lane_f + 1]
                    r_row = gr[ji:ji + 1, :] - pr[jf:jf + 1, :]
                    mask = lower
                else:
                    tot = tot_all[:, lane_f:lane_f + 1]
                    b_col = tot - pc[:, lane_f:lane_f + 1] + lf[:, lane_f:lane_f + 1]
                    r_row = gr[ji:ji + 1, :] - (tot - pr[jf:jf + 1, :] + lfr[jf:jf + 1, :])
                    mask = upper
                b_colb = jnp.broadcast_to(b_col, (ln, ln))
                d_log = jnp.where(mask, b_colb + r_row, -jnp.inf)
                dmax = jnp.broadcast_to(jnp.max(d_log, axis=-1, keepdims=True), (ln, ln))
                inter = b_colb + ms_scr[hh, d, c]
                m_t = jnp.maximum(inter, dmax)
                sb = (qk * jnp.exp2(d_log - m_t)).astype(BF16)
                w_inter = jnp.exp2(inter - m_t)
                qc = _dot(q, cs_scr[hh, d, c])
                qn = _dot_nt(q, jnp.broadcast_to(ns_scr[hh, d, c], (ln, M_DK)).astype(BF16))
                num = qc * w_inter + _dot(sb, v)
                den = qn * w_inter + _dot(sb, ones)
                hc = num / jnp.maximum(jnp.abs(den), jnp.exp2(-m_t))
                hsum = hc if hsum is None else hsum + hc
            y_ref[pl.ds(r0, ln), hh * LANES:(hh + 1) * LANES] = (
                _rms(hsum, M_DV) * gn_ref[...] * og_ref[pl.ds(r0, ln), hh * LANES:(hh + 1) * LANES]
            ).astype(y_ref.dtype)
        return carry

    lax.fori_loop(0, n_chunks, readout, 0, unroll=min(2, n_chunks))

    if emit_state:
        for hh in range(heads):
            for d in range(2):
                cf_ref[d, hh] = c_scr[hh, d]
                nf_ref[d, hh] = n_scr[hh, d]
                mf_ref[d, hh] = m_scr[hh, d] * (1.0 / LOG2E)


def _mlstm_call(qkv, og, mg, lw, batch, seq, init, emit_state):
    n = batch * seq
    ln = MLSTM_CHUNK
    nc = seq // ln
    h = M_HEADS
    hs = MLSTM_HEADS_PER_STEP
    ng = h // hs

    def tok(colblock):
        return pl.BlockSpec((seq, hs * LANES), lambda b, hg: (b, colblock(hg)))

    in_specs = [
        tok(lambda hg: hg), tok(lambda hg: ng + hg), tok(lambda hg: 2 * ng + hg),
        tok(lambda hg: hg),
        pl.BlockSpec((seq, LANES), lambda b, hg: (b, 0)),
        pl.BlockSpec((1, M_DV), lambda b, hg: (0, 0)),
    ]
    args = [qkv, qkv, qkv, og, mg, lw["m_norm_g"]]
    state_specs = [
        pl.BlockSpec((None, 2, hs, M_DK, M_DV), lambda b, hg: (b, 0, hg, 0, 0)),
        pl.BlockSpec((None, 2, hs, 1, M_DK), lambda b, hg: (b, 0, hg, 0, 0)),
        pl.BlockSpec((None, 2, hs, 1, ln), lambda b, hg: (b, 0, hg, 0, 0)),
    ]
    if init is not None:
        c_all, n_all, m_all, layer = init
        depth = c_all.shape[1]
        in_specs += [
            pl.BlockSpec((None, None, 2, hs, M_DK, M_DV), lambda b, hg: (b, layer, 0, hg, 0, 0)),
            pl.BlockSpec((None, None, 2, hs, 1, M_DK), lambda b, hg: (b, layer, 0, hg, 0, 0)),
            pl.BlockSpec((None, None, 2, hs, 1, ln), lambda b, hg: (b, layer, 0, hg, 0, 0)),
        ]
        args += [c_all, n_all.reshape(batch, depth, 2, h, 1, M_DK),
                 jnp.broadcast_to(m_all[..., None, None], (batch, depth, 2, h, 1, ln))]
    out_shape = [jax.ShapeDtypeStruct((n, h * M_DV), BF16)]
    out_specs = [tok(lambda hg: hg)]
    if emit_state:
        out_shape += [
            jax.ShapeDtypeStruct((batch, 2, h, M_DK, M_DV), F32),
            jax.ShapeDtypeStruct((batch, 2, h, 1, M_DK), F32),
            jax.ShapeDtypeStruct((batch, 2, h, 1, ln), F32),
        ]
        out_specs += state_specs
    per = (hs, 2, nc)
    outs = pl.pallas_call(
        functools.partial(_mlstm_kernel, n_chunks=nc, heads=hs, has_init=init is not None,
                          emit_state=emit_state),
        out_shape=out_shape,
        grid=(batch, ng),
        in_specs=in_specs,
        out_specs=out_specs,
        scratch_shapes=[
            pltpu.VMEM((seq, LANES), F32), pltpu.VMEM((seq, LANES), F32),
            pltpu.VMEM((nc, 4 * h, ln), F32), pltpu.VMEM((nc, 4 * h, ln), F32), pltpu.VMEM((nc, 4 * h, ln), F32),
            pltpu.VMEM(per + (M_DK, M_DV), F32), pltpu.VMEM(per + (1, M_DK), F32),
            pltpu.VMEM(per + (1, ln), F32), pltpu.VMEM(per + (1, ln), F32),
            pltpu.VMEM(per + (M_DK, M_DV), BF16), pltpu.VMEM(per + (1, M_DK), F32), pltpu.VMEM(per + (1, ln), F32),
            pltpu.VMEM((hs, 2, M_DK, M_DV), F32), pltpu.VMEM((hs, 2, 1, M_DK), F32), pltpu.VMEM((hs, 2, 1, ln), F32),
        ],
        compiler_params=_params(2),
        name="mlstm",
    )(*args)
    if emit_state:
        ym, cf, nf, mf = outs
        return ym, (cf, nf[:, :, :, 0, :], mf[:, :, :, 0, 0])
    return outs[0], None


HEAD_V = 64
VT_ROWS = 80


def _ones_row(v_t):
    return jnp.where(lax.broadcasted_iota(jnp.int32, v_t.shape, 0) == HEAD_V, 1.0, v_t)


def _flash_heads(qs, keys_of, vt_chunk, n_keys, o_ref):
    def values(e, s_t):
        m = acc = None
        for c0 in range(0, n_keys, KEY_BLOCK):
            c1 = min(c0 + KEY_BLOCK, n_keys)
            s = s_t[c0:c1, :]
            mc = jnp.max(s, axis=0, keepdims=True)
            if c0 == 0:
                m = mc
                acc = _dot(vt_chunk(e, c0, c1), jnp.exp2(s - mc).astype(BF16))
            else:
                m_new = jnp.maximum(m, mc)
                acc = acc * jnp.exp2(m - m_new) + _dot(vt_chunk(e, c0, c1), jnp.exp2(s - m_new).astype(BF16))
                m = m_new
        return acc[0:HEAD_V, :] / acc[HEAD_V:HEAD_V + 1, :]

    if n_keys <= KEY_BLOCK:
        s_all = [_dot_nt(keys_of(e), qb) for e, qb in enumerate(qs)]
        outs = [values(e, s_t) for e, s_t in enumerate(s_all)]
    else:
        outs = []
        s_next = _dot_nt(keys_of(0), qs[0])
        for e in range(len(qs)):
            s_t = s_next
            if e + 1 < len(qs):
                s_next = _dot_nt(keys_of(e + 1), qs[e + 1])
            outs.append(values(e, s_t))
    for e in range(0, len(outs), 2):
        pair = jnp.concatenate([outs[e], outs[e + 1]], axis=0)
        o_ref[:, (e // 2) * LANES:(e // 2 + 1) * LANES] = pair.T.astype(o_ref.dtype)


def _mla_kernel(*refs, past, rotary, heads):
    q_ref, ckv_ref, akr_ref = refs[:3]
    pos = 3
    if rotary:
        mg_ref = refs[pos]
        pos += 1
    if past:
        ckvc_ref, krc_ref = refs[pos:pos + 2]
        pos += 2
    wk_ref, wvt_ref, kg_ref = refs[pos:pos + 3]
    pos += 3
    o_ref, kbuf, vbuf = refs[pos:]
    seq = ckv_ref.shape[0]

    @pl.when(pl.program_id(2) == 0)
    def _():
        for e in range(heads):
            wk_h = wk_ref[:, e * LANES:(e + 1) * LANES]
            wvt_h = wvt_ref[e * VT_ROWS:(e + 1) * VT_ROWS, :]

            def keys(ckv, kr, kr_ss):
                cb = ckv.astype(BF16)
                k_nope = _dot(cb, wk_h)
                if kr_ss is None:
                    kcat = k_nope + kr
                    ss = jnp.sum(kcat * kcat, axis=-1, keepdims=True)
                    kcat = kcat * kg_ref[...]
                else:
                    ss = jnp.sum(k_nope * k_nope, axis=-1, keepdims=True) + kr_ss
                    kcat = k_nope * kg_ref[...] + kr
                return _ones_row(_dot_nt(wvt_h, cb)), kcat * lax.rsqrt(ss * (1.0 / A_QK) + EPS)

            kr_ss = mg_ref[:, KR_SS_LANE:KR_SS_LANE + 1] if rotary else None
            v_t, kn = keys(ckv_ref[...], akr_ref[...], kr_ss)
            kbuf[e, past:past + seq, :] = kn.astype(BF16)
            vbuf[e, :, past:past + seq] = v_t.astype(BF16)
            if past:
                v_tc, knc = keys(ckvc_ref[...], krc_ref[...], None)
                kbuf[e, 0:past, :] = knc.astype(BF16)
                vbuf[e, :, 0:past] = v_tc.astype(BF16)

    qs = [q_ref[:, e * LANES:(e + 1) * LANES] for e in range(heads)]
    _flash_heads(qs, lambda e: kbuf[e], lambda e, c0, c1: vbuf[e, :, c0:c1], past + seq, o_ref)


def _mla_call(qa, ckv, akr, mg, lw, batch, seq, cache, rotary):
    n = batch * seq
    tq = min(Q_BLOCK, seq)
    nq = seq // tq
    past = 0 if cache is None else cache[0].shape[2]
    heads = MLA_HEADS_PER_STEP if nq > 1 else A_HEADS
    in_specs = [
        pl.BlockSpec((tq, heads * LANES), lambda b, hg, qi: (b * nq + qi, hg)),
        pl.BlockSpec((seq, LANES), lambda b, hg, qi: (b, 0)),
        pl.BlockSpec((seq, LANES), lambda b, hg, qi: (b, 0)),
    ]
    args = [qa, ckv, akr]
    if rotary:
        in_specs += [pl.BlockSpec((seq, LANES), lambda b, hg, qi: (b, 0))]
        args += [mg]
    layer = lw["layer"]
    if past:
        in_specs += [pl.BlockSpec((None, None, past, LANES), lambda b, hg, qi: (b, layer, 0, 0))] * 2
        args += [cache[0], cache[1]]
    in_specs += [
        pl.BlockSpec((None, A_KV_LORA, heads * LANES), lambda b, hg, qi: (layer, 0, hg)),
        pl.BlockSpec((None, heads * VT_ROWS, A_KV_LORA), lambda b, hg, qi: (layer, hg, 0)),
        pl.BlockSpec((1, LANES), lambda b, hg, qi: (0, 0)),
    ]
    args += [lw["w_uk"], lw["w_uvt"], lw["a_knorm_g"]]
    return pl.pallas_call(
        functools.partial(_mla_kernel, past=past, rotary=rotary, heads=heads),
        out_shape=jax.ShapeDtypeStruct((n, A_HEADS * A_V), BF16),
        grid=(batch, A_HEADS // heads, nq),
        in_specs=in_specs,
        out_specs=pl.BlockSpec((tq, heads * A_V), lambda b, hg, qi: (b * nq + qi, hg)),
        scratch_shapes=[pltpu.VMEM((heads, past + seq, LANES), BF16),
                        pltpu.VMEM((heads, VT_ROWS, past + seq), BF16)],
        compiler_params=_params(3),
        name="mla_attn",
    )(*args)


def _gqa_kernel(*refs, past, groups):
    q_ref, k_ref, v_ref = refs[:3]
    pos = 3
    if past:
        kc_ref, vc_ref = refs[pos:pos + 2]
        pos += 2
    o_ref, kbuf, vbuf = refs[pos:]
    seq = k_ref.shape[0]
    qw = G_GROUP * G_HD

    @pl.when(pl.program_id(2) == 0)
    def _():
        for gi in range(groups):
            grp = pl.program_id(1) * groups + gi

            r = lax.broadcasted_iota(jnp.int32, (VT_ROWS, LANES), 0)
            c = lax.broadcasted_iota(jnp.int32, (VT_ROWS, LANES), 1)
            pick = jnp.where((c == r + grp * G_HD) & (r < G_HD), 1.0, 0.0).astype(BF16)

            def fill(r0, k, v):
                low = _lane(k.shape) < G_HD
                k_lo = jnp.where(grp == 0, k, pltpu.roll(k, G_HD, 1))
                rows = k.shape[0]
                kbuf[gi, r0:r0 + rows, :] = jnp.where(low, k_lo, pltpu.roll(k_lo, G_HD, 1)).astype(BF16)
                vbuf[gi, :, r0:r0 + rows] = _ones_row(_dot_nt(pick, v.astype(BF16))).astype(BF16)

            fill(past, k_ref[...], v_ref[...])
            if past:
                fill(0, kc_ref[...], vc_ref[...])

    qs = []
    for gi in range(groups):
        for j in range(G_GROUP):
            col = q_ref[:, gi * qw + (j // 2) * LANES:gi * qw + (j // 2 + 1) * LANES]
            keep = (_lane(col.shape) < G_HD) == (j % 2 == 0)
            qs.append(jnp.where(keep, col, jnp.zeros_like(col)))
    _flash_heads(qs, lambda e: kbuf[e // G_GROUP], lambda e, c0, c1: vbuf[e // G_GROUP, :, c0:c1],
                 past + seq, o_ref)


def _gqa_call(gq, gk, gv, batch, seq, cache):
    n = batch * seq
    tq = min(Q_BLOCK, seq)
    nq = seq // tq
    past = 0 if cache is None else cache[0].shape[2]
    groups = GQA_GROUPS_PER_STEP if nq > 1 else G_KV_HEADS
    qw = G_GROUP * G_HD
    kvw = G_KV_HEADS * G_HD
    in_specs = [
        pl.BlockSpec((tq, groups * qw), lambda b, g, qi: (b * nq + qi, g)),
        pl.BlockSpec((seq, kvw), lambda b, g, qi: (b, 0)),
        pl.BlockSpec((seq, kvw), lambda b, g, qi: (b, 0)),
    ]
    args = [gq, gk, gv]
    if past:
        layer = cache[2]
        in_specs += [pl.BlockSpec((None, None, past, kvw), lambda b, g, qi: (b, layer, 0, 0))] * 2
        args += [cache[0], cache[1]]
    return pl.pallas_call(
        functools.partial(_gqa_kernel, past=past, groups=groups),
        out_shape=jax.ShapeDtypeStruct((n, G_HEADS * G_HD), BF16),
        grid=(batch, G_KV_HEADS // groups, nq),
        in_specs=in_specs,
        out_specs=pl.BlockSpec((tq, groups * qw), lambda b, g, qi: (b * nq + qi, g)),
        scratch_shapes=[pltpu.VMEM((groups, past + seq, LANES), BF16),
                        pltpu.VMEM((groups, VT_ROWS, past + seq), BF16)],
        compiler_params=_params(3),
        name="gqa_attn",
    )(*args)


def _merge_ffn_kernel(x_ref, ym_ref, ya_ref, yg_ref, gates_ref, mod_ref, g2_ref, wb_ref, wo_ref, wfi_ref,
                      wfo_ref, o_ref, *, ff_chunks):
    x = x_ref[...]
    d = x.shape[-1]
    mixed = None
    for i, y_ref in enumerate((ym_ref, ya_ref, yg_ref)):
        br = _dot(y_ref[...].astype(BF16), wb_ref[i]) * gates_ref[:, i * d:(i + 1) * d]
        mixed = br if mixed is None else mixed + br
    gt1 = mod_ref[2:3, :]
    x1 = x + gt1 * _dot(mixed.astype(BF16), wo_ref[...])

    sh2 = mod_ref[3:4, :]
    sc2 = mod_ref[4:5, :]
    gt2 = mod_ref[5:6, :]
    h2 = ((_rms(x1, d) * g2_ref[...]) * (1.0 + sc2) + sh2).astype(BF16)
    d_ff = wfo_ref.shape[0]
    acc = None
    for c0, c1 in ff_chunks:
        ug = _dot(h2, wfi_ref[:, c0:c1])
        uv = _dot(h2, wfi_ref[:, d_ff + c0:d_ff + c1])
        act = (ug * _sigmoid(ug) * uv).astype(BF16)
        part = _dot(act, wfo_ref[c0:c1, :])
        acc = part if acc is None else acc + part
    o_ref[...] = x1 + gt2 * acc


def _merge_ffn_call(x, ym, ya, yg, gates, mod, lw, mod_row0, rows_per_mod):
    n, d = x.shape
    tm = TOKEN_BLOCK
    d_ff = lw["w_ffn_out"].shape[1]
    layer = lw["layer"]
    split = -(-(d_ff // 2) // MXU_COLS) * MXU_COLS
    ff_chunks = ((0, split), (split, d_ff)) if 0 < split < d_ff else ((0, d_ff),)

    def mod_idx(i):
        return (mod_row0 + (i * tm) // rows_per_mod, 0, 0)

    def rows(width):
        return pl.BlockSpec((tm, width), lambda i: (i, 0))

    return pl.pallas_call(
        functools.partial(_merge_ffn_kernel, ff_chunks=ff_chunks),
        out_shape=jax.ShapeDtypeStruct((n, d), F32),
        grid=(n // tm,),
        in_specs=[
            rows(d), rows(BRANCH_WIDTH), rows(BRANCH_WIDTH), rows(BRANCH_WIDTH), rows(N_BRANCH * d),
            pl.BlockSpec((None, N_MOD, d), mod_idx),
            _resident((1, d)),
            _resident((N_BRANCH, BRANCH_WIDTH, d), layer),
            _resident((d, d), layer),
            _resident((d, 2 * d_ff), layer),
            _resident((d_ff, d), layer),
        ],
        out_specs=rows(d),
        compiler_params=_params(1),
        name="merge_ffn",
    )(x, ym, ya, yg, gates, mod, lw["norm2_g"], lw["w_branch"], lw["w_out"], lw["w_ffn_in"], lw["w_ffn_out"])


def _mla_q_layout(a):
    lead = a.shape[:-1]
    a = a.reshape(lead + (A_HEADS, A_QK))
    z = jnp.zeros(lead + (A_HEADS, LANES - A_QK), a.dtype)
    return jnp.concatenate([a[..., A_NOPE:], z, a[..., :A_NOPE]], axis=-1).reshape(lead + (A_HEADS * LANES,))


def _stacked_weights(w_in, w_uq, w_ukv, w_branch, w_out, w_ffn_in, w_ffn_out):
    depth, d, _ = w_in.shape
    hw = M_HEADS * M_DK
    sizes = (N_BRANCH * d, hw, hw, hw, hw, 4 * M_HEADS, A_Q_LORA, A_KV_LORA, A_ROPE, G_HEADS * G_HD,
             G_KV_HEADS * G_HD, G_KV_HEADS * G_HD)
    edges = np.concatenate([[0], np.cumsum(sizes)]).tolist()
    (wmg, waq, wakv, wakr, wgq, wgk, wgv) = [w_in[:, :, a:b] for a, b in zip(edges[5:-1], edges[6:])]
    misc_pad = jnp.zeros((depth, d, LANES - A_ROPE - 4 * M_HEADS), w_in.dtype)
    tail = jnp.concatenate([waq, wakv, wakr, wmg, misc_pad, wgq, wgk, wgv], axis=2).astype(BF16)
    ukv = w_ukv.reshape(depth, A_KV_LORA, A_HEADS, A_NOPE + A_V)
    uk = jnp.pad(ukv[..., :A_NOPE], ((0, 0), (0, 0), (0, 0), (LANES - A_NOPE, 0)))
    uvt = jnp.pad(ukv[..., A_NOPE:].transpose(0, 2, 3, 1), ((0, 0), (0, 0), (0, VT_ROWS - A_V), (0, 0)))
    return dict(
        w_in=w_in.astype(BF16),
        w_in_tail=tail,
        w_uq=_mla_q_layout(w_uq).astype(BF16),
        w_uk=uk.reshape(depth, A_KV_LORA, A_HEADS * LANES).astype(BF16),
        w_uvt=uvt.reshape(depth, A_HEADS * VT_ROWS, A_KV_LORA).astype(BF16),
        w_branch=w_branch.astype(BF16),
        w_out=w_out.astype(BF16),
        w_ffn_in=w_ffn_in.astype(BF16),
        w_ffn_out=w_ffn_out.astype(BF16),
    )


def _layer_vectors(l, b_mgate, norm1_g, m_norm_g, a_qlora_g, a_kvlora_g, a_qnorm_g, a_knorm_g, g_qnorm_g,
                   g_knorm_g, norm2_g):
    return dict(
        layer=l,
        b_mgate=jnp.pad(b_mgate[l][None, :], ((0, 0), (MG_LANE0, LANES - MG_LANE0 - 4 * M_HEADS))),
        norm1_g=norm1_g[l][None, :],
        m_norm_g=m_norm_g[l][None, :],
        a_qlora_g=a_qlora_g[l][None, :],
        a_kvlora_g=a_kvlora_g[l][None, :],
        a_qnorm_g=_mla_q_layout(jnp.tile(a_qnorm_g[l], A_HEADS)[None, :])[:, :LANES],
        a_knorm_g=_mla_q_layout(jnp.tile(a_knorm_g[l], A_HEADS)[None, :])[:, :LANES],
        g_qnorm_g=jnp.tile(g_qnorm_g[l], LANES // G_HD)[None, :],
        g_knorm_g=jnp.tile(g_knorm_g[l], G_KV_HEADS)[None, :],
        norm2_g=norm2_g[l][None, :],
    )


def _axial_angles(seq, rot_dim):
    n_freq = rot_dim // 4
    freqs = ROPE_BASE ** (-jnp.arange(n_freq, dtype=F32) / n_freq)
    t = jnp.arange(seq)
    row = (t // GRID_W).astype(F32)
    col = (t % GRID_W).astype(F32)
    return jnp.concatenate([row[:, None] * freqs, col[:, None] * freqs], axis=-1)


def _rope_tables(seq):
    ang = _axial_angles(seq, A_ROPE)
    one = jnp.ones((seq, LANES - A_ROPE), F32)
    mla_cos = jnp.concatenate([jnp.cos(ang), jnp.cos(ang), one], axis=-1)
    mla_sin = jnp.concatenate([-jnp.sin(ang), jnp.sin(ang), 0.0 * one], axis=-1)
    ang = _axial_angles(seq, G_HD)
    cos = jnp.concatenate([jnp.cos(ang), jnp.cos(ang)], axis=-1)
    sin = jnp.concatenate([-jnp.sin(ang), jnp.sin(ang)], axis=-1)
    gqa = (jnp.tile(cos, (1, LANES // G_HD)), jnp.tile(sin, (1, LANES // G_HD)))
    return (mla_cos, mla_sin), gqa


def _layer(x, mod, lw, batch, seq, mod_row0, rows_per_mod, ctx, rope):
    (gates, qkv, og, mg, qa, ckv, akr, gq, gk, gv) = _inproj_call(x, mod, lw, mod_row0, rows_per_mod, rope, seq)
    if ctx is None:
        ym, state = _mlstm_call(qkv, og, mg, lw, batch, seq, None, True)
        ya = _mla_call(qa, ckv, akr, mg, lw, batch, seq, None, False)
        yg = _gqa_call(gq, gk, gv, batch, seq, None)
        new_ctx = dict(state=state, ckv=ckv, kr=akr[:, :A_ROPE], gk=gk, gv=gv)
    else:
        ym, _ = _mlstm_call(qkv, og, mg, lw, batch, seq, ctx["mlstm"], False)
        ya = _mla_call(qa, ckv, akr, mg, lw, batch, seq, ctx["mla"], True)
        yg = _gqa_call(gq, gk, gv, batch, seq, ctx["gqa"])
        new_ctx = None
    x = _merge_ffn_call(x, ym, ya, yg, gates, mod, lw, mod_row0, rows_per_mod)
    return x, new_ctx


def kernel(x_prompt, x_sample, state_mlstm_C, state_mlstm_n, state_mlstm_m, cache_mla_ckv, cache_mla_krope,
           cache_gqa_k, cache_gqa_v, c, c_ctx, w_mod, b_mod, norm1_g, w_in, b_mgate, m_norm_g, a_qlora_g,
           a_kvlora_g, w_uq, w_ukv, a_qnorm_g, a_knorm_g, g_qnorm_g, g_knorm_g, w_branch, w_out, norm2_g,
           w_ffn_in, w_ffn_out):
    batch, seq, d = x_prompt.shape
    dbatch, dseq, _ = x_sample.shape
    depth = w_in.shape[0]
    past = cache_mla_ckv.shape[2]

    n_rows = -(-(1 + dbatch) // SUBLANES) * SUBLANES
    cond = jnp.concatenate([c_ctx[None, :], c, jnp.zeros((n_rows - 1 - dbatch, d), F32)], axis=0)
    mod_all = _mod_call(cond, w_mod, b_mod).reshape(depth, n_rows, N_MOD, d)

    rope = _rope_tables(dseq)
    xp = x_prompt.reshape(batch * seq, d)
    xs = x_sample.reshape(dbatch * dseq, d)
    ctx_layers = []
    krope_cache = jnp.pad(cache_mla_krope, ((0, 0), (0, 0), (0, 0), (0, LANES - A_ROPE)))
    gqa_k_cache = cache_gqa_k.reshape(dbatch, depth, past, G_KV_HEADS * G_HD)
    gqa_v_cache = cache_gqa_v.reshape(dbatch, depth, past, G_KV_HEADS * G_HD)
    stacked = _stacked_weights(w_in, w_uq, w_ukv, w_branch, w_out, w_ffn_in, w_ffn_out)
    for l in range(depth):
        lw = dict(stacked, **_layer_vectors(l, b_mgate, norm1_g, m_norm_g, a_qlora_g, a_kvlora_g, a_qnorm_g,
                                            a_knorm_g, g_qnorm_g, g_knorm_g, norm2_g))
        xp, st = _layer(xp, mod_all[l], lw, batch, seq, 0, batch * seq, None, None)
        ctx_layers.append(st)
        ctx = dict(mlstm=(state_mlstm_C, state_mlstm_n, state_mlstm_m, l), mla=(cache_mla_ckv, krope_cache),
                   gqa=(gqa_k_cache, gqa_v_cache, l))
        xs, _ = _layer(xs, mod_all[l], lw, dbatch, dseq, 1, dseq, ctx, rope)

    def stack(fn):
        return jnp.stack([fn(s) for s in ctx_layers], axis=1)

    new_c = stack(lambda s: s["state"][0])
    new_n = stack(lambda s: s["state"][1])
    new_m = stack(lambda s: s["state"][2])
    new_ckv = stack(lambda s: s["ckv"].reshape(batch, seq, A_KV_LORA))
    new_kr = stack(lambda s: s["kr"].reshape(batch, seq, A_ROPE))
    new_gk = stack(lambda s: s["gk"].reshape(batch, seq, G_KV_HEADS, G_HD))
    new_gv = stack(lambda s: s["gv"].reshape(batch, seq, G_KV_HEADS, G_HD))
    return (xp.reshape(batch, seq, d), xs.reshape(dbatch, dseq, d), new_c, new_n, new_m, new_ckv, new_kr,
            new_gk, new_gv)
```

```python
import functools

import numpy as np
import jax
import jax.numpy as jnp
from jax import lax
from jax.experimental import pallas as pl
from jax.experimental.pallas import tpu as pltpu

F32 = jnp.float32
BF16 = jnp.bfloat16

LANES = 128
SUBLANES = 8
VMEM_LIMIT_BYTES = 56 * 1024 * 1024

EPS = 1e-6
ROPE_BASE = 10000.0
GRID_W = 64

M_HEADS = 4
M_DK = 128
M_DV = 128
A_HEADS = 8
A_NOPE = 64
A_ROPE = 32
A_QK = A_NOPE + A_ROPE
A_V = 64
A_Q_LORA = 256
A_KV_LORA = 128
G_HEADS = 8
G_KV_HEADS = 2
G_GROUP = G_HEADS // G_KV_HEADS
G_HD = 64
N_BRANCH = 3
BRANCH_WIDTH = 512
N_MOD = 6

TOKEN_BLOCK = 512
Q_BLOCK = 256
KEY_BLOCK = 256
MLA_HEADS_PER_STEP = 8
GQA_GROUPS_PER_STEP = 2
SHORT_SEQ_BATCH_PER_STEP = 2
MLSTM_CHUNK = 128
MLSTM_HEADS_PER_STEP = 4
LOG2E = 1.4426950408889634


def _params(n_axes):
    return pltpu.CompilerParams(dimension_semantics=("arbitrary",) * n_axes,
                                vmem_limit_bytes=VMEM_LIMIT_BYTES)


def _resident(shape, layer=None):
    nd = len(shape)
    if layer is None:
        return pl.BlockSpec(shape, lambda *_: (0,) * nd, pipeline_mode=pl.Buffered(1))
    return pl.BlockSpec((None,) + tuple(shape), lambda *_: (layer,) + (0,) * nd, pipeline_mode=pl.Buffered(1))


def _lane(shape, axis=None):
    return lax.broadcasted_iota(jnp.int32, shape, len(shape) - 1 if axis is None else axis)


def _dot(a, b):
    return jnp.dot(a, b, preferred_element_type=F32)


def _dot_nt(a, b):
    return lax.dot_general(a, b, (((1,), (1,)), ((), ())), preferred_element_type=F32)


def _split3(a):
    hi = a.astype(BF16)
    r1 = a - hi.astype(F32)
    mid = r1.astype(BF16)
    lo = (r1 - mid.astype(F32)).astype(BF16)
    return hi, mid, lo


def _dot01(a, m01):
    hi, mid, lo = _split3(a)
    return _dot(hi, m01) + _dot(mid, m01) + _dot(lo, m01)


def _dot01_left(m01, a):
    hi, mid, lo = _split3(a)
    return _dot(m01, hi) + _dot(m01, mid) + _dot(m01, lo)


def _sigmoid(x):
    return 0.5 * jnp.tanh(0.5 * x) + 0.5


def _log_sigmoid(x):
    return jnp.minimum(x, 0.0) - jnp.log(1.0 + jnp.exp(-jnp.abs(x)))


def _rms(x, width):
    ms = jnp.sum(x * x, axis=-1, keepdims=True) * (1.0 / width)
    return x * lax.rsqrt(ms + EPS)


def _rope(x, cos, sin_signed, half, period):
    n = x.shape[-1]
    first = (_lane(x.shape) % period) < half
    swapped = jnp.where(first, pltpu.roll(x, n - half, x.ndim - 1), pltpu.roll(x, half, x.ndim - 1))
    return x * cos + swapped * sin_signed


def _mod_kernel(c_ref, w_ref, b_ref, o_ref):
    c = c_ref[...]
    a = c * _sigmoid(c)
    o_ref[...] = _dot01(a, w_ref[...].astype(BF16)) + b_ref[...]


def _mod_call(cond, w_mod, b_mod):
    depth, d, n = w_mod.shape
    rows = cond.shape[0]
    tn = 1536
    return pl.pallas_call(
        _mod_kernel,
        out_shape=jax.ShapeDtypeStruct((depth, rows, n), F32),
        grid=(depth, n // tn),
        in_specs=[
            pl.BlockSpec((rows, d), lambda l, j: (0, 0)),
            pl.BlockSpec((None, d, tn), lambda l, j: (l, 0, j)),
            pl.BlockSpec((None, 1, tn), lambda l, j: (l, 0, j)),
        ],
        out_specs=pl.BlockSpec((None, rows, tn), lambda l, j: (l, 0, j)),
        compiler_params=_params(2),
        name="adaln_mod",
    )(cond, w_mod, b_mod.reshape(depth, 1, n))


MXU_COLS = 256
_GATE0, _GATE1 = 0, 3072
_MQ0 = 3072
_MK0 = 3584
_MV0 = 4096
_MO0 = 4608
_AQ0 = 5120
_AKV0 = 5376
_MISC0 = 5504
_GQ0 = 5632
_GK0 = 6144
_GV0 = 6272
_WIN_COLS = 6400
MG_LANE0 = A_ROPE
KR_SS_LANE = 64


def _head_pair_ms(x):
    low = (_lane(x.shape) % LANES) < G_HD
    sq = x * x
    cols = []
    for c0 in range(0, x.shape[-1], LANES):
        s = sq[:, c0:c0 + LANES]
        lo = jnp.sum(jnp.where(low[:, c0:c0 + LANES], s, 0.0), axis=-1, keepdims=True)
        hi = jnp.sum(jnp.where(low[:, c0:c0 + LANES], 0.0, s), axis=-1, keepdims=True)
        cols.append(jnp.where(low[:, c0:c0 + LANES], lo, hi))
    ms = cols[0] if len(cols) == 1 else jnp.concatenate(cols, axis=-1)
    return ms * (1.0 / G_HD)


def _inproj_kernel(*refs, rotary):
    (x_ref, mod_ref, g1_ref, w_ref, wt_ref, bmg_ref, gql_ref, wuq_ref, gkvl_ref, gkn_ref, aqn_ref,
     gqn_ref, akn_ref) = refs[:13]
    pos = 13
    if rotary:
        ca_ref, sa_ref, cg_ref, sg_ref = refs[pos:pos + 4]
        pos += 4
    (gates_ref, qkv_ref, og_ref, mg_ref, qa_ref, ckv_ref, akr_ref, gq_ref, gk_ref, gv_ref) = refs[pos:]
    x = x_ref[...]
    d = x.shape[-1]
    sh1 = mod_ref[0:1, :]
    sc1 = mod_ref[1:2, :]
    h = (_rms(x, d) * g1_ref[...]) * (1.0 + sc1) + sh1
    hb = h.astype(BF16)

    def proj(c0, width):
        if c0 < _AQ0:
            return _dot(hb, w_ref[:, c0:c0 + width])
        return _dot(hb, wt_ref[:, c0 - _AQ0:c0 - _AQ0 + width])

    aq = _rms(proj(_AQ0, A_Q_LORA), A_Q_LORA) * gql_ref[...]
    qa = _dot(aq.astype(BF16), wuq_ref[...])
    for hd in range(A_HEADS):
        qh = _rms(qa[:, hd * LANES:(hd + 1) * LANES], A_QK) * aqn_ref[...]
        if rotary:
            qh = _rope(qh, ca_ref[...], sa_ref[...], A_ROPE // 2, LANES)
        qa_ref[:, hd * LANES:(hd + 1) * LANES] = (qh * (A_QK ** -0.5 * LOG2E)).astype(qa_ref.dtype)
    akv_misc = proj(_AKV0, A_KV_LORA + LANES)
    ckv_ref[...] = _rms(akv_misc[:, :A_KV_LORA], A_KV_LORA) * gkvl_ref[...]
    misc = akv_misc[:, A_KV_LORA:]
    kr = jnp.where(_lane(misc.shape) < A_ROPE, misc, 0.0)
    mg = misc + bmg_ref[...]
    if rotary:
        akr_ref[...] = _rope(kr * akn_ref[...], ca_ref[...], sa_ref[...], A_ROPE // 2, LANES)
        mg = jnp.where(_lane(mg.shape) == KR_SS_LANE, jnp.sum(kr * kr, axis=-1, keepdims=True), mg)
    else:
        akr_ref[...] = kr
    mg_ref[...] = mg

    gq_all = proj(_GQ0, G_HEADS * G_HD)
    for c0 in range(0, G_HEADS * G_HD, LANES):
        gq = gq_all[:, c0:c0 + LANES]
        gq = gq * lax.rsqrt(_head_pair_ms(gq) + EPS) * gqn_ref[...]
        if rotary:
            gq = _rope(gq, cg_ref[...], sg_ref[...], G_HD // 2, G_HD)
        gq_ref[:, c0:c0 + LANES] = (gq * (G_HD ** -0.5 * LOG2E)).astype(gq_ref.dtype)
    kvw = G_KV_HEADS * G_HD
    gkv = proj(_GK0, 2 * kvw)
    gk = gkv[:, :kvw]
    gk = gk * lax.rsqrt(_head_pair_ms(gk) + EPS) * gkn_ref[...]
    if rotary:
        gk = _rope(gk, cg_ref[...], sg_ref[...], G_HD // 2, G_HD)
    gk_ref[...] = gk
    gv_ref[...] = gkv[:, kvw:]

    hw = M_HEADS * M_DK
    gates_ref[...] = _sigmoid(proj(_GATE0, _GATE1 - _GATE0)).astype(gates_ref.dtype)
    og_ref[...] = _sigmoid(proj(_MO0, hw)).astype(og_ref.dtype)
    qkv_ref[:, 0:hw] = proj(_MQ0, hw).astype(qkv_ref.dtype)
    qkv_ref[:, hw:2 * hw] = (proj(_MK0, hw) * (M_DK ** -0.5)).astype(qkv_ref.dtype)
    qkv_ref[:, 2 * hw:3 * hw] = proj(_MV0, hw).astype(qkv_ref.dtype)


def _inproj_call(x, mod, lw, mod_row0, rows_per_mod, rope, seq):
    n, d = x.shape
    tm = TOKEN_BLOCK
    hw = M_HEADS * M_DK

    def mod_idx(i):
        return (mod_row0 + (i * tm) // rows_per_mod, 0, 0)

    def rows(width):
        return pl.BlockSpec((tm, width), lambda i: (i, 0))

    in_specs = [
        rows(d),
        pl.BlockSpec((None, N_MOD, d), mod_idx),
        _resident((1, d)),
        _resident(lw["w_in"].shape[1:], lw["layer"]),
        _resident((d, _WIN_COLS - _AQ0), lw["layer"]),
        _resident((1, LANES)),
        _resident((1, A_Q_LORA)),
        _resident((A_Q_LORA, A_HEADS * LANES), lw["layer"]),
        _resident((1, A_KV_LORA)),
        _resident((1, LANES)),
        _resident((1, LANES)),
        _resident((1, LANES)),
        _resident((1, LANES)),
    ]
    args = [x, mod, lw["norm1_g"], lw["w_in"], lw["w_in_tail"], lw["b_mgate"], lw["a_qlora_g"], lw["w_uq"],
            lw["a_kvlora_g"], lw["g_knorm_g"], lw["a_qnorm_g"], lw["g_qnorm_g"], lw["a_knorm_g"]]
    if rope is not None:
        blocks_per_seq = seq // tm
        in_specs += [pl.BlockSpec((tm, LANES), lambda i: (i % blocks_per_seq, 0))] * 4
        args += [rope[0][0], rope[0][1], rope[1][0], rope[1][1]]
    out_widths = [3 * d, 3 * hw, hw, LANES, A_HEADS * LANES, A_KV_LORA, LANES, G_HEADS * G_HD,
                  G_KV_HEADS * G_HD, G_KV_HEADS * G_HD]
    out_dtypes = [BF16, BF16, BF16, F32, BF16, F32, F32, BF16, F32, F32]
    return pl.pallas_call(
        functools.partial(_inproj_kernel, rotary=rope is not None),
        out_shape=[jax.ShapeDtypeStruct((n, w), t) for w, t in zip(out_widths, out_dtypes)],
        grid=(n // tm,),
        in_specs=in_specs,
        out_specs=[rows(w) for w in out_widths],
        compiler_params=_params(1),
        name="in_proj",
    )(*args)


def _mlstm_kernel(*refs, n_chunks, heads, has_init, emit_state):
    q_ref, k_ref, v_ref, og_ref, mg_ref, gn_ref = refs[:6]
    pos = 6
    if has_init:
        c0_ref, n0_ref, m0_ref = refs[pos:pos + 3]
        pos += 3
    y_ref = refs[pos]
    pos += 1
    if emit_state:
        cf_ref, nf_ref, mf_ref = refs[pos:pos + 3]
        pos += 3
    (pcol_scr, lfc_scr, grow_scr, prow_scr, lfr_scr, u_scr, nu_scr, gm_scr, tot_scr, cs_scr, ns_scr, ms_scr,
     c_scr, n_scr, m_scr) = refs[pos:]
    assert heads == M_HEADS
    n_gates = 4 * M_HEADS

    ln = MLSTM_CHUNK
    row = lax.broadcasted_iota(jnp.int32, (ln, ln), 0)
    col = lax.broadcasted_iota(jnp.int32, (ln, ln), 1)
    lower = col <= row
    upper = col >= row
    tril = jnp.where(lower, 1.0, 0.0).astype(BF16)
    triu = jnp.where(upper, 1.0, 0.0).astype(BF16)
    ones = jnp.ones((ln, ln), BF16)

    def gate_index(hh, d):
        return 2 * d * heads + hh, (2 * d + 1) * heads + hh

    def summaries(c, carry):
        r0 = pl.multiple_of(c * ln, ln)
        g = mg_ref[pl.ds(r0, ln), :]
        lf = _log_sigmoid(g) * LOG2E
        pcol_scr[pl.ds(r0, ln), :] = _dot01_left(tril, lf)
        lfc_scr[pl.ds(r0, ln), :] = lf
        g_rows = g.T[MG_LANE0:MG_LANE0 + n_gates, :]
        gr = g_rows * LOG2E
        lfr = _log_sigmoid(g_rows) * LOG2E
        pr = _dot01(lfr, triu)
        grow_scr[c] = gr
        prow_scr[c] = pr
        lfr_scr[c] = lfr
        for hh in range(heads):
            kb = k_ref[pl.ds(r0, ln), hh * LANES:(hh + 1) * LANES]
            k_t = kb.astype(F32).T
            v = v_ref[pl.ds(r0, ln), hh * LANES:(hh + 1) * LANES]
            for d in range(2):
                ji, jf = gate_index(hh, d)
                tot = pr[jf:jf + 1, ln - 1:ln]
                b_row = pr[jf:jf + 1, :] if d == 0 else tot - pr[jf:jf + 1, :] + lfr[jf:jf + 1, :]
                g_row = tot - b_row + gr[ji:ji + 1, :]
                gmax = jnp.max(g_row, axis=-1, keepdims=True)
                wg = jnp.exp2(g_row - gmax)
                u_scr[hh, d, c] = _dot((k_t * wg).astype(BF16), v)
                nu_scr[hh, d, c] = _dot(jnp.broadcast_to(wg, (SUBLANES, ln)).astype(BF16), kb)[0:1, :]
                gm_scr[hh, d, c] = jnp.broadcast_to(gmax, (1, ln))
                tot_scr[hh, d, c] = jnp.broadcast_to(tot, (1, ln))
        return carry

    lax.fori_loop(0, n_chunks, summaries, 0, unroll=min(2, n_chunks))

    for hh in range(heads):
        for d in range(2):
            if has_init:
                c_scr[hh, d] = c0_ref[d, hh]
                n_scr[hh, d] = n0_ref[d, hh]
                m_scr[hh, d] = m0_ref[d, hh] * LOG2E
            else:
                c_scr[hh, d] = jnp.zeros((M_DK, M_DV), F32)
                n_scr[hh, d] = jnp.zeros((1, M_DK), F32)
                m_scr[hh, d] = jnp.zeros((1, ln), F32)

    def scan(j, carry):
        for hh in range(heads):
            for d in range(2):
                c = j if d == 0 else n_chunks - 1 - j
                cst = c_scr[hh, d]
                nst = n_scr[hh, d]
                mst = m_scr[hh, d]
                cs_scr[hh, d, c] = cst.astype(BF16)
                ns_scr[hh, d, c] = nst
                ms_scr[hh, d, c] = mst
                gmax = gm_scr[hh, d, c]
                total = tot_scr[hh, d, c] + mst
                m_new = jnp.maximum(total, gmax)
                decay = jnp.exp2(total - m_new)
                scale = jnp.exp2(gmax - m_new)
                c_scr[hh, d] = cst * decay + u_scr[hh, d, c] * scale
                n_scr[hh, d] = nst * decay + nu_scr[hh, d, c] * scale
                m_scr[hh, d] = m_new
        return carry

    lax.fori_loop(0, n_chunks, scan, 0)

    def readout(c, carry):
        r0 = pl.multiple_of(c * ln, ln)
        pc = pcol_scr[pl.ds(r0, ln), :]
        lf = lfc_scr[pl.ds(r0, ln), :]
        gr = grow_scr[c]
        pr = prow_scr[c]
        lfr = lfr_scr[c]
        tot_all = pc[ln - 1:ln, :]
        for hh in range(heads):
            q = q_ref[pl.ds(r0, ln), hh * LANES:(hh + 1) * LANES]
            k = k_ref[pl.ds(r0, ln), hh * LANES:(hh + 1) * LANES]
            v = v_ref[pl.ds(r0, ln), hh * LANES:(hh + 1) * LANES]
            qk = _dot_nt(q, k)
            hsum = None
            for d in range(2):
                ji, jf = gate_index(hh, d)
                lane_f = MG_LANE0 + jf
                if d == 0:
                    b_col = pc[:, lane_f:lane_f + 1]
                    r_row = gr[ji:ji + 1, :] - pr[jf:jf + 1, :]
                    mask = lower
                else:
                    tot = tot_all[:, lane_f:lane_f + 1]
                    b_col = tot - pc[:, lane_f:lane_f + 1] + lf[:, lane_f:lane_f + 1]
                    r_row = gr[ji:ji + 1, :] - (tot - pr[jf:jf + 1, :] + lfr[jf:jf + 1, :])
                    mask = upper
                b_colb = jnp.broadcast_to(b_col, (ln, ln))
                d_log = jnp.where(mask, b_colb + r_row, -jnp.inf)
                dmax = jnp.broadcast_to(jnp.max(d_log, axis=-1, keepdims=True), (ln, ln))
                inter = b_colb + ms_scr[hh, d, c]
                m_t = jnp.maximum(inter, dmax)
                sb = (qk * jnp.exp2(d_log - m_t)).astype(BF16)
                w_inter = jnp.exp2(inter - m_t)
                qc = _dot(q, cs_scr[hh, d, c])
                qn = _dot_nt(q, jnp.broadcast_to(ns_scr[hh, d, c], (ln, M_DK)).astype(BF16))
                num = qc * w_inter + _dot(sb, v)
                den = qn * w_inter + _dot(sb, ones)
                hc = num / jnp.maximum(jnp.abs(den), jnp.exp2(-m_t))
                hsum = hc if hsum is None else hsum + hc
            y_ref[pl.ds(r0, ln), hh * LANES:(hh + 1) * LANES] = (
                _rms(hsum, M_DV) * gn_ref[...] * og_ref[pl.ds(r0, ln), hh * LANES:(hh + 1) * LANES]
            ).astype(y_ref.dtype)
        return carry

    lax.fori_loop(0, n_chunks, readout, 0, unroll=min(2, n_chunks))

    if emit_state:
        for hh in range(heads):
            for d in range(2):
                cf_ref[d, hh] = c_scr[hh, d]
                nf_ref[d, hh] = n_scr[hh, d]
                mf_ref[d, hh] = m_scr[hh, d] * (1.0 / LOG2E)


def _mlstm_call(qkv, og, mg, lw, batch, seq, init, emit_state):
    n = batch * seq
    ln = MLSTM_CHUNK
    nc = seq // ln
    h = M_HEADS
    hs = MLSTM_HEADS_PER_STEP
    ng = h // hs

    def tok(colblock):
        return pl.BlockSpec((seq, hs * LANES), lambda b, hg: (b, colblock(hg)))

    in_specs = [
        tok(lambda hg: hg), tok(lambda hg: ng + hg), tok(lambda hg: 2 * ng + hg),
        tok(lambda hg: hg),
        pl.BlockSpec((seq, LANES), lambda b, hg: (b, 0)),
        pl.BlockSpec((1, M_DV), lambda b, hg: (0, 0)),
    ]
    args = [qkv, qkv, qkv, og, mg, lw["m_norm_g"]]
    state_specs = [
        pl.BlockSpec((None, 2, hs, M_DK, M_DV), lambda b, hg: (b, 0, hg, 0, 0)),
        pl.BlockSpec((None, 2, hs, 1, M_DK), lambda b, hg: (b, 0, hg, 0, 0)),
        pl.BlockSpec((None, 2, hs, 1, ln), lambda b, hg: (b, 0, hg, 0, 0)),
    ]
    if init is not None:
        c_all, n_all, m_all, layer = init
        depth = c_all.shape[1]
        in_specs += [
            pl.BlockSpec((None, None, 2, hs, M_DK, M_DV), lambda b, hg: (b, layer, 0, hg, 0, 0)),
            pl.BlockSpec((None, None, 2, hs, 1, M_DK), lambda b, hg: (b, layer, 0, hg, 0, 0)),
            pl.BlockSpec((None, None, 2, hs, 1, ln), lambda b, hg: (b, layer, 0, hg, 0, 0)),
        ]
        args += [c_all, n_all.reshape(batch, depth, 2, h, 1, M_DK),
                 jnp.broadcast_to(m_all[..., None, None], (batch, depth, 2, h, 1, ln))]
    out_shape = [jax.ShapeDtypeStruct((n, h * M_DV), BF16)]
    out_specs = [tok(lambda hg: hg)]
    if emit_state:
        out_shape += [
            jax.ShapeDtypeStruct((batch, 2, h, M_DK, M_DV), F32),
            jax.ShapeDtypeStruct((batch, 2, h, 1, M_DK), F32),
            jax.ShapeDtypeStruct((batch, 2, h, 1, ln), F32),
        ]
        out_specs += state_specs
    per = (hs, 2, nc)
    outs = pl.pallas_call(
        functools.partial(_mlstm_kernel, n_chunks=nc, heads=hs, has_init=init is not None,
                          emit_state=emit_state),
        out_shape=out_shape,
        grid=(batch, ng),
        in_specs=in_specs,
        out_specs=out_specs,
        scratch_shapes=[
            pltpu.VMEM((seq, LANES), F32), pltpu.VMEM((seq, LANES), F32),
            pltpu.VMEM((nc, 4 * h, ln), F32), pltpu.VMEM((nc, 4 * h, ln), F32), pltpu.VMEM((nc, 4 * h, ln), F32),
            pltpu.VMEM(per + (M_DK, M_DV), F32), pltpu.VMEM(per + (1, M_DK), F32),
            pltpu.VMEM(per + (1, ln), F32), pltpu.VMEM(per + (1, ln), F32),
            pltpu.VMEM(per + (M_DK, M_DV), BF16), pltpu.VMEM(per + (1, M_DK), F32), pltpu.VMEM(per + (1, ln), F32),
            pltpu.VMEM((hs, 2, M_DK, M_DV), F32), pltpu.VMEM((hs, 2, 1, M_DK), F32), pltpu.VMEM((hs, 2, 1, ln), F32),
        ],
        compiler_params=_params(2),
        name="mlstm",
    )(*args)
    if emit_state:
        ym, cf, nf, mf = outs
        return ym, (cf, nf[:, :, :, 0, :], mf[:, :, :, 0, 0])
    return outs[0], None


HEAD_V = 64
VT_ROWS = 80


def _ones_row(v_t):
    return jnp.where(lax.broadcasted_iota(jnp.int32, v_t.shape, 0) == HEAD_V, 1.0, v_t)


def _flash_heads(qs, keys_of, vt_chunk, n_keys, o_ref, heads_per_rows):
    def values(e, s_t):
        m = acc = None
        for c0 in range(0, n_keys, KEY_BLOCK):
            c1 = min(c0 + KEY_BLOCK, n_keys)
            s = s_t[c0:c1, :]
            mc = jnp.max(s, axis=0, keepdims=True)
            if c0 == 0:
                m = mc
                acc = _dot(vt_chunk(e, c0, c1), jnp.exp2(s - mc).astype(BF16))
            else:
                m_new = jnp.maximum(m, mc)
                acc = acc * jnp.exp2(m - m_new) + _dot(vt_chunk(e, c0, c1), jnp.exp2(s - m_new).astype(BF16))
                m = m_new
        return acc[0:HEAD_V, :] / acc[HEAD_V:HEAD_V + 1, :]

    if n_keys <= KEY_BLOCK:
        s_all = [_dot_nt(keys_of(e), qb) for e, qb in enumerate(qs)]
        outs = [values(e, s_t) for e, s_t in enumerate(s_all)]
    else:
        outs = []
        s_next = _dot_nt(keys_of(0), qs[0])
        for e in range(len(qs)):
            s_t = s_next
            if e + 1 < len(qs):
                s_next = _dot_nt(keys_of(e + 1), qs[e + 1])
            outs.append(values(e, s_t))
    tq = qs[0].shape[0]
    for e in range(0, len(outs), 2):
        pair = jnp.concatenate([outs[e], outs[e + 1]], axis=0)
        r0 = (e // heads_per_rows) * tq
        c0 = ((e % heads_per_rows) // 2) * LANES
        o_ref[r0:r0 + tq, c0:c0 + LANES] = pair.T.astype(o_ref.dtype)


def _batch_per_step(batch, nq, past):
    return SHORT_SEQ_BATCH_PER_STEP if (nq == 1 and past == 0 and batch % SHORT_SEQ_BATCH_PER_STEP == 0) else 1


def _mla_kernel(*refs, past, rotary, heads, bb):
    q_ref, ckv_ref, akr_ref = refs[:3]
    pos = 3
    if rotary:
        mg_ref = refs[pos]
        pos += 1
    if past:
        ckvc_ref, krc_ref = refs[pos:pos + 2]
        pos += 2
    wk_ref, wvt_ref, kg_ref = refs[pos:pos + 3]
    pos += 3
    o_ref, kbuf, vbuf = refs[pos:]
    seq = ckv_ref.shape[0] // bb
    tq = q_ref.shape[0] // bb

    @pl.when(pl.program_id(2) == 0)
    def _():
        for i in range(bb * heads):
            sb, e = divmod(i, heads)
            rows = slice(sb * seq, (sb + 1) * seq)
            wk_h = wk_ref[:, e * LANES:(e + 1) * LANES]
            wvt_h = wvt_ref[e * VT_ROWS:(e + 1) * VT_ROWS, :]

            def keys(ckv, kr, kr_ss):
                cb = ckv.astype(BF16)
                k_nope = _dot(cb, wk_h)
                if kr_ss is None:
                    kcat = k_nope + kr
                    ss = jnp.sum(kcat * kcat, axis=-1, keepdims=True)
                    kcat = kcat * kg_ref[...]
                else:
                    ss = jnp.sum(k_nope * k_nope, axis=-1, keepdims=True) + kr_ss
                    kcat = k_nope * kg_ref[...] + kr
                return _ones_row(_dot_nt(wvt_h, cb)), kcat * lax.rsqrt(ss * (1.0 / A_QK) + EPS)

            kr_ss = mg_ref[rows, KR_SS_LANE:KR_SS_LANE + 1] if rotary else None
            v_t, kn = keys(ckv_ref[rows, :], akr_ref[rows, :], kr_ss)
            kbuf[i, past:past + seq, :] = kn.astype(BF16)
            vbuf[i, :, past:past + seq] = v_t.astype(BF16)
            if past:
                v_tc, knc = keys(ckvc_ref[...], krc_ref[...], None)
                kbuf[i, 0:past, :] = knc.astype(BF16)
                vbuf[i, :, 0:past] = v_tc.astype(BF16)

    qs = [q_ref[sb * tq:(sb + 1) * tq, e * LANES:(e + 1) * LANES] for sb in range(bb) for e in range(heads)]
    _flash_heads(qs, lambda i: kbuf[i], lambda i, c0, c1: vbuf[i, :, c0:c1], past + seq, o_ref, heads)


def _mla_call(qa, ckv, akr, mg, lw, batch, seq, cache, rotary):
    n = batch * seq
    tq = min(Q_BLOCK, seq)
    nq = seq // tq
    past = 0 if cache is None else cache[0].shape[2]
    heads = MLA_HEADS_PER_STEP if nq > 1 else A_HEADS
    bb = _batch_per_step(batch, nq, past)
    in_specs = [
        pl.BlockSpec((bb * tq, heads * LANES), lambda b, hg, qi: (b * nq + qi, hg)),
        pl.BlockSpec((bb * seq, LANES), lambda b, hg, qi: (b, 0)),
        pl.BlockSpec((bb * seq, LANES), lambda b, hg, qi: (b, 0)),
    ]
    args = [qa, ckv, akr]
    if rotary:
        in_specs += [pl.BlockSpec((bb * seq, LANES), lambda b, hg, qi: (b, 0))]
        args += [mg]
    layer = lw["layer"]
    if past:
        in_specs += [pl.BlockSpec((None, None, past, LANES), lambda b, hg, qi: (b, layer, 0, 0))] * 2
        args += [cache[0], cache[1]]
    in_specs += [
        pl.BlockSpec((None, A_KV_LORA, heads * LANES), lambda b, hg, qi: (layer, 0, hg)),
        pl.BlockSpec((None, heads * VT_ROWS, A_KV_LORA), lambda b, hg, qi: (layer, hg, 0)),
        pl.BlockSpec((1, LANES), lambda b, hg, qi: (0, 0)),
    ]
    args += [lw["w_uk"], lw["w_uvt"], lw["a_knorm_g"]]
    return pl.pallas_call(
        functools.partial(_mla_kernel, past=past, rotary=rotary, heads=heads, bb=bb),
        out_shape=jax.ShapeDtypeStruct((n, A_HEADS * A_V), BF16),
        grid=(batch // bb, A_HEADS // heads, nq),
        in_specs=in_specs,
        out_specs=pl.BlockSpec((bb * tq, heads * A_V), lambda b, hg, qi: (b * nq + qi, hg)),
        scratch_shapes=[pltpu.VMEM((bb * heads, past + seq, LANES), BF16),
                        pltpu.VMEM((bb * heads, VT_ROWS, past + seq), BF16)],
        compiler_params=_params(3),
        name="mla_attn",
    )(*args)


def _gqa_kernel(*refs, past, groups, bb):
    q_ref, k_ref, v_ref = refs[:3]
    pos = 3
    if past:
        kc_ref, vc_ref = refs[pos:pos + 2]
        pos += 2
    o_ref, kbuf, vbuf = refs[pos:]
    seq = k_ref.shape[0] // bb
    tq = q_ref.shape[0] // bb
    qw = G_GROUP * G_HD

    @pl.when(pl.program_id(2) == 0)
    def _():
        for i in range(bb * groups):
            sb, gi = divmod(i, groups)
            grp = pl.program_id(1) * groups + gi

            r = lax.broadcasted_iota(jnp.int32, (VT_ROWS, LANES), 0)
            c = lax.broadcasted_iota(jnp.int32, (VT_ROWS, LANES), 1)
            pick = jnp.where((c == r + grp * G_HD) & (r < G_HD), 1.0, 0.0).astype(BF16)

            def fill(r0, k, v):
                low = _lane(k.shape) < G_HD
                k_lo = jnp.where(grp == 0, k, pltpu.roll(k, G_HD, 1))
                rows = k.shape[0]
                kbuf[i, r0:r0 + rows, :] = jnp.where(low, k_lo, pltpu.roll(k_lo, G_HD, 1)).astype(BF16)
                vbuf[i, :, r0:r0 + rows] = _ones_row(_dot_nt(pick, v.astype(BF16))).astype(BF16)

            fill(past, k_ref[sb * seq:(sb + 1) * seq, :], v_ref[sb * seq:(sb + 1) * seq, :])
            if past:
                fill(0, kc_ref[...], vc_ref[...])

    qs = []
    for i in range(bb * groups):
        sb, gi = divmod(i, groups)
        for j in range(G_GROUP):
            col = q_ref[sb * tq:(sb + 1) * tq, gi * qw + (j // 2) * LANES:gi * qw + (j // 2 + 1) * LANES]
            keep = (_lane(col.shape) < G_HD) == (j % 2 == 0)
            qs.append(jnp.where(keep, col, jnp.zeros_like(col)))
    _flash_heads(qs, lambda e: kbuf[e // G_GROUP], lambda e, c0, c1: vbuf[e // G_GROUP, :, c0:c1],
                 past + seq, o_ref, groups * G_GROUP)


def _gqa_call(gq, gk, gv, batch, seq, cache):
    n = batch * seq
    tq = min(Q_BLOCK, seq)
    nq = seq // tq
    past = 0 if cache is None else cache[0].shape[2]
    groups = GQA_GROUPS_PER_STEP if nq > 1 else G_KV_HEADS
    bb = _batch_per_step(batch, nq, past)
    qw = G_GROUP * G_HD
    kvw = G_KV_HEADS * G_HD
    in_specs = [
        pl.BlockSpec((bb * tq, groups * qw), lambda b, g, qi: (b * nq + qi, g)),
        pl.BlockSpec((bb * seq, kvw), lambda b, g, qi: (b, 0)),
        pl.BlockSpec((bb * seq, kvw), lambda b, g, qi: (b, 0)),
    ]
    args = [gq, gk, gv]
    if past:
        layer = cache[2]
        in_specs += [pl.BlockSpec((None, None, past, kvw), lambda b, g, qi: (b, layer, 0, 0))] * 2
        args += [cache[0], cache[1]]
    return pl.pallas_call(
        functools.partial(_gqa_kernel, past=past, groups=groups, bb=bb),
        out_shape=jax.ShapeDtypeStruct((n, G_HEADS * G_HD), BF16),
        grid=(batch // bb, G_KV_HEADS // groups, nq),
        in_specs=in_specs,
        out_specs=pl.BlockSpec((bb * tq, groups * qw), lambda b, g, qi: (b * nq + qi, g)),
        scratch_shapes=[pltpu.VMEM((bb * groups, past + seq, LANES), BF16),
                        pltpu.VMEM((bb * groups, VT_ROWS, past + seq), BF16)],
        compiler_params=_params(3),
        name="gqa_attn",
    )(*args)


def _merge_ffn_kernel(x_ref, ym_ref, ya_ref, yg_ref, gates_ref, mod_ref, g2_ref, wb_ref, wo_ref, wfi_ref,
                      wfo_ref, o_ref, *, ff_chunks):
    x = x_ref[...]
    d = x.shape[-1]
    mixed = None
    for i, y_ref in enumerate((ym_ref, ya_ref, yg_ref)):
        br = _dot(y_ref[...].astype(BF16), wb_ref[i]) * gates_ref[:, i * d:(i + 1) * d]
        mixed = br if mixed is None else mixed + br
    gt1 = mod_ref[2:3, :]
    x1 = x + gt1 * _dot(mixed.astype(BF16), wo_ref[...])

    sh2 = mod_ref[3:4, :]
    sc2 = mod_ref[4:5, :]
    gt2 = mod_ref[5:6, :]
    h2 = ((_rms(x1, d) * g2_ref[...]) * (1.0 + sc2) + sh2).astype(BF16)
    d_ff = wfo_ref.shape[0]
    acc = None
    for c0, c1 in ff_chunks:
        ug = _dot(h2, wfi_ref[:, c0:c1])
        uv = _dot(h2, wfi_ref[:, d_ff + c0:d_ff + c1])
        act = (ug * _sigmoid(ug) * uv).astype(BF16)
        part = _dot(act, wfo_ref[c0:c1, :])
        acc = part if acc is None else acc + part
    o_ref[...] = x1 + gt2 * acc


def _merge_ffn_call(x, ym, ya, yg, gates, mod, lw, mod_row0, rows_per_mod):
    n, d = x.shape
    tm = TOKEN_BLOCK
    d_ff = lw["w_ffn_out"].shape[1]
    layer = lw["layer"]
    split = -(-(d_ff // 2) // MXU_COLS) * MXU_COLS
    ff_chunks = ((0, split), (split, d_ff)) if 0 < split < d_ff else ((0, d_ff),)

    def mod_idx(i):
        return (mod_row0 + (i * tm) // rows_per_mod, 0, 0)

    def rows(width):
        return pl.BlockSpec((tm, width), lambda i: (i, 0))

    return pl.pallas_call(
        functools.partial(_merge_ffn_kernel, ff_chunks=ff_chunks),
        out_shape=jax.ShapeDtypeStruct((n, d), F32),
        grid=(n // tm,),
        in_specs=[
            rows(d), rows(BRANCH_WIDTH), rows(BRANCH_WIDTH), rows(BRANCH_WIDTH), rows(N_BRANCH * d),
            pl.BlockSpec((None, N_MOD, d), mod_idx),
            _resident((1, d)),
            _resident((N_BRANCH, BRANCH_WIDTH, d), layer),
            _resident((d, d), layer),
            _resident((d, 2 * d_ff), layer),
            _resident((d_ff, d), layer),
        ],
        out_specs=rows(d),
        compiler_params=_params(1),
        name="merge_ffn",
    )(x, ym, ya, yg, gates, mod, lw["norm2_g"], lw["w_branch"], lw["w_out"], lw["w_ffn_in"], lw["w_ffn_out"])


def _mla_q_layout(a):
    lead = a.shape[:-1]
    a = a.reshape(lead + (A_HEADS, A_QK))
    z = jnp.zeros(lead + (A_HEADS, LANES - A_QK), a.dtype)
    return jnp.concatenate([a[..., A_NOPE:], z, a[..., :A_NOPE]], axis=-1).reshape(lead + (A_HEADS * LANES,))


def _stacked_weights(w_in, w_uq, w_ukv, w_branch, w_out, w_ffn_in, w_ffn_out):
    depth, d, _ = w_in.shape
    hw = M_HEADS * M_DK
    sizes = (N_BRANCH * d, hw, hw, hw, hw, 4 * M_HEADS, A_Q_LORA, A_KV_LORA, A_ROPE, G_HEADS * G_HD,
             G_KV_HEADS * G_HD, G_KV_HEADS * G_HD)
    edges = np.concatenate([[0], np.cumsum(sizes)]).tolist()
    (wmg, waq, wakv, wakr, wgq, wgk, wgv) = [w_in[:, :, a:b] for a, b in zip(edges[5:-1], edges[6:])]
    misc_pad = jnp.zeros((depth, d, LANES - A_ROPE - 4 * M_HEADS), w_in.dtype)
    tail = jnp.concatenate([waq, wakv, wakr, wmg, misc_pad, wgq, wgk, wgv], axis=2).astype(BF16)
    ukv = w_ukv.reshape(depth, A_KV_LORA, A_HEADS, A_NOPE + A_V)
    uk = jnp.pad(ukv[..., :A_NOPE], ((0, 0), (0, 0), (0, 0), (LANES - A_NOPE, 0)))
    uvt = jnp.pad(ukv[..., A_NOPE:].transpose(0, 2, 3, 1), ((0, 0), (0, 0), (0, VT_ROWS - A_V), (0, 0)))
    return dict(
        w_in=w_in.astype(BF16),
        w_in_tail=tail,
        w_uq=_mla_q_layout(w_uq).astype(BF16),
        w_uk=uk.reshape(depth, A_KV_LORA, A_HEADS * LANES).astype(BF16),
        w_uvt=uvt.reshape(depth, A_HEADS * VT_ROWS, A_KV_LORA).astype(BF16),
        w_branch=w_branch.astype(BF16),
        w_out=w_out.astype(BF16),
        w_ffn_in=w_ffn_in.astype(BF16),
        w_ffn_out=w_ffn_out.astype(BF16),
    )


def _layer_vectors(l, b_mgate, norm1_g, m_norm_g, a_qlora_g, a_kvlora_g, a_qnorm_g, a_knorm_g, g_qnorm_g,
                   g_knorm_g, norm2_g):
    return dict(
        layer=l,
        b_mgate=jnp.pad(b_mgate[l][None, :], ((0, 0), (MG_LANE0, LANES - MG_LANE0 - 4 * M_HEADS))),
        norm1_g=norm1_g[l][None, :],
        m_norm_g=m_norm_g[l][None, :],
        a_qlora_g=a_qlora_g[l][None, :],
        a_kvlora_g=a_kvlora_g[l][None, :],
        a_qnorm_g=_mla_q_layout(jnp.tile(a_qnorm_g[l], A_HEADS)[None, :])[:, :LANES],
        a_knorm_g=_mla_q_layout(jnp.tile(a_knorm_g[l], A_HEADS)[None, :])[:, :LANES],
        g_qnorm_g=jnp.tile(g_qnorm_g[l], LANES // G_HD)[None, :],
        g_knorm_g=jnp.tile(g_knorm_g[l], G_KV_HEADS)[None, :],
        norm2_g=norm2_g[l][None, :],
    )


def _axial_angles(seq, rot_dim):
    n_freq = rot_dim // 4
    freqs = ROPE_BASE ** (-jnp.arange(n_freq, dtype=F32) / n_freq)
    t = jnp.arange(seq)
    row = (t // GRID_W).astype(F32)
    col = (t % GRID_W).astype(F32)
    return jnp.concatenate([row[:, None] * freqs, col[:, None] * freqs], axis=-1)


def _rope_tables(seq):
    ang = _axial_angles(seq, A_ROPE)
    one = jnp.ones((seq, LANES - A_ROPE), F32)
    mla_cos = jnp.concatenate([jnp.cos(ang), jnp.cos(ang), one], axis=-1)
    mla_sin = jnp.concatenate([-jnp.sin(ang), jnp.sin(ang), 0.0 * one], axis=-1)
    ang = _axial_angles(seq, G_HD)
    cos = jnp.concatenate([jnp.cos(ang), jnp.cos(ang)], axis=-1)
    sin = jnp.concatenate([-jnp.sin(ang), jnp.sin(ang)], axis=-1)
    gqa = (jnp.tile(cos, (1, LANES // G_HD)), jnp.tile(sin, (1, LANES // G_HD)))
    return (mla_cos, mla_sin), gqa


def _layer(x, mod, lw, batch, seq, mod_row0, rows_per_mod, ctx, rope):
    (gates, qkv, og, mg, qa, ckv, akr, gq, gk, gv) = _inproj_call(x, mod, lw, mod_row0, rows_per_mod, rope, seq)
    if ctx is None:
        ym, state = _mlstm_call(qkv, og, mg, lw, batch, seq, None, True)
        ya = _mla_call(qa, ckv, akr, mg, lw, batch, seq, None, False)
        yg = _gqa_call(gq, gk, gv, batch, seq, None)
        new_ctx = dict(state=state, ckv=ckv, kr=akr[:, :A_ROPE], gk=gk, gv=gv)
    else:
        ym, _ = _mlstm_call(qkv, og, mg, lw, batch, seq, ctx["mlstm"], False)
        ya = _mla_call(qa, ckv, akr, mg, lw, batch, seq, ctx["mla"], True)
        yg = _gqa_call(gq, gk, gv, batch, seq, ctx["gqa"])
        new_ctx = None
    x = _merge_ffn_call(x, ym, ya, yg, gates, mod, lw, mod_row0, rows_per_mod)
    return x, new_ctx


def kernel(x_prompt, x_sample, state_mlstm_C, state_mlstm_n, state_mlstm_m, cache_mla_ckv, cache_mla_krope,
           cache_gqa_k, cache_gqa_v, c, c_ctx, w_mod, b_mod, norm1_g, w_in, b_mgate, m_norm_g, a_qlora_g,
           a_kvlora_g, w_uq, w_ukv, a_qnorm_g, a_knorm_g, g_qnorm_g, g_knorm_g, w_branch, w_out, norm2_g,
           w_ffn_in, w_ffn_out):
    batch, seq, d = x_prompt.shape
    dbatch, dseq, _ = x_sample.shape
    depth = w_in.shape[0]
    past = cache_mla_ckv.shape[2]
    for tokens, length in ((batch * seq, seq), (dbatch * dseq, dseq)):
        assert tokens % TOKEN_BLOCK == 0 and length % MLSTM_CHUNK == 0 and length % min(Q_BLOCK, length) == 0
        assert TOKEN_BLOCK % length == 0 or length % TOKEN_BLOCK == 0
    assert dseq % TOKEN_BLOCK == 0 and dseq % GRID_W == 0 and past % LANES == 0
    assert MLSTM_HEADS_PER_STEP == M_HEADS

    n_rows = -(-(1 + dbatch) // SUBLANES) * SUBLANES
    cond = jnp.concatenate([c_ctx[None, :], c, jnp.zeros((n_rows - 1 - dbatch, d), F32)], axis=0)
    mod_all = _mod_call(cond, w_mod, b_mod).reshape(depth, n_rows, N_MOD, d)

    rope = _rope_tables(dseq)
    xp = x_prompt.reshape(batch * seq, d)
    xs = x_sample.reshape(dbatch * dseq, d)
    ctx_layers = []
    krope_cache = jnp.pad(cache_mla_krope, ((0, 0), (0, 0), (0, 0), (0, LANES - A_ROPE)))
    gqa_k_cache = cache_gqa_k.reshape(dbatch, depth, past, G_KV_HEADS * G_HD)
    gqa_v_cache = cache_gqa_v.reshape(dbatch, depth, past, G_KV_HEADS * G_HD)
    stacked = _stacked_weights(w_in, w_uq, w_ukv, w_branch, w_out, w_ffn_in, w_ffn_out)
    for l in range(depth):
        lw = dict(stacked, **_layer_vectors(l, b_mgate, norm1_g, m_norm_g, a_qlora_g, a_kvlora_g, a_qnorm_g,
                                            a_knorm_g, g_qnorm_g, g_knorm_g, norm2_g))
        xp, st = _layer(xp, mod_all[l], lw, batch, seq, 0, batch * seq, None, None)
        ctx_layers.append(st)
        ctx = dict(mlstm=(state_mlstm_C, state_mlstm_n, state_mlstm_m, l), mla=(cache_mla_ckv, krope_cache),
                   gqa=(gqa_k_cache, gqa_v_cache, l))
        xs, _ = _layer(xs, mod_all[l], lw, dbatch, dseq, 1, dseq, ctx, rope)

    def stack(fn):
        return jnp.stack([fn(s) for s in ctx_layers], axis=1)

    new_c = stack(lambda s: s["state"][0])
    new_n = stack(lambda s: s["state"][1])
    new_m = stack(lambda s: s["state"][2])
    new_ckv = stack(lambda s: s["ckv"].reshape(batch, seq, A_KV_LORA))
    new_kr = stack(lambda s: s["kr"].reshape(batch, seq, A_ROPE))
    new_gk = stack(lambda s: s["gk"].reshape(batch, seq, G_KV_HEADS, G_HD))
    new_gv = stack(lambda s: s["gv"].reshape(batch, seq, G_KV_HEADS, G_HD))
    return (xp.reshape(batch, seq, d), xs.reshape(dbatch, dseq, d), new_c, new_n, new_m, new_ckv, new_kr,
            new_gk, new_gv)
```

```python
import functools

import numpy as np
import jax
import jax.numpy as jnp
from jax import lax
from jax.experimental import pallas as pl
from jax.experimental.pallas import tpu as pltpu

F32 = jnp.float32
BF16 = jnp.bfloat16

LANES = 128
SUBLANES = 8
VMEM_LIMIT_BYTES = 56 * 1024 * 1024

EPS = 1e-6
ROPE_BASE = 10000.0
GRID_W = 64

M_HEADS = 4
M_DK = 128
M_DV = 128
A_HEADS = 8
A_NOPE = 64
A_ROPE = 32
A_QK = A_NOPE + A_ROPE
A_V = 64
A_Q_LORA = 256
A_KV_LORA = 128
G_HEADS = 8
G_KV_HEADS = 2
G_GROUP = G_HEADS // G_KV_HEADS
G_HD = 64
N_BRANCH = 3
BRANCH_WIDTH = 512
N_MOD = 6

TOKEN_BLOCK = 512
Q_BLOCK = 256
KEY_BLOCK = 256
MLA_HEADS_PER_STEP = 8
GQA_GROUPS_PER_STEP = 2
SHORT_SEQ_BATCH_PER_STEP = 4
SCORES_AHEAD = 2
MLSTM_CHUNK = 128
MLSTM_HEADS_PER_STEP = 4
LOG2E = 1.4426950408889634


def _params(n_axes):
    return pltpu.CompilerParams(dimension_semantics=("arbitrary",) * n_axes,
                                vmem_limit_bytes=VMEM_LIMIT_BYTES)


def _resident(shape, layer=None):
    nd = len(shape)
    if layer is None:
        return pl.BlockSpec(shape, lambda *_: (0,) * nd, pipeline_mode=pl.Buffered(1))
    return pl.BlockSpec((None,) + tuple(shape), lambda *_: (layer,) + (0,) * nd, pipeline_mode=pl.Buffered(1))


def _lane(shape, axis=None):
    return lax.broadcasted_iota(jnp.int32, shape, len(shape) - 1 if axis is None else axis)


def _dot(a, b):
    return jnp.dot(a, b, preferred_element_type=F32)


def _dot_nt(a, b):
    return lax.dot_general(a, b, (((1,), (1,)), ((), ())), preferred_element_type=F32)


def _split3(a):
    hi = a.astype(BF16)
    r1 = a - hi.astype(F32)
    mid = r1.astype(BF16)
    lo = (r1 - mid.astype(F32)).astype(BF16)
    return hi, mid, lo


def _dot01(a, m01):
    hi, mid, lo = _split3(a)
    return _dot(hi, m01) + _dot(mid, m01) + _dot(lo, m01)


def _dot01_left(m01, a):
    hi, mid, lo = _split3(a)
    return _dot(m01, hi) + _dot(m01, mid) + _dot(m01, lo)


def _sigmoid(x):
    return 0.5 * jnp.tanh(0.5 * x) + 0.5


def _log_sigmoid(x):
    return jnp.minimum(x, 0.0) - jnp.log(1.0 + jnp.exp(-jnp.abs(x)))


def _rms(x, width):
    ms = jnp.sum(x * x, axis=-1, keepdims=True) * (1.0 / width)
    return x * lax.rsqrt(ms + EPS)


def _rope(x, cos, sin_signed, half, period):
    n = x.shape[-1]
    first = (_lane(x.shape) % period) < half
    swapped = jnp.where(first, pltpu.roll(x, n - half, x.ndim - 1), pltpu.roll(x, half, x.ndim - 1))
    return x * cos + swapped * sin_signed


def _mod_kernel(c_ref, w_ref, b_ref, o_ref):
    c = c_ref[...]
    a = c * _sigmoid(c)
    o_ref[...] = _dot01(a, w_ref[...].astype(BF16)) + b_ref[...]


def _mod_call(cond, w_mod, b_mod):
    depth, d, n = w_mod.shape
    rows = cond.shape[0]
    tn = 1536
    return pl.pallas_call(
        _mod_kernel,
        out_shape=jax.ShapeDtypeStruct((depth, rows, n), F32),
        grid=(depth, n // tn),
        in_specs=[
            pl.BlockSpec((rows, d), lambda l, j: (0, 0)),
            pl.BlockSpec((None, d, tn), lambda l, j: (l, 0, j)),
            pl.BlockSpec((None, 1, tn), lambda l, j: (l, 0, j)),
        ],
        out_specs=pl.BlockSpec((None, rows, tn), lambda l, j: (l, 0, j)),
        compiler_params=_params(2),
        name="adaln_mod",
    )(cond, w_mod, b_mod.reshape(depth, 1, n))


MXU_COLS = 256
_GATE0, _GATE1 = 0, 3072
_MQ0 = 3072
_MK0 = 3584
_MV0 = 4096
_MO0 = 4608
_AQ0 = 5120
_AKV0 = 5376
_MISC0 = 5504
_GQ0 = 5632
_GK0 = 6144
_GV0 = 6272
_WIN_COLS = 6400
MG_LANE0 = A_ROPE
KR_SS_LANE = 64


def _head_pair_ms(x):
    low = (_lane(x.shape) % LANES) < G_HD
    sq = x * x
    cols = []
    for c0 in range(0, x.shape[-1], LANES):
        s = sq[:, c0:c0 + LANES]
        lo = jnp.sum(jnp.where(low[:, c0:c0 + LANES], s, 0.0), axis=-1, keepdims=True)
        hi = jnp.sum(jnp.where(low[:, c0:c0 + LANES], 0.0, s), axis=-1, keepdims=True)
        cols.append(jnp.where(low[:, c0:c0 + LANES], lo, hi))
    ms = cols[0] if len(cols) == 1 else jnp.concatenate(cols, axis=-1)
    return ms * (1.0 / G_HD)


def _inproj_kernel(*refs, rotary):
    (x_ref, mod_ref, g1_ref, w_ref, wt_ref, bmg_ref, gql_ref, wuq_ref, gkvl_ref, gkn_ref, aqn_ref,
     gqn_ref, akn_ref) = refs[:13]
    pos = 13
    if rotary:
        ca_ref, sa_ref, cg_ref, sg_ref = refs[pos:pos + 4]
        pos += 4
    (gates_ref, qkv_ref, og_ref, mg_ref, qa_ref, ckv_ref, akr_ref, gq_ref, gk_ref, gv_ref) = refs[pos:]
    x = x_ref[...]
    d = x.shape[-1]
    sh1 = mod_ref[0:1, :]
    sc1 = mod_ref[1:2, :]
    h = (_rms(x, d) * g1_ref[...]) * (1.0 + sc1) + sh1
    hb = h.astype(BF16)

    def proj(c0, width):
        if c0 < _AQ0:
            return _dot(hb, w_ref[:, c0:c0 + width])
        return _dot(hb, wt_ref[:, c0 - _AQ0:c0 - _AQ0 + width])

    aq = _rms(proj(_AQ0, A_Q_LORA), A_Q_LORA) * gql_ref[...]
    qa = _dot(aq.astype(BF16), wuq_ref[...])
    for hd in range(A_HEADS):
        qh = _rms(qa[:, hd * LANES:(hd + 1) * LANES], A_QK) * aqn_ref[...]
        if rotary:
            qh = _rope(qh, ca_ref[...], sa_ref[...], A_ROPE // 2, LANES)
        qa_ref[:, hd * LANES:(hd + 1) * LANES] = (qh * (A_QK ** -0.5 * LOG2E)).astype(qa_ref.dtype)
    akv_misc = proj(_AKV0, A_KV_LORA + LANES)
    ckv_ref[...] = _rms(akv_misc[:, :A_KV_LORA], A_KV_LORA) * gkvl_ref[...]
    misc = akv_misc[:, A_KV_LORA:]
    kr = jnp.where(_lane(misc.shape) < A_ROPE, misc, 0.0)
    mg = misc + bmg_ref[...]
    if rotary:
        akr_ref[...] = _rope(kr * akn_ref[...], ca_ref[...], sa_ref[...], A_ROPE // 2, LANES)
        mg = jnp.where(_lane(mg.shape) == KR_SS_LANE, jnp.sum(kr * kr, axis=-1, keepdims=True), mg)
    else:
        akr_ref[...] = kr
    mg_ref[...] = mg

    gq_all = proj(_GQ0, G_HEADS * G_HD)
    for c0 in range(0, G_HEADS * G_HD, LANES):
        gq = gq_all[:, c0:c0 + LANES]
        gq = gq * lax.rsqrt(_head_pair_ms(gq) + EPS) * gqn_ref[...]
        if rotary:
            gq = _rope(gq, cg_ref[...], sg_ref[...], G_HD // 2, G_HD)
        gq_ref[:, c0:c0 + LANES] = (gq * (G_HD ** -0.5 * LOG2E)).astype(gq_ref.dtype)
    kvw = G_KV_HEADS * G_HD
    gkv = proj(_GK0, 2 * kvw)
    gk = gkv[:, :kvw]
    gk = gk * lax.rsqrt(_head_pair_ms(gk) + EPS) * gkn_ref[...]
    if rotary:
        gk = _rope(gk, cg_ref[...], sg_ref[...], G_HD // 2, G_HD)
    gk_ref[...] = gk
    gv_ref[...] = gkv[:, kvw:]

    hw = M_HEADS * M_DK
    gates_ref[...] = _sigmoid(proj(_GATE0, _GATE1 - _GATE0)).astype(gates_ref.dtype)
    og_ref[...] = _sigmoid(proj(_MO0, hw)).astype(og_ref.dtype)
    qkv_ref[:, 0:hw] = proj(_MQ0, hw).astype(qkv_ref.dtype)
    qkv_ref[:, hw:2 * hw] = (proj(_MK0, hw) * (M_DK ** -0.5)).astype(qkv_ref.dtype)
    qkv_ref[:, 2 * hw:3 * hw] = proj(_MV0, hw).astype(qkv_ref.dtype)


def _inproj_call(x, mod, lw, mod_row0, rows_per_mod, rope, seq):
    n, d = x.shape
    tm = TOKEN_BLOCK
    hw = M_HEADS * M_DK

    def mod_idx(i):
        return (mod_row0 + (i * tm) // rows_per_mod, 0, 0)

    def rows(width):
        return pl.BlockSpec((tm, width), lambda i: (i, 0))

    in_specs = [
        rows(d),
        pl.BlockSpec((None, N_MOD, d), mod_idx),
        _resident((1, d)),
        _resident(lw["w_in"].shape[1:], lw["layer"]),
        _resident((d, _WIN_COLS - _AQ0), lw["layer"]),
        _resident((1, LANES)),
        _resident((1, A_Q_LORA)),
        _resident((A_Q_LORA, A_HEADS * LANES), lw["layer"]),
        _resident((1, A_KV_LORA)),
        _resident((1, LANES)),
        _resident((1, LANES)),
        _resident((1, LANES)),
        _resident((1, LANES)),
    ]
    args = [x, mod, lw["norm1_g"], lw["w_in"], lw["w_in_tail"], lw["b_mgate"], lw["a_qlora_g"], lw["w_uq"],
            lw["a_kvlora_g"], lw["g_knorm_g"], lw["a_qnorm_g"], lw["g_qnorm_g"], lw["a_knorm_g"]]
    if rope is not None:
        blocks_per_seq = seq // tm
        in_specs += [pl.BlockSpec((tm, LANES), lambda i: (i % blocks_per_seq, 0))] * 4
        args += [rope[0][0], rope[0][1], rope[1][0], rope[1][1]]
    out_widths = [3 * d, 3 * hw, hw, LANES, A_HEADS * LANES, A_KV_LORA, LANES, G_HEADS * G_HD,
                  G_KV_HEADS * G_HD, G_KV_HEADS * G_HD]
    out_dtypes = [BF16, BF16, BF16, F32, BF16, F32, F32, BF16, F32, F32]
    return pl.pallas_call(
        functools.partial(_inproj_kernel, rotary=rope is not None),
        out_shape=[jax.ShapeDtypeStruct((n, w), t) for w, t in zip(out_widths, out_dtypes)],
        grid=(n // tm,),
        in_specs=in_specs,
        out_specs=[rows(w) for w in out_widths],
        compiler_params=_params(1),
        name="in_proj",
    )(*args)


def _mlstm_kernel(*refs, n_chunks, heads, has_init, emit_state):
    q_ref, k_ref, v_ref, og_ref, mg_ref, gn_ref = refs[:6]
    pos = 6
    if has_init:
        c0_ref, n0_ref, m0_ref = refs[pos:pos + 3]
        pos += 3
    y_ref = refs[pos]
    pos += 1
    if emit_state:
        cf_ref, nf_ref, mf_ref = refs[pos:pos + 3]
        pos += 3
    (pcol_scr, lfc_scr, grow_scr, prow_scr, lfr_scr, u_scr, nu_scr, gm_scr, tot_scr, cs_scr, ns_scr, ms_scr,
     c_scr, n_scr, m_scr) = refs[pos:]
    assert heads == M_HEADS
    n_gates = 4 * M_HEADS

    ln = MLSTM_CHUNK
    row = lax.broadcasted_iota(jnp.int32, (ln, ln), 0)
    col = lax.broadcasted_iota(jnp.int32, (ln, ln), 1)
    lower = col <= row
    upper = col >= row
    tril = jnp.where(lower, 1.0, 0.0).astype(BF16)
    triu = jnp.where(upper, 1.0, 0.0).astype(BF16)
    ones = jnp.ones((ln, ln), BF16)

    def gate_index(hh, d):
        return 2 * d * heads + hh, (2 * d + 1) * heads + hh

    def summaries(c, carry):
        r0 = pl.multiple_of(c * ln, ln)
        g = mg_ref[pl.ds(r0, ln), :]
        lf = _log_sigmoid(g) * LOG2E
        pcol_scr[pl.ds(r0, ln), :] = _dot01_left(tril, lf)
        lfc_scr[pl.ds(r0, ln), :] = lf
        g_rows = g.T[MG_LANE0:MG_LANE0 + n_gates, :]
        gr = g_rows * LOG2E
        lfr = _log_sigmoid(g_rows) * LOG2E
        pr = _dot01(lfr, triu)
        grow_scr[c] = gr
        prow_scr[c] = pr
        lfr_scr[c] = lfr
        for hh in range(heads):
            kb = k_ref[pl.ds(r0, ln), hh * LANES:(hh + 1) * LANES]
            k_t = kb.astype(F32).T
            v = v_ref[pl.ds(r0, ln), hh * LANES:(hh + 1) * LANES]
            for d in range(2):
                ji, jf = gate_index(hh, d)
                tot = pr[jf:jf + 1, ln - 1:ln]
                b_row = pr[jf:jf + 1, :] if d == 0 else tot - pr[jf:jf + 1, :] + lfr[jf:jf + 1, :]
                g_row = tot - b_row + gr[ji:ji + 1, :]
                gmax = jnp.max(g_row, axis=-1, keepdims=True)
                wg = jnp.exp2(g_row - gmax)
                u_scr[hh, d, c] = _dot((k_t * wg).astype(BF16), v)
                nu_scr[hh, d, c] = _dot(jnp.broadcast_to(wg, (SUBLANES, ln)).astype(BF16), kb)[0:1, :]
                gm_scr[hh, d, c] = jnp.broadcast_to(gmax, (1, ln))
                tot_scr[hh, d, c] = jnp.broadcast_to(tot, (1, ln))
        return carry

    lax.fori_loop(0, n_chunks, summaries, 0, unroll=min(2, n_chunks))

    for hh in range(heads):
        for d in range(2):
            if has_init:
                c_scr[hh, d] = c0_ref[d, hh]
                n_scr[hh, d] = n0_ref[d, hh]
                m_scr[hh, d] = m0_ref[d, hh] * LOG2E
            else:
                c_scr[hh, d] = jnp.zeros((M_DK, M_DV), F32)
                n_scr[hh, d] = jnp.zeros((1, M_DK), F32)
                m_scr[hh, d] = jnp.zeros((1, ln), F32)

    def scan(j, carry):
        for hh in range(heads):
            for d in range(2):
                c = j if d == 0 else n_chunks - 1 - j
                cst = c_scr[hh, d]
                nst = n_scr[hh, d]
                mst = m_scr[hh, d]
                cs_scr[hh, d, c] = cst.astype(BF16)
                ns_scr[hh, d, c] = nst
                ms_scr[hh, d, c] = mst
                gmax = gm_scr[hh, d, c]
                total = tot_scr[hh, d, c] + mst
                m_new = jnp.maximum(total, gmax)
                decay = jnp.exp2(total - m_new)
                scale = jnp.exp2(gmax - m_new)
                c_scr[hh, d] = cst * decay + u_scr[hh, d, c] * scale
                n_scr[hh, d] = nst * decay + nu_scr[hh, d, c] * scale
                m_scr[hh, d] = m_new
        return carry

    lax.fori_loop(0, n_chunks, scan, 0)

    def readout(c, carry):
        r0 = pl.multiple_of(c * ln, ln)
        pc = pcol_scr[pl.ds(r0, ln), :]
        lf = lfc_scr[pl.ds(r0, ln), :]
        gr = grow_scr[c]
        pr = prow_scr[c]
        lfr = lfr_scr[c]
        tot_all = pc[ln - 1:ln, :]
        for hh in range(heads):
            q = q_ref[pl.ds(r0, ln), hh * LANES:(hh + 1) * LANES]
            k = k_ref[pl.ds(r0, ln), hh * LANES:(hh + 1) * LANES]
            v = v_ref[pl.ds(r0, ln), hh * LANES:(hh + 1) * LANES]
            qk = _dot_nt(q, k)
            hsum = None
            for d in range(2):
                ji, jf = gate_index(hh, d)
                lane_f = MG_LANE0 + jf
                if d == 0:
                    b_col = pc[:, lane_f:lane_f + 1]
                    r_row = gr[ji:ji + 1, :] - pr[jf:jf + 1, :]
                    mask = lower
                else:
                    tot = tot_all[:, lane_f:lane_f + 1]
                    b_col = tot - pc[:, lane_f:lane_f + 1] + lf[:, lane_f:lane_f + 1]
                    r_row = gr[ji:ji + 1, :] - (tot - pr[jf:jf + 1, :] + lfr[jf:jf + 1, :])
                    mask = upper
                b_colb = jnp.broadcast_to(b_col, (ln, ln))
                d_log = jnp.where(mask, b_colb + r_row, -jnp.inf)
                dmax = jnp.broadcast_to(jnp.max(d_log, axis=-1, keepdims=True), (ln, ln))
                inter = b_colb + ms_scr[hh, d, c]
                m_t = jnp.maximum(inter, dmax)
                sb = (qk * jnp.exp2(d_log - m_t)).astype(BF16)
                w_inter = jnp.exp2(inter - m_t)
                qc = _dot(q, cs_scr[hh, d, c])
                qn = _dot_nt(q, jnp.broadcast_to(ns_scr[hh, d, c], (ln, M_DK)).astype(BF16))
                num = qc * w_inter + _dot(sb, v)
                den = qn * w_inter + _dot(sb, ones)
                hc = num / jnp.maximum(jnp.abs(den), jnp.exp2(-m_t))
                hsum = hc if hsum is None else hsum + hc
            y_ref[pl.ds(r0, ln), hh * LANES:(hh + 1) * LANES] = (
                _rms(hsum, M_DV) * gn_ref[...] * og_ref[pl.ds(r0, ln), hh * LANES:(hh + 1) * LANES]
            ).astype(y_ref.dtype)
        return carry

    lax.fori_loop(0, n_chunks, readout, 0, unroll=min(2, n_chunks))

    if emit_state:
        for hh in range(heads):
            for d in range(2):
                cf_ref[d, hh] = c_scr[hh, d]
                nf_ref[d, hh] = n_scr[hh, d]
                mf_ref[d, hh] = m_scr[hh, d] * (1.0 / LOG2E)


def _mlstm_call(qkv, og, mg, lw, batch, seq, init, emit_state):
    n = batch * seq
    ln = MLSTM_CHUNK
    nc = seq // ln
    h = M_HEADS
    hs = MLSTM_HEADS_PER_STEP
    ng = h // hs

    def tok(colblock):
        return pl.BlockSpec((seq, hs * LANES), lambda b, hg: (b, colblock(hg)))

    in_specs = [
        tok(lambda hg: hg), tok(lambda hg: ng + hg), tok(lambda hg: 2 * ng + hg),
        tok(lambda hg: hg),
        pl.BlockSpec((seq, LANES), lambda b, hg: (b, 0)),
        pl.BlockSpec((1, M_DV), lambda b, hg: (0, 0)),
    ]
    args = [qkv, qkv, qkv, og, mg, lw["m_norm_g"]]
    state_specs = [
        pl.BlockSpec((None, 2, hs, M_DK, M_DV), lambda b, hg: (b, 0, hg, 0, 0)),
        pl.BlockSpec((None, 2, hs, 1, M_DK), lambda b, hg: (b, 0, hg, 0, 0)),
        pl.BlockSpec((None, 2, hs, 1, ln), lambda b, hg: (b, 0, hg, 0, 0)),
    ]
    if init is not None:
        c_all, n_all, m_all, layer = init
        depth = c_all.shape[1]
        in_specs += [
            pl.BlockSpec((None, None, 2, hs, M_DK, M_DV), lambda b, hg: (b, layer, 0, hg, 0, 0)),
            pl.BlockSpec((None, None, 2, hs, 1, M_DK), lambda b, hg: (b, layer, 0, hg, 0, 0)),
            pl.BlockSpec((None, None, 2, hs, 1, ln), lambda b, hg: (b, layer, 0, hg, 0, 0)),
        ]
        args += [c_all, n_all.reshape(batch, depth, 2, h, 1, M_DK),
                 jnp.broadcast_to(m_all[..., None, None], (batch, depth, 2, h, 1, ln))]
    out_shape = [jax.ShapeDtypeStruct((n, h * M_DV), BF16)]
    out_specs = [tok(lambda hg: hg)]
    if emit_state:
        out_shape += [
            jax.ShapeDtypeStruct((batch, 2, h, M_DK, M_DV), F32),
            jax.ShapeDtypeStruct((batch, 2, h, 1, M_DK), F32),
            jax.ShapeDtypeStruct((batch, 2, h, 1, ln), F32),
        ]
        out_specs += state_specs
    per = (hs, 2, nc)
    outs = pl.pallas_call(
        functools.partial(_mlstm_kernel, n_chunks=nc, heads=hs, has_init=init is not None,
                          emit_state=emit_state),
        out_shape=out_shape,
        grid=(batch, ng),
        in_specs=in_specs,
        out_specs=out_specs,
        scratch_shapes=[
            pltpu.VMEM((seq, LANES), F32), pltpu.VMEM((seq, LANES), F32),
            pltpu.VMEM((nc, 4 * h, ln), F32), pltpu.VMEM((nc, 4 * h, ln), F32), pltpu.VMEM((nc, 4 * h, ln), F32),
            pltpu.VMEM(per + (M_DK, M_DV), F32), pltpu.VMEM(per + (1, M_DK), F32),
            pltpu.VMEM(per + (1, ln), F32), pltpu.VMEM(per + (1, ln), F32),
            pltpu.VMEM(per + (M_DK, M_DV), BF16), pltpu.VMEM(per + (1, M_DK), F32), pltpu.VMEM(per + (1, ln), F32),
            pltpu.VMEM((hs, 2, M_DK, M_DV), F32), pltpu.VMEM((hs, 2, 1, M_DK), F32), pltpu.VMEM((hs, 2, 1, ln), F32),
        ],
        compiler_params=_params(2),
        name="mlstm",
    )(*args)
    if emit_state:
        ym, cf, nf, mf = outs
        return ym, (cf, nf[:, :, :, 0, :], mf[:, :, :, 0, 0])
    return outs[0], None


HEAD_V = 64
VT_ROWS = 80


def _ones_row(v_t):
    return jnp.where(lax.broadcasted_iota(jnp.int32, v_t.shape, 0) == HEAD_V, 1.0, v_t)


def _flash_heads(qs, keys_of, vt_chunk, n_keys, o_ref, heads_per_rows):
    def values(e, s_t):
        m = acc = None
        for c0 in range(0, n_keys, KEY_BLOCK):
            c1 = min(c0 + KEY_BLOCK, n_keys)
            s = s_t[c0:c1, :]
            mc = jnp.max(s, axis=0, keepdims=True)
            if c0 == 0:
                m = mc
                acc = _dot(vt_chunk(e, c0, c1), jnp.exp2(s - mc).astype(BF16))
            else:
                m_new = jnp.maximum(m, mc)
                acc = acc * jnp.exp2(m - m_new) + _dot(vt_chunk(e, c0, c1), jnp.exp2(s - m_new).astype(BF16))
                m = m_new
        return acc[0:HEAD_V, :] / acc[HEAD_V:HEAD_V + 1, :]

    if n_keys <= KEY_BLOCK:
        s_all = [_dot_nt(keys_of(e), qb) for e, qb in enumerate(qs)]
        outs = [values(e, s_t) for e, s_t in enumerate(s_all)]
    else:
        outs = []
        pending = [_dot_nt(keys_of(e), qs[e]) for e in range(min(SCORES_AHEAD, len(qs)))]
        for e in range(len(qs)):
            s_t = pending.pop(0)
            if e + SCORES_AHEAD < len(qs):
                pending.append(_dot_nt(keys_of(e + SCORES_AHEAD), qs[e + SCORES_AHEAD]))
            outs.append(values(e, s_t))
    tq = qs[0].shape[0]
    for e in range(0, len(outs), 2):
        pair = jnp.concatenate([outs[e], outs[e + 1]], axis=0)
        r0 = (e // heads_per_rows) * tq
        c0 = ((e % heads_per_rows) // 2) * LANES
        o_ref[r0:r0 + tq, c0:c0 + LANES] = pair.T.astype(o_ref.dtype)


def _batch_per_step(batch, nq, past):
    return SHORT_SEQ_BATCH_PER_STEP if (nq == 1 and past == 0 and batch % SHORT_SEQ_BATCH_PER_STEP == 0) else 1


def _mla_kernel(*refs, past, rotary, heads, bb):
    q_ref, ckv_ref, akr_ref = refs[:3]
    pos = 3
    if rotary:
        mg_ref = refs[pos]
        pos += 1
    if past:
        ckvc_ref, krc_ref = refs[pos:pos + 2]
        pos += 2
    wk_ref, wvt_ref, kg_ref = refs[pos:pos + 3]
    pos += 3
    o_ref, kbuf, vbuf = refs[pos:]
    seq = ckv_ref.shape[0] // bb
    tq = q_ref.shape[0] // bb

    @pl.when(pl.program_id(2) == 0)
    def _():
        for i in range(bb * heads):
            sb, e = divmod(i, heads)
            rows = slice(sb * seq, (sb + 1) * seq)
            wk_h = wk_ref[:, e * LANES:(e + 1) * LANES]
            wvt_h = wvt_ref[e * VT_ROWS:(e + 1) * VT_ROWS, :]

            def keys(ckv, kr, kr_ss):
                cb = ckv.astype(BF16)
                k_nope = _dot(cb, wk_h)
                if kr_ss is None:
                    kcat = k_nope + kr
                    ss = jnp.sum(kcat * kcat, axis=-1, keepdims=True)
                    kcat = kcat * kg_ref[...]
                else:
                    ss = jnp.sum(k_nope * k_nope, axis=-1, keepdims=True) + kr_ss
                    kcat = k_nope * kg_ref[...] + kr
                return _ones_row(_dot_nt(wvt_h, cb)), kcat * lax.rsqrt(ss * (1.0 / A_QK) + EPS)

            kr_ss = mg_ref[rows, KR_SS_LANE:KR_SS_LANE + 1] if rotary else None
            v_t, kn = keys(ckv_ref[rows, :], akr_ref[rows, :], kr_ss)
            kbuf[i, past:past + seq, :] = kn.astype(BF16)
            vbuf[i, :, past:past + seq] = v_t.astype(BF16)
            if past:
                v_tc, knc = keys(ckvc_ref[...], krc_ref[...], None)
                kbuf[i, 0:past, :] = knc.astype(BF16)
                vbuf[i, :, 0:past] = v_tc.astype(BF16)

    qs = [q_ref[sb * tq:(sb + 1) * tq, e * LANES:(e + 1) * LANES] for sb in range(bb) for e in range(heads)]
    _flash_heads(qs, lambda i: kbuf[i], lambda i, c0, c1: vbuf[i, :, c0:c1], past + seq, o_ref, heads)


def _mla_call(qa, ckv, akr, mg, lw, batch, seq, cache, rotary):
    n = batch * seq
    tq = min(Q_BLOCK, seq)
    nq = seq // tq
    past = 0 if cache is None else cache[0].shape[2]
    heads = MLA_HEADS_PER_STEP if nq > 1 else A_HEADS
    bb = _batch_per_step(batch, nq, past)
    in_specs = [
        pl.BlockSpec((bb * tq, heads * LANES), lambda b, hg, qi: (b * nq + qi, hg)),
        pl.BlockSpec((bb * seq, LANES), lambda b, hg, qi: (b, 0)),
        pl.BlockSpec((bb * seq, LANES), lambda b, hg, qi: (b, 0)),
    ]
    args = [qa, ckv, akr]
    if rotary:
        in_specs += [pl.BlockSpec((bb * seq, LANES), lambda b, hg, qi: (b, 0))]
        args += [mg]
    layer = lw["layer"]
    if past:
        in_specs += [pl.BlockSpec((None, None, past, LANES), lambda b, hg, qi: (b, layer, 0, 0))] * 2
        args += [cache[0], cache[1]]
    in_specs += [
        pl.BlockSpec((None, A_KV_LORA, heads * LANES), lambda b, hg, qi: (layer, 0, hg)),
        pl.BlockSpec((None, heads * VT_ROWS, A_KV_LORA), lambda b, hg, qi: (layer, hg, 0)),
        pl.BlockSpec((1, LANES), lambda b, hg, qi: (0, 0)),
    ]
    args += [lw["w_uk"], lw["w_uvt"], lw["a_knorm_g"]]
    return pl.pallas_call(
        functools.partial(_mla_kernel, past=past, rotary=rotary, heads=heads, bb=bb),
        out_shape=jax.ShapeDtypeStruct((n, A_HEADS * A_V), BF16),
        grid=(batch // bb, A_HEADS // heads, nq),
        in_specs=in_specs,
        out_specs=pl.BlockSpec((bb * tq, heads * A_V), lambda b, hg, qi: (b * nq + qi, hg)),
        scratch_shapes=[pltpu.VMEM((bb * heads, past + seq, LANES), BF16),
                        pltpu.VMEM((bb * heads, VT_ROWS, past + seq), BF16)],
        compiler_params=_params(3),
        name="mla_attn",
    )(*args)


def _gqa_kernel(*refs, past, groups, bb):
    q_ref, k_ref, v_ref = refs[:3]
    pos = 3
    if past:
        kc_ref, vc_ref = refs[pos:pos + 2]
        pos += 2
    o_ref, kbuf, vbuf = refs[pos:]
    seq = k_ref.shape[0] // bb
    tq = q_ref.shape[0] // bb
    qw = G_GROUP * G_HD

    @pl.when(pl.program_id(2) == 0)
    def _():
        for i in range(bb * groups):
            sb, gi = divmod(i, groups)
            grp = pl.program_id(1) * groups + gi

            r = lax.broadcasted_iota(jnp.int32, (VT_ROWS, LANES), 0)
            c = lax.broadcasted_iota(jnp.int32, (VT_ROWS, LANES), 1)
            pick = jnp.where((c == r + grp * G_HD) & (r < G_HD), 1.0, 0.0).astype(BF16)

            def fill(r0, k, v):
                low = _lane(k.shape) < G_HD
                k_lo = jnp.where(grp == 0, k, pltpu.roll(k, G_HD, 1))
                rows = k.shape[0]
                kbuf[i, r0:r0 + rows, :] = jnp.where(low, k_lo, pltpu.roll(k_lo, G_HD, 1)).astype(BF16)
                vbuf[i, :, r0:r0 + rows] = _ones_row(_dot_nt(pick, v.astype(BF16))).astype(BF16)

            fill(past, k_ref[sb * seq:(sb + 1) * seq, :], v_ref[sb * seq:(sb + 1) * seq, :])
            if past:
                fill(0, kc_ref[...], vc_ref[...])

    qs = []
    for i in range(bb * groups):
        sb, gi = divmod(i, groups)
        for j in range(G_GROUP):
            col = q_ref[sb * tq:(sb + 1) * tq, gi * qw + (j // 2) * LANES:gi * qw + (j // 2 + 1) * LANES]
            keep = (_lane(col.shape) < G_HD) == (j % 2 == 0)
            qs.append(jnp.where(keep, col, jnp.zeros_like(col)))
    _flash_heads(qs, lambda e: kbuf[e // G_GROUP], lambda e, c0, c1: vbuf[e // G_GROUP, :, c0:c1],
                 past + seq, o_ref, groups * G_GROUP)


def _gqa_call(gq, gk, gv, batch, seq, cache):
    n = batch * seq
    tq = min(Q_BLOCK, seq)
    nq = seq // tq
    past = 0 if cache is None else cache[0].shape[2]
    groups = GQA_GROUPS_PER_STEP if nq > 1 else G_KV_HEADS
    bb = _batch_per_step(batch, nq, past)
    qw = G_GROUP * G_HD
    kvw = G_KV_HEADS * G_HD
    in_specs = [
        pl.BlockSpec((bb * tq, groups * qw), lambda b, g, qi: (b * nq + qi, g)),
        pl.BlockSpec((bb * seq, kvw), lambda b, g, qi: (b, 0)),
        pl.BlockSpec((bb * seq, kvw), lambda b, g, qi: (b, 0)),
    ]
    args = [gq, gk, gv]
    if past:
        layer = cache[2]
        in_specs += [pl.BlockSpec((None, None, past, kvw), lambda b, g, qi: (b, layer, 0, 0))] * 2
        args += [cache[0], cache[1]]
    return pl.pallas_call(
        functools.partial(_gqa_kernel, past=past, groups=groups, bb=bb),
        out_shape=jax.ShapeDtypeStruct((n, G_HEADS * G_HD), BF16),
        grid=(batch // bb, G_KV_HEADS // groups, nq),
        in_specs=in_specs,
        out_specs=pl.BlockSpec((bb * tq, groups * qw), lambda b, g, qi: (b * nq + qi, g)),
        scratch_shapes=[pltpu.VMEM((bb * groups, past + seq, LANES), BF16),
                        pltpu.VMEM((bb * groups, VT_ROWS, past + seq), BF16)],
        compiler_params=_params(3),
        name="gqa_attn",
    )(*args)


def _merge_ffn_kernel(x_ref, ym_ref, ya_ref, yg_ref, gates_ref, mod_ref, g2_ref, wb_ref, wo_ref, wfi_ref,
                      wfo_ref, o_ref, *, ff_chunks):
    x = x_ref[...]
    d = x.shape[-1]
    mixed = None
    for i, y_ref in enumerate((ym_ref, ya_ref, yg_ref)):
        br = _dot(y_ref[...].astype(BF16), wb_ref[i]) * gates_ref[:, i * d:(i + 1) * d]
        mixed = br if mixed is None else mixed + br
    gt1 = mod_ref[2:3, :]
    x1 = x + gt1 * _dot(mixed.astype(BF16), wo_ref[...])

    sh2 = mod_ref[3:4, :]
    sc2 = mod_ref[4:5, :]
    gt2 = mod_ref[5:6, :]
    h2 = ((_rms(x1, d) * g2_ref[...]) * (1.0 + sc2) + sh2).astype(BF16)
    d_ff = wfo_ref.shape[0]
    acc = None
    for c0, c1 in ff_chunks:
        ug = _dot(h2, wfi_ref[:, c0:c1])
        uv = _dot(h2, wfi_ref[:, d_ff + c0:d_ff + c1])
        act = (ug * _sigmoid(ug) * uv).astype(BF16)
        part = _dot(act, wfo_ref[c0:c1, :])
        acc = part if acc is None else acc + part
    o_ref[...] = x1 + gt2 * acc


def _merge_ffn_call(x, ym, ya, yg, gates, mod, lw, mod_row0, rows_per_mod):
    n, d = x.shape
    tm = TOKEN_BLOCK
    d_ff = lw["w_ffn_out"].shape[1]
    layer = lw["layer"]
    split = -(-(d_ff // 2) // MXU_COLS) * MXU_COLS
    ff_chunks = ((0, split), (split, d_ff)) if 0 < split < d_ff else ((0, d_ff),)

    def mod_idx(i):
        return (mod_row0 + (i * tm) // rows_per_mod, 0, 0)

    def rows(width):
        return pl.BlockSpec((tm, width), lambda i: (i, 0))

    return pl.pallas_call(
        functools.partial(_merge_ffn_kernel, ff_chunks=ff_chunks),
        out_shape=jax.ShapeDtypeStruct((n, d), F32),
        grid=(n // tm,),
        in_specs=[
            rows(d), rows(BRANCH_WIDTH), rows(BRANCH_WIDTH), rows(BRANCH_WIDTH), rows(N_BRANCH * d),
            pl.BlockSpec((None, N_MOD, d), mod_idx),
            _resident((1, d)),
            _resident((N_BRANCH, BRANCH_WIDTH, d), layer),
            _resident((d, d), layer),
            _resident((d, 2 * d_ff), layer),
            _resident((d_ff, d), layer),
        ],
        out_specs=rows(d),
        compiler_params=_params(1),
        name="merge_ffn",
    )(x, ym, ya, yg, gates, mod, lw["norm2_g"], lw["w_branch"], lw["w_out"], lw["w_ffn_in"], lw["w_ffn_out"])


def _mla_q_layout(a):
    lead = a.shape[:-1]
    a = a.reshape(lead + (A_HEADS, A_QK))
    z = jnp.zeros(lead + (A_HEADS, LANES - A_QK), a.dtype)
    return jnp.concatenate([a[..., A_NOPE:], z, a[..., :A_NOPE]], axis=-1).reshape(lead + (A_HEADS * LANES,))


def _stacked_weights(w_in, w_uq, w_ukv, w_branch, w_out, w_ffn_in, w_ffn_out):
    depth, d, _ = w_in.shape
    hw = M_HEADS * M_DK
    sizes = (N_BRANCH * d, hw, hw, hw, hw, 4 * M_HEADS, A_Q_LORA, A_KV_LORA, A_ROPE, G_HEADS * G_HD,
             G_KV_HEADS * G_HD, G_KV_HEADS * G_HD)
    edges = np.concatenate([[0], np.cumsum(sizes)]).tolist()
    (wmg, waq, wakv, wakr, wgq, wgk, wgv) = [w_in[:, :, a:b] for a, b in zip(edges[5:-1], edges[6:])]
    misc_pad = jnp.zeros((depth, d, LANES - A_ROPE - 4 * M_HEADS), w_in.dtype)
    tail = jnp.concatenate([waq, wakv, wakr, wmg, misc_pad, wgq, wgk, wgv], axis=2).astype(BF16)
    ukv = w_ukv.reshape(depth, A_KV_LORA, A_HEADS, A_NOPE + A_V)
    uk = jnp.pad(ukv[..., :A_NOPE], ((0, 0), (0, 0), (0, 0), (LANES - A_NOPE, 0)))
    uvt = jnp.pad(ukv[..., A_NOPE:].transpose(0, 2, 3, 1), ((0, 0), (0, 0), (0, VT_ROWS - A_V), (0, 0)))
    return dict(
        w_in=w_in.astype(BF16),
        w_in_tail=tail,
        w_uq=_mla_q_layout(w_uq).astype(BF16),
        w_uk=uk.reshape(depth, A_KV_LORA, A_HEADS * LANES).astype(BF16),
        w_uvt=uvt.reshape(depth, A_HEADS * VT_ROWS, A_KV_LORA).astype(BF16),
        w_branch=w_branch.astype(BF16),
        w_out=w_out.astype(BF16),
        w_ffn_in=w_ffn_in.astype(BF16),
        w_ffn_out=w_ffn_out.astype(BF16),
    )


def _layer_vectors(l, b_mgate, norm1_g, m_norm_g, a_qlora_g, a_kvlora_g, a_qnorm_g, a_knorm_g, g_qnorm_g,
                   g_knorm_g, norm2_g):
    return dict(
        layer=l,
        b_mgate=jnp.pad(b_mgate[l][None, :], ((0, 0), (MG_LANE0, LANES - MG_LANE0 - 4 * M_HEADS))),
        norm1_g=norm1_g[l][None, :],
        m_norm_g=m_norm_g[l][None, :],
        a_qlora_g=a_qlora_g[l][None, :],
        a_kvlora_g=a_kvlora_g[l][None, :],
        a_qnorm_g=_mla_q_layout(jnp.tile(a_qnorm_g[l], A_HEADS)[None, :])[:, :LANES],
        a_knorm_g=_mla_q_layout(jnp.tile(a_knorm_g[l], A_HEADS)[None, :])[:, :LANES],
        g_qnorm_g=jnp.tile(g_qnorm_g[l], LANES // G_HD)[None, :],
        g_knorm_g=jnp.tile(g_knorm_g[l], G_KV_HEADS)[None, :],
        norm2_g=norm2_g[l][None, :],
    )


def _axial_angles(seq, rot_dim):
    n_freq = rot_dim // 4
    freqs = ROPE_BASE ** (-jnp.arange(n_freq, dtype=F32) / n_freq)
    t = jnp.arange(seq)
    row = (t // GRID_W).astype(F32)
    col = (t % GRID_W).astype(F32)
    return jnp.concatenate([row[:, None] * freqs, col[:, None] * freqs], axis=-1)


def _rope_tables(seq):
    ang = _axial_angles(seq, A_ROPE)
    one = jnp.ones((seq, LANES - A_ROPE), F32)
    mla_cos = jnp.concatenate([jnp.cos(ang), jnp.cos(ang), one], axis=-1)
    mla_sin = jnp.concatenate([-jnp.sin(ang), jnp.sin(ang), 0.0 * one], axis=-1)
    ang = _axial_angles(seq, G_HD)
    cos = jnp.concatenate([jnp.cos(ang), jnp.cos(ang)], axis=-1)
    sin = jnp.concatenate([-jnp.sin(ang), jnp.sin(ang)], axis=-1)
    gqa = (jnp.tile(cos, (1, LANES // G_HD)), jnp.tile(sin, (1, LANES // G_HD)))
    return (mla_cos, mla_sin), gqa


def _layer(x, mod, lw, batch, seq, mod_row0, rows_per_mod, ctx, rope):
    (gates, qkv, og, mg, qa, ckv, akr, gq, gk, gv) = _inproj_call(x, mod, lw, mod_row0, rows_per_mod, rope, seq)
    if ctx is None:
        ym, state = _mlstm_call(qkv, og, mg, lw, batch, seq, None, True)
        ya = _mla_call(qa, ckv, akr, mg, lw, batch, seq, None, False)
        yg = _gqa_call(gq, gk, gv, batch, seq, None)
        new_ctx = dict(state=state, ckv=ckv, kr=akr[:, :A_ROPE], gk=gk, gv=gv)
    else:
        ym, _ = _mlstm_call(qkv, og, mg, lw, batch, seq, ctx["mlstm"], False)
        ya = _mla_call(qa, ckv, akr, mg, lw, batch, seq, ctx["mla"], True)
        yg = _gqa_call(gq, gk, gv, batch, seq, ctx["gqa"])
        new_ctx = None
    x = _merge_ffn_call(x, ym, ya, yg, gates, mod, lw, mod_row0, rows_per_mod)
    return x, new_ctx


def kernel(x_prompt, x_sample, state_mlstm_C, state_mlstm_n, state_mlstm_m, cache_mla_ckv, cache_mla_krope,
           cache_gqa_k, cache_gqa_v, c, c_ctx, w_mod, b_mod, norm1_g, w_in, b_mgate, m_norm_g, a_qlora_g,
           a_kvlora_g, w_uq, w_ukv, a_qnorm_g, a_knorm_g, g_qnorm_g, g_knorm_g, w_branch, w_out, norm2_g,
           w_ffn_in, w_ffn_out):
    batch, seq, d = x_prompt.shape
    dbatch, dseq, _ = x_sample.shape
    depth = w_in.shape[0]
    past = cache_mla_ckv.shape[2]
    for tokens, length in ((batch * seq, seq), (dbatch * dseq, dseq)):
        assert tokens % TOKEN_BLOCK == 0 and length % MLSTM_CHUNK == 0 and length % min(Q_BLOCK, length) == 0
        assert TOKEN_BLOCK % length == 0 or length % TOKEN_BLOCK == 0
    assert dseq % TOKEN_BLOCK == 0 and dseq % GRID_W == 0 and past % LANES == 0
    assert MLSTM_HEADS_PER_STEP == M_HEADS

    n_rows = -(-(1 + dbatch) // SUBLANES) * SUBLANES
    cond = jnp.concatenate([c_ctx[None, :], c, jnp.zeros((n_rows - 1 - dbatch, d), F32)], axis=0)
    mod_all = _mod_call(cond, w_mod, b_mod).reshape(depth, n_rows, N_MOD, d)

    rope = _rope_tables(dseq)
    xp = x_prompt.reshape(batch * seq, d)
    xs = x_sample.reshape(dbatch * dseq, d)
    ctx_layers = []
    krope_cache = jnp.pad(cache_mla_krope, ((0, 0), (0, 0), (0, 0), (0, LANES - A_ROPE)))
    gqa_k_cache = cache_gqa_k.reshape(dbatch, depth, past, G_KV_HEADS * G_HD)
    gqa_v_cache = cache_gqa_v.reshape(dbatch, depth, past, G_KV_HEADS * G_HD)
    stacked = _stacked_weights(w_in, w_uq, w_ukv, w_branch, w_out, w_ffn_in, w_ffn_out)
    for l in range(depth):
        lw = dict(stacked, **_layer_vectors(l, b_mgate, norm1_g, m_norm_g, a_qlora_g, a_kvlora_g, a_qnorm_g,
                                            a_knorm_g, g_qnorm_g, g_knorm_g, norm2_g))
        xp, st = _layer(xp, mod_all[l], lw, batch, seq, 0, batch * seq, None, None)
        ctx_layers.append(st)
        ctx = dict(mlstm=(state_mlstm_C, state_mlstm_n, state_mlstm_m, l), mla=(cache_mla_ckv, krope_cache),
                   gqa=(gqa_k_cache, gqa_v_cache, l))
        xs, _ = _layer(xs, mod_all[l], lw, dbatch, dseq, 1, dseq, ctx, rope)

    def stack(fn):
        return jnp.stack([fn(s) for s in ctx_layers], axis=1)

    new_c = stack(lambda s: s["state"][0])
    new_n = stack(lambda s: s["state"][1])
    new_m = stack(lambda s: s["state"][2])
    new_ckv = stack(lambda s: s["ckv"].reshape(batch, seq, A_KV_LORA))
    new_kr = stack(lambda s: s["kr"].reshape(batch, seq, A_ROPE))
    new_gk = stack(lambda s: s["gk"].reshape(batch, seq, G_KV_HEADS, G_HD))
    new_gv = stack(lambda s: s["gv"].reshape(batch, seq, G_KV_HEADS, G_HD))
    return (xp.reshape(batch, seq, d), xs.reshape(dbatch, dseq, d), new_c, new_n, new_m, new_ckv, new_kr,
            new_gk, new_gv)
```

```python
import functools

import numpy as np
import jax
import jax.numpy as jnp
from jax import lax
from jax.experimental import pallas as pl
from jax.experimental.pallas import tpu as pltpu

F32 = jnp.float32
BF16 = jnp.bfloat16

LANES = 128
SUBLANES = 8
VMEM_LIMIT_BYTES = 56 * 1024 * 1024

EPS = 1e-6
ROPE_BASE = 10000.0
GRID_W = 64

M_HEADS = 4
M_DK = 128
M_DV = 128
A_HEADS = 8
A_NOPE = 64
A_ROPE = 32
A_QK = A_NOPE + A_ROPE
A_V = 64
A_Q_LORA = 256
A_KV_LORA = 128
G_HEADS = 8
G_KV_HEADS = 2
G_GROUP = G_HEADS // G_KV_HEADS
G_HD = 64
N_BRANCH = 3
BRANCH_WIDTH = 512
N_MOD = 6

MOD_COL_BLOCKS = 4
TOKEN_BLOCK = 512
Q_BLOCK = 256
KEY_BLOCK = 256
MLA_HEADS_PER_STEP = 8
GQA_GROUPS_PER_STEP = 2
SHORT_SEQ_BATCH_PER_STEP = 4
SCORES_AHEAD = 2
MLSTM_CHUNK = 128
MLSTM_HEADS_PER_STEP = 4
LOG2E = 1.4426950408889634


def _params(n_axes):
    return pltpu.CompilerParams(dimension_semantics=("arbitrary",) * n_axes,
                                vmem_limit_bytes=VMEM_LIMIT_BYTES)


def _resident(shape, layer=None):
    nd = len(shape)
    if layer is None:
        return pl.BlockSpec(shape, lambda *_: (0,) * nd, pipeline_mode=pl.Buffered(1))
    return pl.BlockSpec((None,) + tuple(shape), lambda *_: (layer,) + (0,) * nd, pipeline_mode=pl.Buffered(1))


def _lane(shape, axis=None):
    return lax.broadcasted_iota(jnp.int32, shape, len(shape) - 1 if axis is None else axis)


def _dot(a, b):
    return jnp.dot(a, b, preferred_element_type=F32)


def _dot_nt(a, b):
    return lax.dot_general(a, b, (((1,), (1,)), ((), ())), preferred_element_type=F32)


def _split3(a):
    hi = a.astype(BF16)
    r1 = a - hi.astype(F32)
    mid = r1.astype(BF16)
    lo = (r1 - mid.astype(F32)).astype(BF16)
    return hi, mid, lo


def _dot01(a, m01):
    hi, mid, lo = _split3(a)
    return _dot(hi, m01) + _dot(mid, m01) + _dot(lo, m01)


def _dot01_left(m01, a):
    hi, mid, lo = _split3(a)
    return _dot(m01, hi) + _dot(m01, mid) + _dot(m01, lo)


def _sigmoid(x):
    return 0.5 * jnp.tanh(0.5 * x) + 0.5


def _log_sigmoid(x):
    return jnp.minimum(x, 0.0) - jnp.log(1.0 + jnp.exp(-jnp.abs(x)))


def _rms(x, width):
    ms = jnp.sum(x * x, axis=-1, keepdims=True) * (1.0 / width)
    return x * lax.rsqrt(ms + EPS)


def _rope(x, cos, sin_signed, half, period):
    n = x.shape[-1]
    first = (_lane(x.shape) % period) < half
    swapped = jnp.where(first, pltpu.roll(x, n - half, x.ndim - 1), pltpu.roll(x, half, x.ndim - 1))
    return x * cos + swapped * sin_signed


def _mod_kernel(c_ref, w_ref, b_ref, o_ref):
    c = c_ref[...]
    a = c * _sigmoid(c)
    o_ref[...] = _dot01(a, w_ref[...].astype(BF16)) + b_ref[...]


def _mod_call(cond, w_mod, b_mod):
    depth, d, n = w_mod.shape
    rows = cond.shape[0]
    tn = n // MOD_COL_BLOCKS
    return pl.pallas_call(
        _mod_kernel,
        out_shape=jax.ShapeDtypeStruct((depth, rows, n), F32),
        grid=(depth, n // tn),
        in_specs=[
            pl.BlockSpec((rows, d), lambda l, j: (0, 0)),
            pl.BlockSpec((None, d, tn), lambda l, j: (l, 0, j)),
            pl.BlockSpec((None, 1, tn), lambda l, j: (l, 0, j)),
        ],
        out_specs=pl.BlockSpec((None, rows, tn), lambda l, j: (l, 0, j)),
        compiler_params=_params(2),
        name="adaln_mod",
    )(cond, w_mod, b_mod.reshape(depth, 1, n))


MXU_COLS = 256
D_MODEL = 1024
_GATE0, _GATE1 = 0, N_BRANCH * D_MODEL
_MQ0 = _GATE1
_MK0 = _MQ0 + M_HEADS * M_DK
_MV0 = _MK0 + M_HEADS * M_DK
_MO0 = _MV0 + M_HEADS * M_DV
_AQ0 = _MO0 + M_HEADS * M_DV
_AKV0 = _AQ0 + A_Q_LORA
_MISC0 = _AKV0 + A_KV_LORA
_GQ0 = _MISC0 + LANES
_GK0 = _GQ0 + G_HEADS * G_HD
_GV0 = _GK0 + G_KV_HEADS * G_HD
_WIN_COLS = _GV0 + G_KV_HEADS * G_HD
assert _WIN_COLS % MXU_COLS == 0 and _AQ0 % MXU_COLS == 0
MG_LANE0 = A_ROPE
KR_SS_LANE = 64
assert MG_LANE0 + 4 * M_HEADS <= KR_SS_LANE < LANES


def _head_pair_ms(x):
    low = (_lane(x.shape) % LANES) < G_HD
    sq = x * x
    cols = []
    for c0 in range(0, x.shape[-1], LANES):
        s = sq[:, c0:c0 + LANES]
        lo = jnp.sum(jnp.where(low[:, c0:c0 + LANES], s, 0.0), axis=-1, keepdims=True)
        hi = jnp.sum(jnp.where(low[:, c0:c0 + LANES], 0.0, s), axis=-1, keepdims=True)
        cols.append(jnp.where(low[:, c0:c0 + LANES], lo, hi))
    ms = cols[0] if len(cols) == 1 else jnp.concatenate(cols, axis=-1)
    return ms * (1.0 / G_HD)


def _inproj_kernel(*refs, rotary):
    (x_ref, mod_ref, g1_ref, w_ref, wt_ref, bmg_ref, gql_ref, wuq_ref, gkvl_ref, gkn_ref, aqn_ref,
     gqn_ref, akn_ref) = refs[:13]
    pos = 13
    if rotary:
        ca_ref, sa_ref, cg_ref, sg_ref = refs[pos:pos + 4]
        pos += 4
    (gates_ref, qkv_ref, og_ref, mg_ref, qa_ref, ckv_ref, akr_ref, gq_ref, gk_ref, gv_ref) = refs[pos:]
    x = x_ref[...]
    d = x.shape[-1]
    sh1 = mod_ref[0:1, :]
    sc1 = mod_ref[1:2, :]
    h = (_rms(x, d) * g1_ref[...]) * (1.0 + sc1) + sh1
    hb = h.astype(BF16)

    def proj(c0, width):
        if c0 < _AQ0:
            return _dot(hb, w_ref[:, c0:c0 + width])
        return _dot(hb, wt_ref[:, c0 - _AQ0:c0 - _AQ0 + width])

    aq = _rms(proj(_AQ0, A_Q_LORA), A_Q_LORA) * gql_ref[...]
    qa = _dot(aq.astype(BF16), wuq_ref[...])
    for hd in range(A_HEADS):
        qh = _rms(qa[:, hd * LANES:(hd + 1) * LANES], A_QK) * aqn_ref[...]
        if rotary:
            qh = _rope(qh, ca_ref[...], sa_ref[...], A_ROPE // 2, LANES)
        qa_ref[:, hd * LANES:(hd + 1) * LANES] = (qh * (A_QK ** -0.5 * LOG2E)).astype(qa_ref.dtype)
    akv_misc = proj(_AKV0, A_KV_LORA + LANES)
    ckv_ref[...] = _rms(akv_misc[:, :A_KV_LORA], A_KV_LORA) * gkvl_ref[...]
    misc = akv_misc[:, A_KV_LORA:]
    kr = jnp.where(_lane(misc.shape) < A_ROPE, misc, 0.0)
    mg = misc + bmg_ref[...]
    if rotary:
        akr_ref[...] = _rope(kr * akn_ref[...], ca_ref[...], sa_ref[...], A_ROPE // 2, LANES)
        mg = jnp.where(_lane(mg.shape) == KR_SS_LANE, jnp.sum(kr * kr, axis=-1, keepdims=True), mg)
    else:
        akr_ref[...] = kr
    mg_ref[...] = mg

    gq_all = proj(_GQ0, G_HEADS * G_HD)
    for c0 in range(0, G_HEADS * G_HD, LANES):
        gq = gq_all[:, c0:c0 + LANES]
        gq = gq * lax.rsqrt(_head_pair_ms(gq) + EPS) * gqn_ref[...]
        if rotary:
            gq = _rope(gq, cg_ref[...], sg_ref[...], G_HD // 2, G_HD)
        gq_ref[:, c0:c0 + LANES] = (gq * (G_HD ** -0.5 * LOG2E)).astype(gq_ref.dtype)
    kvw = G_KV_HEADS * G_HD
    gkv = proj(_GK0, 2 * kvw)
    gk = gkv[:, :kvw]
    gk = gk * lax.rsqrt(_head_pair_ms(gk) + EPS) * gkn_ref[...]
    if rotary:
        gk = _rope(gk, cg_ref[...], sg_ref[...], G_HD // 2, G_HD)
    gk_ref[...] = gk
    gv_ref[...] = gkv[:, kvw:]

    hw = M_HEADS * M_DK
    gates_ref[...] = _sigmoid(proj(_GATE0, _GATE1 - _GATE0)).astype(gates_ref.dtype)
    og_ref[...] = _sigmoid(proj(_MO0, hw)).astype(og_ref.dtype)
    qkv_ref[:, 0:hw] = proj(_MQ0, hw).astype(qkv_ref.dtype)
    qkv_ref[:, hw:2 * hw] = (proj(_MK0, hw) * (M_DK ** -0.5)).astype(qkv_ref.dtype)
    qkv_ref[:, 2 * hw:3 * hw] = proj(_MV0, hw).astype(qkv_ref.dtype)


def _inproj_call(x, mod, lw, mod_row0, rows_per_mod, rope, seq):
    n, d = x.shape
    tm = TOKEN_BLOCK
    hw = M_HEADS * M_DK

    def mod_idx(i):
        return (mod_row0 + (i * tm) // rows_per_mod, 0, 0)

    def rows(width):
        return pl.BlockSpec((tm, width), lambda i: (i, 0))

    in_specs = [
        rows(d),
        pl.BlockSpec((None, N_MOD, d), mod_idx),
        _resident((1, d)),
        _resident(lw["w_in"].shape[1:], lw["layer"]),
        _resident((d, _WIN_COLS - _AQ0), lw["layer"]),
        _resident((1, LANES)),
        _resident((1, A_Q_LORA)),
        _resident((A_Q_LORA, A_HEADS * LANES), lw["layer"]),
        _resident((1, A_KV_LORA)),
        _resident((1, LANES)),
        _resident((1, LANES)),
        _resident((1, LANES)),
        _resident((1, LANES)),
    ]
    args = [x, mod, lw["norm1_g"], lw["w_in"], lw["w_in_tail"], lw["b_mgate"], lw["a_qlora_g"], lw["w_uq"],
            lw["a_kvlora_g"], lw["g_knorm_g"], lw["a_qnorm_g"], lw["g_qnorm_g"], lw["a_knorm_g"]]
    if rope is not None:
        blocks_per_seq = seq // tm
        in_specs += [pl.BlockSpec((tm, LANES), lambda i: (i % blocks_per_seq, 0))] * 4
        args += [rope[0][0], rope[0][1], rope[1][0], rope[1][1]]
    out_widths = [3 * d, 3 * hw, hw, LANES, A_HEADS * LANES, A_KV_LORA, LANES, G_HEADS * G_HD,
                  G_KV_HEADS * G_HD, G_KV_HEADS * G_HD]
    out_dtypes = [BF16, BF16, BF16, F32, BF16, F32, F32, BF16, F32, F32]
    return pl.pallas_call(
        functools.partial(_inproj_kernel, rotary=rope is not None),
        out_shape=[jax.ShapeDtypeStruct((n, w), t) for w, t in zip(out_widths, out_dtypes)],
        grid=(n // tm,),
        in_specs=in_specs,
        out_specs=[rows(w) for w in out_widths],
        compiler_params=_params(1),
        name="in_proj",
    )(*args)


def _mlstm_kernel(*refs, n_chunks, heads, has_init, emit_state):
    q_ref, k_ref, v_ref, og_ref, mg_ref, gn_ref = refs[:6]
    pos = 6
    if has_init:
        c0_ref, n0_ref, m0_ref = refs[pos:pos + 3]
        pos += 3
    y_ref = refs[pos]
    pos += 1
    if emit_state:
        cf_ref, nf_ref, mf_ref = refs[pos:pos + 3]
        pos += 3
    (pcol_scr, lfc_scr, grow_scr, prow_scr, lfr_scr, u_scr, nu_scr, gm_scr, tot_scr, cs_scr, ns_scr, ms_scr,
     c_scr, n_scr, m_scr) = refs[pos:]
    assert heads == M_HEADS
    n_gates = 4 * M_HEADS

    ln = MLSTM_CHUNK
    row = lax.broadcasted_iota(jnp.int32, (ln, ln), 0)
    col = lax.broadcasted_iota(jnp.int32, (ln, ln), 1)
    lower = col <= row
    upper = col >= row
    tril = jnp.where(lower, 1.0, 0.0).astype(BF16)
    triu = jnp.where(upper, 1.0, 0.0).astype(BF16)
    ones = jnp.ones((ln, ln), BF16)

    def gate_index(hh, d):
        return 2 * d * heads + hh, (2 * d + 1) * heads + hh

    def summaries(c, carry):
        r0 = pl.multiple_of(c * ln, ln)
        g = mg_ref[pl.ds(r0, ln), :]
        lf = _log_sigmoid(g) * LOG2E
        pcol_scr[pl.ds(r0, ln), :] = _dot01_left(tril, lf)
        lfc_scr[pl.ds(r0, ln), :] = lf
        g_rows = g.T[MG_LANE0:MG_LANE0 + n_gates, :]
        gr = g_rows * LOG2E
        lfr = _log_sigmoid(g_rows) * LOG2E
        pr = _dot01(lfr, triu)
        grow_scr[c] = gr
        prow_scr[c] = pr
        lfr_scr[c] = lfr
        for hh in range(heads):
            kb = k_ref[pl.ds(r0, ln), hh * LANES:(hh + 1) * LANES]
            k_t = kb.astype(F32).T
            v = v_ref[pl.ds(r0, ln), hh * LANES:(hh + 1) * LANES]
            for d in range(2):
                ji, jf = gate_index(hh, d)
                tot = pr[jf:jf + 1, ln - 1:ln]
                b_row = pr[jf:jf + 1, :] if d == 0 else tot - pr[jf:jf + 1, :] + lfr[jf:jf + 1, :]
                g_row = tot - b_row + gr[ji:ji + 1, :]
                gmax = jnp.max(g_row, axis=-1, keepdims=True)
                wg = jnp.exp2(g_row - gmax)
                u_scr[hh, d, c] = _dot((k_t * wg).astype(BF16), v)
                nu_scr[hh, d, c] = _dot(jnp.broadcast_to(wg, (SUBLANES, ln)).astype(BF16), kb)[0:1, :]
                gm_scr[hh, d, c] = jnp.broadcast_to(gmax, (1, ln))
                tot_scr[hh, d, c] = jnp.broadcast_to(tot, (1, ln))
        return carry

    lax.fori_loop(0, n_chunks, summaries, 0, unroll=min(2, n_chunks))

    for hh in range(heads):
        for d in range(2):
            if has_init:
                c_scr[hh, d] = c0_ref[d, hh]
                n_scr[hh, d] = n0_ref[d, hh]
                m_scr[hh, d] = m0_ref[d, hh] * LOG2E
            else:
                c_scr[hh, d] = jnp.zeros((M_DK, M_DV), F32)
                n_scr[hh, d] = jnp.zeros((1, M_DK), F32)
                m_scr[hh, d] = jnp.zeros((1, ln), F32)

    def scan(j, carry):
        for hh in range(heads):
            for d in range(2):
                c = j if d == 0 else n_chunks - 1 - j
                cst = c_scr[hh, d]
                nst = n_scr[hh, d]
                mst = m_scr[hh, d]
                cs_scr[hh, d, c] = cst.astype(BF16)
                ns_scr[hh, d, c] = nst
                ms_scr[hh, d, c] = mst
                gmax = gm_scr[hh, d, c]
                total = tot_scr[hh, d, c] + mst
                m_new = jnp.maximum(total, gmax)
                decay = jnp.exp2(total - m_new)
                scale = jnp.exp2(gmax - m_new)
                c_scr[hh, d] = cst * decay + u_scr[hh, d, c] * scale
                n_scr[hh, d] = nst * decay + nu_scr[hh, d, c] * scale
                m_scr[hh, d] = m_new
        return carry

    lax.fori_loop(0, n_chunks, scan, 0)

    def readout(c, carry):
        r0 = pl.multiple_of(c * ln, ln)
        pc = pcol_scr[pl.ds(r0, ln), :]
        lf = lfc_scr[pl.ds(r0, ln), :]
        gr = grow_scr[c]
        pr = prow_scr[c]
        lfr = lfr_scr[c]
        tot_all = pc[ln - 1:ln, :]
        for hh in range(heads):
            q = q_ref[pl.ds(r0, ln), hh * LANES:(hh + 1) * LANES]
            k = k_ref[pl.ds(r0, ln), hh * LANES:(hh + 1) * LANES]
            v = v_ref[pl.ds(r0, ln), hh * LANES:(hh + 1) * LANES]
            qk = _dot_nt(q, k)
            hsum = None
            for d in range(2):
                ji, jf = gate_index(hh, d)
                lane_f = MG_LANE0 + jf
                if d == 0:
                    b_col = pc[:, lane_f:lane_f + 1]
                    r_row = gr[ji:ji + 1, :] - pr[jf:jf + 1, :]
                    mask = lower
                else:
                    tot = tot_all[:, lane_f:lane_f + 1]
                    b_col = tot - pc[:, lane_f:lane_f + 1] + lf[:, lane_f:lane_f + 1]
                    r_row = gr[ji:ji + 1, :] - (tot - pr[jf:jf + 1, :] + lfr[jf:jf + 1, :])
                    mask = upper
                b_colb = jnp.broadcast_to(b_col, (ln, ln))
                d_log = jnp.where(mask, b_colb + r_row, -jnp.inf)
                dmax = jnp.broadcast_to(jnp.max(d_log, axis=-1, keepdims=True), (ln, ln))
                inter = b_colb + ms_scr[hh, d, c]
                m_t = jnp.maximum(inter, dmax)
                sb = (qk * jnp.exp2(d_log - m_t)).astype(BF16)
                w_inter = jnp.exp2(inter - m_t)
                qc = _dot(q, cs_scr[hh, d, c])
                qn = _dot_nt(q, jnp.broadcast_to(ns_scr[hh, d, c], (ln, M_DK)).astype(BF16))
                num = qc * w_inter + _dot(sb, v)
                den = qn * w_inter + _dot(sb, ones)
                hc = num / jnp.maximum(jnp.abs(den), jnp.exp2(-m_t))
                hsum = hc if hsum is None else hsum + hc
            y_ref[pl.ds(r0, ln), hh * LANES:(hh + 1) * LANES] = (
                _rms(hsum, M_DV) * gn_ref[...] * og_ref[pl.ds(r0, ln), hh * LANES:(hh + 1) * LANES]
            ).astype(y_ref.dtype)
        return carry

    lax.fori_loop(0, n_chunks, readout, 0, unroll=min(2, n_chunks))

    if emit_state:
        for hh in range(heads):
            for d in range(2):
                cf_ref[d, hh] = c_scr[hh, d]
                nf_ref[d, hh] = n_scr[hh, d]
                mf_ref[d, hh] = m_scr[hh, d] * (1.0 / LOG2E)


def _mlstm_call(qkv, og, mg, lw, batch, seq, init, emit_state):
    n = batch * seq
    ln = MLSTM_CHUNK
    nc = seq // ln
    h = M_HEADS
    hs = MLSTM_HEADS_PER_STEP
    ng = h // hs

    def tok(colblock):
        return pl.BlockSpec((seq, hs * LANES), lambda b, hg: (b, colblock(hg)))

    in_specs = [
        tok(lambda hg: hg), tok(lambda hg: ng + hg), tok(lambda hg: 2 * ng + hg),
        tok(lambda hg: hg),
        pl.BlockSpec((seq, LANES), lambda b, hg: (b, 0)),
        pl.BlockSpec((1, M_DV), lambda b, hg: (0, 0)),
    ]
    args = [qkv, qkv, qkv, og, mg, lw["m_norm_g"]]
    state_specs = [
        pl.BlockSpec((None, 2, hs, M_DK, M_DV), lambda b, hg: (b, 0, hg, 0, 0)),
        pl.BlockSpec((None, 2, hs, 1, M_DK), lambda b, hg: (b, 0, hg, 0, 0)),
        pl.BlockSpec((None, 2, hs, 1, ln), lambda b, hg: (b, 0, hg, 0, 0)),
    ]
    if init is not None:
        c_all, n_all, m_all, layer = init
        depth = c_all.shape[1]
        in_specs += [
            pl.BlockSpec((None, None, 2, hs, M_DK, M_DV), lambda b, hg: (b, layer, 0, hg, 0, 0)),
            pl.BlockSpec((None, None, 2, hs, 1, M_DK), lambda b, hg: (b, layer, 0, hg, 0, 0)),
            pl.BlockSpec((None, None, 2, hs, 1, ln), lambda b, hg: (b, layer, 0, hg, 0, 0)),
        ]
        args += [c_all, n_all.reshape(batch, depth, 2, h, 1, M_DK),
                 jnp.broadcast_to(m_all[..., None, None], (batch, depth, 2, h, 1, ln))]
    out_shape = [jax.ShapeDtypeStruct((n, h * M_DV), BF16)]
    out_specs = [tok(lambda hg: hg)]
    if emit_state:
        out_shape += [
            jax.ShapeDtypeStruct((batch, 2, h, M_DK, M_DV), F32),
            jax.ShapeDtypeStruct((batch, 2, h, 1, M_DK), F32),
            jax.ShapeDtypeStruct((batch, 2, h, 1, ln), F32),
        ]
        out_specs += state_specs
    per = (hs, 2, nc)
    outs = pl.pallas_call(
        functools.partial(_mlstm_kernel, n_chunks=nc, heads=hs, has_init=init is not None,
                          emit_state=emit_state),
        out_shape=out_shape,
        grid=(batch, ng),
        in_specs=in_specs,
        out_specs=out_specs,
        scratch_shapes=[
            pltpu.VMEM((seq, LANES), F32), pltpu.VMEM((seq, LANES), F32),
            pltpu.VMEM((nc, 4 * h, ln), F32), pltpu.VMEM((nc, 4 * h, ln), F32), pltpu.VMEM((nc, 4 * h, ln), F32),
            pltpu.VMEM(per + (M_DK, M_DV), F32), pltpu.VMEM(per + (1, M_DK), F32),
            pltpu.VMEM(per + (1, ln), F32), pltpu.VMEM(per + (1, ln), F32),
            pltpu.VMEM(per + (M_DK, M_DV), BF16), pltpu.VMEM(per + (1, M_DK), F32), pltpu.VMEM(per + (1, ln), F32),
            pltpu.VMEM((hs, 2, M_DK, M_DV), F32), pltpu.VMEM((hs, 2, 1, M_DK), F32), pltpu.VMEM((hs, 2, 1, ln), F32),
        ],
        compiler_params=_params(2),
        name="mlstm",
    )(*args)
    if emit_state:
        ym, cf, nf, mf = outs
        return ym, (cf, nf[:, :, :, 0, :], mf[:, :, :, 0, 0])
    return outs[0], None


HEAD_V = 64
BF16_ROWS = 2 * SUBLANES
VT_ROWS = HEAD_V + BF16_ROWS


def _ones_row(v_t):
    return jnp.where(lax.broadcasted_iota(jnp.int32, v_t.shape, 0) == HEAD_V, 1.0, v_t)


def _flash_heads(qs, keys_of, vt_chunk, n_keys, o_ref, heads_per_rows):
    def values(e, s_t):
        m = acc = None
        for c0 in range(0, n_keys, KEY_BLOCK):
            c1 = min(c0 + KEY_BLOCK, n_keys)
            s = s_t[c0:c1, :]
            mc = jnp.max(s, axis=0, keepdims=True)
            if c0 == 0:
                m = mc
                acc = _dot(vt_chunk(e, c0, c1), jnp.exp2(s - mc).astype(BF16))
            else:
                m_new = jnp.maximum(m, mc)
                acc = acc * jnp.exp2(m - m_new) + _dot(vt_chunk(e, c0, c1), jnp.exp2(s - m_new).astype(BF16))
                m = m_new
        return acc[0:HEAD_V, :] / acc[HEAD_V:HEAD_V + 1, :]

    if n_keys <= KEY_BLOCK:
        s_all = [_dot_nt(keys_of(e), qb) for e, qb in enumerate(qs)]
        outs = [values(e, s_t) for e, s_t in enumerate(s_all)]
    else:
        outs = []
        pending = [_dot_nt(keys_of(e), qs[e]) for e in range(min(SCORES_AHEAD, len(qs)))]
        for e in range(len(qs)):
            s_t = pending.pop(0)
            if e + SCORES_AHEAD < len(qs):
                pending.append(_dot_nt(keys_of(e + SCORES_AHEAD), qs[e + SCORES_AHEAD]))
            outs.append(values(e, s_t))
    tq = qs[0].shape[0]
    for e in range(0, len(outs), 2):
        pair = jnp.concatenate([outs[e], outs[e + 1]], axis=0)
        r0 = (e // heads_per_rows) * tq
        c0 = ((e % heads_per_rows) // 2) * LANES
        o_ref[r0:r0 + tq, c0:c0 + LANES] = pair.T.astype(o_ref.dtype)


def _batch_per_step(batch, nq, past):
    return SHORT_SEQ_BATCH_PER_STEP if (nq == 1 and past == 0 and batch % SHORT_SEQ_BATCH_PER_STEP == 0) else 1


def _mla_kernel(*refs, past, rotary, heads, bb):
    q_ref, ckv_ref, akr_ref = refs[:3]
    pos = 3
    if rotary:
        mg_ref = refs[pos]
        pos += 1
    if past:
        ckvc_ref, krc_ref = refs[pos:pos + 2]
        pos += 2
    wk_ref, wvt_ref, kg_ref = refs[pos:pos + 3]
    pos += 3
    o_ref, kbuf, vbuf = refs[pos:]
    seq = ckv_ref.shape[0] // bb
    tq = q_ref.shape[0] // bb

    @pl.when(pl.program_id(2) == 0)
    def _():
        for i in range(bb * heads):
            sb, e = divmod(i, heads)
            rows = slice(sb * seq, (sb + 1) * seq)
            wk_h = wk_ref[:, e * LANES:(e + 1) * LANES]
            wvt_h = wvt_ref[e * VT_ROWS:(e + 1) * VT_ROWS, :]

            def keys(ckv, kr, kr_ss):
                cb = ckv.astype(BF16)
                k_nope = _dot(cb, wk_h)
                if kr_ss is None:
                    kcat = k_nope + kr
                    ss = jnp.sum(kcat * kcat, axis=-1, keepdims=True)
                    kcat = kcat * kg_ref[...]
                else:
                    ss = jnp.sum(k_nope * k_nope, axis=-1, keepdims=True) + kr_ss
                    kcat = k_nope * kg_ref[...] + kr
                return _ones_row(_dot_nt(wvt_h, cb)), kcat * lax.rsqrt(ss * (1.0 / A_QK) + EPS)

            kr_ss = mg_ref[rows, KR_SS_LANE:KR_SS_LANE + 1] if rotary else None
            v_t, kn = keys(ckv_ref[rows, :], akr_ref[rows, :], kr_ss)
            kbuf[i, past:past + seq, :] = kn.astype(BF16)
            vbuf[i, :, past:past + seq] = v_t.astype(BF16)
            if past:
                v_tc, knc = keys(ckvc_ref[...], krc_ref[...], None)
                kbuf[i, 0:past, :] = knc.astype(BF16)
                vbuf[i, :, 0:past] = v_tc.astype(BF16)

    qs = [q_ref[sb * tq:(sb + 1) * tq, e * LANES:(e + 1) * LANES] for sb in range(bb) for e in range(heads)]
    _flash_heads(qs, lambda i: kbuf[i], lambda i, c0, c1: vbuf[i, :, c0:c1], past + seq, o_ref, heads)


def _mla_call(qa, ckv, akr, mg, lw, batch, seq, cache, rotary):
    n = batch * seq
    tq = min(Q_BLOCK, seq)
    nq = seq // tq
    past = 0 if cache is None else cache[0].shape[2]
    heads = MLA_HEADS_PER_STEP if nq > 1 else A_HEADS
    bb = _batch_per_step(batch, nq, past)
    in_specs = [
        pl.BlockSpec((bb * tq, heads * LANES), lambda b, hg, qi: (b * nq + qi, hg)),
        pl.BlockSpec((bb * seq, LANES), lambda b, hg, qi: (b, 0)),
        pl.BlockSpec((bb * seq, LANES), lambda b, hg, qi: (b, 0)),
    ]
    args = [qa, ckv, akr]
    if rotary:
        in_specs += [pl.BlockSpec((bb * seq, LANES), lambda b, hg, qi: (b, 0))]
        args += [mg]
    layer = lw["layer"]
    if past:
        in_specs += [pl.BlockSpec((None, None, past, LANES), lambda b, hg, qi: (b, layer, 0, 0))] * 2
        args += [cache[0], cache[1]]
    in_specs += [
        pl.BlockSpec((None, A_KV_LORA, heads * LANES), lambda b, hg, qi: (layer, 0, hg)),
        pl.BlockSpec((None, heads * VT_ROWS, A_KV_LORA), lambda b, hg, qi: (layer, hg, 0)),
        pl.BlockSpec((1, LANES), lambda b, hg, qi: (0, 0)),
    ]
    args += [lw["w_uk"], lw["w_uvt"], lw["a_knorm_g"]]
    return pl.pallas_call(
        functools.partial(_mla_kernel, past=past, rotary=rotary, heads=heads, bb=bb),
        out_shape=jax.ShapeDtypeStruct((n, A_HEADS * A_V), BF16),
        grid=(batch // bb, A_HEADS // heads, nq),
        in_specs=in_specs,
        out_specs=pl.BlockSpec((bb * tq, heads * A_V), lambda b, hg, qi: (b * nq + qi, hg)),
        scratch_shapes=[pltpu.VMEM((bb * heads, past + seq, LANES), BF16),
                        pltpu.VMEM((bb * heads, VT_ROWS, past + seq), BF16)],
        compiler_params=_params(3),
        name="mla_attn",
    )(*args)


def _gqa_kernel(*refs, past, groups, bb):
    q_ref, k_ref, v_ref = refs[:3]
    pos = 3
    if past:
        kc_ref, vc_ref = refs[pos:pos + 2]
        pos += 2
    o_ref, kbuf, vbuf = refs[pos:]
    seq = k_ref.shape[0] // bb
    tq = q_ref.shape[0] // bb
    qw = G_GROUP * G_HD

    @pl.when(pl.program_id(2) == 0)
    def _():
        for i in range(bb * groups):
            sb, gi = divmod(i, groups)
            grp = pl.program_id(1) * groups + gi

            r = lax.broadcasted_iota(jnp.int32, (VT_ROWS, LANES), 0)
            c = lax.broadcasted_iota(jnp.int32, (VT_ROWS, LANES), 1)
            pick = jnp.where((c == r + grp * G_HD) & (r < G_HD), 1.0, 0.0).astype(BF16)

            def fill(r0, k, v):
                low = _lane(k.shape) < G_HD
                k_lo = jnp.where(grp == 0, k, pltpu.roll(k, G_HD, 1))
                rows = k.shape[0]
                kbuf[i, r0:r0 + rows, :] = jnp.where(low, k_lo, pltpu.roll(k_lo, G_HD, 1)).astype(BF16)
                vbuf[i, :, r0:r0 + rows] = _ones_row(_dot_nt(pick, v.astype(BF16))).astype(BF16)

            fill(past, k_ref[sb * seq:(sb + 1) * seq, :], v_ref[sb * seq:(sb + 1) * seq, :])
            if past:
                fill(0, kc_ref[...], vc_ref[...])

    qs = []
    for i in range(bb * groups):
        sb, gi = divmod(i, groups)
        for j in range(G_GROUP):
            col = q_ref[sb * tq:(sb + 1) * tq, gi * qw + (j // 2) * LANES:gi * qw + (j // 2 + 1) * LANES]
            keep = (_lane(col.shape) < G_HD) == (j % 2 == 0)
            qs.append(jnp.where(keep, col, jnp.zeros_like(col)))
    _flash_heads(qs, lambda e: kbuf[e // G_GROUP], lambda e, c0, c1: vbuf[e // G_GROUP, :, c0:c1],
                 past + seq, o_ref, groups * G_GROUP)


def _gqa_call(gq, gk, gv, batch, seq, cache):
    n = batch * seq
    tq = min(Q_BLOCK, seq)
    nq = seq // tq
    past = 0 if cache is None else cache[0].shape[2]
    groups = GQA_GROUPS_PER_STEP if nq > 1 else G_KV_HEADS
    bb = _batch_per_step(batch, nq, past)
    qw = G_GROUP * G_HD
    kvw = G_KV_HEADS * G_HD
    in_specs = [
        pl.BlockSpec((bb * tq, groups * qw), lambda b, g, qi: (b * nq + qi, g)),
        pl.BlockSpec((bb * seq, kvw), lambda b, g, qi: (b, 0)),
        pl.BlockSpec((bb * seq, kvw), lambda b, g, qi: (b, 0)),
    ]
    args = [gq, gk, gv]
    if past:
        layer = cache[2]
        in_specs += [pl.BlockSpec((None, None, past, kvw), lambda b, g, qi: (b, layer, 0, 0))] * 2
        args += [cache[0], cache[1]]
    return pl.pallas_call(
        functools.partial(_gqa_kernel, past=past, groups=groups, bb=bb),
        out_shape=jax.ShapeDtypeStruct((n, G_HEADS * G_HD), BF16),
        grid=(batch // bb, G_KV_HEADS // groups, nq),
        in_specs=in_specs,
        out_specs=pl.BlockSpec((bb * tq, groups * qw), lambda b, g, qi: (b * nq + qi, g)),
        scratch_shapes=[pltpu.VMEM((bb * groups, past + seq, LANES), BF16),
                        pltpu.VMEM((bb * groups, VT_ROWS, past + seq), BF16)],
        compiler_params=_params(3),
        name="gqa_attn",
    )(*args)


def _merge_ffn_kernel(x_ref, ym_ref, ya_ref, yg_ref, gates_ref, mod_ref, g2_ref, wb_ref, wo_ref, wfi_ref,
                      wfo_ref, o_ref, *, ff_chunks):
    x = x_ref[...]
    d = x.shape[-1]
    mixed = None
    for i, y_ref in enumerate((ym_ref, ya_ref, yg_ref)):
        br = _dot(y_ref[...].astype(BF16), wb_ref[i]) * gates_ref[:, i * d:(i + 1) * d]
        mixed = br if mixed is None else mixed + br
    gt1 = mod_ref[2:3, :]
    x1 = x + gt1 * _dot(mixed.astype(BF16), wo_ref[...])

    sh2 = mod_ref[3:4, :]
    sc2 = mod_ref[4:5, :]
    gt2 = mod_ref[5:6, :]
    h2 = ((_rms(x1, d) * g2_ref[...]) * (1.0 + sc2) + sh2).astype(BF16)
    d_ff = wfo_ref.shape[0]
    acc = None
    for c0, c1 in ff_chunks:
        ug = _dot(h2, wfi_ref[:, c0:c1])
        uv = _dot(h2, wfi_ref[:, d_ff + c0:d_ff + c1])
        act = (ug * _sigmoid(ug) * uv).astype(BF16)
        part = _dot(act, wfo_ref[c0:c1, :])
        acc = part if acc is None else acc + part
    o_ref[...] = x1 + gt2 * acc


def _merge_ffn_call(x, ym, ya, yg, gates, mod, lw, mod_row0, rows_per_mod):
    n, d = x.shape
    tm = TOKEN_BLOCK
    d_ff = lw["w_ffn_out"].shape[1]
    layer = lw["layer"]
    split = -(-(d_ff // 2) // MXU_COLS) * MXU_COLS
    ff_chunks = ((0, split), (split, d_ff)) if 0 < split < d_ff else ((0, d_ff),)

    def mod_idx(i):
        return (mod_row0 + (i * tm) // rows_per_mod, 0, 0)

    def rows(width):
        return pl.BlockSpec((tm, width), lambda i: (i, 0))

    return pl.pallas_call(
        functools.partial(_merge_ffn_kernel, ff_chunks=ff_chunks),
        out_shape=jax.ShapeDtypeStruct((n, d), F32),
        grid=(n // tm,),
        in_specs=[
            rows(d), rows(BRANCH_WIDTH), rows(BRANCH_WIDTH), rows(BRANCH_WIDTH), rows(N_BRANCH * d),
            pl.BlockSpec((None, N_MOD, d), mod_idx),
            _resident((1, d)),
            _resident((N_BRANCH, BRANCH_WIDTH, d), layer),
            _resident((d, d), layer),
            _resident((d, 2 * d_ff), layer),
            _resident((d_ff, d), layer),
        ],
        out_specs=rows(d),
        compiler_params=_params(1),
        name="merge_ffn",
    )(x, ym, ya, yg, gates, mod, lw["norm2_g"], lw["w_branch"], lw["w_out"], lw["w_ffn_in"], lw["w_ffn_out"])


def _mla_q_layout(a):
    lead = a.shape[:-1]
    a = a.reshape(lead + (A_HEADS, A_QK))
    z = jnp.zeros(lead + (A_HEADS, LANES - A_QK), a.dtype)
    return jnp.concatenate([a[..., A_NOPE:], z, a[..., :A_NOPE]], axis=-1).reshape(lead + (A_HEADS * LANES,))


def _stacked_weights(w_in, w_uq, w_ukv, w_branch, w_out, w_ffn_in, w_ffn_out):
    depth, d, _ = w_in.shape
    hw = M_HEADS * M_DK
    sizes = (N_BRANCH * d, hw, hw, hw, hw, 4 * M_HEADS, A_Q_LORA, A_KV_LORA, A_ROPE, G_HEADS * G_HD,
             G_KV_HEADS * G_HD, G_KV_HEADS * G_HD)
    edges = np.concatenate([[0], np.cumsum(sizes)]).tolist()
    (wmg, waq, wakv, wakr, wgq, wgk, wgv) = [w_in[:, :, a:b] for a, b in zip(edges[5:-1], edges[6:])]
    misc_pad = jnp.zeros((depth, d, LANES - A_ROPE - 4 * M_HEADS), w_in.dtype)
    tail = jnp.concatenate([waq, wakv, wakr, wmg, misc_pad, wgq, wgk, wgv], axis=2).astype(BF16)
    ukv = w_ukv.reshape(depth, A_KV_LORA, A_HEADS, A_NOPE + A_V)
    uk = jnp.pad(ukv[..., :A_NOPE], ((0, 0), (0, 0), (0, 0), (LANES - A_NOPE, 0)))
    uvt = jnp.pad(ukv[..., A_NOPE:].transpose(0, 2, 3, 1), ((0, 0), (0, 0), (0, VT_ROWS - A_V), (0, 0)))
    return dict(
        w_in=w_in.astype(BF16),
        w_in_tail=tail,
        w_uq=_mla_q_layout(w_uq).astype(BF16),
        w_uk=uk.reshape(depth, A_KV_LORA, A_HEADS * LANES).astype(BF16),
        w_uvt=uvt.reshape(depth, A_HEADS * VT_ROWS, A_KV_LORA).astype(BF16),
        w_branch=w_branch.astype(BF16),
        w_out=w_out.astype(BF16),
        w_ffn_in=w_ffn_in.astype(BF16),
        w_ffn_out=w_ffn_out.astype(BF16),
    )


def _layer_vectors(l, b_mgate, norm1_g, m_norm_g, a_qlora_g, a_kvlora_g, a_qnorm_g, a_knorm_g, g_qnorm_g,
                   g_knorm_g, norm2_g):
    return dict(
        layer=l,
        b_mgate=jnp.pad(b_mgate[l][None, :], ((0, 0), (MG_LANE0, LANES - MG_LANE0 - 4 * M_HEADS))),
        norm1_g=norm1_g[l][None, :],
        m_norm_g=m_norm_g[l][None, :],
        a_qlora_g=a_qlora_g[l][None, :],
        a_kvlora_g=a_kvlora_g[l][None, :],
        a_qnorm_g=_mla_q_layout(jnp.tile(a_qnorm_g[l], A_HEADS)[None, :])[:, :LANES],
        a_knorm_g=_mla_q_layout(jnp.tile(a_knorm_g[l], A_HEADS)[None, :])[:, :LANES],
        g_qnorm_g=jnp.tile(g_qnorm_g[l], LANES // G_HD)[None, :],
        g_knorm_g=jnp.tile(g_knorm_g[l], G_KV_HEADS)[None, :],
        norm2_g=norm2_g[l][None, :],
    )


def _axial_angles(seq, rot_dim):
    n_freq = rot_dim // 4
    freqs = ROPE_BASE ** (-jnp.arange(n_freq, dtype=F32) / n_freq)
    t = jnp.arange(seq)
    row = (t // GRID_W).astype(F32)
    col = (t % GRID_W).astype(F32)
    return jnp.concatenate([row[:, None] * freqs, col[:, None] * freqs], axis=-1)


def _rope_tables(seq):
    ang = _axial_angles(seq, A_ROPE)
    one = jnp.ones((seq, LANES - A_ROPE), F32)
    mla_cos = jnp.concatenate([jnp.cos(ang), jnp.cos(ang), one], axis=-1)
    mla_sin = jnp.concatenate([-jnp.sin(ang), jnp.sin(ang), 0.0 * one], axis=-1)
    ang = _axial_angles(seq, G_HD)
    cos = jnp.concatenate([jnp.cos(ang), jnp.cos(ang)], axis=-1)
    sin = jnp.concatenate([-jnp.sin(ang), jnp.sin(ang)], axis=-1)
    gqa = (jnp.tile(cos, (1, LANES // G_HD)), jnp.tile(sin, (1, LANES // G_HD)))
    return (mla_cos, mla_sin), gqa


def _layer(x, mod, lw, batch, seq, mod_row0, rows_per_mod, ctx, rope):
    (gates, qkv, og, mg, qa, ckv, akr, gq, gk, gv) = _inproj_call(x, mod, lw, mod_row0, rows_per_mod, rope, seq)
    if ctx is None:
        ym, state = _mlstm_call(qkv, og, mg, lw, batch, seq, None, True)
        ya = _mla_call(qa, ckv, akr, mg, lw, batch, seq, None, False)
        yg = _gqa_call(gq, gk, gv, batch, seq, None)
        new_ctx = dict(state=state, ckv=ckv, kr=akr[:, :A_ROPE], gk=gk, gv=gv)
    else:
        ym, _ = _mlstm_call(qkv, og, mg, lw, batch, seq, ctx["mlstm"], False)
        ya = _mla_call(qa, ckv, akr, mg, lw, batch, seq, ctx["mla"], True)
        yg = _gqa_call(gq, gk, gv, batch, seq, ctx["gqa"])
        new_ctx = None
    x = _merge_ffn_call(x, ym, ya, yg, gates, mod, lw, mod_row0, rows_per_mod)
    return x, new_ctx


def kernel(x_prompt, x_sample, state_mlstm_C, state_mlstm_n, state_mlstm_m, cache_mla_ckv, cache_mla_krope,
           cache_gqa_k, cache_gqa_v, c, c_ctx, w_mod, b_mod, norm1_g, w_in, b_mgate, m_norm_g, a_qlora_g,
           a_kvlora_g, w_uq, w_ukv, a_qnorm_g, a_knorm_g, g_qnorm_g, g_knorm_g, w_branch, w_out, norm2_g,
           w_ffn_in, w_ffn_out):
    batch, seq, d = x_prompt.shape
    dbatch, dseq, _ = x_sample.shape
    depth = w_in.shape[0]
    past = cache_mla_ckv.shape[2]
    for tokens, length in ((batch * seq, seq), (dbatch * dseq, dseq)):
        assert tokens % TOKEN_BLOCK == 0 and length % MLSTM_CHUNK == 0 and length % min(Q_BLOCK, length) == 0
        assert TOKEN_BLOCK % length == 0 or length % TOKEN_BLOCK == 0
    assert d == D_MODEL and (N_MOD * d) % (MOD_COL_BLOCKS * LANES) == 0
    assert dseq % TOKEN_BLOCK == 0 and dseq % GRID_W == 0 and past % LANES == 0
    assert MLSTM_HEADS_PER_STEP == M_HEADS

    n_rows = -(-(1 + dbatch) // SUBLANES) * SUBLANES
    cond = jnp.concatenate([c_ctx[None, :], c, jnp.zeros((n_rows - 1 - dbatch, d), F32)], axis=0)
    mod_all = _mod_call(cond, w_mod, b_mod).reshape(depth, n_rows, N_MOD, d)

    rope = _rope_tables(dseq)
    xp = x_prompt.reshape(batch * seq, d)
    xs = x_sample.reshape(dbatch * dseq, d)
    ctx_layers = []
    krope_cache = jnp.pad(cache_mla_krope, ((0, 0), (0, 0), (0, 0), (0, LANES - A_ROPE)))
    gqa_k_cache = cache_gqa_k.reshape(dbatch, depth, past, G_KV_HEADS * G_HD)
    gqa_v_cache = cache_gqa_v.reshape(dbatch, depth, past, G_KV_HEADS * G_HD)
    stacked = _stacked_weights(w_in, w_uq, w_ukv, w_branch, w_out, w_ffn_in, w_ffn_out)
    for l in range(depth):
        lw = dict(stacked, **_layer_vectors(l, b_mgate, norm1_g, m_norm_g, a_qlora_g, a_kvlora_g, a_qnorm_g,
                                            a_knorm_g, g_qnorm_g, g_knorm_g, norm2_g))
        xp, st = _layer(xp, mod_all[l], lw, batch, seq, 0, batch * seq, None, None)
        ctx_layers.append(st)
        ctx = dict(mlstm=(state_mlstm_C, state_mlstm_n, state_mlstm_m, l), mla=(cache_mla_ckv, krope_cache),
                   gqa=(gqa_k_cache, gqa_v_cache, l))
        xs, _ = _layer(xs, mod_all[l], lw, dbatch, dseq, 1, dseq, ctx, rope)

    def stack(fn):
        return jnp.stack([fn(s) for s in ctx_layers], axis=1)

    new_c = stack(lambda s: s["state"][0])
    new_n = stack(lambda s: s["state"][1])
    new_m = stack(lambda s: s["state"][2])
    new_ckv = stack(lambda s: s["ckv"].reshape(batch, seq, A_KV_LORA))
    new_kr = stack(lambda s: s["kr"].reshape(batch, seq, A_ROPE))
    new_gk = stack(lambda s: s["gk"].reshape(batch, seq, G_KV_HEADS, G_HD))
    new_gv = stack(lambda s: s["gv"].reshape(batch, seq, G_KV_HEADS, G_HD))
    return (xp.reshape(batch, seq, d), xs.reshape(dbatch, dseq, d), new_c, new_n, new_m, new_ckv, new_kr,
            new_gk, new_gv)
```

```python
import functools

import numpy as np
import jax
import jax.numpy as jnp
from jax import lax
from jax.experimental import pallas as pl
from jax.experimental.pallas import tpu as pltpu

F32 = jnp.float32
BF16 = jnp.bfloat16

LANES = 128
SUBLANES = 8
VMEM_LIMIT_BYTES = 56 * 1024 * 1024

EPS = 1e-6
ROPE_BASE = 10000.0
GRID_W = 64

M_HEADS = 4
M_DK = 128
M_DV = 128
A_HEADS = 8
A_NOPE = 64
A_ROPE = 32
A_QK = A_NOPE + A_ROPE
A_V = 64
A_Q_LORA = 256
A_KV_LORA = 128
G_HEADS = 8
G_KV_HEADS = 2
G_GROUP = G_HEADS // G_KV_HEADS
G_HD = 64
N_BRANCH = 3
BRANCH_WIDTH = 512
N_MOD = 6

MOD_COL_BLOCKS = 4
TOKEN_BLOCK = 512
Q_BLOCK = 256
KEY_BLOCK = 256
MLA_HEADS_PER_STEP = 8
GQA_GROUPS_PER_STEP = 2
SHORT_SEQ_BATCH_PER_STEP = 4
SCORES_AHEAD = 2
MLSTM_CHUNK = 128
SHORT_SEQ_CHUNK = 256
MLSTM_HEADS_PER_STEP = 4
LOG2E = 1.4426950408889634


def _params(n_axes):
    return pltpu.CompilerParams(dimension_semantics=("arbitrary",) * n_axes,
                                vmem_limit_bytes=VMEM_LIMIT_BYTES)


def _resident(shape, layer=None):
    nd = len(shape)
    if layer is None:
        return pl.BlockSpec(shape, lambda *_: (0,) * nd, pipeline_mode=pl.Buffered(1))
    return pl.BlockSpec((None,) + tuple(shape), lambda *_: (layer,) + (0,) * nd, pipeline_mode=pl.Buffered(1))


def _lane(shape, axis=None):
    return lax.broadcasted_iota(jnp.int32, shape, len(shape) - 1 if axis is None else axis)


def _dot(a, b):
    return jnp.dot(a, b, preferred_element_type=F32)


def _dot_nt(a, b):
    return lax.dot_general(a, b, (((1,), (1,)), ((), ())), preferred_element_type=F32)


def _split3(a):
    hi = a.astype(BF16)
    r1 = a - hi.astype(F32)
    mid = r1.astype(BF16)
    lo = (r1 - mid.astype(F32)).astype(BF16)
    return hi, mid, lo


def _dot01(a, m01):
    hi, mid, lo = _split3(a)
    return _dot(hi, m01) + _dot(mid, m01) + _dot(lo, m01)


def _dot01_left(m01, a):
    hi, mid, lo = _split3(a)
    return _dot(m01, hi) + _dot(m01, mid) + _dot(m01, lo)


def _sigmoid(x):
    return 0.5 * jnp.tanh(0.5 * x) + 0.5


def _log_sigmoid(x):
    return jnp.minimum(x, 0.0) - jnp.log(1.0 + jnp.exp(-jnp.abs(x)))


def _rms(x, width):
    ms = jnp.sum(x * x, axis=-1, keepdims=True) * (1.0 / width)
    return x * lax.rsqrt(ms + EPS)


def _rope(x, cos, sin_signed, half, period):
    n = x.shape[-1]
    first = (_lane(x.shape) % period) < half
    swapped = jnp.where(first, pltpu.roll(x, n - half, x.ndim - 1), pltpu.roll(x, half, x.ndim - 1))
    return x * cos + swapped * sin_signed


def _mod_kernel(c_ref, w_ref, b_ref, o_ref):
    c = c_ref[...]
    a = c * _sigmoid(c)
    o_ref[...] = _dot01(a, w_ref[...].astype(BF16)) + b_ref[...]


def _mod_call(cond, w_mod, b_mod):
    depth, d, n = w_mod.shape
    rows = cond.shape[0]
    tn = n // MOD_COL_BLOCKS
    return pl.pallas_call(
        _mod_kernel,
        out_shape=jax.ShapeDtypeStruct((depth, rows, n), F32),
        grid=(depth, n // tn),
        in_specs=[
            pl.BlockSpec((rows, d), lambda l, j: (0, 0)),
            pl.BlockSpec((None, d, tn), lambda l, j: (l, 0, j)),
            pl.BlockSpec((None, 1, tn), lambda l, j: (l, 0, j)),
        ],
        out_specs=pl.BlockSpec((None, rows, tn), lambda l, j: (l, 0, j)),
        compiler_params=_params(2),
        name="adaln_mod",
    )(cond, w_mod, b_mod.reshape(depth, 1, n))


MXU_COLS = 256
D_MODEL = 1024
_GATE0, _GATE1 = 0, N_BRANCH * D_MODEL
_MQ0 = _GATE1
_MK0 = _MQ0 + M_HEADS * M_DK
_MV0 = _MK0 + M_HEADS * M_DK
_MO0 = _MV0 + M_HEADS * M_DV
_AQ0 = _MO0 + M_HEADS * M_DV
_AKV0 = _AQ0 + A_Q_LORA
_MISC0 = _AKV0 + A_KV_LORA
_GQ0 = _MISC0 + LANES
_GK0 = _GQ0 + G_HEADS * G_HD
_GV0 = _GK0 + G_KV_HEADS * G_HD
_WIN_COLS = _GV0 + G_KV_HEADS * G_HD
assert _WIN_COLS % MXU_COLS == 0 and _AQ0 % MXU_COLS == 0
MG_LANE0 = A_ROPE
KR_SS_LANE = 64
assert MG_LANE0 + 4 * M_HEADS <= KR_SS_LANE < LANES


def _head_pair_ms(x):
    low = (_lane(x.shape) % LANES) < G_HD
    sq = x * x
    cols = []
    for c0 in range(0, x.shape[-1], LANES):
        s = sq[:, c0:c0 + LANES]
        lo = jnp.sum(jnp.where(low[:, c0:c0 + LANES], s, 0.0), axis=-1, keepdims=True)
        hi = jnp.sum(jnp.where(low[:, c0:c0 + LANES], 0.0, s), axis=-1, keepdims=True)
        cols.append(jnp.where(low[:, c0:c0 + LANES], lo, hi))
    ms = cols[0] if len(cols) == 1 else jnp.concatenate(cols, axis=-1)
    return ms * (1.0 / G_HD)


def _inproj_kernel(*refs, rotary):
    (x_ref, mod_ref, g1_ref, w_ref, wt_ref, bmg_ref, gql_ref, wuq_ref, gkvl_ref, gkn_ref, aqn_ref,
     gqn_ref, akn_ref) = refs[:13]
    pos = 13
    if rotary:
        ca_ref, sa_ref, cg_ref, sg_ref = refs[pos:pos + 4]
        pos += 4
    (gates_ref, qkv_ref, og_ref, mg_ref, qa_ref, ckv_ref, akr_ref, gq_ref, gk_ref, gv_ref) = refs[pos:]
    x = x_ref[...]
    d = x.shape[-1]
    sh1 = mod_ref[0:1, :]
    sc1 = mod_ref[1:2, :]
    h = (_rms(x, d) * g1_ref[...]) * (1.0 + sc1) + sh1
    hb = h.astype(BF16)

    def proj(c0, width):
        if c0 < _AQ0:
            return _dot(hb, w_ref[:, c0:c0 + width])
        return _dot(hb, wt_ref[:, c0 - _AQ0:c0 - _AQ0 + width])

    aq = _rms(proj(_AQ0, A_Q_LORA), A_Q_LORA) * gql_ref[...]
    qa = _dot(aq.astype(BF16), wuq_ref[...])
    for hd in range(A_HEADS):
        qh = _rms(qa[:, hd * LANES:(hd + 1) * LANES], A_QK) * aqn_ref[...]
        if rotary:
            qh = _rope(qh, ca_ref[...], sa_ref[...], A_ROPE // 2, LANES)
        qa_ref[:, hd * LANES:(hd + 1) * LANES] = (qh * (A_QK ** -0.5 * LOG2E)).astype(qa_ref.dtype)
    akv_misc = proj(_AKV0, A_KV_LORA + LANES)
    ckv_ref[...] = _rms(akv_misc[:, :A_KV_LORA], A_KV_LORA) * gkvl_ref[...]
    misc = akv_misc[:, A_KV_LORA:]
    kr = jnp.where(_lane(misc.shape) < A_ROPE, misc, 0.0)
    mg = misc + bmg_ref[...]
    if rotary:
        akr_ref[...] = _rope(kr * akn_ref[...], ca_ref[...], sa_ref[...], A_ROPE // 2, LANES)
        mg = jnp.where(_lane(mg.shape) == KR_SS_LANE, jnp.sum(kr * kr, axis=-1, keepdims=True), mg)
    else:
        akr_ref[...] = kr
    mg_ref[...] = mg

    gq_all = proj(_GQ0, G_HEADS * G_HD)
    for c0 in range(0, G_HEADS * G_HD, LANES):
        gq = gq_all[:, c0:c0 + LANES]
        gq = gq * lax.rsqrt(_head_pair_ms(gq) + EPS) * gqn_ref[...]
        if rotary:
            gq = _rope(gq, cg_ref[...], sg_ref[...], G_HD // 2, G_HD)
        gq_ref[:, c0:c0 + LANES] = (gq * (G_HD ** -0.5 * LOG2E)).astype(gq_ref.dtype)
    kvw = G_KV_HEADS * G_HD
    gkv = proj(_GK0, 2 * kvw)
    gk = gkv[:, :kvw]
    gk = gk * lax.rsqrt(_head_pair_ms(gk) + EPS) * gkn_ref[...]
    if rotary:
        gk = _rope(gk, cg_ref[...], sg_ref[...], G_HD // 2, G_HD)
    gk_ref[...] = gk
    gv_ref[...] = gkv[:, kvw:]

    hw = M_HEADS * M_DK
    gates_ref[...] = _sigmoid(proj(_GATE0, _GATE1 - _GATE0)).astype(gates_ref.dtype)
    og_ref[...] = _sigmoid(proj(_MO0, hw)).astype(og_ref.dtype)
    qkv_ref[:, 0:hw] = proj(_MQ0, hw).astype(qkv_ref.dtype)
    qkv_ref[:, hw:2 * hw] = (proj(_MK0, hw) * (M_DK ** -0.5)).astype(qkv_ref.dtype)
    qkv_ref[:, 2 * hw:3 * hw] = proj(_MV0, hw).astype(qkv_ref.dtype)


def _inproj_call(x, mod, lw, mod_row0, rows_per_mod, rope, seq):
    n, d = x.shape
    tm = TOKEN_BLOCK
    hw = M_HEADS * M_DK

    def mod_idx(i):
        return (mod_row0 + (i * tm) // rows_per_mod, 0, 0)

    def rows(width):
        return pl.BlockSpec((tm, width), lambda i: (i, 0))

    in_specs = [
        rows(d),
        pl.BlockSpec((None, N_MOD, d), mod_idx),
        _resident((1, d)),
        _resident(lw["w_in"].shape[1:], lw["layer"]),
        _resident((d, _WIN_COLS - _AQ0), lw["layer"]),
        _resident((1, LANES)),
        _resident((1, A_Q_LORA)),
        _resident((A_Q_LORA, A_HEADS * LANES), lw["layer"]),
        _resident((1, A_KV_LORA)),
        _resident((1, LANES)),
        _resident((1, LANES)),
        _resident((1, LANES)),
        _resident((1, LANES)),
    ]
    args = [x, mod, lw["norm1_g"], lw["w_in"], lw["w_in_tail"], lw["b_mgate"], lw["a_qlora_g"], lw["w_uq"],
            lw["a_kvlora_g"], lw["g_knorm_g"], lw["a_qnorm_g"], lw["g_qnorm_g"], lw["a_knorm_g"]]
    if rope is not None:
        blocks_per_seq = seq // tm
        in_specs += [pl.BlockSpec((tm, LANES), lambda i: (i % blocks_per_seq, 0))] * 4
        args += [rope[0][0], rope[0][1], rope[1][0], rope[1][1]]
    out_widths = [3 * d, 3 * hw, hw, LANES, A_HEADS * LANES, A_KV_LORA, LANES, G_HEADS * G_HD,
                  G_KV_HEADS * G_HD, G_KV_HEADS * G_HD]
    out_dtypes = [BF16, BF16, BF16, F32, BF16, F32, F32, BF16, F32, F32]
    return pl.pallas_call(
        functools.partial(_inproj_kernel, rotary=rope is not None),
        out_shape=[jax.ShapeDtypeStruct((n, w), t) for w, t in zip(out_widths, out_dtypes)],
        grid=(n // tm,),
        in_specs=in_specs,
        out_specs=[rows(w) for w in out_widths],
        compiler_params=_params(1),
        name="in_proj",
    )(*args)


def _mlstm_kernel(*refs, n_chunks, ln, heads, has_init, emit_state):
    q_ref, k_ref, v_ref, og_ref, mg_ref, gn_ref = refs[:6]
    pos = 6
    if has_init:
        c0_ref, n0_ref, m0_ref = refs[pos:pos + 3]
        pos += 3
    y_ref = refs[pos]
    pos += 1
    if emit_state:
        cf_ref, nf_ref, mf_ref = refs[pos:pos + 3]
        pos += 3
    (pcol_scr, lfc_scr, grow_scr, prow_scr, lfr_scr, u_scr, nu_scr, gm_scr, tot_scr, cs_scr, ns_scr, ms_scr,
     c_scr, n_scr, m_scr) = refs[pos:]
    assert heads == M_HEADS
    n_gates = 4 * M_HEADS

    row = lax.broadcasted_iota(jnp.int32, (ln, ln), 0)
    col = lax.broadcasted_iota(jnp.int32, (ln, ln), 1)
    lower = col <= row
    upper = col >= row
    tril = jnp.where(lower, 1.0, 0.0).astype(BF16)
    triu = jnp.where(upper, 1.0, 0.0).astype(BF16)
    ones = jnp.ones((ln, LANES), BF16)

    def gate_index(hh, d):
        return 2 * d * heads + hh, (2 * d + 1) * heads + hh

    def summaries(c, carry):
        r0 = pl.multiple_of(c * ln, ln)
        g = mg_ref[pl.ds(r0, ln), :]
        lf = _log_sigmoid(g) * LOG2E
        pcol_scr[pl.ds(r0, ln), :] = _dot01_left(tril, lf)
        lfc_scr[pl.ds(r0, ln), :] = lf
        g_rows = g.T[MG_LANE0:MG_LANE0 + n_gates, :]
        gr = g_rows * LOG2E
        lfr = _log_sigmoid(g_rows) * LOG2E
        pr = _dot01(lfr, triu)
        grow_scr[c] = gr
        prow_scr[c] = pr
        lfr_scr[c] = lfr
        for hh in range(heads):
            kb = k_ref[pl.ds(r0, ln), hh * LANES:(hh + 1) * LANES]
            k_t = kb.astype(F32).T
            v = v_ref[pl.ds(r0, ln), hh * LANES:(hh + 1) * LANES]
            for d in range(2):
                ji, jf = gate_index(hh, d)
                tot = pr[jf:jf + 1, ln - 1:ln]
                b_row = pr[jf:jf + 1, :] if d == 0 else tot - pr[jf:jf + 1, :] + lfr[jf:jf + 1, :]
                g_row = tot - b_row + gr[ji:ji + 1, :]
                gmax = jnp.max(g_row, axis=-1, keepdims=True)
                wg = jnp.exp2(g_row - gmax)
                u_scr[hh, d, c] = _dot((k_t * wg).astype(BF16), v)
                nu_scr[hh, d, c] = _dot(jnp.broadcast_to(wg, (SUBLANES, ln)).astype(BF16), kb)[0:1, :]
                gm_scr[hh, d, c] = jnp.broadcast_to(gmax, (1, LANES))
                tot_scr[hh, d, c] = jnp.broadcast_to(tot, (1, LANES))
        return carry

    lax.fori_loop(0, n_chunks, summaries, 0, unroll=min(2, n_chunks))

    for hh in range(heads):
        for d in range(2):
            if has_init:
                c_scr[hh, d] = c0_ref[d, hh]
                n_scr[hh, d] = n0_ref[d, hh]
                m_scr[hh, d] = m0_ref[d, hh] * LOG2E
            else:
                c_scr[hh, d] = jnp.zeros((M_DK, M_DV), F32)
                n_scr[hh, d] = jnp.zeros((1, M_DK), F32)
                m_scr[hh, d] = jnp.zeros((1, LANES), F32)

    def scan(j, carry):
        for hh in range(heads):
            for d in range(2):
                c = j if d == 0 else n_chunks - 1 - j
                cst = c_scr[hh, d]
                nst = n_scr[hh, d]
                mst = m_scr[hh, d]
                cs_scr[hh, d, c] = cst.astype(BF16)
                ns_scr[hh, d, c] = nst
                ms_scr[hh, d, c] = mst
                gmax = gm_scr[hh, d, c]
                total = tot_scr[hh, d, c] + mst
                m_new = jnp.maximum(total, gmax)
                decay = jnp.exp2(total - m_new)
                scale = jnp.exp2(gmax - m_new)
                c_scr[hh, d] = cst * decay + u_scr[hh, d, c] * scale
                n_scr[hh, d] = nst * decay + nu_scr[hh, d, c] * scale
                m_scr[hh, d] = m_new
        return carry

    lax.fori_loop(0, n_chunks, scan, 0)

    def readout(c, carry):
        r0 = pl.multiple_of(c * ln, ln)
        pc = pcol_scr[pl.ds(r0, ln), :]
        lf = lfc_scr[pl.ds(r0, ln), :]
        gr = grow_scr[c]
        pr = prow_scr[c]
        lfr = lfr_scr[c]
        tot_all = pc[ln - 1:ln, :]
        for hh in range(heads):
            q = q_ref[pl.ds(r0, ln), hh * LANES:(hh + 1) * LANES]
            k = k_ref[pl.ds(r0, ln), hh * LANES:(hh + 1) * LANES]
            v = v_ref[pl.ds(r0, ln), hh * LANES:(hh + 1) * LANES]
            qk = _dot_nt(q, k)
            hsum = None
            for d in range(2):
                ji, jf = gate_index(hh, d)
                lane_f = MG_LANE0 + jf
                if d == 0:
                    b_col = pc[:, lane_f:lane_f + 1]
                    r_row = gr[ji:ji + 1, :] - pr[jf:jf + 1, :]
                    mask = lower
                else:
                    tot = tot_all[:, lane_f:lane_f + 1]
                    b_col = tot - pc[:, lane_f:lane_f + 1] + lf[:, lane_f:lane_f + 1]
                    r_row = gr[ji:ji + 1, :] - (tot - pr[jf:jf + 1, :] + lfr[jf:jf + 1, :])
                    mask = upper
                b_colb = jnp.broadcast_to(b_col, (ln, ln))
                d_log = jnp.where(mask, b_colb + r_row, -jnp.inf)
                dmax = jnp.broadcast_to(jnp.max(d_log, axis=-1, keepdims=True), (ln, ln))
                inter = b_colb + ms_scr[hh, d, c][:, :1]
                m_t = jnp.maximum(inter, dmax)
                sb = (qk * jnp.exp2(d_log - m_t)).astype(BF16)
                m_t = m_t[:, :LANES]
                w_inter = jnp.exp2(inter[:, :LANES] - m_t)
                qc = _dot(q, cs_scr[hh, d, c])
                qn = _dot_nt(q, jnp.broadcast_to(ns_scr[hh, d, c], (LANES, M_DK)).astype(BF16))
                num = qc * w_inter + _dot(sb, v)
                den = qn * w_inter + _dot(sb, ones)
                hc = num / jnp.maximum(jnp.abs(den), jnp.exp2(-m_t))
                hsum = hc if hsum is None else hsum + hc
            y_ref[pl.ds(r0, ln), hh * LANES:(hh + 1) * LANES] = (
                _rms(hsum, M_DV) * gn_ref[...] * og_ref[pl.ds(r0, ln), hh * LANES:(hh + 1) * LANES]
            ).astype(y_ref.dtype)
        return carry

    lax.fori_loop(0, n_chunks, readout, 0, unroll=min(2, n_chunks))

    if emit_state:
        for hh in range(heads):
            for d in range(2):
                cf_ref[d, hh] = c_scr[hh, d]
                nf_ref[d, hh] = n_scr[hh, d]
                mf_ref[d, hh] = m_scr[hh, d] * (1.0 / LOG2E)


def _mlstm_call(qkv, og, mg, lw, batch, seq, init, emit_state):
    n = batch * seq
    ln = MLSTM_CHUNK if seq > SHORT_SEQ_CHUNK else seq
    nc = seq // ln
    h = M_HEADS
    hs = MLSTM_HEADS_PER_STEP
    ng = h // hs

    def tok(colblock):
        return pl.BlockSpec((seq, hs * LANES), lambda b, hg: (b, colblock(hg)))

    in_specs = [
        tok(lambda hg: hg), tok(lambda hg: ng + hg), tok(lambda hg: 2 * ng + hg),
        tok(lambda hg: hg),
        pl.BlockSpec((seq, LANES), lambda b, hg: (b, 0)),
        pl.BlockSpec((1, M_DV), lambda b, hg: (0, 0)),
    ]
    args = [qkv, qkv, qkv, og, mg, lw["m_norm_g"]]
    state_specs = [
        pl.BlockSpec((None, 2, hs, M_DK, M_DV), lambda b, hg: (b, 0, hg, 0, 0)),
        pl.BlockSpec((None, 2, hs, 1, M_DK), lambda b, hg: (b, 0, hg, 0, 0)),
        pl.BlockSpec((None, 2, hs, 1, LANES), lambda b, hg: (b, 0, hg, 0, 0)),
    ]
    if init is not None:
        c_all, n_all, m_all, layer = init
        depth = c_all.shape[1]
        in_specs += [
            pl.BlockSpec((None, None, 2, hs, M_DK, M_DV), lambda b, hg: (b, layer, 0, hg, 0, 0)),
            pl.BlockSpec((None, None, 2, hs, 1, M_DK), lambda b, hg: (b, layer, 0, hg, 0, 0)),
            pl.BlockSpec((None, None, 2, hs, 1, LANES), lambda b, hg: (b, layer, 0, hg, 0, 0)),
        ]
        args += [c_all, n_all.reshape(batch, depth, 2, h, 1, M_DK),
                 jnp.broadcast_to(m_all[..., None, None], (batch, depth, 2, h, 1, LANES))]
    out_shape = [jax.ShapeDtypeStruct((n, h * M_DV), BF16)]
    out_specs = [tok(lambda hg: hg)]
    if emit_state:
        out_shape += [
            jax.ShapeDtypeStruct((batch, 2, h, M_DK, M_DV), F32),
            jax.ShapeDtypeStruct((batch, 2, h, 1, M_DK), F32),
            jax.ShapeDtypeStruct((batch, 2, h, 1, LANES), F32),
        ]
        out_specs += state_specs
    per = (hs, 2, nc)
    outs = pl.pallas_call(
        functools.partial(_mlstm_kernel, n_chunks=nc, ln=ln, heads=hs, has_init=init is not None,
                          emit_state=emit_state),
        out_shape=out_shape,
        grid=(batch, ng),
        in_specs=in_specs,
        out_specs=out_specs,
        scratch_shapes=[
            pltpu.VMEM((seq, LANES), F32), pltpu.VMEM((seq, LANES), F32),
            pltpu.VMEM((nc, 4 * h, ln), F32), pltpu.VMEM((nc, 4 * h, ln), F32), pltpu.VMEM((nc, 4 * h, ln), F32),
            pltpu.VMEM(per + (M_DK, M_DV), F32), pltpu.VMEM(per + (1, M_DK), F32),
            pltpu.VMEM(per + (1, LANES), F32), pltpu.VMEM(per + (1, LANES), F32),
            pltpu.VMEM(per + (M_DK, M_DV), BF16), pltpu.VMEM(per + (1, M_DK), F32), pltpu.VMEM(per + (1, LANES), F32),
            pltpu.VMEM((hs, 2, M_DK, M_DV), F32), pltpu.VMEM((hs, 2, 1, M_DK), F32), pltpu.VMEM((hs, 2, 1, LANES), F32),
        ],
        compiler_params=_params(2),
        name="mlstm",
    )(*args)
    if emit_state:
        ym, cf, nf, mf = outs
        return ym, (cf, nf[:, :, :, 0, :], mf[:, :, :, 0, 0])
    return outs[0], None


HEAD_V = 64
BF16_ROWS = 2 * SUBLANES
VT_ROWS = HEAD_V + BF16_ROWS


def _ones_row(v_t):
    return jnp.where(lax.broadcasted_iota(jnp.int32, v_t.shape, 0) == HEAD_V, 1.0, v_t)


def _flash_heads(qs, keys_of, vt_chunk, n_keys, o_ref, heads_per_rows):
    def values(e, s_t):
        m = acc = None
        for c0 in range(0, n_keys, KEY_BLOCK):
            c1 = min(c0 + KEY_BLOCK, n_keys)
            s = s_t[c0:c1, :]
            mc = jnp.max(s, axis=0, keepdims=True)
            if c0 == 0:
                m = mc
                acc = _dot(vt_chunk(e, c0, c1), jnp.exp2(s - mc).astype(BF16))
            else:
                m_new = jnp.maximum(m, mc)
                acc = acc * jnp.exp2(m - m_new) + _dot(vt_chunk(e, c0, c1), jnp.exp2(s - m_new).astype(BF16))
                m = m_new
        return acc[0:HEAD_V, :] / acc[HEAD_V:HEAD_V + 1, :]

    if n_keys <= KEY_BLOCK:
        s_all = [_dot_nt(keys_of(e), qb) for e, qb in enumerate(qs)]
        outs = [values(e, s_t) for e, s_t in enumerate(s_all)]
    else:
        outs = []
        pending = [_dot_nt(keys_of(e), qs[e]) for e in range(min(SCORES_AHEAD, len(qs)))]
        for e in range(len(qs)):
            s_t = pending.pop(0)
            if e + SCORES_AHEAD < len(qs):
                pending.append(_dot_nt(keys_of(e + SCORES_AHEAD), qs[e + SCORES_AHEAD]))
            outs.append(values(e, s_t))
    tq = qs[0].shape[0]
    for e in range(0, len(outs), 2):
        pair = jnp.concatenate([outs[e], outs[e + 1]], axis=0)
        r0 = (e // heads_per_rows) * tq
        c0 = ((e % heads_per_rows) // 2) * LANES
        o_ref[r0:r0 + tq, c0:c0 + LANES] = pair.T.astype(o_ref.dtype)


def _batch_per_step(batch, nq, past):
    return SHORT_SEQ_BATCH_PER_STEP if (nq == 1 and past == 0 and batch % SHORT_SEQ_BATCH_PER_STEP == 0) else 1


def _mla_kernel(*refs, past, rotary, heads, bb):
    q_ref, ckv_ref, akr_ref = refs[:3]
    pos = 3
    if rotary:
        mg_ref = refs[pos]
        pos += 1
    if past:
        ckvc_ref, krc_ref = refs[pos:pos + 2]
        pos += 2
    wk_ref, wvt_ref, kg_ref = refs[pos:pos + 3]
    pos += 3
    o_ref, kbuf, vbuf = refs[pos:]
    seq = ckv_ref.shape[0] // bb
    tq = q_ref.shape[0] // bb

    @pl.when(pl.program_id(2) == 0)
    def _():
        for i in range(bb * heads):
            sb, e = divmod(i, heads)
            rows = slice(sb * seq, (sb + 1) * seq)
            wk_h = wk_ref[:, e * LANES:(e + 1) * LANES]
            wvt_h = wvt_ref[e * VT_ROWS:(e + 1) * VT_ROWS, :]

            def keys(ckv, kr, kr_ss):
                cb = ckv.astype(BF16)
                k_nope = _dot(cb, wk_h)
                if kr_ss is None:
                    kcat = k_nope + kr
                    ss = jnp.sum(kcat * kcat, axis=-1, keepdims=True)
                    kcat = kcat * kg_ref[...]
                else:
                    ss = jnp.sum(k_nope * k_nope, axis=-1, keepdims=True) + kr_ss
                    kcat = k_nope * kg_ref[...] + kr
                return _ones_row(_dot_nt(wvt_h, cb)), kcat * lax.rsqrt(ss * (1.0 / A_QK) + EPS)

            kr_ss = mg_ref[rows, KR_SS_LANE:KR_SS_LANE + 1] if rotary else None
            v_t, kn = keys(ckv_ref[rows, :], akr_ref[rows, :], kr_ss)
            kbuf[i, past:past + seq, :] = kn.astype(BF16)
            vbuf[i, :, past:past + seq] = v_t.astype(BF16)
            if past:
                v_tc, knc = keys(ckvc_ref[...], krc_ref[...], None)
                kbuf[i, 0:past, :] = knc.astype(BF16)
                vbuf[i, :, 0:past] = v_tc.astype(BF16)

    qs = [q_ref[sb * tq:(sb + 1) * tq, e * LANES:(e + 1) * LANES] for sb in range(bb) for e in range(heads)]
    _flash_heads(qs, lambda i: kbuf[i], lambda i, c0, c1: vbuf[i, :, c0:c1], past + seq, o_ref, heads)


def _mla_call(qa, ckv, akr, mg, lw, batch, seq, cache, rotary):
    n = batch * seq
    tq = min(Q_BLOCK, seq)
    nq = seq // tq
    past = 0 if cache is None else cache[0].shape[2]
    heads = MLA_HEADS_PER_STEP if nq > 1 else A_HEADS
    bb = _batch_per_step(batch, nq, past)
    in_specs = [
        pl.BlockSpec((bb * tq, heads * LANES), lambda b, hg, qi: (b * nq + qi, hg)),
        pl.BlockSpec((bb * seq, LANES), lambda b, hg, qi: (b, 0)),
        pl.BlockSpec((bb * seq, LANES), lambda b, hg, qi: (b, 0)),
    ]
    args = [qa, ckv, akr]
    if rotary:
        in_specs += [pl.BlockSpec((bb * seq, LANES), lambda b, hg, qi: (b, 0))]
        args += [mg]
    layer = lw["layer"]
    if past:
        in_specs += [pl.BlockSpec((None, None, past, LANES), lambda b, hg, qi: (b, layer, 0, 0))] * 2
        args += [cache[0], cache[1]]
    in_specs += [
        pl.BlockSpec((None, A_KV_LORA, heads * LANES), lambda b, hg, qi: (layer, 0, hg)),
        pl.BlockSpec((None, heads * VT_ROWS, A_KV_LORA), lambda b, hg, qi: (layer, hg, 0)),
        pl.BlockSpec((1, LANES), lambda b, hg, qi: (0, 0)),
    ]
    args += [lw["w_uk"], lw["w_uvt"], lw["a_knorm_g"]]
    return pl.pallas_call(
        functools.partial(_mla_kernel, past=past, rotary=rotary, heads=heads, bb=bb),
        out_shape=jax.ShapeDtypeStruct((n, A_HEADS * A_V), BF16),
        grid=(batch // bb, A_HEADS // heads, nq),
        in_specs=in_specs,
        out_specs=pl.BlockSpec((bb * tq, heads * A_V), lambda b, hg, qi: (b * nq + qi, hg)),
        scratch_shapes=[pltpu.VMEM((bb * heads, past + seq, LANES), BF16),
                        pltpu.VMEM((bb * heads, VT_ROWS, past + seq), BF16)],
        compiler_params=_params(3),
        name="mla_attn",
    )(*args)


def _gqa_kernel(*refs, past, groups, bb):
    q_ref, k_ref, v_ref = refs[:3]
    pos = 3
    if past:
        kc_ref, vc_ref = refs[pos:pos + 2]
        pos += 2
    o_ref, kbuf, vbuf = refs[pos:]
    seq = k_ref.shape[0] // bb
    tq = q_ref.shape[0] // bb
    qw = G_GROUP * G_HD

    @pl.when(pl.program_id(2) == 0)
    def _():
        for i in range(bb * groups):
            sb, gi = divmod(i, groups)
            grp = pl.program_id(1) * groups + gi

            r = lax.broadcasted_iota(jnp.int32, (VT_ROWS, LANES), 0)
            c = lax.broadcasted_iota(jnp.int32, (VT_ROWS, LANES), 1)
            pick = jnp.where((c == r + grp * G_HD) & (r < G_HD), 1.0, 0.0).astype(BF16)

            def fill(r0, k, v):
                low = _lane(k.shape) < G_HD
                k_lo = jnp.where(grp == 0, k, pltpu.roll(k, G_HD, 1))
                rows = k.shape[0]
                kbuf[i, r0:r0 + rows, :] = jnp.where(low, k_lo, pltpu.roll(k_lo, G_HD, 1)).astype(BF16)
                vbuf[i, :, r0:r0 + rows] = _ones_row(_dot_nt(pick, v.astype(BF16))).astype(BF16)

            fill(past, k_ref[sb * seq:(sb + 1) * seq, :], v_ref[sb * seq:(sb + 1) * seq, :])
            if past:
                fill(0, kc_ref[...], vc_ref[...])

    qs = []
    for i in range(bb * groups):
        sb, gi = divmod(i, groups)
        for j in range(G_GROUP):
            col = q_ref[sb * tq:(sb + 1) * tq, gi * qw + (j // 2) * LANES:gi * qw + (j // 2 + 1) * LANES]
            keep = (_lane(col.shape) < G_HD) == (j % 2 == 0)
            qs.append(jnp.where(keep, col, jnp.zeros_like(col)))
    _flash_heads(qs, lambda e: kbuf[e // G_GROUP], lambda e, c0, c1: vbuf[e // G_GROUP, :, c0:c1],
                 past + seq, o_ref, groups * G_GROUP)


def _gqa_call(gq, gk, gv, batch, seq, cache):
    n = batch * seq
    tq = min(Q_BLOCK, seq)
    nq = seq // tq
    past = 0 if cache is None else cache[0].shape[2]
    groups = GQA_GROUPS_PER_STEP if nq > 1 else G_KV_HEADS
    bb = _batch_per_step(batch, nq, past)
    qw = G_GROUP * G_HD
    kvw = G_KV_HEADS * G_HD
    in_specs = [
        pl.BlockSpec((bb * tq, groups * qw), lambda b, g, qi: (b * nq + qi, g)),
        pl.BlockSpec((bb * seq, kvw), lambda b, g, qi: (b, 0)),
        pl.BlockSpec((bb * seq, kvw), lambda b, g, qi: (b, 0)),
    ]
    args = [gq, gk, gv]
    if past:
        layer = cache[2]
        in_specs += [pl.BlockSpec((None, None, past, kvw), lambda b, g, qi: (b, layer, 0, 0))] * 2
        args += [cache[0], cache[1]]
    return pl.pallas_call(
        functools.partial(_gqa_kernel, past=past, groups=groups, bb=bb),
        out_shape=jax.ShapeDtypeStruct((n, G_HEADS * G_HD), BF16),
        grid=(batch // bb, G_KV_HEADS // groups, nq),
        in_specs=in_specs,
        out_specs=pl.BlockSpec((bb * tq, groups * qw), lambda b, g, qi: (b * nq + qi, g)),
        scratch_shapes=[pltpu.VMEM((bb * groups, past + seq, LANES), BF16),
                        pltpu.VMEM((bb * groups, VT_ROWS, past + seq), BF16)],
        compiler_params=_params(3),
        name="gqa_attn",
    )(*args)


def _merge_ffn_kernel(x_ref, ym_ref, ya_ref, yg_ref, gates_ref, mod_ref, g2_ref, wb_ref, wo_ref, wfi_ref,
                      wfo_ref, o_ref, *, ff_chunks):
    x = x_ref[...]
    d = x.shape[-1]
    mixed = None
    for i, y_ref in enumerate((ym_ref, ya_ref, yg_ref)):
        br = _dot(y_ref[...].astype(BF16), wb_ref[i]) * gates_ref[:, i * d:(i + 1) * d]
        mixed = br if mixed is None else mixed + br
    gt1 = mod_ref[2:3, :]
    x1 = x + gt1 * _dot(mixed.astype(BF16), wo_ref[...])

    sh2 = mod_ref[3:4, :]
    sc2 = mod_ref[4:5, :]
    gt2 = mod_ref[5:6, :]
    h2 = ((_rms(x1, d) * g2_ref[...]) * (1.0 + sc2) + sh2).astype(BF16)
    d_ff = wfo_ref.shape[0]
    acc = None
    for c0, c1 in ff_chunks:
        ug = _dot(h2, wfi_ref[:, c0:c1])
        uv = _dot(h2, wfi_ref[:, d_ff + c0:d_ff + c1])
        act = (ug * _sigmoid(ug) * uv).astype(BF16)
        part = _dot(act, wfo_ref[c0:c1, :])
        acc = part if acc is None else acc + part
    o_ref[...] = x1 + gt2 * acc


def _merge_ffn_call(x, ym, ya, yg, gates, mod, lw, mod_row0, rows_per_mod):
    n, d = x.shape
    tm = TOKEN_BLOCK
    d_ff = lw["w_ffn_out"].shape[1]
    layer = lw["layer"]
    split = -(-(d_ff // 2) // MXU_COLS) * MXU_COLS
    ff_chunks = ((0, split), (split, d_ff)) if 0 < split < d_ff else ((0, d_ff),)

    def mod_idx(i):
        return (mod_row0 + (i * tm) // rows_per_mod, 0, 0)

    def rows(width):
        return pl.BlockSpec((tm, width), lambda i: (i, 0))

    return pl.pallas_call(
        functools.partial(_merge_ffn_kernel, ff_chunks=ff_chunks),
        out_shape=jax.ShapeDtypeStruct((n, d), F32),
        grid=(n // tm,),
        in_specs=[
            rows(d), rows(BRANCH_WIDTH), rows(BRANCH_WIDTH), rows(BRANCH_WIDTH), rows(N_BRANCH * d),
            pl.BlockSpec((None, N_MOD, d), mod_idx),
            _resident((1, d)),
            _resident((N_BRANCH, BRANCH_WIDTH, d), layer),
            _resident((d, d), layer),
            _resident((d, 2 * d_ff), layer),
            _resident((d_ff, d), layer),
        ],
        out_specs=rows(d),
        compiler_params=_params(1),
        name="merge_ffn",
    )(x, ym, ya, yg, gates, mod, lw["norm2_g"], lw["w_branch"], lw["w_out"], lw["w_ffn_in"], lw["w_ffn_out"])


def _mla_q_layout(a):
    lead = a.shape[:-1]
    a = a.reshape(lead + (A_HEADS, A_QK))
    z = jnp.zeros(lead + (A_HEADS, LANES - A_QK), a.dtype)
    return jnp.concatenate([a[..., A_NOPE:], z, a[..., :A_NOPE]], axis=-1).reshape(lead + (A_HEADS * LANES,))


def _stacked_weights(w_in, w_uq, w_ukv, w_branch, w_out, w_ffn_in, w_ffn_out):
    depth, d, _ = w_in.shape
    hw = M_HEADS * M_DK
    sizes = (N_BRANCH * d, hw, hw, hw, hw, 4 * M_HEADS, A_Q_LORA, A_KV_LORA, A_ROPE, G_HEADS * G_HD,
             G_KV_HEADS * G_HD, G_KV_HEADS * G_HD)
    edges = np.concatenate([[0], np.cumsum(sizes)]).tolist()
    (wmg, waq, wakv, wakr, wgq, wgk, wgv) = [w_in[:, :, a:b] for a, b in zip(edges[5:-1], edges[6:])]
    misc_pad = jnp.zeros((depth, d, LANES - A_ROPE - 4 * M_HEADS), w_in.dtype)
    tail = jnp.concatenate([waq, wakv, wakr, wmg, misc_pad, wgq, wgk, wgv], axis=2).astype(BF16)
    ukv = w_ukv.reshape(depth, A_KV_LORA, A_HEADS, A_NOPE + A_V)
    uk = jnp.pad(ukv[..., :A_NOPE], ((0, 0), (0, 0), (0, 0), (LANES - A_NOPE, 0)))
    uvt = jnp.pad(ukv[..., A_NOPE:].transpose(0, 2, 3, 1), ((0, 0), (0, 0), (0, VT_ROWS - A_V), (0, 0)))
    return dict(
        w_in=w_in.astype(BF16),
        w_in_tail=tail,
        w_uq=_mla_q_layout(w_uq).astype(BF16),
        w_uk=uk.reshape(depth, A_KV_LORA, A_HEADS * LANES).astype(BF16),
        w_uvt=uvt.reshape(depth, A_HEADS * VT_ROWS, A_KV_LORA).astype(BF16),
        w_branch=w_branch.astype(BF16),
        w_out=w_out.astype(BF16),
        w_ffn_in=w_ffn_in.astype(BF16),
        w_ffn_out=w_ffn_out.astype(BF16),
    )


def _layer_vectors(l, b_mgate, norm1_g, m_norm_g, a_qlora_g, a_kvlora_g, a_qnorm_g, a_knorm_g, g_qnorm_g,
                   g_knorm_g, norm2_g):
    return dict(
        layer=l,
        b_mgate=jnp.pad(b_mgate[l][None, :], ((0, 0), (MG_LANE0, LANES - MG_LANE0 - 4 * M_HEADS))),
        norm1_g=norm1_g[l][None, :],
        m_norm_g=m_norm_g[l][None, :],
        a_qlora_g=a_qlora_g[l][None, :],
        a_kvlora_g=a_kvlora_g[l][None, :],
        a_qnorm_g=_mla_q_layout(jnp.tile(a_qnorm_g[l], A_HEADS)[None, :])[:, :LANES],
        a_knorm_g=_mla_q_layout(jnp.tile(a_knorm_g[l], A_HEADS)[None, :])[:, :LANES],
        g_qnorm_g=jnp.tile(g_qnorm_g[l], LANES // G_HD)[None, :],
        g_knorm_g=jnp.tile(g_knorm_g[l], G_KV_HEADS)[None, :],
        norm2_g=norm2_g[l][None, :],
    )


def _axial_angles(seq, rot_dim):
    n_freq = rot_dim // 4
    freqs = ROPE_BASE ** (-jnp.arange(n_freq, dtype=F32) / n_freq)
    t = jnp.arange(seq)
    row = (t // GRID_W).astype(F32)
    col = (t % GRID_W).astype(F32)
    return jnp.concatenate([row[:, None] * freqs, col[:, None] * freqs], axis=-1)


def _rope_tables(seq):
    ang = _axial_angles(seq, A_ROPE)
    one = jnp.ones((seq, LANES - A_ROPE), F32)
    mla_cos = jnp.concatenate([jnp.cos(ang), jnp.cos(ang), one], axis=-1)
    mla_sin = jnp.concatenate([-jnp.sin(ang), jnp.sin(ang), 0.0 * one], axis=-1)
    ang = _axial_angles(seq, G_HD)
    cos = jnp.concatenate([jnp.cos(ang), jnp.cos(ang)], axis=-1)
    sin = jnp.concatenate([-jnp.sin(ang), jnp.sin(ang)], axis=-1)
    gqa = (jnp.tile(cos, (1, LANES // G_HD)), jnp.tile(sin, (1, LANES // G_HD)))
    return (mla_cos, mla_sin), gqa


def _layer(x, mod, lw, batch, seq, mod_row0, rows_per_mod, ctx, rope):
    (gates, qkv, og, mg, qa, ckv, akr, gq, gk, gv) = _inproj_call(x, mod, lw, mod_row0, rows_per_mod, rope, seq)
    if ctx is None:
        ym, state = _mlstm_call(qkv, og, mg, lw, batch, seq, None, True)
        ya = _mla_call(qa, ckv, akr, mg, lw, batch, seq, None, False)
        yg = _gqa_call(gq, gk, gv, batch, seq, None)
        new_ctx = dict(state=state, ckv=ckv, kr=akr[:, :A_ROPE], gk=gk, gv=gv)
    else:
        ym, _ = _mlstm_call(qkv, og, mg, lw, batch, seq, ctx["mlstm"], False)
        ya = _mla_call(qa, ckv, akr, mg, lw, batch, seq, ctx["mla"], True)
        yg = _gqa_call(gq, gk, gv, batch, seq, ctx["gqa"])
        new_ctx = None
    x = _merge_ffn_call(x, ym, ya, yg, gates, mod, lw, mod_row0, rows_per_mod)
    return x, new_ctx


def kernel(x_prompt, x_sample, state_mlstm_C, state_mlstm_n, state_mlstm_m, cache_mla_ckv, cache_mla_krope,
           cache_gqa_k, cache_gqa_v, c, c_ctx, w_mod, b_mod, norm1_g, w_in, b_mgate, m_norm_g, a_qlora_g,
           a_kvlora_g, w_uq, w_ukv, a_qnorm_g, a_knorm_g, g_qnorm_g, g_knorm_g, w_branch, w_out, norm2_g,
           w_ffn_in, w_ffn_out):
    batch, seq, d = x_prompt.shape
    dbatch, dseq, _ = x_sample.shape
    depth = w_in.shape[0]
    past = cache_mla_ckv.shape[2]
    for tokens, length in ((batch * seq, seq), (dbatch * dseq, dseq)):
        assert tokens % TOKEN_BLOCK == 0 and length % MLSTM_CHUNK == 0 and length % min(Q_BLOCK, length) == 0
        assert TOKEN_BLOCK % length == 0 or length % TOKEN_BLOCK == 0
    assert d == D_MODEL and (N_MOD * d) % (MOD_COL_BLOCKS * LANES) == 0
    assert dseq % TOKEN_BLOCK == 0 and dseq % GRID_W == 0 and past % LANES == 0
    assert MLSTM_HEADS_PER_STEP == M_HEADS

    n_rows = -(-(1 + dbatch) // SUBLANES) * SUBLANES
    cond = jnp.concatenate([c_ctx[None, :], c, jnp.zeros((n_rows - 1 - dbatch, d), F32)], axis=0)
    mod_all = _mod_call(cond, w_mod, b_mod).reshape(depth, n_rows, N_MOD, d)

    rope = _rope_tables(dseq)
    xp = x_prompt.reshape(batch * seq, d)
    xs = x_sample.reshape(dbatch * dseq, d)
    ctx_layers = []
    krope_cache = jnp.pad(cache_mla_krope, ((0, 0), (0, 0), (0, 0), (0, LANES - A_ROPE)))
    gqa_k_cache = cache_gqa_k.reshape(dbatch, depth, past, G_KV_HEADS * G_HD)
    gqa_v_cache = cache_gqa_v.reshape(dbatch, depth, past, G_KV_HEADS * G_HD)
    stacked = _stacked_weights(w_in, w_uq, w_ukv, w_branch, w_out, w_ffn_in, w_ffn_out)
    for l in range(depth):
        lw = dict(stacked, **_layer_vectors(l, b_mgate, norm1_g, m_norm_g, a_qlora_g, a_kvlora_g, a_qnorm_g,
                                            a_knorm_g, g_qnorm_g, g_knorm_g, norm2_g))
        xp, st = _layer(xp, mod_all[l], lw, batch, seq, 0, batch * seq, None, None)
        ctx_layers.append(st)
        ctx = dict(mlstm=(state_mlstm_C, state_mlstm_n, state_mlstm_m, l), mla=(cache_mla_ckv, krope_cache),
                   gqa=(gqa_k_cache, gqa_v_cache, l))
        xs, _ = _layer(xs, mod_all[l], lw, dbatch, dseq, 1, dseq, ctx, rope)

    def stack(fn):
        return jnp.stack([fn(s) for s in ctx_layers], axis=1)

    new_c = stack(lambda s: s["state"][0])
    new_n = stack(lambda s: s["state"][1])
    new_m = stack(lambda s: s["state"][2])
    new_ckv = stack(lambda s: s["ckv"].reshape(batch, seq, A_KV_LORA))
    new_kr = stack(lambda s: s["kr"].reshape(batch, seq, A_ROPE))
    new_gk = stack(lambda s: s["gk"].reshape(batch, seq, G_KV_HEADS, G_HD))
    new_gv = stack(lambda s: s["gv"].reshape(batch, seq, G_KV_HEADS, G_HD))
    return (xp.reshape(batch, seq, d), xs.reshape(dbatch, dseq, d), new_c, new_n, new_m, new_ckv, new_kr,
            new_gk, new_gv)
```

```python
import functools

import numpy as np
import jax
import jax.numpy as jnp
from jax import lax
from jax.experimental import pallas as pl
from jax.experimental.pallas import tpu as pltpu

F32 = jnp.float32
BF16 = jnp.bfloat16

LANES = 128
SUBLANES = 8
VMEM_LIMIT_BYTES = 56 * 1024 * 1024

EPS = 1e-6
ROPE_BASE = 10000.0
GRID_W = 64

M_HEADS = 4
M_DK = 128
M_DV = 128
A_HEADS = 8
A_NOPE = 64
A_ROPE = 32
A_QK = A_NOPE + A_ROPE
A_V = 64
A_Q_LORA = 256
A_KV_LORA = 128
G_HEADS = 8
G_KV_HEADS = 2
G_GROUP = G_HEADS // G_KV_HEADS
G_HD = 64
N_BRANCH = 3
BRANCH_WIDTH = 512
N_MOD = 6

MOD_COL_BLOCKS = 4
TOKEN_BLOCK = 512
Q_BLOCK = 256
KEY_BLOCK = 256
MLA_HEADS_PER_STEP = 8
GQA_GROUPS_PER_STEP = 2
SHORT_SEQ_BATCH_PER_STEP = 4
SCORES_AHEAD = 2
MLSTM_CHUNK = 128
MLSTM_HEADS_PER_STEP = 4
LOG2E = 1.4426950408889634


def _params(n_axes, vmem_mib=None):
    limit = VMEM_LIMIT_BYTES if vmem_mib is None else vmem_mib * 1024 * 1024
    return pltpu.CompilerParams(dimension_semantics=("arbitrary",) * n_axes, vmem_limit_bytes=limit)


def _resident(shape, layer=None):
    nd = len(shape)
    if layer is None:
        return pl.BlockSpec(shape, lambda *_: (0,) * nd, pipeline_mode=pl.Buffered(1))
    return pl.BlockSpec((None,) + tuple(shape), lambda *_: (layer,) + (0,) * nd, pipeline_mode=pl.Buffered(1))


def _lane(shape, axis=None):
    return lax.broadcasted_iota(jnp.int32, shape, len(shape) - 1 if axis is None else axis)


def _dot(a, b):
    return jnp.dot(a, b, preferred_element_type=F32)


def _dot_nt(a, b):
    return lax.dot_general(a, b, (((1,), (1,)), ((), ())), preferred_element_type=F32)


def _split3(a):
    hi = a.astype(BF16)
    r1 = a - hi.astype(F32)
    mid = r1.astype(BF16)
    lo = (r1 - mid.astype(F32)).astype(BF16)
    return hi, mid, lo


def _dot01(a, m01):
    hi, mid, lo = _split3(a)
    return _dot(hi, m01) + _dot(mid, m01) + _dot(lo, m01)


def _dot01_left(m01, a):
    hi, mid, lo = _split3(a)
    return _dot(m01, hi) + _dot(m01, mid) + _dot(m01, lo)


def _sigmoid(x):
    return 0.5 * jnp.tanh(0.5 * x) + 0.5


def _log_sigmoid(x):
    return jnp.minimum(x, 0.0) - jnp.log(1.0 + jnp.exp(-jnp.abs(x)))


def _rms(x, width):
    ms = jnp.sum(x * x, axis=-1, keepdims=True) * (1.0 / width)
    return x * lax.rsqrt(ms + EPS)


def _rope(x, cos, sin_signed, half, period):
    n = x.shape[-1]
    first = (_lane(x.shape) % period) < half
    swapped = jnp.where(first, pltpu.roll(x, n - half, x.ndim - 1), pltpu.roll(x, half, x.ndim - 1))
    return x * cos + swapped * sin_signed


def _mod_kernel(c_ref, w_ref, b_ref, o_ref):
    c = c_ref[...]
    a = c * _sigmoid(c)
    o_ref[...] = _dot01(a, w_ref[...].astype(BF16)) + b_ref[...]


def _mod_call(cond, w_mod, b_mod):
    depth, d, n = w_mod.shape
    rows = cond.shape[0]
    tn = n // MOD_COL_BLOCKS
    return pl.pallas_call(
        _mod_kernel,
        out_shape=jax.ShapeDtypeStruct((depth, rows, n), F32),
        grid=(depth, n // tn),
        in_specs=[
            pl.BlockSpec((rows, d), lambda l, j: (0, 0)),
            pl.BlockSpec((None, d, tn), lambda l, j: (l, 0, j)),
            pl.BlockSpec((None, 1, tn), lambda l, j: (l, 0, j)),
        ],
        out_specs=pl.BlockSpec((None, rows, tn), lambda l, j: (l, 0, j)),
        compiler_params=_params(2, 32),
        name="adaln_mod",
    )(cond, w_mod, b_mod.reshape(depth, 1, n))


MXU_COLS = 256
D_MODEL = 1024
_GATE0, _GATE1 = 0, N_BRANCH * D_MODEL
_MQ0 = _GATE1
_MK0 = _MQ0 + M_HEADS * M_DK
_MV0 = _MK0 + M_HEADS * M_DK
_MO0 = _MV0 + M_HEADS * M_DV
_AQ0 = _MO0 + M_HEADS * M_DV
_AKV0 = _AQ0 + A_Q_LORA
_MISC0 = _AKV0 + A_KV_LORA
_GQ0 = _MISC0 + LANES
_GK0 = _GQ0 + G_HEADS * G_HD
_GV0 = _GK0 + G_KV_HEADS * G_HD
_WIN_COLS = _GV0 + G_KV_HEADS * G_HD
assert _WIN_COLS % MXU_COLS == 0 and _AQ0 % MXU_COLS == 0
MG_LANE0 = A_ROPE
KR_SS_LANE = 64
assert MG_LANE0 + 4 * M_HEADS <= KR_SS_LANE < LANES


def _head_pair_ms(x):
    low = (_lane(x.shape) % LANES) < G_HD
    sq = x * x
    cols = []
    for c0 in range(0, x.shape[-1], LANES):
        s = sq[:, c0:c0 + LANES]
        lo = jnp.sum(jnp.where(low[:, c0:c0 + LANES], s, 0.0), axis=-1, keepdims=True)
        hi = jnp.sum(jnp.where(low[:, c0:c0 + LANES], 0.0, s), axis=-1, keepdims=True)
        cols.append(jnp.where(low[:, c0:c0 + LANES], lo, hi))
    ms = cols[0] if len(cols) == 1 else jnp.concatenate(cols, axis=-1)
    return ms * (1.0 / G_HD)


def _inproj_kernel(*refs, rotary):
    (x_ref, mod_ref, g1_ref, w_ref, wt_ref, bmg_ref, gql_ref, wuq_ref, gkvl_ref, gkn_ref, aqn_ref,
     gqn_ref, akn_ref) = refs[:13]
    pos = 13
    if rotary:
        ca_ref, sa_ref, cg_ref, sg_ref = refs[pos:pos + 4]
        pos += 4
    (gates_ref, qkv_ref, og_ref, mg_ref, qa_ref, ckv_ref, akr_ref, gq_ref, gk_ref, gv_ref) = refs[pos:]
    x = x_ref[...]
    d = x.shape[-1]
    sh1 = mod_ref[0:1, :]
    sc1 = mod_ref[1:2, :]
    h = (_rms(x, d) * g1_ref[...]) * (1.0 + sc1) + sh1
    hb = h.astype(BF16)

    def proj(c0, width):
        if c0 < _AQ0:
            return _dot(hb, w_ref[:, c0:c0 + width])
        return _dot(hb, wt_ref[:, c0 - _AQ0:c0 - _AQ0 + width])

    aq = _rms(proj(_AQ0, A_Q_LORA), A_Q_LORA) * gql_ref[...]
    qa = _dot(aq.astype(BF16), wuq_ref[...])
    for hd in range(A_HEADS):
        qh = _rms(qa[:, hd * LANES:(hd + 1) * LANES], A_QK) * aqn_ref[...]
        if rotary:
            qh = _rope(qh, ca_ref[...], sa_ref[...], A_ROPE // 2, LANES)
        qa_ref[:, hd * LANES:(hd + 1) * LANES] = (qh * (A_QK ** -0.5 * LOG2E)).astype(qa_ref.dtype)
    akv_misc = proj(_AKV0, A_KV_LORA + LANES)
    ckv_ref[...] = _rms(akv_misc[:, :A_KV_LORA], A_KV_LORA) * gkvl_ref[...]
    misc = akv_misc[:, A_KV_LORA:]
    kr = jnp.where(_lane(misc.shape) < A_ROPE, misc, 0.0)
    mg = misc + bmg_ref[...]
    if rotary:
        akr_ref[...] = _rope(kr * akn_ref[...], ca_ref[...], sa_ref[...], A_ROPE // 2, LANES)
        mg = jnp.where(_lane(mg.shape) == KR_SS_LANE, jnp.sum(kr * kr, axis=-1, keepdims=True), mg)
    else:
        akr_ref[...] = kr
    mg_ref[...] = mg

    gq_all = proj(_GQ0, G_HEADS * G_HD)
    for c0 in range(0, G_HEADS * G_HD, LANES):
        gq = gq_all[:, c0:c0 + LANES]
        gq = gq * lax.rsqrt(_head_pair_ms(gq) + EPS) * gqn_ref[...]
        if rotary:
            gq = _rope(gq, cg_ref[...], sg_ref[...], G_HD // 2, G_HD)
        gq_ref[:, c0:c0 + LANES] = (gq * (G_HD ** -0.5 * LOG2E)).astype(gq_ref.dtype)
    kvw = G_KV_HEADS * G_HD
    gkv = proj(_GK0, 2 * kvw)
    gk = gkv[:, :kvw]
    gk = gk * lax.rsqrt(_head_pair_ms(gk) + EPS) * gkn_ref[...]
    if rotary:
        gk = _rope(gk, cg_ref[...], sg_ref[...], G_HD // 2, G_HD)
    gk_ref[...] = gk
    gv_ref[...] = gkv[:, kvw:]

    hw = M_HEADS * M_DK
    gates_ref[...] = _sigmoid(proj(_GATE0, _GATE1 - _GATE0)).astype(gates_ref.dtype)
    og_ref[...] = _sigmoid(proj(_MO0, hw)).astype(og_ref.dtype)
    qkv_ref[:, 0:hw] = proj(_MQ0, hw).astype(qkv_ref.dtype)
    qkv_ref[:, hw:2 * hw] = (proj(_MK0, hw) * (M_DK ** -0.5)).astype(qkv_ref.dtype)
    qkv_ref[:, 2 * hw:3 * hw] = proj(_MV0, hw).astype(qkv_ref.dtype)


def _inproj_call(x, mod, lw, mod_row0, rows_per_mod, rope, seq):
    n, d = x.shape
    tm = TOKEN_BLOCK
    hw = M_HEADS * M_DK

    def mod_idx(i):
        return (mod_row0 + (i * tm) // rows_per_mod, 0, 0)

    def rows(width):
        return pl.BlockSpec((tm, width), lambda i: (i, 0))

    in_specs = [
        rows(d),
        pl.BlockSpec((None, N_MOD, d), mod_idx),
        _resident((1, d)),
        _resident(lw["w_in"].shape[1:], lw["layer"]),
        _resident((d, _WIN_COLS - _AQ0), lw["layer"]),
        _resident((1, LANES)),
        _resident((1, A_Q_LORA)),
        _resident((A_Q_LORA, A_HEADS * LANES), lw["layer"]),
        _resident((1, A_KV_LORA)),
        _resident((1, LANES)),
        _resident((1, LANES)),
        _resident((1, LANES)),
        _resident((1, LANES)),
    ]
    args = [x, mod, lw["norm1_g"], lw["w_in"], lw["w_in_tail"], lw["b_mgate"], lw["a_qlora_g"], lw["w_uq"],
            lw["a_kvlora_g"], lw["g_knorm_g"], lw["a_qnorm_g"], lw["g_qnorm_g"], lw["a_knorm_g"]]
    if rope is not None:
        blocks_per_seq = seq // tm
        in_specs += [pl.BlockSpec((tm, LANES), lambda i: (i % blocks_per_seq, 0))] * 4
        args += [rope[0][0], rope[0][1], rope[1][0], rope[1][1]]
    out_widths = [3 * d, 3 * hw, hw, LANES, A_HEADS * LANES, A_KV_LORA, LANES, G_HEADS * G_HD,
                  G_KV_HEADS * G_HD, G_KV_HEADS * G_HD]
    out_dtypes = [BF16, BF16, BF16, F32, BF16, F32, F32, BF16, F32, F32]
    return pl.pallas_call(
        functools.partial(_inproj_kernel, rotary=rope is not None),
        out_shape=[jax.ShapeDtypeStruct((n, w), t) for w, t in zip(out_widths, out_dtypes)],
        grid=(n // tm,),
        in_specs=in_specs,
        out_specs=[rows(w) for w in out_widths],
        compiler_params=_params(1, 48),
        name="in_proj",
    )(*args)


def _mlstm_kernel(*refs, n_chunks, heads, has_init, emit_state):
    q_ref, k_ref, v_ref, og_ref, mg_ref, gn_ref = refs[:6]
    pos = 6
    if has_init:
        c0_ref, n0_ref, m0_ref = refs[pos:pos + 3]
        pos += 3
    y_ref = refs[pos]
    pos += 1
    if emit_state:
        cf_ref, nf_ref, mf_ref = refs[pos:pos + 3]
        pos += 3
    (pcol_scr, lfc_scr, grow_scr, prow_scr, lfr_scr, u_scr, nu_scr, gm_scr, tot_scr, cs_scr, ns_scr, ms_scr,
     c_scr, n_scr, m_scr) = refs[pos:]
    assert heads == M_HEADS
    n_gates = 4 * M_HEADS

    ln = MLSTM_CHUNK
    row = lax.broadcasted_iota(jnp.int32, (ln, ln), 0)
    col = lax.broadcasted_iota(jnp.int32, (ln, ln), 1)
    lower = col <= row
    upper = col >= row
    tril = jnp.where(lower, 1.0, 0.0).astype(BF16)
    triu = jnp.where(upper, 1.0, 0.0).astype(BF16)
    ones = jnp.ones((ln, ln), BF16)

    def gate_index(hh, d):
        return 2 * d * heads + hh, (2 * d + 1) * heads + hh

    def summaries(c, carry):
        r0 = pl.multiple_of(c * ln, ln)
        g = mg_ref[pl.ds(r0, ln), :]
        lf = _log_sigmoid(g) * LOG2E
        pcol_scr[pl.ds(r0, ln), :] = _dot01_left(tril, lf)
        lfc_scr[pl.ds(r0, ln), :] = lf
        g_rows = g.T[MG_LANE0:MG_LANE0 + n_gates, :]
        gr = g_rows * LOG2E
        lfr = _log_sigmoid(g_rows) * LOG2E
        pr = _dot01(lfr, triu)
        grow_scr[c] = gr
        prow_scr[c] = pr
        lfr_scr[c] = lfr
        for hh in range(heads):
            kb = k_ref[pl.ds(r0, ln), hh * LANES:(hh + 1) * LANES]
            k_t = kb.astype(F32).T
            v = v_ref[pl.ds(r0, ln), hh * LANES:(hh + 1) * LANES]
            for d in range(2):
                ji, jf = gate_index(hh, d)
                tot = pr[jf:jf + 1, ln - 1:ln]
                b_row = pr[jf:jf + 1, :] if d == 0 else tot - pr[jf:jf + 1, :] + lfr[jf:jf + 1, :]
                g_row = tot - b_row + gr[ji:ji + 1, :]
                gmax = jnp.max(g_row, axis=-1, keepdims=True)
                wg = jnp.exp2(g_row - gmax)
                u_scr[hh, d, c] = _dot((k_t * wg).astype(BF16), v)
                nu_scr[hh, d, c] = _dot(jnp.broadcast_to(wg, (SUBLANES, ln)).astype(BF16), kb)[0:1, :]
                gm_scr[hh, d, c] = jnp.broadcast_to(gmax, (1, ln))
                tot_scr[hh, d, c] = jnp.broadcast_to(tot, (1, ln))
        return carry

    lax.fori_loop(0, n_chunks, summaries, 0, unroll=min(2, n_chunks))

    for hh in range(heads):
        for d in range(2):
            if has_init:
                c_scr[hh, d] = c0_ref[d, hh]
                n_scr[hh, d] = n0_ref[d, hh]
                m_scr[hh, d] = m0_ref[d, hh] * LOG2E
            else:
                c_scr[hh, d] = jnp.zeros((M_DK, M_DV), F32)
                n_scr[hh, d] = jnp.zeros((1, M_DK), F32)
                m_scr[hh, d] = jnp.zeros((1, ln), F32)

    def scan(j, carry):
        for hh in range(heads):
            for d in range(2):
                c = j if d == 0 else n_chunks - 1 - j
                cst = c_scr[hh, d]
                nst = n_scr[hh, d]
                mst = m_scr[hh, d]
                cs_scr[hh, d, c] = cst.astype(BF16)
                ns_scr[hh, d, c] = nst
                ms_scr[hh, d, c] = mst
                gmax = gm_scr[hh, d, c]
                total = tot_scr[hh, d, c] + mst
                m_new = jnp.maximum(total, gmax)
                decay = jnp.exp2(total - m_new)
                scale = jnp.exp2(gmax - m_new)
                c_scr[hh, d] = cst * decay + u_scr[hh, d, c] * scale
                n_scr[hh, d] = nst * decay + nu_scr[hh, d, c] * scale
                m_scr[hh, d] = m_new
        return carry

    lax.fori_loop(0, n_chunks, scan, 0)

    def readout(c, carry):
        r0 = pl.multiple_of(c * ln, ln)
        pc = pcol_scr[pl.ds(r0, ln), :]
        lf = lfc_scr[pl.ds(r0, ln), :]
        gr = grow_scr[c]
        pr = prow_scr[c]
        lfr = lfr_scr[c]
        tot_all = pc[ln - 1:ln, :]
        for hh in range(heads):
            q = q_ref[pl.ds(r0, ln), hh * LANES:(hh + 1) * LANES]
            k = k_ref[pl.ds(r0, ln), hh * LANES:(hh + 1) * LANES]
            v = v_ref[pl.ds(r0, ln), hh * LANES:(hh + 1) * LANES]
            qk = _dot_nt(q, k)
            hsum = None
            for d in range(2):
                ji, jf = gate_index(hh, d)
                lane_f = MG_LANE0 + jf
                if d == 0:
                    b_col = pc[:, lane_f:lane_f + 1]
                    r_row = gr[ji:ji + 1, :] - pr[jf:jf + 1, :]
                    mask = lower
                else:
                    tot = tot_all[:, lane_f:lane_f + 1]
                    b_col = tot - pc[:, lane_f:lane_f + 1] + lf[:, lane_f:lane_f + 1]
                    r_row = gr[ji:ji + 1, :] - (tot - pr[jf:jf + 1, :] + lfr[jf:jf + 1, :])
                    mask = upper
                b_colb = jnp.broadcast_to(b_col, (ln, ln))
                d_log = jnp.where(mask, b_colb + r_row, -jnp.inf)
                dmax = jnp.broadcast_to(jnp.max(d_log, axis=-1, keepdims=True), (ln, ln))
                inter = b_colb + ms_scr[hh, d, c]
                m_t = jnp.maximum(inter, dmax)
                sb = (qk * jnp.exp2(d_log - m_t)).astype(BF16)
                w_inter = jnp.exp2(inter - m_t)
                qc = _dot(q, cs_scr[hh, d, c])
                qn = _dot_nt(q, jnp.broadcast_to(ns_scr[hh, d, c], (ln, M_DK)).astype(BF16))
                num = qc * w_inter + _dot(sb, v)
                den = qn * w_inter + _dot(sb, ones)
                hc = num / jnp.maximum(jnp.abs(den), jnp.exp2(-m_t))
                hsum = hc if hsum is None else hsum + hc
            y_ref[pl.ds(r0, ln), hh * LANES:(hh + 1) * LANES] = (
                _rms(hsum, M_DV) * gn_ref[...] * og_ref[pl.ds(r0, ln), hh * LANES:(hh + 1) * LANES]
            ).astype(y_ref.dtype)
        return carry

    lax.fori_loop(0, n_chunks, readout, 0, unroll=min(2, n_chunks))

    if emit_state:
        for hh in range(heads):
            for d in range(2):
                cf_ref[d, hh] = c_scr[hh, d]
                nf_ref[d, hh] = n_scr[hh, d]
                mf_ref[d, hh] = m_scr[hh, d] * (1.0 / LOG2E)


def _mlstm_call(qkv, og, mg, lw, batch, seq, init, emit_state):
    n = batch * seq
    ln = MLSTM_CHUNK
    nc = seq // ln
    h = M_HEADS
    hs = MLSTM_HEADS_PER_STEP
    ng = h // hs

    def tok(colblock):
        return pl.BlockSpec((seq, hs * LANES), lambda b, hg: (b, colblock(hg)))

    in_specs = [
        tok(lambda hg: hg), tok(lambda hg: ng + hg), tok(lambda hg: 2 * ng + hg),
        tok(lambda hg: hg),
        pl.BlockSpec((seq, LANES), lambda b, hg: (b, 0)),
        pl.BlockSpec((1, M_DV), lambda b, hg: (0, 0)),
    ]
    args = [qkv, qkv, qkv, og, mg, lw["m_norm_g"]]
    state_specs = [
        pl.BlockSpec((None, 2, hs, M_DK, M_DV), lambda b, hg: (b, 0, hg, 0, 0)),
        pl.BlockSpec((None, 2, hs, 1, M_DK), lambda b, hg: (b, 0, hg, 0, 0)),
        pl.BlockSpec((None, 2, hs, 1, ln), lambda b, hg: (b, 0, hg, 0, 0)),
    ]
    if init is not None:
        c_all, n_all, m_all, layer = init
        depth = c_all.shape[1]
        in_specs += [
            pl.BlockSpec((None, None, 2, hs, M_DK, M_DV), lambda b, hg: (b, layer, 0, hg, 0, 0)),
            pl.BlockSpec((None, None, 2, hs, 1, M_DK), lambda b, hg: (b, layer, 0, hg, 0, 0)),
            pl.BlockSpec((None, None, 2, hs, 1, ln), lambda b, hg: (b, layer, 0, hg, 0, 0)),
        ]
        args += [c_all, n_all.reshape(batch, depth, 2, h, 1, M_DK),
                 jnp.broadcast_to(m_all[..., None, None], (batch, depth, 2, h, 1, ln))]
    out_shape = [jax.ShapeDtypeStruct((n, h * M_DV), BF16)]
    out_specs = [tok(lambda hg: hg)]
    if emit_state:
        out_shape += [
            jax.ShapeDtypeStruct((batch, 2, h, M_DK, M_DV), F32),
            jax.ShapeDtypeStruct((batch, 2, h, 1, M_DK), F32),
            jax.ShapeDtypeStruct((batch, 2, h, 1, ln), F32),
        ]
        out_specs += state_specs
    per = (hs, 2, nc)
    outs = pl.pallas_call(
        functools.partial(_mlstm_kernel, n_chunks=nc, heads=hs, has_init=init is not None,
                          emit_state=emit_state),
        out_shape=out_shape,
        grid=(batch, ng),
        in_specs=in_specs,
        out_specs=out_specs,
        scratch_shapes=[
            pltpu.VMEM((seq, LANES), F32), pltpu.VMEM((seq, LANES), F32),
            pltpu.VMEM((nc, 4 * h, ln), F32), pltpu.VMEM((nc, 4 * h, ln), F32), pltpu.VMEM((nc, 4 * h, ln), F32),
            pltpu.VMEM(per + (M_DK, M_DV), F32), pltpu.VMEM(per + (1, M_DK), F32),
            pltpu.VMEM(per + (1, ln), F32), pltpu.VMEM(per + (1, ln), F32),
            pltpu.VMEM(per + (M_DK, M_DV), BF16), pltpu.VMEM(per + (1, M_DK), F32), pltpu.VMEM(per + (1, ln), F32),
            pltpu.VMEM((hs, 2, M_DK, M_DV), F32), pltpu.VMEM((hs, 2, 1, M_DK), F32), pltpu.VMEM((hs, 2, 1, ln), F32),
        ],
        compiler_params=_params(2),
        name="mlstm",
    )(*args)
    if emit_state:
        ym, cf, nf, mf = outs
        return ym, (cf, nf[:, :, :, 0, :], mf[:, :, :, 0, 0])
    return outs[0], None


HEAD_V = 64
BF16_ROWS = 2 * SUBLANES
VT_ROWS = HEAD_V + BF16_ROWS


def _ones_row(v_t):
    return jnp.where(lax.broadcasted_iota(jnp.int32, v_t.shape, 0) == HEAD_V, 1.0, v_t)


def _flash_heads(qs, keys_of, vt_chunk, n_keys, o_ref, heads_per_rows):
    def values(e, s_t):
        m = acc = None
        for c0 in range(0, n_keys, KEY_BLOCK):
            c1 = min(c0 + KEY_BLOCK, n_keys)
            s = s_t[c0:c1, :]
            mc = jnp.max(s, axis=0, keepdims=True)
            if c0 == 0:
                m = mc
                acc = _dot(vt_chunk(e, c0, c1), jnp.exp2(s - mc).astype(BF16))
            else:
                m_new = jnp.maximum(m, mc)
                acc = acc * jnp.exp2(m - m_new) + _dot(vt_chunk(e, c0, c1), jnp.exp2(s - m_new).astype(BF16))
                m = m_new
        return acc[0:HEAD_V, :] / acc[HEAD_V:HEAD_V + 1, :]

    if n_keys <= KEY_BLOCK:
        s_all = [_dot_nt(keys_of(e), qb) for e, qb in enumerate(qs)]
        outs = [values(e, s_t) for e, s_t in enumerate(s_all)]
    else:
        outs = []
        pending = [_dot_nt(keys_of(e), qs[e]) for e in range(min(SCORES_AHEAD, len(qs)))]
        for e in range(len(qs)):
            s_t = pending.pop(0)
            if e + SCORES_AHEAD < len(qs):
                pending.append(_dot_nt(keys_of(e + SCORES_AHEAD), qs[e + SCORES_AHEAD]))
            outs.append(values(e, s_t))
    tq = qs[0].shape[0]
    for e in range(0, len(outs), 2):
        pair = jnp.concatenate([outs[e], outs[e + 1]], axis=0)
        r0 = (e // heads_per_rows) * tq
        c0 = ((e % heads_per_rows) // 2) * LANES
        o_ref[r0:r0 + tq, c0:c0 + LANES] = pair.T.astype(o_ref.dtype)


def _batch_per_step(batch, nq, past):
    return SHORT_SEQ_BATCH_PER_STEP if (nq == 1 and past == 0 and batch % SHORT_SEQ_BATCH_PER_STEP == 0) else 1


def _mla_kernel(*refs, past, rotary, heads, bb):
    q_ref, ckv_ref, akr_ref = refs[:3]
    pos = 3
    if rotary:
        mg_ref = refs[pos]
        pos += 1
    if past:
        ckvc_ref, krc_ref = refs[pos:pos + 2]
        pos += 2
    wk_ref, wvt_ref, kg_ref = refs[pos:pos + 3]
    pos += 3
    o_ref, kbuf, vbuf = refs[pos:]
    seq = ckv_ref.shape[0] // bb
    tq = q_ref.shape[0] // bb

    @pl.when(pl.program_id(2) == 0)
    def _():
        for i in range(bb * heads):
            sb, e = divmod(i, heads)
            rows = slice(sb * seq, (sb + 1) * seq)
            wk_h = wk_ref[:, e * LANES:(e + 1) * LANES]
            wvt_h = wvt_ref[e * VT_ROWS:(e + 1) * VT_ROWS, :]

            def keys(ckv, kr, kr_ss):
                cb = ckv.astype(BF16)
                k_nope = _dot(cb, wk_h)
                if kr_ss is None:
                    kcat = k_nope + kr
                    ss = jnp.sum(kcat * kcat, axis=-1, keepdims=True)
                    kcat = kcat * kg_ref[...]
                else:
                    ss = jnp.sum(k_nope * k_nope, axis=-1, keepdims=True) + kr_ss
                    kcat = k_nope * kg_ref[...] + kr
                return _ones_row(_dot_nt(wvt_h, cb)), kcat * lax.rsqrt(ss * (1.0 / A_QK) + EPS)

            kr_ss = mg_ref[rows, KR_SS_LANE:KR_SS_LANE + 1] if rotary else None
            v_t, kn = keys(ckv_ref[rows, :], akr_ref[rows, :], kr_ss)
            kbuf[i, past:past + seq, :] = kn.astype(BF16)
            vbuf[i, :, past:past + seq] = v_t.astype(BF16)
            if past:
                v_tc, knc = keys(ckvc_ref[...], krc_ref[...], None)
                kbuf[i, 0:past, :] = knc.astype(BF16)
                vbuf[i, :, 0:past] = v_tc.astype(BF16)

    qs = [q_ref[sb * tq:(sb + 1) * tq, e * LANES:(e + 1) * LANES] for sb in range(bb) for e in range(heads)]
    _flash_heads(qs, lambda i: kbuf[i], lambda i, c0, c1: vbuf[i, :, c0:c1], past + seq, o_ref, heads)


def _mla_call(qa, ckv, akr, mg, lw, batch, seq, cache, rotary):
    n = batch * seq
    tq = min(Q_BLOCK, seq)
    nq = seq // tq
    past = 0 if cache is None else cache[0].shape[2]
    heads = MLA_HEADS_PER_STEP if nq > 1 else A_HEADS
    bb = _batch_per_step(batch, nq, past)
    in_specs = [
        pl.BlockSpec((bb * tq, heads * LANES), lambda b, hg, qi: (b * nq + qi, hg)),
        pl.BlockSpec((bb * seq, LANES), lambda b, hg, qi: (b, 0)),
        pl.BlockSpec((bb * seq, LANES), lambda b, hg, qi: (b, 0)),
    ]
    args = [qa, ckv, akr]
    if rotary:
        in_specs += [pl.BlockSpec((bb * seq, LANES), lambda b, hg, qi: (b, 0))]
        args += [mg]
    layer = lw["layer"]
    if past:
        in_specs += [pl.BlockSpec((None, None, past, LANES), lambda b, hg, qi: (b, layer, 0, 0))] * 2
        args += [cache[0], cache[1]]
    in_specs += [
        pl.BlockSpec((None, A_KV_LORA, heads * LANES), lambda b, hg, qi: (layer, 0, hg)),
        pl.BlockSpec((None, heads * VT_ROWS, A_KV_LORA), lambda b, hg, qi: (layer, hg, 0)),
        pl.BlockSpec((1, LANES), lambda b, hg, qi: (0, 0)),
    ]
    args += [lw["w_uk"], lw["w_uvt"], lw["a_knorm_g"]]
    return pl.pallas_call(
        functools.partial(_mla_kernel, past=past, rotary=rotary, heads=heads, bb=bb),
        out_shape=jax.ShapeDtypeStruct((n, A_HEADS * A_V), BF16),
        grid=(batch // bb, A_HEADS // heads, nq),
        in_specs=in_specs,
        out_specs=pl.BlockSpec((bb * tq, heads * A_V), lambda b, hg, qi: (b * nq + qi, hg)),
        scratch_shapes=[pltpu.VMEM((bb * heads, past + seq, LANES), BF16),
                        pltpu.VMEM((bb * heads, VT_ROWS, past + seq), BF16)],
        compiler_params=_params(3, 40),
        name="mla_attn",
    )(*args)


def _gqa_kernel(*refs, past, groups, bb):
    q_ref, k_ref, v_ref = refs[:3]
    pos = 3
    if past:
        kc_ref, vc_ref = refs[pos:pos + 2]
        pos += 2
    o_ref, kbuf, vbuf = refs[pos:]
    seq = k_ref.shape[0] // bb
    tq = q_ref.shape[0] // bb
    qw = G_GROUP * G_HD

    @pl.when(pl.program_id(2) == 0)
    def _():
        for i in range(bb * groups):
            sb, gi = divmod(i, groups)
            grp = pl.program_id(1) * groups + gi

            r = lax.broadcasted_iota(jnp.int32, (VT_ROWS, LANES), 0)
            c = lax.broadcasted_iota(jnp.int32, (VT_ROWS, LANES), 1)
            pick = jnp.where((c == r + grp * G_HD) & (r < G_HD), 1.0, 0.0).astype(BF16)

            def fill(r0, k, v):
                low = _lane(k.shape) < G_HD
                k_lo = jnp.where(grp == 0, k, pltpu.roll(k, G_HD, 1))
                rows = k.shape[0]
                kbuf[i, r0:r0 + rows, :] = jnp.where(low, k_lo, pltpu.roll(k_lo, G_HD, 1)).astype(BF16)
                vbuf[i, :, r0:r0 + rows] = _ones_row(_dot_nt(pick, v.astype(BF16))).astype(BF16)

            fill(past, k_ref[sb * seq:(sb + 1) * seq, :], v_ref[sb * seq:(sb + 1) * seq, :])
            if past:
                fill(0, kc_ref[...], vc_ref[...])

    qs = []
    for i in range(bb * groups):
        sb, gi = divmod(i, groups)
        for j in range(G_GROUP):
            col = q_ref[sb * tq:(sb + 1) * tq, gi * qw + (j // 2) * LANES:gi * qw + (j // 2 + 1) * LANES]
            keep = (_lane(col.shape) < G_HD) == (j % 2 == 0)
            qs.append(jnp.where(keep, col, jnp.zeros_like(col)))
    _flash_heads(qs, lambda e: kbuf[e // G_GROUP], lambda e, c0, c1: vbuf[e // G_GROUP, :, c0:c1],
                 past + seq, o_ref, groups * G_GROUP)


def _gqa_call(gq, gk, gv, batch, seq, cache):
    n = batch * seq
    tq = min(Q_BLOCK, seq)
    nq = seq // tq
    past = 0 if cache is None else cache[0].shape[2]
    groups = GQA_GROUPS_PER_STEP if nq > 1 else G_KV_HEADS
    bb = _batch_per_step(batch, nq, past)
    qw = G_GROUP * G_HD
    kvw = G_KV_HEADS * G_HD
    in_specs = [
        pl.BlockSpec((bb * tq, groups * qw), lambda b, g, qi: (b * nq + qi, g)),
        pl.BlockSpec((bb * seq, kvw), lambda b, g, qi: (b, 0)),
        pl.BlockSpec((bb * seq, kvw), lambda b, g, qi: (b, 0)),
    ]
    args = [gq, gk, gv]
    if past:
        layer = cache[2]
        in_specs += [pl.BlockSpec((None, None, past, kvw), lambda b, g, qi: (b, layer, 0, 0))] * 2
        args += [cache[0], cache[1]]
    return pl.pallas_call(
        functools.partial(_gqa_kernel, past=past, groups=groups, bb=bb),
        out_shape=jax.ShapeDtypeStruct((n, G_HEADS * G_HD), BF16),
        grid=(batch // bb, G_KV_HEADS // groups, nq),
        in_specs=in_specs,
        out_specs=pl.BlockSpec((bb * tq, groups * qw), lambda b, g, qi: (b * nq + qi, g)),
        scratch_shapes=[pltpu.VMEM((bb * groups, past + seq, LANES), BF16),
                        pltpu.VMEM((bb * groups, VT_ROWS, past + seq), BF16)],
        compiler_params=_params(3, 40),
        name="gqa_attn",
    )(*args)


def _merge_ffn_kernel(x_ref, ym_ref, ya_ref, yg_ref, gates_ref, mod_ref, g2_ref, wb_ref, wo_ref, wfi_ref,
                      wfo_ref, o_ref, *, ff_chunks):
    x = x_ref[...]
    d = x.shape[-1]
    mixed = None
    for i, y_ref in enumerate((ym_ref, ya_ref, yg_ref)):
        br = _dot(y_ref[...].astype(BF16), wb_ref[i]) * gates_ref[:, i * d:(i + 1) * d]
        mixed = br if mixed is None else mixed + br
    gt1 = mod_ref[2:3, :]
    x1 = x + gt1 * _dot(mixed.astype(BF16), wo_ref[...])

    sh2 = mod_ref[3:4, :]
    sc2 = mod_ref[4:5, :]
    gt2 = mod_ref[5:6, :]
    h2 = ((_rms(x1, d) * g2_ref[...]) * (1.0 + sc2) + sh2).astype(BF16)
    d_ff = wfo_ref.shape[0]
    acc = None
    for c0, c1 in ff_chunks:
        ug = _dot(h2, wfi_ref[:, c0:c1])
        uv = _dot(h2, wfi_ref[:, d_ff + c0:d_ff + c1])
        act = (ug * _sigmoid(ug) * uv).astype(BF16)
        part = _dot(act, wfo_ref[c0:c1, :])
        acc = part if acc is None else acc + part
    o_ref[...] = x1 + gt2 * acc


def _merge_ffn_call(x, ym, ya, yg, gates, mod, lw, mod_row0, rows_per_mod):
    n, d = x.shape
    tm = TOKEN_BLOCK
    d_ff = lw["w_ffn_out"].shape[1]
    layer = lw["layer"]
    split = -(-(d_ff // 2) // MXU_COLS) * MXU_COLS
    ff_chunks = ((0, split), (split, d_ff)) if 0 < split < d_ff else ((0, d_ff),)

    def mod_idx(i):
        return (mod_row0 + (i * tm) // rows_per_mod, 0, 0)

    def rows(width):
        return pl.BlockSpec((tm, width), lambda i: (i, 0))

    return pl.pallas_call(
        functools.partial(_merge_ffn_kernel, ff_chunks=ff_chunks),
        out_shape=jax.ShapeDtypeStruct((n, d), F32),
        grid=(n // tm,),
        in_specs=[
            rows(d), rows(BRANCH_WIDTH), rows(BRANCH_WIDTH), rows(BRANCH_WIDTH), rows(N_BRANCH * d),
            pl.BlockSpec((None, N_MOD, d), mod_idx),
            _resident((1, d)),
            _resident((N_BRANCH, BRANCH_WIDTH, d), layer),
            _resident((d, d), layer),
            _resident((d, 2 * d_ff), layer),
            _resident((d_ff, d), layer),
        ],
        out_specs=rows(d),
        compiler_params=_params(1),
        name="merge_ffn",
    )(x, ym, ya, yg, gates, mod, lw["norm2_g"], lw["w_branch"], lw["w_out"], lw["w_ffn_in"], lw["w_ffn_out"])


def _mla_q_layout(a):
    lead = a.shape[:-1]
    a = a.reshape(lead + (A_HEADS, A_QK))
    z = jnp.zeros(lead + (A_HEADS, LANES - A_QK), a.dtype)
    return jnp.concatenate([a[..., A_NOPE:], z, a[..., :A_NOPE]], axis=-1).reshape(lead + (A_HEADS * LANES,))


def _stacked_weights(w_in, w_uq, w_ukv, w_branch, w_out, w_ffn_in, w_ffn_out):
    depth, d, _ = w_in.shape
    hw = M_HEADS * M_DK
    sizes = (N_BRANCH * d, hw, hw, hw, hw, 4 * M_HEADS, A_Q_LORA, A_KV_LORA, A_ROPE, G_HEADS * G_HD,
             G_KV_HEADS * G_HD, G_KV_HEADS * G_HD)
    edges = np.concatenate([[0], np.cumsum(sizes)]).tolist()
    (wmg, waq, wakv, wakr, wgq, wgk, wgv) = [w_in[:, :, a:b] for a, b in zip(edges[5:-1], edges[6:])]
    misc_pad = jnp.zeros((depth, d, LANES - A_ROPE - 4 * M_HEADS), w_in.dtype)
    tail = jnp.concatenate([waq, wakv, wakr, wmg, misc_pad, wgq, wgk, wgv], axis=2).astype(BF16)
    ukv = w_ukv.reshape(depth, A_KV_LORA, A_HEADS, A_NOPE + A_V)
    uk = jnp.pad(ukv[..., :A_NOPE], ((0, 0), (0, 0), (0, 0), (LANES - A_NOPE, 0)))
    uvt = jnp.pad(ukv[..., A_NOPE:].transpose(0, 2, 3, 1), ((0, 0), (0, 0), (0, VT_ROWS - A_V), (0, 0)))
    return dict(
        w_in=w_in.astype(BF16),
        w_in_tail=tail,
        w_uq=_mla_q_layout(w_uq).astype(BF16),
        w_uk=uk.reshape(depth, A_KV_LORA, A_HEADS * LANES).astype(BF16),
        w_uvt=uvt.reshape(depth, A_HEADS * VT_ROWS, A_KV_LORA).astype(BF16),
        w_branch=w_branch.astype(BF16),
        w_out=w_out.astype(BF16),
        w_ffn_in=w_ffn_in.astype(BF16),
        w_ffn_out=w_ffn_out.astype(BF16),
    )


def _layer_vectors(l, b_mgate, norm1_g, m_norm_g, a_qlora_g, a_kvlora_g, a_qnorm_g, a_knorm_g, g_qnorm_g,
                   g_knorm_g, norm2_g):
    return dict(
        layer=l,
        b_mgate=jnp.pad(b_mgate[l][None, :], ((0, 0), (MG_LANE0, LANES - MG_LANE0 - 4 * M_HEADS))),
        norm1_g=norm1_g[l][None, :],
        m_norm_g=m_norm_g[l][None, :],
        a_qlora_g=a_qlora_g[l][None, :],
        a_kvlora_g=a_kvlora_g[l][None, :],
        a_qnorm_g=_mla_q_layout(jnp.tile(a_qnorm_g[l], A_HEADS)[None, :])[:, :LANES],
        a_knorm_g=_mla_q_layout(jnp.tile(a_knorm_g[l], A_HEADS)[None, :])[:, :LANES],
        g_qnorm_g=jnp.tile(g_qnorm_g[l], LANES // G_HD)[None, :],
        g_knorm_g=jnp.tile(g_knorm_g[l], G_KV_HEADS)[None, :],
        norm2_g=norm2_g[l][None, :],
    )


def _axial_angles(seq, rot_dim):
    n_freq = rot_dim // 4
    freqs = ROPE_BASE ** (-jnp.arange(n_freq, dtype=F32) / n_freq)
    t = jnp.arange(seq)
    row = (t // GRID_W).astype(F32)
    col = (t % GRID_W).astype(F32)
    return jnp.concatenate([row[:, None] * freqs, col[:, None] * freqs], axis=-1)


def _rope_tables(seq):
    ang = _axial_angles(seq, A_ROPE)
    one = jnp.ones((seq, LANES - A_ROPE), F32)
    mla_cos = jnp.concatenate([jnp.cos(ang), jnp.cos(ang), one], axis=-1)
    mla_sin = jnp.concatenate([-jnp.sin(ang), jnp.sin(ang), 0.0 * one], axis=-1)
    ang = _axial_angles(seq, G_HD)
    cos = jnp.concatenate([jnp.cos(ang), jnp.cos(ang)], axis=-1)
    sin = jnp.concatenate([-jnp.sin(ang), jnp.sin(ang)], axis=-1)
    gqa = (jnp.tile(cos, (1, LANES // G_HD)), jnp.tile(sin, (1, LANES // G_HD)))
    return (mla_cos, mla_sin), gqa


def _layer(x, mod, lw, batch, seq, mod_row0, rows_per_mod, ctx, rope):
    (gates, qkv, og, mg, qa, ckv, akr, gq, gk, gv) = _inproj_call(x, mod, lw, mod_row0, rows_per_mod, rope, seq)
    if ctx is None:
        ym, state = _mlstm_call(qkv, og, mg, lw, batch, seq, None, True)
        ya = _mla_call(qa, ckv, akr, mg, lw, batch, seq, None, False)
        yg = _gqa_call(gq, gk, gv, batch, seq, None)
        new_ctx = dict(state=state, ckv=ckv, kr=akr[:, :A_ROPE], gk=gk, gv=gv)
    else:
        ym, _ = _mlstm_call(qkv, og, mg, lw, batch, seq, ctx["mlstm"], False)
        ya = _mla_call(qa, ckv, akr, mg, lw, batch, seq, ctx["mla"], True)
        yg = _gqa_call(gq, gk, gv, batch, seq, ctx["gqa"])
        new_ctx = None
    x = _merge_ffn_call(x, ym, ya, yg, gates, mod, lw, mod_row0, rows_per_mod)
    return x, new_ctx


def kernel(x_prompt, x_sample, state_mlstm_C, state_mlstm_n, state_mlstm_m, cache_mla_ckv, cache_mla_krope,
           cache_gqa_k, cache_gqa_v, c, c_ctx, w_mod, b_mod, norm1_g, w_in, b_mgate, m_norm_g, a_qlora_g,
           a_kvlora_g, w_uq, w_ukv, a_qnorm_g, a_knorm_g, g_qnorm_g, g_knorm_g, w_branch, w_out, norm2_g,
           w_ffn_in, w_ffn_out):
    batch, seq, d = x_prompt.shape
    dbatch, dseq, _ = x_sample.shape
    depth = w_in.shape[0]
    past = cache_mla_ckv.shape[2]
    for tokens, length in ((batch * seq, seq), (dbatch * dseq, dseq)):
        assert tokens % TOKEN_BLOCK == 0 and length % MLSTM_CHUNK == 0 and length % min(Q_BLOCK, length) == 0
        assert TOKEN_BLOCK % length == 0 or length % TOKEN_BLOCK == 0
    assert d == D_MODEL and (N_MOD * d) % (MOD_COL_BLOCKS * LANES) == 0
    assert dseq % TOKEN_BLOCK == 0 and dseq % GRID_W == 0 and past % LANES == 0
    assert MLSTM_HEADS_PER_STEP == M_HEADS

    n_rows = -(-(1 + dbatch) // SUBLANES) * SUBLANES
    cond = jnp.concatenate([c_ctx[None, :], c, jnp.zeros((n_rows - 1 - dbatch, d), F32)], axis=0)
    mod_all = _mod_call(cond, w_mod, b_mod).reshape(depth, n_rows, N_MOD, d)

    rope = _rope_tables(dseq)
    xp = x_prompt.reshape(batch * seq, d)
    xs = x_sample.reshape(dbatch * dseq, d)
    ctx_layers = []
    krope_cache = jnp.pad(cache_mla_krope, ((0, 0), (0, 0), (0, 0), (0, LANES - A_ROPE)))
    gqa_k_cache = cache_gqa_k.reshape(dbatch, depth, past, G_KV_HEADS * G_HD)
    gqa_v_cache = cache_gqa_v.reshape(dbatch, depth, past, G_KV_HEADS * G_HD)
    stacked = _stacked_weights(w_in, w_uq, w_ukv, w_branch, w_out, w_ffn_in, w_ffn_out)
    for l in range(depth):
        lw = dict(stacked, **_layer_vectors(l, b_mgate, norm1_g, m_norm_g, a_qlora_g, a_kvlora_g, a_qnorm_g,
                                            a_knorm_g, g_qnorm_g, g_knorm_g, norm2_g))
        xp, st = _layer(xp, mod_all[l], lw, batch, seq, 0, batch * seq, None, None)
        ctx_layers.append(st)
        ctx = dict(mlstm=(state_mlstm_C, state_mlstm_n, state_mlstm_m, l), mla=(cache_mla_ckv, krope_cache),
                   gqa=(gqa_k_cache, gqa_v_cache, l))
        xs, _ = _layer(xs, mod_all[l], lw, dbatch, dseq, 1, dseq, ctx, rope)

    def stack(fn):
        return jnp.stack([fn(s) for s in ctx_layers], axis=1)

    new_c = stack(lambda s: s["state"][0])
    new_n = stack(lambda s: s["state"][1])
    new_m = stack(lambda s: s["state"][2])
    new_ckv = stack(lambda s: s["ckv"].reshape(batch, seq, A_KV_LORA))
    new_kr = stack(lambda s: s["kr"].reshape(batch, seq, A_ROPE))
    new_gk = stack(lambda s: s["gk"].reshape(batch, seq, G_KV_HEADS, G_HD))
    new_gv = stack(lambda s: s["gv"].reshape(batch, seq, G_KV_HEADS, G_HD))
    return (xp.reshape(batch, seq, d), xs.reshape(dbatch, dseq, d), new_c, new_n, new_m, new_ckv, new_kr,
            new_gk, new_gv)
```
